```python
import jax, jax.numpy as jnp
from jax import lax
import numpy as np

D_MODEL = 2048
BATCH = 8
SEQ = 8192
DEPTH = 4

MEM_LEN = 256
CONV_DIM = D_MODEL // 2
CONV_KERNEL = 31
SCONV_DIM = D_MODEL // 2
SCONV_KERNEL = 3
XATTN_DIM = D_MODEL // 2
XATTN_HEADS = 4
XATTN_HEAD_DIM = XATTN_DIM // XATTN_HEADS
N_BRANCH = 3
D_FF = 4 * D_MODEL
EPS = 1e-6

IN_SIZES = (CONV_DIM, CONV_DIM, SCONV_DIM, SCONV_DIM, SCONV_DIM, XATTN_DIM, N_BRANCH * D_MODEL)
IN_DIM = int(sum(IN_SIZES))
IN_SPLITS = tuple(int(v) for v in np.cumsum(IN_SIZES)[:-1])

kernel_name = "hybrid_conformer_shortconv_memxattn_block"


def rms_norm(x, g):
    xf = x.astype(jnp.float32)
    y = xf * lax.rsqrt(jnp.mean(xf * xf, axis=-1, keepdims=True) + EPS)
    return (y * g.astype(jnp.float32)).astype(x.dtype)


def layer_norm(x, g, b):
    xf = x.astype(jnp.float32)
    mu = jnp.mean(xf, axis=-1, keepdims=True)
    var = jnp.mean(jnp.square(xf - mu), axis=-1, keepdims=True)
    y = (xf - mu) * lax.rsqrt(var + EPS)
    return (y * g.astype(jnp.float32) + b.astype(jnp.float32)).astype(x.dtype)


def causal_depthwise_conv(u, w):
    k, c = w.shape
    return lax.conv_general_dilated(
        u, w[:, None, :].astype(u.dtype),
        window_strides=(1,), padding=[(k - 1, 0)],
        dimension_numbers=("NWC", "WIO", "NWC"),
        feature_group_count=c)


def memory_cross_attention(q, mem_n, w_kv):
    b, s, _ = q.shape
    m = mem_n.shape[1]
    kv = mem_n @ w_kv
    k, v = jnp.split(kv, 2, axis=-1)
    qh = q.reshape(b, s, XATTN_HEADS, XATTN_HEAD_DIM)
    kh = k.reshape(b, m, XATTN_HEADS, XATTN_HEAD_DIM)
    vh = v.reshape(b, m, XATTN_HEADS, XATTN_HEAD_DIM)
    scale = XATTN_HEAD_DIM ** -0.5
    scores = jnp.einsum("bshd,bmhd->bhsm", qh, kh).astype(jnp.float32) * scale
    probs = jax.nn.softmax(scores, axis=-1).astype(v.dtype)
    o = jnp.einsum("bhsm,bmhd->bshd", probs, vh)
    return o.reshape(b, s, XATTN_DIM)


def hybrid_layer(x, mem, g_mix_pre, w_in, conv_a_w, conv_a_b, ln_a_g, ln_a_b, w_a_out,
                 conv_b_w, w_b_out, g_mem, w_kv, w_x_out, w_o, g_mix_post,
                 g_mlp_pre, w_up, w_down, g_mlp_post):
    b, s, d = x.shape
    h = rms_norm(x, g_mix_pre)
    proj = h @ w_in
    a_val, a_gate, sb, sc, sx, q, gates = jnp.split(proj, IN_SPLITS, axis=-1)

    a = a_val * jax.nn.sigmoid(a_gate)
    a = causal_depthwise_conv(a, conv_a_w) + conv_a_b
    a = jax.nn.silu(layer_norm(a, ln_a_g, ln_a_b))
    y_a = a @ w_a_out

    u = causal_depthwise_conv(sc * sx, conv_b_w)
    y_b = (sb * u) @ w_b_out

    mem_n = rms_norm(mem, g_mem)
    y_x = memory_cross_attention(q, mem_n, w_kv) @ w_x_out

    g = jax.nn.sigmoid(gates).reshape(b, s, N_BRANCH, d)
    merged = g[:, :, 0] * y_a + g[:, :, 1] * y_b + g[:, :, 2] * y_x
    x = x + rms_norm(merged @ w_o, g_mix_post)

    h = rms_norm(x, g_mlp_pre)
    f = jnp.square(jax.nn.relu(h @ w_up)) @ w_down
    x = x + rms_norm(f, g_mlp_post)
    return x


def _fwd_setup_inputs(seed: int = 0) -> dict:
    key = jax.random.key(seed)
    ks = jax.random.split(key, 24)
    f32 = jnp.float32

    def nrm(k, shape, scale):
        return jax.random.normal(k, shape, f32) * scale

    def gain(k, shape):
        return 1.0 + 0.02 * jax.random.normal(k, shape, f32)

    L, D = DEPTH, D_MODEL
    return {
        "x": nrm(ks[0], (BATCH, SEQ, D), 1.0),
        "mem": nrm(ks[1], (BATCH, MEM_LEN, D), 1.0),
        "g_mix_pre": gain(ks[2], (L, D)),
        "w_in": nrm(ks[3], (L, D, IN_DIM), D ** -0.5),
        "conv_a_w": nrm(ks[4], (L, CONV_KERNEL, CONV_DIM), CONV_KERNEL ** -0.5),
        "conv_a_b": nrm(ks[5], (L, CONV_DIM), 0.02),
        "ln_a_g": gain(ks[6], (L, CONV_DIM)),
        "ln_a_b": nrm(ks[7], (L, CONV_DIM), 0.02),
        "w_a_out": nrm(ks[8], (L, CONV_DIM, D), CONV_DIM ** -0.5),
        "conv_b_w": nrm(ks[9], (L, SCONV_KERNEL, SCONV_DIM), SCONV_KERNEL ** -0.5),
        "w_b_out": nrm(ks[10], (L, SCONV_DIM, D), SCONV_DIM ** -0.5),
        "g_mem": gain(ks[11], (L, D)),
        "w_kv": nrm(ks[12], (L, D, 2 * XATTN_DIM), D ** -0.5),
        "w_x_out": nrm(ks[13], (L, XATTN_DIM, D), XATTN_DIM ** -0.5),
        "w_o": nrm(ks[14], (L, D, D), D ** -0.5),
        "g_mix_post": gain(ks[15], (L, D)),
        "g_mlp_pre": gain(ks[16], (L, D)),
        "w_up": nrm(ks[17], (L, D, D_FF), D ** -0.5),
        "w_down": nrm(ks[18], (L, D_FF, D), D_FF ** -0.5),
        "g_mlp_post": gain(ks[19], (L, D)),
    }


def _fwd_reference(x, mem, g_mix_pre, w_in, conv_a_w, conv_a_b, ln_a_g, ln_a_b, w_a_out,
              conv_b_w, w_b_out, g_mem, w_kv, w_x_out, w_o, g_mix_post,
              g_mlp_pre, w_up, w_down, g_mlp_post):
    for l in range(DEPTH):
        x = hybrid_layer(x, mem, g_mix_pre[l], w_in[l], conv_a_w[l], conv_a_b[l],
                         ln_a_g[l], ln_a_b[l], w_a_out[l], conv_b_w[l], w_b_out[l],
                         g_mem[l], w_kv[l], w_x_out[l], w_o[l], g_mix_post[l],
                         g_mlp_pre[l], w_up[l], w_down[l], g_mlp_post[l])
    return x


import jax as _jax
import jax.numpy as _jnp

TWIN_FORMAT = 'train_step'
FWD_PARAMS = ['x', 'mem', 'g_mix_pre', 'w_in', 'conv_a_w', 'conv_a_b', 'ln_a_g', 'ln_a_b', 'w_a_out', 'conv_b_w', 'w_b_out', 'g_mem', 'w_kv', 'w_x_out', 'w_o', 'g_mix_post', 'g_mlp_pre', 'w_up', 'w_down', 'g_mlp_post']
TWIN_WEIGHTS = ['g_mix_pre', 'w_in', 'conv_a_w', 'conv_a_b', 'ln_a_g', 'ln_a_b', 'w_a_out', 'conv_b_w', 'w_b_out', 'g_mem', 'w_kv', 'w_x_out', 'w_o', 'g_mix_post', 'g_mlp_pre', 'w_up', 'w_down', 'g_mlp_post']
TWIN_DIFF_INPUT = 'x'
TWIN_INPUTS = ['x', 'mem', 'g_mix_pre', 'w_in', 'conv_a_w', 'conv_a_b', 'ln_a_g', 'ln_a_b', 'w_a_out', 'conv_b_w', 'w_b_out', 'g_mem', 'w_kv', 'w_x_out', 'w_o', 'g_mix_post', 'g_mlp_pre', 'w_up', 'w_down', 'g_mlp_post', 'loss_target', 'm_g_mix_pre', 'm_w_in', 'm_conv_a_w', 'm_conv_a_b', 'm_ln_a_g', 'm_ln_a_b', 'm_w_a_out', 'm_conv_b_w', 'm_w_b_out', 'm_g_mem', 'm_w_kv', 'm_w_x_out', 'm_w_o', 'm_g_mix_post', 'm_g_mlp_pre', 'm_w_up', 'm_w_down', 'm_g_mlp_post', 'v_g_mix_pre', 'v_w_in', 'v_conv_a_w', 'v_conv_a_b', 'v_ln_a_g', 'v_ln_a_b', 'v_w_a_out', 'v_conv_b_w', 'v_w_b_out', 'v_g_mem', 'v_w_kv', 'v_w_x_out', 'v_w_o', 'v_g_mix_post', 'v_g_mlp_pre', 'v_w_up', 'v_w_down', 'v_g_mlp_post']
TWIN_OUTPUTS = ['loss', 'grad_x', 'grad_g_mix_pre', 'grad_w_in', 'grad_conv_a_w', 'grad_conv_a_b', 'grad_ln_a_g', 'grad_ln_a_b', 'grad_w_a_out', 'grad_conv_b_w', 'grad_w_b_out', 'grad_g_mem', 'grad_w_kv', 'grad_w_x_out', 'grad_w_o', 'grad_g_mix_post', 'grad_g_mlp_pre', 'grad_w_up', 'grad_w_down', 'grad_g_mlp_post', 'delta_g_mix_pre', 'delta_w_in', 'delta_conv_a_w', 'delta_conv_a_b', 'delta_ln_a_g', 'delta_ln_a_b', 'delta_w_a_out', 'delta_conv_b_w', 'delta_w_b_out', 'delta_g_mem', 'delta_w_kv', 'delta_w_x_out', 'delta_w_o', 'delta_g_mix_post', 'delta_g_mlp_pre', 'delta_w_up', 'delta_w_down', 'delta_g_mlp_post', 'new_m_g_mix_pre', 'new_m_w_in', 'new_m_conv_a_w', 'new_m_conv_a_b', 'new_m_ln_a_g', 'new_m_ln_a_b', 'new_m_w_a_out', 'new_m_conv_b_w', 'new_m_w_b_out', 'new_m_g_mem', 'new_m_w_kv', 'new_m_w_x_out', 'new_m_w_o', 'new_m_g_mix_post', 'new_m_g_mlp_pre', 'new_m_w_up', 'new_m_w_down', 'new_m_g_mlp_post', 'new_v_g_mix_pre', 'new_v_w_in', 'new_v_conv_a_w', 'new_v_conv_a_b', 'new_v_ln_a_g', 'new_v_ln_a_b', 'new_v_w_a_out', 'new_v_conv_b_w', 'new_v_w_b_out', 'new_v_g_mem', 'new_v_w_kv', 'new_v_w_x_out', 'new_v_w_o', 'new_v_g_mix_post', 'new_v_g_mlp_pre', 'new_v_w_up', 'new_v_w_down', 'new_v_g_mlp_post']
TWIN_LEAF_KINDS = {'loss': 'loss', 'grad_x': 'grad_x', 'grad_g_mix_pre': 'grad_w', 'grad_w_in': 'grad_w', 'grad_conv_a_w': 'grad_w', 'grad_conv_a_b': 'grad_w', 'grad_ln_a_g': 'grad_w', 'grad_ln_a_b': 'grad_w', 'grad_w_a_out': 'grad_w', 'grad_conv_b_w': 'grad_w', 'grad_w_b_out': 'grad_w', 'grad_g_mem': 'grad_w', 'grad_w_kv': 'grad_w', 'grad_w_x_out': 'grad_w', 'grad_w_o': 'grad_w', 'grad_g_mix_post': 'grad_w', 'grad_g_mlp_pre': 'grad_w', 'grad_w_up': 'grad_w', 'grad_w_down': 'grad_w', 'grad_g_mlp_post': 'grad_w', 'delta_g_mix_pre': 'delta_w', 'delta_w_in': 'delta_w', 'delta_conv_a_w': 'delta_w', 'delta_conv_a_b': 'delta_w', 'delta_ln_a_g': 'delta_w', 'delta_ln_a_b': 'delta_w', 'delta_w_a_out': 'delta_w', 'delta_conv_b_w': 'delta_w', 'delta_w_b_out': 'delta_w', 'delta_g_mem': 'delta_w', 'delta_w_kv': 'delta_w', 'delta_w_x_out': 'delta_w', 'delta_w_o': 'delta_w', 'delta_g_mix_post': 'delta_w', 'delta_g_mlp_pre': 'delta_w', 'delta_w_up': 'delta_w', 'delta_w_down': 'delta_w', 'delta_g_mlp_post': 'delta_w', 'new_m_g_mix_pre': 'new_m', 'new_m_w_in': 'new_m', 'new_m_conv_a_w': 'new_m', 'new_m_conv_a_b': 'new_m', 'new_m_ln_a_g': 'new_m', 'new_m_ln_a_b': 'new_m', 'new_m_w_a_out': 'new_m', 'new_m_conv_b_w': 'new_m', 'new_m_w_b_out': 'new_m', 'new_m_g_mem': 'new_m', 'new_m_w_kv': 'new_m', 'new_m_w_x_out': 'new_m', 'new_m_w_o': 'new_m', 'new_m_g_mix_post': 'new_m', 'new_m_g_mlp_pre': 'new_m', 'new_m_w_up': 'new_m', 'new_m_w_down': 'new_m', 'new_m_g_mlp_post': 'new_m', 'new_v_g_mix_pre': 'new_v', 'new_v_w_in': 'new_v', 'new_v_conv_a_w': 'new_v', 'new_v_conv_a_b': 'new_v', 'new_v_ln_a_g': 'new_v', 'new_v_ln_a_b': 'new_v', 'new_v_w_a_out': 'new_v', 'new_v_conv_b_w': 'new_v', 'new_v_w_b_out': 'new_v', 'new_v_g_mem': 'new_v', 'new_v_w_kv': 'new_v', 'new_v_w_x_out': 'new_v', 'new_v_w_o': 'new_v', 'new_v_g_mix_post': 'new_v', 'new_v_g_mlp_pre': 'new_v', 'new_v_w_up': 'new_v', 'new_v_w_down': 'new_v', 'new_v_g_mlp_post': 'new_v'}


def _forward(args):
    return _fwd_reference(*[args[k] for k in FWD_PARAMS])


def _output_shape():
    def fwd():
        inp = _fwd_setup_inputs(0)
        return _fwd_reference(*[inp[k] for k in FWD_PARAMS])
    out = _jax.eval_shape(fwd)
    return out.shape, out.dtype

N_MICROBATCH = 1
ADAM_LR = 0.001
ADAM_B1 = 0.9
ADAM_B2 = 0.999
ADAM_EPS = 1e-08
ADAM_WD = 0.01
ADAM_STEP = 10
PER_EXAMPLE_BATCH_AXIS = {'x': 0, 'mem': 0, 'loss_target': 0}
SHARED_INPUTS = []
_WEIGHT_DTYPES = {'g_mix_pre': _jnp.float32, 'w_in': _jnp.float32, 'conv_a_w': _jnp.float32, 'conv_a_b': _jnp.float32, 'ln_a_g': _jnp.float32, 'ln_a_b': _jnp.float32, 'w_a_out': _jnp.float32, 'conv_b_w': _jnp.float32, 'w_b_out': _jnp.float32, 'g_mem': _jnp.float32, 'w_kv': _jnp.float32, 'w_x_out': _jnp.float32, 'w_o': _jnp.float32, 'g_mix_post': _jnp.float32, 'g_mlp_pre': _jnp.float32, 'w_up': _jnp.float32, 'w_down': _jnp.float32, 'g_mlp_post': _jnp.float32}
MOMENT_SCALE = {'g_mix_pre': 2.497460e+00, 'w_in': 1.027634e+00, 'conv_a_w': 4.040682e+00, 'conv_a_b': 3.204411e+01, 'ln_a_g': 1.349420e+01, 'ln_a_b': 1.900791e+01, 'w_a_out': 6.200406e+00, 'conv_b_w': 1.117657e+00, 'w_b_out': 8.766828e-01, 'g_mem': 5.623034e-01, 'w_kv': 5.499446e-01, 'w_x_out': 5.526396e-01, 'w_o': 5.703554e+00, 'g_mix_post': 3.241887e+01, 'g_mlp_pre': 4.434452e+00, 'w_up': 2.141815e+00, 'w_down': 1.293620e+01, 'g_mlp_post': 3.557441e+01}


def _to_microbatches(a, axis):
    t = _jnp.moveaxis(a, axis, 0)
    t = t.reshape((N_MICROBATCH, t.shape[0] // N_MICROBATCH) + t.shape[1:])
    return _jnp.moveaxis(t, 1, axis + 1)


def setup_inputs(seed: int = 0) -> dict:
    inp = _fwd_setup_inputs(seed)
    key = _jax.random.fold_in(_jax.random.key(seed), 7919)
    shape, _ = _output_shape()
    out = dict(inp)
    out["loss_target"] = _jax.random.normal(_jax.random.fold_in(key, 0), shape, _jnp.float32)
    for i, name in enumerate(TWIN_WEIGHTS):
        w = inp[name].astype(_jnp.float32)
        if MOMENT_SCALE is None:
            s = _jnp.sqrt(_jnp.mean(_jnp.square(w)) + 1e-30)
        else:
            s = MOMENT_SCALE[name]
        km, kv = _jax.random.split(_jax.random.fold_in(key, i + 1))
        out[name] = w
        out["m_" + name] = s * _jax.random.normal(km, w.shape, _jnp.float32)
        out["v_" + name] = (s * s) * _jax.random.uniform(kv, w.shape, _jnp.float32, 0.5, 1.5)
    if N_MICROBATCH > 1:
        for name, axis in PER_EXAMPLE_BATCH_AXIS.items():
            out[name] = _to_microbatches(out[name], axis)
    return {'x': out['x'], 'mem': out['mem'], 'g_mix_pre': out['g_mix_pre'], 'w_in': out['w_in'], 'conv_a_w': out['conv_a_w'], 'conv_a_b': out['conv_a_b'], 'ln_a_g': out['ln_a_g'], 'ln_a_b': out['ln_a_b'], 'w_a_out': out['w_a_out'], 'conv_b_w': out['conv_b_w'], 'w_b_out': out['w_b_out'], 'g_mem': out['g_mem'], 'w_kv': out['w_kv'], 'w_x_out': out['w_x_out'], 'w_o': out['w_o'], 'g_mix_post': out['g_mix_post'], 'g_mlp_pre': out['g_mlp_pre'], 'w_up': out['w_up'], 'w_down': out['w_down'], 'g_mlp_post': out['g_mlp_post'], 'loss_target': out['loss_target'], 'm_g_mix_pre': out['m_g_mix_pre'], 'm_w_in': out['m_w_in'], 'm_conv_a_w': out['m_conv_a_w'], 'm_conv_a_b': out['m_conv_a_b'], 'm_ln_a_g': out['m_ln_a_g'], 'm_ln_a_b': out['m_ln_a_b'], 'm_w_a_out': out['m_w_a_out'], 'm_conv_b_w': out['m_conv_b_w'], 'm_w_b_out': out['m_w_b_out'], 'm_g_mem': out['m_g_mem'], 'm_w_kv': out['m_w_kv'], 'm_w_x_out': out['m_w_x_out'], 'm_w_o': out['m_w_o'], 'm_g_mix_post': out['m_g_mix_post'], 'm_g_mlp_pre': out['m_g_mlp_pre'], 'm_w_up': out['m_w_up'], 'm_w_down': out['m_w_down'], 'm_g_mlp_post': out['m_g_mlp_post'], 'v_g_mix_pre': out['v_g_mix_pre'], 'v_w_in': out['v_w_in'], 'v_conv_a_w': out['v_conv_a_w'], 'v_conv_a_b': out['v_conv_a_b'], 'v_ln_a_g': out['v_ln_a_g'], 'v_ln_a_b': out['v_ln_a_b'], 'v_w_a_out': out['v_w_a_out'], 'v_conv_b_w': out['v_conv_b_w'], 'v_w_b_out': out['v_w_b_out'], 'v_g_mem': out['v_g_mem'], 'v_w_kv': out['v_w_kv'], 'v_w_x_out': out['v_w_x_out'], 'v_w_o': out['v_w_o'], 'v_g_mix_post': out['v_g_mix_post'], 'v_g_mlp_pre': out['v_g_mlp_pre'], 'v_w_up': out['v_w_up'], 'v_w_down': out['v_w_down'], 'v_g_mlp_post': out['v_g_mlp_post']}


def _loss(weights, diff, rest, loss_target):
    with _jax.named_scope("forward"):
        args = {**rest, TWIN_DIFF_INPUT: diff, **{k: w.astype(_WEIGHT_DTYPES[k]) for k, w in weights.items()}}
        y = _forward(args)
    with _jax.named_scope("loss_head"):
        err = _jnp.square(y.astype(_jnp.float32) - loss_target)
        return 0.5 * _jnp.sum(_jnp.mean(err, axis=-1)) if err.ndim else 0.5 * err


def _adamw(w, g, m, v):
    m = ADAM_B1 * m + (1.0 - ADAM_B1) * g
    v = ADAM_B2 * v + (1.0 - ADAM_B2) * _jnp.square(g)
    m_hat = m / (1.0 - ADAM_B1 ** ADAM_STEP)
    v_hat = v / (1.0 - ADAM_B2 ** ADAM_STEP)
    delta = -ADAM_LR * (m_hat / (_jnp.sqrt(v_hat) + ADAM_EPS) + ADAM_WD * w)
    return delta, m, v


def reference(x, mem, g_mix_pre, w_in, conv_a_w, conv_a_b, ln_a_g, ln_a_b, w_a_out, conv_b_w, w_b_out, g_mem, w_kv, w_x_out, w_o, g_mix_post, g_mlp_pre, w_up, w_down, g_mlp_post, loss_target, m_g_mix_pre, m_w_in, m_conv_a_w, m_conv_a_b, m_ln_a_g, m_ln_a_b, m_w_a_out, m_conv_b_w, m_w_b_out, m_g_mem, m_w_kv, m_w_x_out, m_w_o, m_g_mix_post, m_g_mlp_pre, m_w_up, m_w_down, m_g_mlp_post, v_g_mix_pre, v_w_in, v_conv_a_w, v_conv_a_b, v_ln_a_g, v_ln_a_b, v_w_a_out, v_conv_b_w, v_w_b_out, v_g_mem, v_w_kv, v_w_x_out, v_w_o, v_g_mix_post, v_g_mlp_pre, v_w_up, v_w_down, v_g_mlp_post):
    given = dict(x=x, mem=mem, g_mix_pre=g_mix_pre, w_in=w_in, conv_a_w=conv_a_w, conv_a_b=conv_a_b, ln_a_g=ln_a_g, ln_a_b=ln_a_b, w_a_out=w_a_out, conv_b_w=conv_b_w, w_b_out=w_b_out, g_mem=g_mem, w_kv=w_kv, w_x_out=w_x_out, w_o=w_o, g_mix_post=g_mix_post, g_mlp_pre=g_mlp_pre, w_up=w_up, w_down=w_down, g_mlp_post=g_mlp_post, loss_target=loss_target, m_g_mix_pre=m_g_mix_pre, m_w_in=m_w_in, m_conv_a_w=m_conv_a_w, m_conv_a_b=m_conv_a_b, m_ln_a_g=m_ln_a_g, m_ln_a_b=m_ln_a_b, m_w_a_out=m_w_a_out, m_conv_b_w=m_conv_b_w, m_w_b_out=m_w_b_out, m_g_mem=m_g_mem, m_w_kv=m_w_kv, m_w_x_out=m_w_x_out, m_w_o=m_w_o, m_g_mix_post=m_g_mix_post, m_g_mlp_pre=m_g_mlp_pre, m_w_up=m_w_up, m_w_down=m_w_down, m_g_mlp_post=m_g_mlp_post, v_g_mix_pre=v_g_mix_pre, v_w_in=v_w_in, v_conv_a_w=v_conv_a_w, v_conv_a_b=v_conv_a_b, v_ln_a_g=v_ln_a_g, v_ln_a_b=v_ln_a_b, v_w_a_out=v_w_a_out, v_conv_b_w=v_conv_b_w, v_w_b_out=v_w_b_out, v_g_mem=v_g_mem, v_w_kv=v_w_kv, v_w_x_out=v_w_x_out, v_w_o=v_w_o, v_g_mix_post=v_g_mix_post, v_g_mlp_pre=v_g_mlp_pre, v_w_up=v_w_up, v_w_down=v_w_down, v_g_mlp_post=v_g_mlp_post)
    weights = {n: given[n] for n in TWIN_WEIGHTS}
    shared = {n: given[n] for n in SHARED_INPUTS}
    per_example = {n: given[n] for n in ['x', 'mem']}
    grad_fn = _jax.value_and_grad(_loss, argnums=(0, 1))

    def one_microbatch(ex, loss_target):
        ex = dict(ex)
        diff = ex.pop(TWIN_DIFF_INPUT)
        return grad_fn(weights, diff, {**shared, **ex}, loss_target)

    if N_MICROBATCH == 1:
        loss, (grad_w, grad_x) = one_microbatch(per_example, given["loss_target"])
    else:
        def body(carry, xs):
            loss_sum, grad_sum = carry
            l_k, (gw_k, gx_k) = one_microbatch(xs[0], xs[1])
            with _jax.named_scope("update"):
                return (loss_sum + l_k, _jax.tree.map(_jnp.add, grad_sum, gw_k)), gx_k

        init = (_jnp.zeros((), _jnp.float32), _jax.tree.map(_jnp.zeros_like, weights))
        (loss, grad_w), grad_x = _jax.lax.scan(body, init, (per_example, given["loss_target"]))
    with _jax.named_scope("update"):
        delta_w, new_m, new_v = {}, {}, {}
        for n in TWIN_WEIGHTS:
            delta_w[n], new_m[n], new_v[n] = _adamw(weights[n], grad_w[n], given["m_" + n], given["v_" + n])
    return (loss, grad_x, *[grad_w[n] for n in TWIN_WEIGHTS], *[delta_w[n] for n in TWIN_WEIGHTS],
            *[new_m[n] for n in TWIN_WEIGHTS], *[new_v[n] for n in TWIN_WEIGHTS])
```

```python
import functools

import jax
import jax.numpy as jnp
from jax import lax
from jax.experimental import pallas as pl
from jax.experimental.pallas import tpu as pltpu

F32 = jnp.float32
BF16 = jnp.bfloat16
MESH = pl.DeviceIdType.MESH

NORM_EPS = 1e-6
N_HEADS = 4
ADAM_LR = 0.001
ADAM_B1 = 0.9
ADAM_B2 = 0.999
ADAM_EPS = 1e-08
ADAM_WD = 0.01
ADAM_STEP = 10

N_CHIPS = 4
V7X_VMEM_BYTES = 64 * 1024 * 1024
VMEM_CAP = V7X_VMEM_BYTES - 8 * 1024 * 1024
LANE = 128
SUBLANE_BF16 = 16
HALO = 32
ROW_TILE = 256
ATTN_TILE = 512
MM_TM = 1024
MM_TN = 1024
MM_TK = 2048
EW_VMEM_BYTES = 24 * 1024 * 1024


def _tile(n, pref, align):
    if n <= pref:
        return n
    t = (pref // align) * align
    while t >= align:
        if n % t == 0:
            return t
        t -= align
    return n


def _ew_rows(r, c, n_arrays):
    return _tile(r, max(SUBLANE_BF16, EW_VMEM_BYTES // (2 * n_arrays * c * 4)), SUBLANE_BF16)


def _nbytes(shape, dtype):
    n = 1
    for s in shape:
        if s is not None:
            n *= s
    return n * jnp.dtype(dtype).itemsize


def _params(semantics, block_bytes, temp_bytes=0):
    need = 2 * block_bytes + temp_bytes + (4 << 20)
    return pltpu.CompilerParams(dimension_semantics=semantics,
                                vmem_limit_bytes=int(min(max(need, 16 << 20), VMEM_CAP)))


def _sigmoid(v):
    return 1.0 / (1.0 + jnp.exp(-v))


def _mm_nn(a, b3, out_dtype, name, a_act=None):
    m, k = a.shape
    s, k2, ns = b3.shape
    assert k == k2
    tm = _tile(m, MM_TM, SUBLANE_BF16)
    tn = _tile(ns, MM_TN, LANE)
    tk = _tile(k, MM_TK, LANE)
    q = ns // tn
    nk = k // tk

    def body(a_ref, b_ref, o_ref, *scratch):
        av = a_ref[...]
        if a_act == "relu2":
            r = jnp.maximum(av.astype(F32), 0.0)
            av = r * r
        p = jnp.dot(av.astype(BF16), b_ref[...].astype(BF16), preferred_element_type=F32)
        if nk == 1:
            o_ref[...] = p.astype(o_ref.dtype)
        else:
            acc, = scratch
            kk = pl.program_id(2)

            @pl.when(kk == 0)
            def _():
                acc[...] = p

            @pl.when(kk > 0)
            def _():
                acc[...] += p

            @pl.when(kk == nk - 1)
            def _():
                o_ref[...] = acc[...].astype(o_ref.dtype)

    blocks = (_nbytes((tm, tk), a.dtype) + _nbytes((tk, tn), b3.dtype) + _nbytes((tm, tn), out_dtype))
    return pl.pallas_call(
        body, name=name,
        out_shape=jax.ShapeDtypeStruct((m, s * ns), out_dtype),
        grid=(m // tm, s * q, nk),
        in_specs=[pl.BlockSpec((tm, tk), lambda i, j, c: (i, c)),
                  pl.BlockSpec((None, tk, tn), lambda i, j, c: (j // q, c, j % q))],
        out_specs=pl.BlockSpec((tm, tn), lambda i, j, c: (i, j)),
        scratch_shapes=[pltpu.VMEM((tm, tn), F32)] if nk > 1 else [],
        compiler_params=_params(("parallel", "parallel", "arbitrary"), blocks,
                                3 * _nbytes((tm, tn), F32) + _nbytes((tm, tk), F32)),
    )(a, b3)


def _mm_nt(a, b3, out_dtype, name, relu2_of=None):
    m, n = a.shape
    s, kd, ns = b3.shape
    assert n == s * ns
    tm = _tile(m, MM_TM, SUBLANE_BF16)
    tj = _tile(kd, MM_TN, LANE)
    tc = _tile(ns, MM_TK, LANE)
    q = ns // tc
    nc = s * q

    def body(*refs):
        if relu2_of is None:
            a_ref, b_ref, o_ref = refs[:3]
            scratch = refs[3:]
        else:
            a_ref, b_ref, u_ref, o_ref = refs[:4]
            scratch = refs[4:]

        def finish(p):
            if relu2_of is not None:
                p = p * (2.0 * jnp.maximum(u_ref[...].astype(F32), 0.0))
            o_ref[...] = p.astype(o_ref.dtype)

        p = lax.dot_general(a_ref[...].astype(BF16), b_ref[...].astype(BF16),
                            (((1,), (1,)), ((), ())), preferred_element_type=F32)
        if nc == 1:
            finish(p)
        else:
            acc, = scratch
            cc = pl.program_id(2)

            @pl.when(cc == 0)
            def _():
                acc[...] = p

            @pl.when(cc > 0)
            def _():
                acc[...] += p

            @pl.when(cc == nc - 1)
            def _():
                finish(acc[...])

    in_specs = [pl.BlockSpec((tm, tc), lambda i, j, c: (i, c)),
                pl.BlockSpec((None, tj, tc), lambda i, j, c: (c // q, j, c % q))]
    operands = [a, b3]
    blocks = _nbytes((tm, tc), a.dtype) + _nbytes((tj, tc), b3.dtype) + _nbytes((tm, tj), out_dtype)
    if relu2_of is not None:
        in_specs.append(pl.BlockSpec((tm, tj), lambda i, j, c: (i, j)))
        operands.append(relu2_of)
        blocks += _nbytes((tm, tj), relu2_of.dtype)
    return pl.pallas_call(
        body, name=name,
        out_shape=jax.ShapeDtypeStruct((m, kd), out_dtype),
        grid=(m // tm, kd // tj, nc),
        in_specs=in_specs,
        out_specs=pl.BlockSpec((tm, tj), lambda i, j, c: (i, j)),
        scratch_shapes=[pltpu.VMEM((tm, tj), F32)] if nc > 1 else [],
        compiler_params=_params(("parallel", "parallel", "arbitrary"), blocks,
                                3 * _nbytes((tm, tj), F32)),
    )(*operands)


def _mm_tn(a, g, out_shards, name, a_act=None):
    m, ka = a.shape
    m2, n = g.shape
    assert m == m2
    ns = n // out_shards
    ta = _tile(ka, MM_TM, LANE)
    tn = _tile(ns, MM_TN, LANE)
    tm = _tile(m, MM_TK, SUBLANE_BF16)
    q = ns // tn
    nm = m // tm

    def body(a_ref, g_ref, o_ref, *scratch):
        av = a_ref[...]
        if a_act == "relu2":
            r = jnp.maximum(av.astype(F32), 0.0)
            av = r * r
        p = lax.dot_general(av.astype(BF16), g_ref[...].astype(BF16),
                            (((0,), (0,)), ((), ())), preferred_element_type=F32)
        if nm == 1:
            o_ref[...] = p.astype(o_ref.dtype)
        else:
            acc, = scratch
            cc = pl.program_id(2)

            @pl.when(cc == 0)
            def _():
                acc[...] = p

            @pl.when(cc > 0)
            def _():
                acc[...] += p

            @pl.when(cc == nm - 1)
            def _():
                o_ref[...] = acc[...].astype(o_ref.dtype)

    blocks = _nbytes((tm, ta), a.dtype) + _nbytes((tm, tn), g.dtype) + _nbytes((ta, tn), BF16)
    return pl.pallas_call(
        body, name=name,
        out_shape=jax.ShapeDtypeStruct((out_shards, ka, ns), BF16),
        grid=(ka // ta, out_shards * q, nm),
        in_specs=[pl.BlockSpec((tm, ta), lambda i, j, c: (c, i)),
                  pl.BlockSpec((tm, tn), lambda i, j, c: (c, j))],
        out_specs=pl.BlockSpec((None, ta, tn), lambda i, j, c: (j // q, i, j % q)),
        scratch_shapes=[pltpu.VMEM((ta, tn), F32)] if nm > 1 else [],
        compiler_params=_params(("parallel", "parallel", "arbitrary"), blocks,
                                3 * _nbytes((ta, tn), F32) + _nbytes((tm, ta), F32)),
    )(a, g)


def _rms_scale(v):
    return lax.rsqrt(jnp.mean(v * v, axis=-1, keepdims=True) + NORM_EPS)


def _norm_fwd(x, name, *, z=None, g_post=None, g_next=None, target=None):
    t, d = x.shape
    tr = _tile(t, ROW_TILE, SUBLANE_BF16)
    has_res, has_next, has_loss = z is not None, g_next is not None, target is not None

    def body(*refs):
        it = iter(refs)
        x_ref = next(it)
        z_ref, gp_ref = (next(it), next(it)) if has_res else (None, None)
        gn_ref = next(it) if has_next else None
        t_ref = next(it) if has_loss else None
        xv = x_ref[...]
        if has_res:
            zv = z_ref[...]
            xv = xv + zv * _rms_scale(zv) * gp_ref[...]
            if not has_loss:
                next(it)[...] = xv
        if has_next:
            next(it)[...] = (xv * _rms_scale(xv) * gn_ref[...]).astype(BF16)
        if has_loss:
            e = xv - t_ref[...]
            next(it)[...] = e * (1.0 / d)
            ls_ref = next(it)

            @pl.when(pl.program_id(0) == 0)
            def _():
                ls_ref[...] = jnp.zeros_like(ls_ref)

            ls_ref[...] += jnp.sum(e * e, axis=0, keepdims=True)

    row = pl.BlockSpec((tr, d), lambda i: (i, 0))
    vec = pl.BlockSpec((1, d), lambda i: (0, 0))
    operands, in_specs, out_shape, out_specs = [x], [row], [], []
    if has_res:
        operands += [z, g_post.reshape(1, d)]
        in_specs += [row, vec]
        if not has_loss:
            out_shape.append(jax.ShapeDtypeStruct((t, d), F32))
            out_specs.append(row)
    if has_next:
        operands.append(g_next.reshape(1, d))
        in_specs.append(vec)
        out_shape.append(jax.ShapeDtypeStruct((t, d), BF16))
        out_specs.append(row)
    if has_loss:
        operands.append(target)
        in_specs.append(row)
        out_shape += [jax.ShapeDtypeStruct((t, d), F32), jax.ShapeDtypeStruct((1, d), F32)]
        out_specs += [row, vec]
    return pl.pallas_call(
        body, name=name, out_shape=out_shape, grid=(t // tr,),
        in_specs=in_specs, out_specs=out_specs,
        compiler_params=_params(("arbitrary",), 5 * _nbytes((tr, d), F32), 4 * _nbytes((tr, d), F32)),
    )(*operands)


def _norm_bwd(name, *, dxo=None, pre=None, post=None, want_dx=True):
    ref_arr = dxo if dxo is not None else pre[1]
    t, d = ref_arr.shape
    tr = _tile(t, ROW_TILE, SUBLANE_BF16)
    has_dxo, has_pre, has_post = dxo is not None, pre is not None, post is not None

    def body(*refs):
        it = iter(refs)
        dxo_ref = next(it) if has_dxo else None
        dh_ref, xin_ref, gpre_ref = (next(it), next(it), next(it)) if has_pre else (None,) * 3
        z_ref, gpost_ref = (next(it), next(it)) if has_post else (None, None)
        dx_ref = next(it) if (has_pre and want_dx) else None
        dz_ref = next(it) if has_post else None
        dgpre_ref = next(it) if has_pre else None
        dgpost_ref = next(it) if has_post else None
        first = pl.program_id(0) == 0

        dx = dxo_ref[...] if has_dxo else None
        if has_pre:
            xin = xin_ref[...]
            dh = dh_ref[...].astype(F32)
            r = _rms_scale(xin)
            gy = dh * gpre_ref[...]
            dloc = r * gy - xin * (r * r * r) * jnp.mean(gy * xin, axis=-1, keepdims=True)
            dx = dloc if dx is None else dx + dloc
            if want_dx:
                dx_ref[...] = dx

            @pl.when(first)
            def _():
                dgpre_ref[...] = jnp.zeros_like(dgpre_ref)

            dgpre_ref[...] += jnp.sum(dh * xin * r, axis=0, keepdims=True)
        if has_post:
            zv = z_ref[...]
            r = _rms_scale(zv)
            gy = dx * gpost_ref[...]
            dz = r * gy - zv * (r * r * r) * jnp.mean(gy * zv, axis=-1, keepdims=True)
            dz_ref[...] = dz.astype(BF16)

            @pl.when(first)
            def _():
                dgpost_ref[...] = jnp.zeros_like(dgpost_ref)

            dgpost_ref[...] += jnp.sum(dx * zv * r, axis=0, keepdims=True)

    row = pl.BlockSpec((tr, d), lambda i: (i, 0))
    vec = pl.BlockSpec((1, d), lambda i: (0, 0))
    operands, in_specs, out_shape, out_specs = [], [], [], []
    if has_dxo:
        operands.append(dxo)
        in_specs.append(row)
    if has_pre:
        operands += [pre[0], pre[1], pre[2].reshape(1, d)]
        in_specs += [row, row, vec]
    if has_post:
        operands += [post[0], post[1].reshape(1, d)]
        in_specs += [row, vec]
    if has_pre and want_dx:
        out_shape.append(jax.ShapeDtypeStruct((t, d), F32))
        out_specs.append(row)
    if has_post:
        out_shape.append(jax.ShapeDtypeStruct((t, d), BF16))
        out_specs.append(row)
    if has_pre:
        out_shape.append(jax.ShapeDtypeStruct((1, d), F32))
        out_specs.append(vec)
    if has_post:
        out_shape.append(jax.ShapeDtypeStruct((1, d), F32))
        out_specs.append(vec)
    return pl.pallas_call(
        body, name=name, out_shape=out_shape, grid=(t // tr,),
        in_specs=in_specs, out_specs=out_specs,
        compiler_params=_params(("arbitrary",), 6 * _nbytes((tr, d), F32), 6 * _nbytes((tr, d), F32)),
    )(*operands)


def _seq_tiles(t):
    tr = _tile(t, ROW_TILE, HALO)
    assert tr % HALO == 0 and t % tr == 0
    return tr, t // tr, tr // HALO


def _col(tr, width, cb):
    return pl.BlockSpec((tr, width), lambda i: (i, cb))


def _prev_halo(per, width, cb):
    return pl.BlockSpec((HALO, width), lambda i: (jnp.maximum(i * per - 1, 0), cb))


def _next_halo(per, n_halo, width, cb):
    return pl.BlockSpec((HALO, width), lambda i: (jnp.minimum((i + 1) * per, n_halo - 1), cb))


def _const(shape):
    return pl.BlockSpec(shape, lambda i: (0,) * len(shape))


def _glu(val, gate):
    return val.astype(F32) * _sigmoid(gate.astype(F32))


def _layer_norm_parts(ca):
    mu = jnp.mean(ca, axis=-1, keepdims=True)
    xc = ca - mu
    rs = lax.rsqrt(jnp.mean(xc * xc, axis=-1, keepdims=True) + NORM_EPS)
    return xc * rs, rs


def _branch_a_fwd(proj, cw, cb, lg, lb, name):
    t = proj.shape[0]
    kw, c = cw.shape
    tr, nt, per = _seq_tiles(t)

    def body(av_ref, ag_ref, hv_ref, hg_ref, cw_ref, cb_ref, lg_ref, lb_ref, ca_ref, act_ref, abuf):
        i = pl.program_id(0)
        abuf[pl.ds(0, HALO), :] = jnp.where(i > 0, _glu(hv_ref[...], hg_ref[...]), 0.0)
        abuf[pl.ds(HALO, tr), :] = _glu(av_ref[...], ag_ref[...])
        acc = jnp.zeros((tr, c), F32)
        for k in range(kw):
            acc = acc + cw_ref[pl.ds(k, 1), :] * abuf[pl.ds(HALO - (kw - 1) + k, tr), :]
        ca = acc + cb_ref[...]
        ca_ref[...] = ca
        xh, _ = _layer_norm_parts(ca)
        ln = xh * lg_ref[...] + lb_ref[...]
        act_ref[...] = (ln * _sigmoid(ln)).astype(BF16)

    return pl.pallas_call(
        body, name=name,
        out_shape=[jax.ShapeDtypeStruct((t, c), F32), jax.ShapeDtypeStruct((t, c), BF16)],
        grid=(nt,),
        in_specs=[_col(tr, c, 0), _col(tr, c, 1), _prev_halo(per, c, 0), _prev_halo(per, c, 1),
                  _const((kw, c)), _const((1, c)), _const((1, c)), _const((1, c))],
        out_specs=[_col(tr, c, 0), _col(tr, c, 0)],
        scratch_shapes=[pltpu.VMEM((HALO + tr, c), F32)],
        compiler_params=_params(("arbitrary",), 4 * _nbytes((tr, c), F32), 8 * _nbytes((tr + HALO, c), F32)),
    )(proj, proj, proj, proj, cw, cb.reshape(1, c), lg.reshape(1, c), lb.reshape(1, c))


def _branch_a_bwd(dproj, dact, ca, proj, cw, lg, lb, name):
    t = proj.shape[0]
    kw, c = cw.shape
    tr, nt, per = _seq_tiles(t)
    n_halo = t // HALO

    def body(dproj_in, da_ref, dah_ref, ca_ref, cah_ref, av_ref, ag_ref, hv_ref, hg_ref,
             cw_ref, lg_ref, lb_ref, out_ref, dlg_ref, dlb_ref, dcb_ref, dcw_ref, abuf, dbuf):
        del dproj_in
        i = pl.program_id(0)
        lgv, lbv = lg_ref[...], lb_ref[...]

        def conv_grad(dact_v, ca_v):
            xh, rs = _layer_norm_parts(ca_v)
            ln = xh * lgv + lbv
            sg = _sigmoid(ln)
            dln = dact_v.astype(F32) * (sg * (1.0 + ln * (1.0 - sg)))
            dxh = dln * lgv
            dca = rs * (dxh - jnp.mean(dxh, axis=-1, keepdims=True)
                        - xh * jnp.mean(dxh * xh, axis=-1, keepdims=True))
            return dca, dln, xh

        dca, dln, xh = conv_grad(da_ref[...], ca_ref[...])
        dca_h, _, _ = conv_grad(dah_ref[...], cah_ref[...])
        dbuf[pl.ds(0, tr), :] = dca
        dbuf[pl.ds(tr, HALO), :] = jnp.where(i < nt - 1, dca_h, 0.0)

        @pl.when(i == 0)
        def _():
            dlg_ref[...] = jnp.zeros_like(dlg_ref)
            dlb_ref[...] = jnp.zeros_like(dlb_ref)
            dcb_ref[...] = jnp.zeros_like(dcb_ref)
            dcw_ref[...] = jnp.zeros_like(dcw_ref)

        dlg_ref[...] += jnp.sum(dln * xh, axis=0, keepdims=True)
        dlb_ref[...] += jnp.sum(dln, axis=0, keepdims=True)
        dcb_ref[...] += jnp.sum(dca, axis=0, keepdims=True)

        av = av_ref[...].astype(F32)
        sg = _sigmoid(ag_ref[...].astype(F32))
        abuf[pl.ds(0, HALO), :] = jnp.where(i > 0, _glu(hv_ref[...], hg_ref[...]), 0.0)
        abuf[pl.ds(HALO, tr), :] = av * sg

        d_a = jnp.zeros((tr, c), F32)
        for k in range(kw):
            d_a = d_a + cw_ref[pl.ds(k, 1), :] * dbuf[pl.ds(kw - 1 - k, tr), :]
            dcw_ref[pl.ds(k, 1), :] += jnp.sum(dca * abuf[pl.ds(HALO - (kw - 1) + k, tr), :],
                                               axis=0, keepdims=True)
        out_ref[:, pl.ds(0, c)] = (d_a * sg).astype(BF16)
        out_ref[:, pl.ds(c, c)] = (d_a * av * sg * (1.0 - sg)).astype(BF16)

    vec = _const((1, c))
    return pl.pallas_call(
        body, name=name,
        out_shape=[jax.ShapeDtypeStruct(dproj.shape, BF16)] + [jax.ShapeDtypeStruct((1, c), F32)] * 3
        + [jax.ShapeDtypeStruct((kw, c), F32)],
        grid=(nt,),
        in_specs=[pl.BlockSpec(memory_space=pl.ANY),
                  _col(tr, c, 0), _next_halo(per, n_halo, c, 0),
                  _col(tr, c, 0), _next_halo(per, n_halo, c, 0),
                  _col(tr, c, 0), _col(tr, c, 1), _prev_halo(per, c, 0), _prev_halo(per, c, 1),
                  _const((kw, c)), vec, vec],
        out_specs=[pl.BlockSpec((tr, 2 * c), lambda i: (i, 0)), vec, vec, vec, _const((kw, c))],
        scratch_shapes=[pltpu.VMEM((HALO + tr, c), F32), pltpu.VMEM((HALO + tr, c), F32)],
        input_output_aliases={0: 0},
        compiler_params=_params(("arbitrary",), 6 * _nbytes((tr, c), F32), 12 * _nbytes((tr + HALO, c), F32)),
    )(dproj, dact, dact, ca, ca, proj, proj, proj, proj, cw, lg.reshape(1, c), lb.reshape(1, c))


def _branch_b_fwd(proj, cw, name):
    t = proj.shape[0]
    kw, c = cw.shape
    tr, nt, per = _seq_tiles(t)

    def body(sb_ref, sc_ref, sx_ref, hc_ref, hx_ref, cw_ref, o_ref, pbuf):
        i = pl.program_id(0)
        hp = hc_ref[...].astype(F32) * hx_ref[...].astype(F32)
        pbuf[pl.ds(0, HALO), :] = jnp.where(i > 0, hp, 0.0)
        pbuf[pl.ds(HALO, tr), :] = sc_ref[...].astype(F32) * sx_ref[...].astype(F32)
        u = jnp.zeros((tr, c), F32)
        for k in range(kw):
            u = u + cw_ref[pl.ds(k, 1), :] * pbuf[pl.ds(HALO - (kw - 1) + k, tr), :]
        o_ref[...] = (sb_ref[...].astype(F32) * u).astype(BF16)

    return pl.pallas_call(
        body, name=name,
        out_shape=jax.ShapeDtypeStruct((t, c), BF16),
        grid=(nt,),
        in_specs=[_col(tr, c, 2), _col(tr, c, 3), _col(tr, c, 4),
                  _prev_halo(per, c, 3), _prev_halo(per, c, 4), _const((kw, c))],
        out_specs=_col(tr, c, 0),
        scratch_shapes=[pltpu.VMEM((HALO + tr, c), F32)],
        compiler_params=_params(("arbitrary",), 4 * _nbytes((tr, c), F32), 6 * _nbytes((tr + HALO, c), F32)),
    )(proj, proj, proj, proj, proj, cw)


def _branch_b_bwd(dproj, dbin, proj, cw, name):
    t = proj.shape[0]
    kw, c = cw.shape
    tr, nt, per = _seq_tiles(t)
    n_halo = t // HALO

    def body(dproj_in, db_ref, dbh_ref, sb_ref, sbh_ref, sc_ref, sx_ref, hc_ref, hx_ref, cw_ref,
             out_ref, dcw_ref, pbuf, dubuf):
        del dproj_in
        i = pl.program_id(0)
        j = pl.program_id(1)
        sb = sb_ref[...].astype(F32)
        sc = sc_ref[...].astype(F32)
        sx = sx_ref[...].astype(F32)
        dbin_v = db_ref[...].astype(F32)
        hp = hc_ref[...].astype(F32) * hx_ref[...].astype(F32)
        pbuf[pl.ds(0, HALO), :] = jnp.where(i > 0, hp, 0.0)
        pbuf[pl.ds(HALO, tr), :] = sc * sx
        du = dbin_v * sb
        du_h = dbh_ref[...].astype(F32) * sbh_ref[...].astype(F32)
        dubuf[pl.ds(0, tr), :] = du
        dubuf[pl.ds(tr, HALO), :] = jnp.where(i < nt - 1, du_h, 0.0)

        @pl.when((i == 0) & (j == 0))
        def _():
            dcw_ref[...] = jnp.zeros_like(dcw_ref)

        u = jnp.zeros((tr, c), F32)
        dp = jnp.zeros((tr, c), F32)
        for k in range(kw):
            shifted = pbuf[pl.ds(HALO - (kw - 1) + k, tr), :]
            u = u + cw_ref[pl.ds(k, 1), :] * shifted
            dp = dp + cw_ref[pl.ds(k, 1), :] * dubuf[pl.ds(kw - 1 - k, tr), :]

            @pl.when(j == 0)
            def _():
                dcw_ref[pl.ds(k, 1), :] += jnp.sum(du * shifted, axis=0, keepdims=True)

        res = jnp.where(j == 0, dbin_v * u, jnp.where(j == 1, dp * sx, dp * sc))
        out_ref[...] = res.astype(BF16)

    def colj(cb):
        return pl.BlockSpec((tr, c), lambda i, j: (i, cb))

    def prevj(cb):
        return pl.BlockSpec((HALO, c), lambda i, j: (jnp.maximum(i * per - 1, 0), cb))

    def nextj(cb):
        return pl.BlockSpec((HALO, c), lambda i, j: (jnp.minimum((i + 1) * per, n_halo - 1), cb))

    return pl.pallas_call(
        body, name=name,
        out_shape=[jax.ShapeDtypeStruct(dproj.shape, BF16), jax.ShapeDtypeStruct((kw, c), F32)],
        grid=(nt, 3),
        in_specs=[pl.BlockSpec(memory_space=pl.ANY),
                  colj(0), nextj(0), colj(2), nextj(2), colj(3), colj(4), prevj(3), prevj(4),
                  pl.BlockSpec((kw, c), lambda i, j: (0, 0))],
        out_specs=[pl.BlockSpec((tr, c), lambda i, j: (i, 2 + j)),
                   pl.BlockSpec((kw, c), lambda i, j: (0, 0))],
        scratch_shapes=[pltpu.VMEM((HALO + tr, c), F32), pltpu.VMEM((HALO + tr, c), F32)],
        input_output_aliases={0: 0},
        compiler_params=_params(("arbitrary", "arbitrary"), 6 * _nbytes((tr, c), F32),
                                10 * _nbytes((tr + HALO, c), F32)),
    )(dproj, dbin, dbin, proj, proj, proj, proj, proj, proj, cw)


def _softmax_rows(s):
    e = jnp.exp(s - jnp.max(s, axis=-1, keepdims=True))
    return e / jnp.sum(e, axis=-1, keepdims=True)


def _attn_fwd(proj, kv, name):
    t = proj.shape[0]
    m, c2 = kv.shape
    c = c2 // 2
    hd = c // N_HEADS
    ta = _tile(t, ATTN_TILE, SUBLANE_BF16)
    scale = hd ** -0.5

    def body(q_ref, kv_ref, o_ref):
        for h in range(N_HEADS):
            qh = q_ref[:, pl.ds(h * hd, hd)]
            kh = kv_ref[:, pl.ds(h * hd, hd)]
            vh = kv_ref[:, pl.ds(c + h * hd, hd)]
            s = lax.dot_general(qh, kh, (((1,), (1,)), ((), ())), preferred_element_type=F32) * scale
            p = _softmax_rows(s)
            o_ref[:, pl.ds(h * hd, hd)] = jnp.dot(p.astype(BF16), vh,
                                                  preferred_element_type=F32).astype(BF16)

    return pl.pallas_call(
        body, name=name,
        out_shape=jax.ShapeDtypeStruct((t, c), BF16),
        grid=(t // ta,),
        in_specs=[pl.BlockSpec((ta, c), lambda i: (i, 5)), _const((m, c2))],
        out_specs=pl.BlockSpec((ta, c), lambda i: (i, 0)),
        compiler_params=_params(("parallel",), 2 * _nbytes((ta, c), BF16) + _nbytes((m, c2), BF16),
                                8 * _nbytes((ta, m), F32)),
    )(proj, kv)


def _attn_bwd(dproj, d_o, proj, kv, name):
    t = proj.shape[0]
    m, c2 = kv.shape
    c = c2 // 2
    hd = c // N_HEADS
    ta = _tile(t, ATTN_TILE, SUBLANE_BF16)
    scale = hd ** -0.5

    def body(dproj_in, do_ref, q_ref, kv_ref, dq_ref, dkv_ref):
        del dproj_in

        @pl.when(pl.program_id(0) == 0)
        def _():
            dkv_ref[...] = jnp.zeros_like(dkv_ref)

        for h in range(N_HEADS):
            qh = q_ref[:, pl.ds(h * hd, hd)]
            kh = kv_ref[:, pl.ds(h * hd, hd)]
            vh = kv_ref[:, pl.ds(c + h * hd, hd)]
            doh = do_ref[:, pl.ds(h * hd, hd)]
            s = lax.dot_general(qh, kh, (((1,), (1,)), ((), ())), preferred_element_type=F32) * scale
            p = _softmax_rows(s)
            dp = lax.dot_general(doh, vh, (((1,), (1,)), ((), ())), preferred_element_type=F32)
            ds = (p * (dp - jnp.sum(dp * p, axis=-1, keepdims=True))).astype(BF16)
            dq_ref[:, pl.ds(h * hd, hd)] = (jnp.dot(ds, kh, preferred_element_type=F32) * scale).astype(BF16)
            dkv_ref[:, pl.ds(h * hd, hd)] += lax.dot_general(
                ds, qh, (((0,), (0,)), ((), ())), preferred_element_type=F32) * scale
            dkv_ref[:, pl.ds(c + h * hd, hd)] += lax.dot_general(
                p.astype(BF16), doh, (((0,), (0,)), ((), ())), preferred_element_type=F32)

    return pl.pallas_call(
        body, name=name,
        out_shape=[jax.ShapeDtypeStruct(dproj.shape, BF16), jax.ShapeDtypeStruct((m, c2), F32)],
        grid=(t // ta,),
        in_specs=[pl.BlockSpec(memory_space=pl.ANY),
                  pl.BlockSpec((ta, c), lambda i: (i, 0)), pl.BlockSpec((ta, c), lambda i: (i, 5)),
                  _const((m, c2))],
        out_specs=[pl.BlockSpec((ta, c), lambda i: (i, 5)), _const((m, c2))],
        input_output_aliases={0: 0},
        compiler_params=_params(("arbitrary",), 3 * _nbytes((ta, c), BF16) + 2 * _nbytes((m, c2), F32),
                                10 * _nbytes((ta, m), F32)),
    )(dproj, d_o, proj, kv)


def _merge_fwd(proj, ya, yb, yx, name):
    t, d = ya.shape
    tr = _tile(t, ROW_TILE, SUBLANE_BF16)

    def body(g_ref, ya_ref, yb_ref, yx_ref, o_ref):
        acc = jnp.zeros((tr, d), F32)
        for b, y_ref in enumerate((ya_ref, yb_ref, yx_ref)):
            acc = acc + _sigmoid(g_ref[:, pl.ds(b * d, d)].astype(F32)) * y_ref[...].astype(F32)
        o_ref[...] = acc.astype(BF16)

    row = pl.BlockSpec((tr, d), lambda i: (i, 0))
    return pl.pallas_call(
        body, name=name,
        out_shape=jax.ShapeDtypeStruct((t, d), BF16),
        grid=(t // tr,),
        in_specs=[pl.BlockSpec((tr, 3 * d), lambda i: (i, 1)), row, row, row],
        out_specs=row,
        compiler_params=_params(("parallel",), 7 * _nbytes((tr, d), BF16), 6 * _nbytes((tr, d), F32)),
    )(proj, ya, yb, yx)


def _merge_bwd(dmerged, proj, ya, yb, yx, name):
    t, d = ya.shape
    tr = _tile(t, ROW_TILE, SUBLANE_BF16)

    def body(dm_ref, g_ref, ya_ref, yb_ref, yx_ref, dg_ref, dya_ref, dyb_ref, dyx_ref):
        dm = dm_ref[...].astype(F32)
        for b, (y_ref, dy_ref) in enumerate(((ya_ref, dya_ref), (yb_ref, dyb_ref), (yx_ref, dyx_ref))):
            sg = _sigmoid(g_ref[:, pl.ds(b * d, d)].astype(F32))
            dg_ref[:, pl.ds(b * d, d)] = (dm * y_ref[...].astype(F32) * sg * (1.0 - sg)).astype(BF16)
            dy_ref[...] = (dm * sg).astype(BF16)

    row = pl.BlockSpec((tr, d), lambda i: (i, 0))
    gates = pl.BlockSpec((tr, 3 * d), lambda i: (i, 1))
    return pl.pallas_call(
        body, name=name,
        out_shape=[jax.ShapeDtypeStruct(proj.shape, BF16)] + [jax.ShapeDtypeStruct((t, d), BF16)] * 3,
        grid=(t // tr,),
        in_specs=[row, gates, row, row, row],
        out_specs=[gates, row, row, row],
        compiler_params=_params(("parallel",), 14 * _nbytes((tr, d), BF16), 8 * _nbytes((tr, d), F32)),
    )(dmerged, proj, ya, yb, yx)


def _as3(a):
    return a.reshape((-1,) + a.shape[-2:])


def _ew_call(fn, ins, n_out, out_dtypes, name):
    b, r, c = ins[0].shape
    tr = _ew_rows(r, c, len(ins) + n_out)

    def body(*refs):
        outs = fn(*[ref[...] for ref in refs[:len(ins)]])
        for ref, val in zip(refs[len(ins):], outs):
            ref[...] = val.astype(ref.dtype)

    spec = pl.BlockSpec((None, tr, c), lambda i, j: (i, j, 0))
    return pl.pallas_call(
        body, name=name,
        out_shape=[jax.ShapeDtypeStruct((b, r, c), dt) for dt in out_dtypes],
        grid=(b, r // tr),
        in_specs=[spec] * len(ins), out_specs=[spec] * n_out,
        compiler_params=_params(("parallel", "parallel"), (len(ins) + n_out) * _nbytes((tr, c), F32),
                                6 * _nbytes((tr, c), F32)),
    )(*ins)


def _cast_bf16(w, name):
    return _ew_call(lambda v: (v,), [w], 1, [BF16], name)[0]


def _add_bf16(a, b, name):
    return _ew_call(lambda u, v: (u.astype(F32) + v.astype(F32),), [a, b], 1, [BF16], name)[0]


def _adamw(w, g, m, v, name):
    shape = w.shape
    c1 = 1.0 - ADAM_B1 ** ADAM_STEP
    c2 = 1.0 - ADAM_B2 ** ADAM_STEP

    def fn(wv, gv, mv, vv):
        mn = ADAM_B1 * mv + (1.0 - ADAM_B1) * gv
        vn = ADAM_B2 * vv + (1.0 - ADAM_B2) * (gv * gv)
        delta = -ADAM_LR * ((mn / c1) / (jnp.sqrt(vn / c2) + ADAM_EPS) + ADAM_WD * wv)
        return delta, mn, vn

    outs = _ew_call(fn, [_as3(w), _as3(g), _as3(m), _as3(v)], 3, [F32] * 3, name)
    return [o.reshape(shape) for o in outs]


def _sum_leading(q, name):
    b, nj, r, c = q.shape
    tr = _ew_rows(r, c, nj + 1)

    def body(q_ref, o_ref):
        acc = q_ref[0].astype(F32)
        for j in range(1, nj):
            acc = acc + q_ref[j].astype(F32)
        o_ref[...] = acc

    return pl.pallas_call(
        body, name=name,
        out_shape=jax.ShapeDtypeStruct((b, r, c), F32),
        grid=(b, r // tr),
        in_specs=[pl.BlockSpec((None, nj, tr, c), lambda i, j: (i, 0, j, 0))],
        out_specs=pl.BlockSpec((None, tr, c), lambda i, j: (i, j, 0)),
        compiler_params=_params(("parallel", "parallel"), (nj + 1) * _nbytes((tr, c), F32),
                                2 * _nbytes((tr, c), F32)),
    )(q)


HBM_SPEC = pl.BlockSpec(memory_space=pl.ANY)


def _place():
    x, y, c = lax.axis_index("x"), lax.axis_index("y"), lax.axis_index("c")
    peers = [(1 - x, y), (x, 1 - y), (1 - x, 1 - y)]
    return x, y, c, 2 * x + y, peers


def _remote(src, dst, send_sem, recv_sem, dev):
    return pltpu.make_async_remote_copy(src_ref=src, dst_ref=dst, send_sem=send_sem, recv_sem=recv_sem,
                                        device_id=dev, device_id_type=MESH)


def _gather_weight(wb, name):
    nl, r, cs = wb.shape
    rh = r // 2
    n = 3 * nl

    def body(wb_ref, *rest):
        outs = rest[:nl]
        send, recv, fsend, frecv, lsem = rest[nl:]
        x, y, c, me, peers = _place()
        mine = pl.ds(c * rh, rh)
        other = pl.ds((1 - c) * rh, rh)
        local = [pltpu.make_async_copy(wb_ref.at[l], outs[l].at[me], lsem.at[l]) for l in range(nl)]
        for cp in local:
            cp.start()
        sends = []
        for l in range(nl):
            for j, (px, py) in enumerate(peers):
                k = 3 * l + j
                sends.append(_remote(wb_ref.at[l, mine, :], outs[l].at[me, mine, :],
                                     send.at[k], recv.at[k], (px, py, c)))
                sends[-1].start()
        forwards = []
        for l in range(nl):
            for j, (px, py) in enumerate(peers):
                k = 3 * l + j
                landed = outs[l].at[2 * px + py, mine, :]
                _remote(wb_ref.at[l, mine, :], landed, send.at[k], recv.at[k], (px, py, c)).wait_recv()
                forwards.append(_remote(landed, landed, fsend.at[k], frecv.at[k], (x, y, 1 - c)))
                forwards[-1].start()
        for l in range(nl):
            for j, (px, py) in enumerate(peers):
                k = 3 * l + j
                landed = outs[l].at[2 * px + py, other, :]
                _remote(landed, landed, fsend.at[k], frecv.at[k], (x, y, 1 - c)).wait_recv()
        for cp in sends + forwards:
            cp.wait_send()
        for cp in local:
            cp.wait()

    return pl.pallas_call(
        body, name=name,
        out_shape=[jax.ShapeDtypeStruct((N_CHIPS, r, cs), wb.dtype)] * nl,
        in_specs=[HBM_SPEC], out_specs=[HBM_SPEC] * nl,
        scratch_shapes=[pltpu.SemaphoreType.DMA((n,)), pltpu.SemaphoreType.DMA((n,)),
                        pltpu.SemaphoreType.DMA((n,)), pltpu.SemaphoreType.DMA((n,)),
                        pltpu.SemaphoreType.DMA((nl,))],
    )(wb)


def _gather_small(v, name):
    def body(v_ref, o_ref, send, recv, lsem):
        x, y, c, me, peers = _place()
        local = pltpu.make_async_copy(v_ref, o_ref.at[me], lsem)
        local.start()
        sends = [_remote(v_ref, o_ref.at[me], send.at[j], recv.at[j], (px, py, c))
                 for j, (px, py) in enumerate(peers)]
        for cp in sends:
            cp.start()
        for j, (px, py) in enumerate(peers):
            _remote(v_ref, o_ref.at[2 * px + py], send.at[j], recv.at[j], (px, py, c)).wait_recv()
        for cp in sends:
            cp.wait_send()
        local.wait()

    return pl.pallas_call(
        body, name=name,
        out_shape=jax.ShapeDtypeStruct((N_CHIPS,) + v.shape, v.dtype),
        in_specs=[HBM_SPEC], out_specs=HBM_SPEC,
        scratch_shapes=[pltpu.SemaphoreType.DMA((3,)), pltpu.SemaphoreType.DMA((3,)), pltpu.SemaphoreType.DMA],
    )(v)


def _gather_all(v, name):
    def body(v_ref, o_ref, send, recv, lsem):
        x, y, c = lax.axis_index("x"), lax.axis_index("y"), lax.axis_index("c")
        me = 4 * x + 2 * y + c
        local = pltpu.make_async_copy(v_ref, o_ref.at[me], lsem)
        local.start()
        flips = [(fx, fy, fc) for fx in (0, 1) for fy in (0, 1) for fc in (0, 1)][1:]
        peers = [(x ^ fx, y ^ fy, c ^ fc) for fx, fy, fc in flips]
        sends = [_remote(v_ref, o_ref.at[me], send.at[k], recv.at[k], dev) for k, dev in enumerate(peers)]
        for cp in sends:
            cp.start()
        for k, (px, py, pc) in enumerate(peers):
            _remote(v_ref, o_ref.at[4 * px + 2 * py + pc], send.at[k], recv.at[k], (px, py, pc)).wait_recv()
        for cp in sends:
            cp.wait_send()
        local.wait()

    return pl.pallas_call(
        body, name=name,
        out_shape=jax.ShapeDtypeStruct((8,) + v.shape, v.dtype),
        in_specs=[HBM_SPEC], out_specs=HBM_SPEC,
        scratch_shapes=[pltpu.SemaphoreType.DMA((7,)), pltpu.SemaphoreType.DMA((7,)), pltpu.SemaphoreType.DMA],
    )(v)


def _split_halves(parts, name):
    nl = len(parts)
    _, r, cs = parts[0].shape
    rh = r // 2

    def body(*refs):
        ins = refs[:nl]
        mine_ref, theirs_ref, send, recv, lsem = refs[nl:]
        x, y, c, _, _ = _place()
        keep = pl.ds(c * rh, rh)
        give = pl.ds((1 - c) * rh, rh)
        local = [pltpu.make_async_copy(ins[l].at[:, keep, :], mine_ref.at[l], lsem.at[l]) for l in range(nl)]
        sends = [_remote(ins[l].at[:, give, :], theirs_ref.at[l], send.at[l], recv.at[l], (x, y, 1 - c))
                 for l in range(nl)]
        for cp in local + sends:
            cp.start()
        for cp in sends:
            cp.wait()
        for cp in local:
            cp.wait()

    shape = jax.ShapeDtypeStruct((nl, N_CHIPS, rh, cs), parts[0].dtype)
    return pl.pallas_call(
        body, name=name,
        out_shape=[shape, shape],
        in_specs=[HBM_SPEC] * nl, out_specs=[HBM_SPEC, HBM_SPEC],
        scratch_shapes=[pltpu.SemaphoreType.DMA((nl,)), pltpu.SemaphoreType.DMA((nl,)),
                        pltpu.SemaphoreType.DMA((nl,))],
    )(*parts)


def _exchange_chips(chip_sum, name):
    nl = chip_sum.shape[0]

    def body(s_ref, q_ref, send, recv, lsem):
        x, y, c, me, peers = _place()
        local = pltpu.make_async_copy(s_ref.at[:, me], q_ref.at[:, me], lsem)
        local.start()
        sends = [_remote(s_ref.at[:, 2 * px + py], q_ref.at[:, me], send.at[j], recv.at[j], (px, py, c))
                 for j, (px, py) in enumerate(peers)]
        for cp in sends:
            cp.start()
        for j, (px, py) in enumerate(peers):
            _remote(s_ref.at[:, me], q_ref.at[:, 2 * px + py], send.at[j], recv.at[j], (px, py, c)).wait_recv()
        for cp in sends:
            cp.wait_send()
        local.wait()

    del nl
    return pl.pallas_call(
        body, name=name,
        out_shape=jax.ShapeDtypeStruct(chip_sum.shape, chip_sum.dtype),
        in_specs=[HBM_SPEC], out_specs=HBM_SPEC,
        scratch_shapes=[pltpu.SemaphoreType.DMA((3,)), pltpu.SemaphoreType.DMA((3,)), pltpu.SemaphoreType.DMA],
    )(chip_sum)


def _join_halves(red, name):
    nl, rh, cs = red.shape

    def body(r_ref, o_ref, send, recv, lsem):
        x, y, c, _, _ = _place()
        mine = pl.ds(c * rh, rh)
        other = pl.ds((1 - c) * rh, rh)
        local = pltpu.make_async_copy(r_ref, o_ref.at[:, mine, :], lsem)
        local.start()
        out = _remote(r_ref, o_ref.at[:, mine, :], send, recv, (x, y, 1 - c))
        out.start()
        _remote(r_ref, o_ref.at[:, other, :], send, recv, (x, y, 1 - c)).wait_recv()
        out.wait_send()
        local.wait()

    return pl.pallas_call(
        body, name=name,
        out_shape=jax.ShapeDtypeStruct((nl, 2 * rh, cs), red.dtype),
        in_specs=[HBM_SPEC], out_specs=HBM_SPEC,
        scratch_shapes=[pltpu.SemaphoreType.DMA, pltpu.SemaphoreType.DMA, pltpu.SemaphoreType.DMA],
    )(red)


def _reduce_scatter(parts, tag):
    mine, theirs = _split_halves(parts, "rs_split_" + tag)
    chip_sum = _add_bf16(_as3(mine), _as3(theirs), "rs_add_" + tag).reshape(mine.shape)
    q = _exchange_chips(chip_sum, "rs_xchg_" + tag)
    red = _sum_leading(q, "rs_sum_" + tag)
    return _join_halves(red, "rs_join_" + tag)


COL_SHARDED = ("w_in", "w_a_out", "w_b_out", "w_x_out", "w_up")
ROW_SHARDED = ("w_kv", "w_o", "w_down")
BIG = COL_SHARDED + ROW_SHARDED
SMALL_REPLICATED = ("g_mix_pre", "conv_a_b", "ln_a_g", "ln_a_b", "g_mem", "g_mix_post", "g_mlp_pre", "g_mlp_post")
SMALL_SHARDED = ("conv_a_w", "conv_b_w")
WEIGHT_ORDER = ("g_mix_pre", "w_in", "conv_a_w", "conv_a_b", "ln_a_g", "ln_a_b", "w_a_out", "conv_b_w", "w_b_out",
                "g_mem", "w_kv", "w_x_out", "w_o", "g_mix_post", "g_mlp_pre", "w_up", "w_down", "g_mlp_post")


def _pack_rows(arrays, width):
    rows = [a.reshape(-1, width) for a in arrays]
    total = sum(r.shape[0] for r in rows)
    pad = (-total) % 8
    if pad:
        rows.append(jnp.zeros((pad, width), rows[0].dtype))
    return jnp.concatenate(rows, axis=0)


def _unpack_rows(packed, like, width):
    out, at = [], 0
    for a in like:
        n = a.size // width
        out.append(packed[at:at + n].reshape(a.shape))
        at += n
    return out


def _step(w, m, v, x, mem, target):
    nl = w["w_in"].shape[0]
    d = x.shape[1]
    c = w["conv_a_b"].shape[1]
    ka = w["conv_a_w"].shape[1]

    full = {}
    for name in BIG:
        per_layer = _gather_weight(_cast_bf16(w[name], "cast_" + name), "gather_" + name)
        if name in ROW_SHARDED:
            per_layer = [p.reshape(1, -1, p.shape[-1]) for p in per_layer]
        full[name] = per_layer
    conv_pack = jnp.concatenate([w["conv_a_w"], w["conv_b_w"]], axis=1)
    conv_all = _gather_small(conv_pack, "gather_conv")
    conv_all = jnp.moveaxis(conv_all, 0, 2).reshape(nl, conv_pack.shape[1], c)
    cw_a, cw_b = conv_all[:, :ka], conv_all[:, ka:]

    saved = []
    xl = x
    (h,) = _norm_fwd(x, "norm_first", g_next=w["g_mix_pre"][0])
    dy = loss_cols = None
    for l in range(nl):
        s = {"x": xl, "h": h}
        proj = _mm_nn(h, full["w_in"][l], BF16, "mm_in")
        ca, act_a = _branch_a_fwd(proj, cw_a[l], w["conv_a_b"][l], w["ln_a_g"][l], w["ln_a_b"][l], "branch_a_fwd")
        b_in = _branch_b_fwd(proj, cw_b[l], "branch_b_fwd")
        (mem_n,) = _norm_fwd(mem, "norm_mem", g_next=w["g_mem"][l])
        kv = _mm_nn(mem_n, full["w_kv"][l], BF16, "mm_kv")
        att = _attn_fwd(proj, kv, "attn_fwd")
        ya = _mm_nn(act_a, full["w_a_out"][l], BF16, "mm_a_out")
        yb = _mm_nn(b_in, full["w_b_out"][l], BF16, "mm_b_out")
        yx = _mm_nn(att, full["w_x_out"][l], BF16, "mm_x_out")
        merged = _merge_fwd(proj, ya, yb, yx, "merge_fwd")
        z = _mm_nn(merged, full["w_o"][l], F32, "mm_o")
        x1, h2 = _norm_fwd(xl, "norm_mid", z=z, g_post=w["g_mix_post"][l], g_next=w["g_mlp_pre"][l])
        up = _mm_nn(h2, full["w_up"][l], BF16, "mm_up")
        f = _mm_nn(up, full["w_down"][l], F32, "mm_down", a_act="relu2")
        s.update(proj=proj, ca=ca, act_a=act_a, b_in=b_in, mem_n=mem_n, kv=kv, att=att, ya=ya, yb=yb, yx=yx,
                 merged=merged, z=z, x1=x1, h2=h2, up=up, f=f)
        saved.append(s)
        if l + 1 < nl:
            xl, h = _norm_fwd(x1, "norm_mid", z=f, g_post=w["g_mlp_post"][l], g_next=w["g_mix_pre"][l + 1])
        else:
            dy, loss_cols = _norm_fwd(x1, "norm_loss", z=f, g_post=w["g_mlp_post"][l], target=target)

    part = {name: [None] * nl for name in BIG}
    small = {name: [None] * nl for name in SMALL_REPLICATED + SMALL_SHARDED}
    dxo = dy
    d_f, small["g_mlp_post"][nl - 1] = _norm_bwd("norm_bwd_top", dxo=dy,
                                                 post=(saved[-1]["f"], w["g_mlp_post"][nl - 1]))
    grad_x = None
    for l in reversed(range(nl)):
        s = saved[l]
        d_up = _mm_nt(d_f, full["w_down"][l], BF16, "mm_down_dx", relu2_of=s["up"])
        part["w_down"][l] = _mm_tn(s["up"], d_f, 1, "mm_down_dw", a_act="relu2").reshape(N_CHIPS, -1, d)
        d_h2 = _mm_nt(d_up, full["w_up"][l], F32, "mm_up_dx")
        part["w_up"][l] = _mm_tn(s["h2"], d_up, N_CHIPS, "mm_up_dw")
        dx1, d_z, small["g_mlp_pre"][l], small["g_mix_post"][l] = _norm_bwd(
            "norm_bwd_mid", dxo=dxo, pre=(d_h2, s["x1"], w["g_mlp_pre"][l]), post=(s["z"], w["g_mix_post"][l]))
        d_merged = _mm_nt(d_z, full["w_o"][l], BF16, "mm_o_dx")
        part["w_o"][l] = _mm_tn(s["merged"], d_z, 1, "mm_o_dw").reshape(N_CHIPS, -1, d)
        dproj, d_ya, d_yb, d_yx = _merge_bwd(d_merged, s["proj"], s["ya"], s["yb"], s["yx"], "merge_bwd")
        d_act_a = _mm_nt(d_ya, full["w_a_out"][l], BF16, "mm_a_out_dx")
        part["w_a_out"][l] = _mm_tn(s["act_a"], d_ya, N_CHIPS, "mm_a_out_dw")
        d_b_in = _mm_nt(d_yb, full["w_b_out"][l], BF16, "mm_b_out_dx")
        part["w_b_out"][l] = _mm_tn(s["b_in"], d_yb, N_CHIPS, "mm_b_out_dw")
        d_att = _mm_nt(d_yx, full["w_x_out"][l], BF16, "mm_x_out_dx")
        part["w_x_out"][l] = _mm_tn(s["att"], d_yx, N_CHIPS, "mm_x_out_dw")
        dproj, small["ln_a_g"][l], small["ln_a_b"][l], small["conv_a_b"][l], small["conv_a_w"][l] = _branch_a_bwd(
            dproj, d_act_a, s["ca"], s["proj"], cw_a[l], w["ln_a_g"][l], w["ln_a_b"][l], "branch_a_bwd")
        dproj, small["conv_b_w"][l] = _branch_b_bwd(dproj, d_b_in, s["proj"], cw_b[l], "branch_b_bwd")
        dproj, d_kv = _attn_bwd(dproj, d_att, s["proj"], s["kv"], "attn_bwd")
        part["w_kv"][l] = _mm_tn(s["mem_n"], d_kv, 1, "mm_kv_dw").reshape(N_CHIPS, -1, 2 * c)
        d_mem_n = _mm_nt(d_kv, full["w_kv"][l], F32, "mm_kv_dx")
        (small["g_mem"][l],) = _norm_bwd("norm_bwd_mem", pre=(d_mem_n, mem, w["g_mem"][l]), want_dx=False)
        d_h = _mm_nt(dproj, full["w_in"][l], F32, "mm_in_dx")
        part["w_in"][l] = _mm_tn(s["h"], dproj, N_CHIPS, "mm_in_dw")
        if l > 0:
            dxo, d_f, small["g_mix_pre"][l], small["g_mlp_post"][l - 1] = _norm_bwd(
                "norm_bwd_mid", dxo=dx1, pre=(d_h, s["x"], w["g_mix_pre"][l]),
                post=(saved[l - 1]["f"], w["g_mlp_post"][l - 1]))
        else:
            grad_x, small["g_mix_pre"][0] = _norm_bwd("norm_bwd_last", dxo=dx1,
                                                      pre=(d_h, s["x"], w["g_mix_pre"][0]))

    grads, delta, new_m, new_v = {}, {}, {}, {}
    for name in BIG:
        grads[name] = _reduce_scatter(part[name], name)
        delta[name], new_m[name], new_v[name] = _adamw(w[name], grads[name], m[name], v[name], "adamw_" + name)

    small_names = SMALL_REPLICATED + SMALL_SHARDED
    stacked = [jnp.stack(small[n]) for n in small_names]
    packed = _pack_rows(stacked, c)
    total = _sum_leading(_gather_all(packed, "gather_small_grads")[None], "sum_small_grads")[0]
    reduced = dict(zip(small_names, _unpack_rows(total, stacked, c)))
    chip = 2 * lax.axis_index("x") + lax.axis_index("y")
    cs = c // N_CHIPS
    for name in SMALL_SHARDED:
        grads[name] = lax.dynamic_slice_in_dim(reduced[name], chip * cs, cs, axis=2)
    for name in SMALL_REPLICATED:
        grads[name] = reduced[name].reshape(w[name].shape)
    for group, width in ((SMALL_REPLICATED, c), (SMALL_SHARDED, cs)):
        packs = [_pack_rows([src[n] for n in group], width)[None] for src in (w, grads, m, v)]
        outs = _adamw(*packs, "adamw_small_%d" % width)
        for dst, out in zip((delta, new_m, new_v), outs):
            dst.update(zip(group, _unpack_rows(out[0], [w[n] for n in group], width)))

    loss = lax.psum(0.5 * jnp.sum(loss_cols) / d, ("x", "y", "c"))
    return loss, grad_x, grads, delta, new_m, new_v


def kernel(x, mem, g_mix_pre, w_in, conv_a_w, conv_a_b, ln_a_g, ln_a_b, w_a_out, conv_b_w, w_b_out, g_mem, w_kv, w_x_out, w_o, g_mix_post, g_mlp_pre, w_up, w_down, g_mlp_post, loss_target, m_g_mix_pre, m_w_in, m_conv_a_w, m_conv_a_b, m_ln_a_g, m_ln_a_b, m_w_a_out, m_conv_b_w, m_w_b_out, m_g_mem, m_w_kv, m_w_x_out, m_w_o, m_g_mix_post, m_g_mlp_pre, m_w_up, m_w_down, m_g_mlp_post, v_g_mix_pre, v_w_in, v_conv_a_w, v_conv_a_b, v_ln_a_g, v_ln_a_b, v_w_a_out, v_conv_b_w, v_w_b_out, v_g_mem, v_w_kv, v_w_x_out, v_w_o, v_g_mix_post, v_g_mlp_pre, v_w_up, v_w_down, v_g_mlp_post):
    w = dict(g_mix_pre=g_mix_pre, w_in=w_in, conv_a_w=conv_a_w, conv_a_b=conv_a_b, ln_a_g=ln_a_g, ln_a_b=ln_a_b,
             w_a_out=w_a_out, conv_b_w=conv_b_w, w_b_out=w_b_out, g_mem=g_mem, w_kv=w_kv, w_x_out=w_x_out, w_o=w_o,
             g_mix_post=g_mix_post, g_mlp_pre=g_mlp_pre, w_up=w_up, w_down=w_down, g_mlp_post=g_mlp_post)
    m = dict(g_mix_pre=m_g_mix_pre, w_in=m_w_in, conv_a_w=m_conv_a_w, conv_a_b=m_conv_a_b, ln_a_g=m_ln_a_g,
             ln_a_b=m_ln_a_b, w_a_out=m_w_a_out, conv_b_w=m_conv_b_w, w_b_out=m_w_b_out, g_mem=m_g_mem, w_kv=m_w_kv,
             w_x_out=m_w_x_out, w_o=m_w_o, g_mix_post=m_g_mix_post, g_mlp_pre=m_g_mlp_pre, w_up=m_w_up,
             w_down=m_w_down, g_mlp_post=m_g_mlp_post)
    v = dict(g_mix_pre=v_g_mix_pre, w_in=v_w_in, conv_a_w=v_conv_a_w, conv_a_b=v_conv_a_b, ln_a_g=v_ln_a_g,
             ln_a_b=v_ln_a_b, w_a_out=v_w_a_out, conv_b_w=v_conv_b_w, w_b_out=v_w_b_out, g_mem=v_g_mem, w_kv=v_w_kv,
             w_x_out=v_w_x_out, w_o=v_w_o, g_mix_post=v_g_mix_post, g_mlp_pre=v_g_mlp_pre, w_up=v_w_up,
             w_down=v_w_down, g_mlp_post=v_g_mlp_post)
    loss, grad_x, grads, delta, new_m, new_v = _step(w, m, v, x[0], mem[0], loss_target[0])
    out = [loss, grad_x[None]]
    for group in (grads, delta, new_m, new_v):
        out += [group[n] for n in WEIGHT_ORDER]
    return tuple(out)
```

```python
import functools

import jax
import jax.numpy as jnp
from jax import lax
from jax.experimental import pallas as pl
from jax.experimental.pallas import tpu as pltpu

F32 = jnp.float32
BF16 = jnp.bfloat16
MESH = pl.DeviceIdType.MESH

NORM_EPS = 1e-6
N_HEADS = 4
ADAM_LR = 0.001
ADAM_B1 = 0.9
ADAM_B2 = 0.999
ADAM_EPS = 1e-08
ADAM_WD = 0.01
ADAM_STEP = 10

N_CHIPS = 4
V7X_VMEM_BYTES = 64 * 1024 * 1024
VMEM_CAP = V7X_VMEM_BYTES - 8 * 1024 * 1024
LANE = 128
SUBLANE_BF16 = 16
HALO = 32
ROW_TILE = 256
ATTN_TILE = 512
MM_TM = 1024
MM_TN = 1024
MM_TK = 2048
EW_VMEM_BYTES = 24 * 1024 * 1024


def _tile(n, pref, align):
    if n <= pref:
        return n
    t = (pref // align) * align
    while t >= align:
        if n % t == 0:
            return t
        t -= align
    return n


def _ew_rows(r, c, n_arrays):
    return _tile(r, max(SUBLANE_BF16, EW_VMEM_BYTES // (2 * n_arrays * c * 4)), SUBLANE_BF16)


def _nbytes(shape, dtype):
    n = 1
    for s in shape:
        if s is not None:
            n *= s
    return n * jnp.dtype(dtype).itemsize


def _params(semantics, block_bytes, temp_bytes=0):
    need = 2 * block_bytes + temp_bytes + (4 << 20)
    return pltpu.CompilerParams(dimension_semantics=semantics,
                                vmem_limit_bytes=int(min(max(need, 16 << 20), VMEM_CAP)))


def _sigmoid(v):
    return 1.0 / (1.0 + jnp.exp(-v))


def _mm_nn(a, b3, out_dtype, name, a_act=None):
    m, k = a.shape
    s, k2, ns = b3.shape
    assert k == k2
    tm = _tile(m, MM_TM, SUBLANE_BF16)
    tn = _tile(ns, MM_TN, LANE)
    tk = _tile(k, MM_TK, LANE)
    q = ns // tn
    nk = k // tk

    def body(a_ref, b_ref, o_ref, *scratch):
        av = a_ref[...]
        if a_act == "relu2":
            r = jnp.maximum(av.astype(F32), 0.0)
            av = r * r
        p = jnp.dot(av.astype(BF16), b_ref[...].astype(BF16), preferred_element_type=F32)
        if nk == 1:
            o_ref[...] = p.astype(o_ref.dtype)
        else:
            acc, = scratch
            kk = pl.program_id(2)

            @pl.when(kk == 0)
            def _():
                acc[...] = p

            @pl.when(kk > 0)
            def _():
                acc[...] += p

            @pl.when(kk == nk - 1)
            def _():
                o_ref[...] = acc[...].astype(o_ref.dtype)

    blocks = (_nbytes((tm, tk), a.dtype) + _nbytes((tk, tn), b3.dtype) + _nbytes((tm, tn), out_dtype))
    return pl.pallas_call(
        body, name=name,
        out_shape=jax.ShapeDtypeStruct((m, s * ns), out_dtype),
        grid=(m // tm, s * q, nk),
        in_specs=[pl.BlockSpec((tm, tk), lambda i, j, c: (i, c)),
                  pl.BlockSpec((None, tk, tn), lambda i, j, c: (j // q, c, j % q))],
        out_specs=pl.BlockSpec((tm, tn), lambda i, j, c: (i, j)),
        scratch_shapes=[pltpu.VMEM((tm, tn), F32)] if nk > 1 else [],
        compiler_params=_params(("parallel", "parallel", "arbitrary"), blocks,
                                3 * _nbytes((tm, tn), F32) + _nbytes((tm, tk), F32)),
    )(a, b3)


def _mm_nt(a, b3, out_dtype, name, relu2_of=None):
    m, n = a.shape
    s, kd, ns = b3.shape
    assert n == s * ns
    tm = _tile(m, MM_TM, SUBLANE_BF16)
    tj = _tile(kd, MM_TN, LANE)
    tc = _tile(ns, MM_TK, LANE)
    q = ns // tc
    nc = s * q

    def body(*refs):
        if relu2_of is None:
            a_ref, b_ref, o_ref = refs[:3]
            scratch = refs[3:]
        else:
            a_ref, b_ref, u_ref, o_ref = refs[:4]
            scratch = refs[4:]

        def finish(p):
            if relu2_of is not None:
                p = p * (2.0 * jnp.maximum(u_ref[...].astype(F32), 0.0))
            o_ref[...] = p.astype(o_ref.dtype)

        p = lax.dot_general(a_ref[...].astype(BF16), b_ref[...].astype(BF16),
                            (((1,), (1,)), ((), ())), preferred_element_type=F32)
        if nc == 1:
            finish(p)
        else:
            acc, = scratch
            cc = pl.program_id(2)

            @pl.when(cc == 0)
            def _():
                acc[...] = p

            @pl.when(cc > 0)
            def _():
                acc[...] += p

            @pl.when(cc == nc - 1)
            def _():
                finish(acc[...])

    in_specs = [pl.BlockSpec((tm, tc), lambda i, j, c: (i, c)),
                pl.BlockSpec((None, tj, tc), lambda i, j, c: (c // q, j, c % q))]
    operands = [a, b3]
    blocks = _nbytes((tm, tc), a.dtype) + _nbytes((tj, tc), b3.dtype) + _nbytes((tm, tj), out_dtype)
    if relu2_of is not None:
        in_specs.append(pl.BlockSpec((tm, tj), lambda i, j, c: (i, j)))
        operands.append(relu2_of)
        blocks += _nbytes((tm, tj), relu2_of.dtype)
    return pl.pallas_call(
        body, name=name,
        out_shape=jax.ShapeDtypeStruct((m, kd), out_dtype),
        grid=(m // tm, kd // tj, nc),
        in_specs=in_specs,
        out_specs=pl.BlockSpec((tm, tj), lambda i, j, c: (i, j)),
        scratch_shapes=[pltpu.VMEM((tm, tj), F32)] if nc > 1 else [],
        compiler_params=_params(("parallel", "parallel", "arbitrary"), blocks,
                                3 * _nbytes((tm, tj), F32)),
    )(*operands)


def _mm_tn(a, g, out_shards, name, a_act=None):
    m, ka = a.shape
    m2, n = g.shape
    assert m == m2
    ns = n // out_shards
    ta = _tile(ka, MM_TM, LANE)
    tn = _tile(ns, MM_TN, LANE)
    tm = _tile(m, MM_TK, SUBLANE_BF16)
    q = ns // tn
    nm = m // tm

    def body(a_ref, g_ref, o_ref, *scratch):
        av = a_ref[...]
        if a_act == "relu2":
            r = jnp.maximum(av.astype(F32), 0.0)
            av = r * r
        p = lax.dot_general(av.astype(BF16), g_ref[...].astype(BF16),
                            (((0,), (0,)), ((), ())), preferred_element_type=F32)
        if nm == 1:
            o_ref[...] = p.astype(o_ref.dtype)
        else:
            acc, = scratch
            cc = pl.program_id(2)

            @pl.when(cc == 0)
            def _():
                acc[...] = p

            @pl.when(cc > 0)
            def _():
                acc[...] += p

            @pl.when(cc == nm - 1)
            def _():
                o_ref[...] = acc[...].astype(o_ref.dtype)

    blocks = _nbytes((tm, ta), a.dtype) + _nbytes((tm, tn), g.dtype) + _nbytes((ta, tn), BF16)
    return pl.pallas_call(
        body, name=name,
        out_shape=jax.ShapeDtypeStruct((out_shards, ka, ns), BF16),
        grid=(ka // ta, out_shards * q, nm),
        in_specs=[pl.BlockSpec((tm, ta), lambda i, j, c: (c, i)),
                  pl.BlockSpec((tm, tn), lambda i, j, c: (c, j))],
        out_specs=pl.BlockSpec((None, ta, tn), lambda i, j, c: (j // q, i, j % q)),
        scratch_shapes=[pltpu.VMEM((ta, tn), F32)] if nm > 1 else [],
        compiler_params=_params(("parallel", "parallel", "arbitrary"), blocks,
                                3 * _nbytes((ta, tn), F32) + _nbytes((tm, ta), F32)),
    )(a, g)


def _rms_scale(v):
    return lax.rsqrt(jnp.mean(v * v, axis=-1, keepdims=True) + NORM_EPS)


def _norm_fwd(x, name, *, z=None, g_post=None, g_next=None, target=None):
    t, d = x.shape
    tr = _tile(t, ROW_TILE, SUBLANE_BF16)
    has_res, has_next, has_loss = z is not None, g_next is not None, target is not None

    def body(*refs):
        it = iter(refs)
        x_ref = next(it)
        z_ref, gp_ref = (next(it), next(it)) if has_res else (None, None)
        gn_ref = next(it) if has_next else None
        t_ref = next(it) if has_loss else None
        xv = x_ref[...]
        if has_res:
            zv = z_ref[...]
            xv = xv + zv * _rms_scale(zv) * gp_ref[...]
            if not has_loss:
                next(it)[...] = xv
        if has_next:
            next(it)[...] = (xv * _rms_scale(xv) * gn_ref[...]).astype(BF16)
        if has_loss:
            e = xv - t_ref[...]
            next(it)[...] = e * (1.0 / d)
            ls_ref = next(it)

            @pl.when(pl.program_id(0) == 0)
            def _():
                ls_ref[...] = jnp.zeros_like(ls_ref)

            ls_ref[...] += jnp.sum(e * e, axis=0, keepdims=True)

    row = pl.BlockSpec((tr, d), lambda i: (i, 0))
    vec = pl.BlockSpec((1, d), lambda i: (0, 0))
    operands, in_specs, out_shape, out_specs = [x], [row], [], []
    if has_res:
        operands += [z, g_post.reshape(1, d)]
        in_specs += [row, vec]
        if not has_loss:
            out_shape.append(jax.ShapeDtypeStruct((t, d), F32))
            out_specs.append(row)
    if has_next:
        operands.append(g_next.reshape(1, d))
        in_specs.append(vec)
        out_shape.append(jax.ShapeDtypeStruct((t, d), BF16))
        out_specs.append(row)
    if has_loss:
        operands.append(target)
        in_specs.append(row)
        out_shape += [jax.ShapeDtypeStruct((t, d), F32), jax.ShapeDtypeStruct((1, d), F32)]
        out_specs += [row, vec]
    return pl.pallas_call(
        body, name=name, out_shape=out_shape, grid=(t // tr,),
        in_specs=in_specs, out_specs=out_specs,
        compiler_params=_params(("arbitrary",), 5 * _nbytes((tr, d), F32), 4 * _nbytes((tr, d), F32)),
    )(*operands)


def _norm_bwd(name, *, dxo=None, pre=None, post=None, want_dx=True):
    ref_arr = dxo if dxo is not None else pre[1]
    t, d = ref_arr.shape
    tr = _tile(t, ROW_TILE, SUBLANE_BF16)
    has_dxo, has_pre, has_post = dxo is not None, pre is not None, post is not None

    def body(*refs):
        it = iter(refs)
        dxo_ref = next(it) if has_dxo else None
        dh_ref, xin_ref, gpre_ref = (next(it), next(it), next(it)) if has_pre else (None,) * 3
        z_ref, gpost_ref = (next(it), next(it)) if has_post else (None, None)
        dx_ref = next(it) if (has_pre and want_dx) else None
        dz_ref = next(it) if has_post else None
        dgpre_ref = next(it) if has_pre else None
        dgpost_ref = next(it) if has_post else None
        first = pl.program_id(0) == 0

        dx = dxo_ref[...] if has_dxo else None
        if has_pre:
            xin = xin_ref[...]
            dh = dh_ref[...].astype(F32)
            r = _rms_scale(xin)
            gy = dh * gpre_ref[...]
            dloc = r * gy - xin * (r * r * r) * jnp.mean(gy * xin, axis=-1, keepdims=True)
            dx = dloc if dx is None else dx + dloc
            if want_dx:
                dx_ref[...] = dx

            @pl.when(first)
            def _():
                dgpre_ref[...] = jnp.zeros_like(dgpre_ref)

            dgpre_ref[...] += jnp.sum(dh * xin * r, axis=0, keepdims=True)
        if has_post:
            zv = z_ref[...]
            r = _rms_scale(zv)
            gy = dx * gpost_ref[...]
            dz = r * gy - zv * (r * r * r) * jnp.mean(gy * zv, axis=-1, keepdims=True)
            dz_ref[...] = dz.astype(BF16)

            @pl.when(first)
            def _():
                dgpost_ref[...] = jnp.zeros_like(dgpost_ref)

            dgpost_ref[...] += jnp.sum(dx * zv * r, axis=0, keepdims=True)

    row = pl.BlockSpec((tr, d), lambda i: (i, 0))
    vec = pl.BlockSpec((1, d), lambda i: (0, 0))
    operands, in_specs, out_shape, out_specs = [], [], [], []
    if has_dxo:
        operands.append(dxo)
        in_specs.append(row)
    if has_pre:
        operands += [pre[0], pre[1], pre[2].reshape(1, d)]
        in_specs += [row, row, vec]
    if has_post:
        operands += [post[0], post[1].reshape(1, d)]
        in_specs += [row, vec]
    if has_pre and want_dx:
        out_shape.append(jax.ShapeDtypeStruct((t, d), F32))
        out_specs.append(row)
    if has_post:
        out_shape.append(jax.ShapeDtypeStruct((t, d), BF16))
        out_specs.append(row)
    if has_pre:
        out_shape.append(jax.ShapeDtypeStruct((1, d), F32))
        out_specs.append(vec)
    if has_post:
        out_shape.append(jax.ShapeDtypeStruct((1, d), F32))
        out_specs.append(vec)
    return pl.pallas_call(
        body, name=name, out_shape=out_shape, grid=(t // tr,),
        in_specs=in_specs, out_specs=out_specs,
        compiler_params=_params(("arbitrary",), 6 * _nbytes((tr, d), F32), 6 * _nbytes((tr, d), F32)),
    )(*operands)


def _seq_tiles(t):
    tr = _tile(t, ROW_TILE, HALO)
    assert tr % HALO == 0 and t % tr == 0
    return tr, t // tr, tr // HALO


def _col(tr, width, cb):
    return pl.BlockSpec((tr, width), lambda i: (i, cb))


def _prev_halo(per, width, cb):
    return pl.BlockSpec((HALO, width), lambda i: (jnp.maximum(i * per - 1, 0), cb))


def _next_halo(per, n_halo, width, cb):
    return pl.BlockSpec((HALO, width), lambda i: (jnp.minimum((i + 1) * per, n_halo - 1), cb))


def _const(shape):
    return pl.BlockSpec(shape, lambda i: (0,) * len(shape))


def _glu(val, gate):
    return val.astype(F32) * _sigmoid(gate.astype(F32))


def _layer_norm_parts(ca):
    mu = jnp.mean(ca, axis=-1, keepdims=True)
    xc = ca - mu
    rs = lax.rsqrt(jnp.mean(xc * xc, axis=-1, keepdims=True) + NORM_EPS)
    return xc * rs, rs


def _branch_a_fwd(proj, cw, cb, lg, lb, name):
    t = proj.shape[0]
    kw, c = cw.shape
    tr, nt, per = _seq_tiles(t)

    def body(av_ref, ag_ref, hv_ref, hg_ref, cw_ref, cb_ref, lg_ref, lb_ref, ca_ref, act_ref, abuf):
        i = pl.program_id(0)
        abuf[pl.ds(0, HALO), :] = jnp.where(i > 0, _glu(hv_ref[...], hg_ref[...]), 0.0)
        abuf[pl.ds(HALO, tr), :] = _glu(av_ref[...], ag_ref[...])
        acc = jnp.zeros((tr, c), F32)
        for k in range(kw):
            acc = acc + cw_ref[pl.ds(k, 1), :] * abuf[pl.ds(HALO - (kw - 1) + k, tr), :]
        ca = acc + cb_ref[...]
        ca_ref[...] = ca
        xh, _ = _layer_norm_parts(ca)
        ln = xh * lg_ref[...] + lb_ref[...]
        act_ref[...] = (ln * _sigmoid(ln)).astype(BF16)

    return pl.pallas_call(
        body, name=name,
        out_shape=[jax.ShapeDtypeStruct((t, c), F32), jax.ShapeDtypeStruct((t, c), BF16)],
        grid=(nt,),
        in_specs=[_col(tr, c, 0), _col(tr, c, 1), _prev_halo(per, c, 0), _prev_halo(per, c, 1),
                  _const((kw, c)), _const((1, c)), _const((1, c)), _const((1, c))],
        out_specs=[_col(tr, c, 0), _col(tr, c, 0)],
        scratch_shapes=[pltpu.VMEM((HALO + tr, c), F32)],
        compiler_params=_params(("arbitrary",), 4 * _nbytes((tr, c), F32), 8 * _nbytes((tr + HALO, c), F32)),
    )(proj, proj, proj, proj, cw, cb.reshape(1, c), lg.reshape(1, c), lb.reshape(1, c))


def _branch_a_bwd(dproj, dact, ca, proj, cw, lg, lb, name):
    t = proj.shape[0]
    kw, c = cw.shape
    tr, nt, per = _seq_tiles(t)
    n_halo = t // HALO

    def body(dproj_in, da_ref, dah_ref, ca_ref, cah_ref, av_ref, ag_ref, hv_ref, hg_ref,
             cw_ref, lg_ref, lb_ref, out_ref, dlg_ref, dlb_ref, dcb_ref, dcw_ref, abuf, dbuf):
        del dproj_in
        i = pl.program_id(0)
        lgv, lbv = lg_ref[...], lb_ref[...]

        def conv_grad(dact_v, ca_v):
            xh, rs = _layer_norm_parts(ca_v)
            ln = xh * lgv + lbv
            sg = _sigmoid(ln)
            dln = dact_v.astype(F32) * (sg * (1.0 + ln * (1.0 - sg)))
            dxh = dln * lgv
            dca = rs * (dxh - jnp.mean(dxh, axis=-1, keepdims=True)
                        - xh * jnp.mean(dxh * xh, axis=-1, keepdims=True))
            return dca, dln, xh

        dca, dln, xh = conv_grad(da_ref[...], ca_ref[...])
        dca_h, _, _ = conv_grad(dah_ref[...], cah_ref[...])
        dbuf[pl.ds(0, tr), :] = dca
        dbuf[pl.ds(tr, HALO), :] = jnp.where(i < nt - 1, dca_h, 0.0)

        @pl.when(i == 0)
        def _():
            dlg_ref[...] = jnp.zeros_like(dlg_ref)
            dlb_ref[...] = jnp.zeros_like(dlb_ref)
            dcb_ref[...] = jnp.zeros_like(dcb_ref)
            dcw_ref[...] = jnp.zeros_like(dcw_ref)

        dlg_ref[...] += jnp.sum(dln * xh, axis=0, keepdims=True)
        dlb_ref[...] += jnp.sum(dln, axis=0, keepdims=True)
        dcb_ref[...] += jnp.sum(dca, axis=0, keepdims=True)

        av = av_ref[...].astype(F32)
        sg = _sigmoid(ag_ref[...].astype(F32))
        abuf[pl.ds(0, HALO), :] = jnp.where(i > 0, _glu(hv_ref[...], hg_ref[...]), 0.0)
        abuf[pl.ds(HALO, tr), :] = av * sg

        d_a = jnp.zeros((tr, c), F32)
        for k in range(kw):
            d_a = d_a + cw_ref[pl.ds(k, 1), :] * dbuf[pl.ds(kw - 1 - k, tr), :]
            dcw_ref[pl.ds(k, 1), :] += jnp.sum(dca * abuf[pl.ds(HALO - (kw - 1) + k, tr), :],
                                               axis=0, keepdims=True)
        out_ref[:, pl.ds(0, c)] = (d_a * sg).astype(BF16)
        out_ref[:, pl.ds(c, c)] = (d_a * av * sg * (1.0 - sg)).astype(BF16)

    vec = _const((1, c))
    return pl.pallas_call(
        body, name=name,
        out_shape=[jax.ShapeDtypeStruct(dproj.shape, BF16)] + [jax.ShapeDtypeStruct((1, c), F32)] * 3
        + [jax.ShapeDtypeStruct((kw, c), F32)],
        grid=(nt,),
        in_specs=[pl.BlockSpec(memory_space=pl.ANY),
                  _col(tr, c, 0), _next_halo(per, n_halo, c, 0),
                  _col(tr, c, 0), _next_halo(per, n_halo, c, 0),
                  _col(tr, c, 0), _col(tr, c, 1), _prev_halo(per, c, 0), _prev_halo(per, c, 1),
                  _const((kw, c)), vec, vec],
        out_specs=[pl.BlockSpec((tr, 2 * c), lambda i: (i, 0)), vec, vec, vec, _const((kw, c))],
        scratch_shapes=[pltpu.VMEM((HALO + tr, c), F32), pltpu.VMEM((HALO + tr, c), F32)],
        input_output_aliases={0: 0},
        compiler_params=_params(("arbitrary",), 6 * _nbytes((tr, c), F32), 12 * _nbytes((tr + HALO, c), F32)),
    )(dproj, dact, dact, ca, ca, proj, proj, proj, proj, cw, lg.reshape(1, c), lb.reshape(1, c))


def _branch_b_fwd(proj, cw, name):
    t = proj.shape[0]
    kw, c = cw.shape
    tr, nt, per = _seq_tiles(t)

    def body(sb_ref, sc_ref, sx_ref, hc_ref, hx_ref, cw_ref, o_ref, pbuf):
        i = pl.program_id(0)
        hp = hc_ref[...].astype(F32) * hx_ref[...].astype(F32)
        pbuf[pl.ds(0, HALO), :] = jnp.where(i > 0, hp, 0.0)
        pbuf[pl.ds(HALO, tr), :] = sc_ref[...].astype(F32) * sx_ref[...].astype(F32)
        u = jnp.zeros((tr, c), F32)
        for k in range(kw):
            u = u + cw_ref[pl.ds(k, 1), :] * pbuf[pl.ds(HALO - (kw - 1) + k, tr), :]
        o_ref[...] = (sb_ref[...].astype(F32) * u).astype(BF16)

    return pl.pallas_call(
        body, name=name,
        out_shape=jax.ShapeDtypeStruct((t, c), BF16),
        grid=(nt,),
        in_specs=[_col(tr, c, 2), _col(tr, c, 3), _col(tr, c, 4),
                  _prev_halo(per, c, 3), _prev_halo(per, c, 4), _const((kw, c))],
        out_specs=_col(tr, c, 0),
        scratch_shapes=[pltpu.VMEM((HALO + tr, c), F32)],
        compiler_params=_params(("arbitrary",), 4 * _nbytes((tr, c), F32), 6 * _nbytes((tr + HALO, c), F32)),
    )(proj, proj, proj, proj, proj, cw)


def _branch_b_bwd(dproj, dbin, proj, cw, name):
    t = proj.shape[0]
    kw, c = cw.shape
    tr, nt, per = _seq_tiles(t)
    n_halo = t // HALO

    def body(dproj_in, db_ref, dbh_ref, sb_ref, sbh_ref, sc_ref, sx_ref, hc_ref, hx_ref, cw_ref,
             out_ref, dcw_ref, pbuf, dubuf):
        del dproj_in
        i = pl.program_id(0)
        j = pl.program_id(1)
        sb = sb_ref[...].astype(F32)
        sc = sc_ref[...].astype(F32)
        sx = sx_ref[...].astype(F32)
        dbin_v = db_ref[...].astype(F32)
        hp = hc_ref[...].astype(F32) * hx_ref[...].astype(F32)
        pbuf[pl.ds(0, HALO), :] = jnp.where(i > 0, hp, 0.0)
        pbuf[pl.ds(HALO, tr), :] = sc * sx
        du = dbin_v * sb
        du_h = dbh_ref[...].astype(F32) * sbh_ref[...].astype(F32)
        dubuf[pl.ds(0, tr), :] = du
        dubuf[pl.ds(tr, HALO), :] = jnp.where(i < nt - 1, du_h, 0.0)

        @pl.when((i == 0) & (j == 0))
        def _():
            dcw_ref[...] = jnp.zeros_like(dcw_ref)

        u = jnp.zeros((tr, c), F32)
        dp = jnp.zeros((tr, c), F32)
        for k in range(kw):
            shifted = pbuf[pl.ds(HALO - (kw - 1) + k, tr), :]
            u = u + cw_ref[pl.ds(k, 1), :] * shifted
            dp = dp + cw_ref[pl.ds(k, 1), :] * dubuf[pl.ds(kw - 1 - k, tr), :]

            @pl.when(j == 0)
            def _():
                dcw_ref[pl.ds(k, 1), :] += jnp.sum(du * shifted, axis=0, keepdims=True)

        res = jnp.where(j == 0, dbin_v * u, jnp.where(j == 1, dp * sx, dp * sc))
        out_ref[...] = res.astype(BF16)

    def colj(cb):
        return pl.BlockSpec((tr, c), lambda i, j: (i, cb))

    def prevj(cb):
        return pl.BlockSpec((HALO, c), lambda i, j: (jnp.maximum(i * per - 1, 0), cb))

    def nextj(cb):
        return pl.BlockSpec((HALO, c), lambda i, j: (jnp.minimum((i + 1) * per, n_halo - 1), cb))

    return pl.pallas_call(
        body, name=name,
        out_shape=[jax.ShapeDtypeStruct(dproj.shape, BF16), jax.ShapeDtypeStruct((kw, c), F32)],
        grid=(nt, 3),
        in_specs=[pl.BlockSpec(memory_space=pl.ANY),
                  colj(0), nextj(0), colj(2), nextj(2), colj(3), colj(4), prevj(3), prevj(4),
                  pl.BlockSpec((kw, c), lambda i, j: (0, 0))],
        out_specs=[pl.BlockSpec((tr, c), lambda i, j: (i, 2 + j)),
                   pl.BlockSpec((kw, c), lambda i, j: (0, 0))],
        scratch_shapes=[pltpu.VMEM((HALO + tr, c), F32), pltpu.VMEM((HALO + tr, c), F32)],
        input_output_aliases={0: 0},
        compiler_params=_params(("arbitrary", "arbitrary"), 6 * _nbytes((tr, c), F32),
                                10 * _nbytes((tr + HALO, c), F32)),
    )(dproj, dbin, dbin, proj, proj, proj, proj, proj, proj, cw)


def _softmax_rows(s):
    e = jnp.exp(s - jnp.max(s, axis=-1, keepdims=True))
    return e / jnp.sum(e, axis=-1, keepdims=True)


def _attn_fwd(proj, kv, name):
    t = proj.shape[0]
    m, c2 = kv.shape
    c = c2 // 2
    hd = c // N_HEADS
    ta = _tile(t, ATTN_TILE, SUBLANE_BF16)
    scale = hd ** -0.5

    def body(q_ref, kv_ref, o_ref):
        for h in range(N_HEADS):
            qh = q_ref[:, pl.ds(h * hd, hd)]
            kh = kv_ref[:, pl.ds(h * hd, hd)]
            vh = kv_ref[:, pl.ds(c + h * hd, hd)]
            s = lax.dot_general(qh, kh, (((1,), (1,)), ((), ())), preferred_element_type=F32) * scale
            p = _softmax_rows(s)
            o_ref[:, pl.ds(h * hd, hd)] = jnp.dot(p.astype(BF16), vh,
                                                  preferred_element_type=F32).astype(BF16)

    return pl.pallas_call(
        body, name=name,
        out_shape=jax.ShapeDtypeStruct((t, c), BF16),
        grid=(t // ta,),
        in_specs=[pl.BlockSpec((ta, c), lambda i: (i, 5)), _const((m, c2))],
        out_specs=pl.BlockSpec((ta, c), lambda i: (i, 0)),
        compiler_params=_params(("parallel",), 2 * _nbytes((ta, c), BF16) + _nbytes((m, c2), BF16),
                                8 * _nbytes((ta, m), F32)),
    )(proj, kv)


def _attn_bwd(dproj, d_o, proj, kv, name):
    t = proj.shape[0]
    m, c2 = kv.shape
    c = c2 // 2
    hd = c // N_HEADS
    ta = _tile(t, ATTN_TILE, SUBLANE_BF16)
    scale = hd ** -0.5

    def body(dproj_in, do_ref, q_ref, kv_ref, dq_ref, dkv_ref):
        del dproj_in

        @pl.when(pl.program_id(0) == 0)
        def _():
            dkv_ref[...] = jnp.zeros_like(dkv_ref)

        for h in range(N_HEADS):
            qh = q_ref[:, pl.ds(h * hd, hd)]
            kh = kv_ref[:, pl.ds(h * hd, hd)]
            vh = kv_ref[:, pl.ds(c + h * hd, hd)]
            doh = do_ref[:, pl.ds(h * hd, hd)]
            s = lax.dot_general(qh, kh, (((1,), (1,)), ((), ())), preferred_element_type=F32) * scale
            p = _softmax_rows(s)
            dp = lax.dot_general(doh, vh, (((1,), (1,)), ((), ())), preferred_element_type=F32)
            ds = (p * (dp - jnp.sum(dp * p, axis=-1, keepdims=True))).astype(BF16)
            dq_ref[:, pl.ds(h * hd, hd)] = (jnp.dot(ds, kh, preferred_element_type=F32) * scale).astype(BF16)
            dkv_ref[:, pl.ds(h * hd, hd)] += lax.dot_general(
                ds, qh, (((0,), (0,)), ((), ())), preferred_element_type=F32) * scale
            dkv_ref[:, pl.ds(c + h * hd, hd)] += lax.dot_general(
                p.astype(BF16), doh, (((0,), (0,)), ((), ())), preferred_element_type=F32)

    return pl.pallas_call(
        body, name=name,
        out_shape=[jax.ShapeDtypeStruct(dproj.shape, BF16), jax.ShapeDtypeStruct((m, c2), F32)],
        grid=(t // ta,),
        in_specs=[pl.BlockSpec(memory_space=pl.ANY),
                  pl.BlockSpec((ta, c), lambda i: (i, 0)), pl.BlockSpec((ta, c), lambda i: (i, 5)),
                  _const((m, c2))],
        out_specs=[pl.BlockSpec((ta, c), lambda i: (i, 5)), _const((m, c2))],
        input_output_aliases={0: 0},
        compiler_params=_params(("arbitrary",), 3 * _nbytes((ta, c), BF16) + 2 * _nbytes((m, c2), F32),
                                10 * _nbytes((ta, m), F32)),
    )(dproj, d_o, proj, kv)


def _merge_fwd(proj, ya, yb, yx, name):
    t, d = ya.shape
    tr = _tile(t, ROW_TILE, SUBLANE_BF16)

    def body(g_ref, ya_ref, yb_ref, yx_ref, o_ref):
        acc = jnp.zeros((tr, d), F32)
        for b, y_ref in enumerate((ya_ref, yb_ref, yx_ref)):
            acc = acc + _sigmoid(g_ref[:, pl.ds(b * d, d)].astype(F32)) * y_ref[...].astype(F32)
        o_ref[...] = acc.astype(BF16)

    row = pl.BlockSpec((tr, d), lambda i: (i, 0))
    return pl.pallas_call(
        body, name=name,
        out_shape=jax.ShapeDtypeStruct((t, d), BF16),
        grid=(t // tr,),
        in_specs=[pl.BlockSpec((tr, 3 * d), lambda i: (i, 1)), row, row, row],
        out_specs=row,
        compiler_params=_params(("parallel",), 7 * _nbytes((tr, d), BF16), 6 * _nbytes((tr, d), F32)),
    )(proj, ya, yb, yx)


def _merge_bwd(dmerged, proj, ya, yb, yx, name):
    t, d = ya.shape
    tr = _tile(t, ROW_TILE, SUBLANE_BF16)

    def body(dm_ref, g_ref, ya_ref, yb_ref, yx_ref, dg_ref, dya_ref, dyb_ref, dyx_ref):
        dm = dm_ref[...].astype(F32)
        for b, (y_ref, dy_ref) in enumerate(((ya_ref, dya_ref), (yb_ref, dyb_ref), (yx_ref, dyx_ref))):
            sg = _sigmoid(g_ref[:, pl.ds(b * d, d)].astype(F32))
            dg_ref[:, pl.ds(b * d, d)] = (dm * y_ref[...].astype(F32) * sg * (1.0 - sg)).astype(BF16)
            dy_ref[...] = (dm * sg).astype(BF16)

    row = pl.BlockSpec((tr, d), lambda i: (i, 0))
    gates = pl.BlockSpec((tr, 3 * d), lambda i: (i, 1))
    return pl.pallas_call(
        body, name=name,
        out_shape=[jax.ShapeDtypeStruct(proj.shape, BF16)] + [jax.ShapeDtypeStruct((t, d), BF16)] * 3,
        grid=(t // tr,),
        in_specs=[row, gates, row, row, row],
        out_specs=[gates, row, row, row],
        compiler_params=_params(("parallel",), 14 * _nbytes((tr, d), BF16), 8 * _nbytes((tr, d), F32)),
    )(dmerged, proj, ya, yb, yx)


def _as3(a):
    return a.reshape((-1,) + a.shape[-2:])


def _ew_call(fn, ins, n_out, out_dtypes, name):
    b, r, c = ins[0].shape
    tr = _ew_rows(r, c, len(ins) + n_out)

    def body(*refs):
        outs = fn(*[ref[...] for ref in refs[:len(ins)]])
        for ref, val in zip(refs[len(ins):], outs):
            ref[...] = val.astype(ref.dtype)

    spec = pl.BlockSpec((None, tr, c), lambda i, j: (i, j, 0))
    return pl.pallas_call(
        body, name=name,
        out_shape=[jax.ShapeDtypeStruct((b, r, c), dt) for dt in out_dtypes],
        grid=(b, r // tr),
        in_specs=[spec] * len(ins), out_specs=[spec] * n_out,
        compiler_params=_params(("parallel", "parallel"), (len(ins) + n_out) * _nbytes((tr, c), F32),
                                6 * _nbytes((tr, c), F32)),
    )(*ins)


def _my_chip():
    return 2 * lax.axis_index("x") + lax.axis_index("y")


def _my_core():
    return lax.axis_index("c")


def _cast_to_slot(w, l, name):
    _, r, cs = w.shape
    tr = _ew_rows(r, cs, 2)

    def body(w_ref, o_ref):
        o_ref[...] = w_ref[...].astype(BF16)

    return pl.pallas_call(
        body, name=name,
        out_shape=jax.ShapeDtypeStruct((N_CHIPS, r, cs), BF16),
        grid=(r // tr,),
        in_specs=[pl.BlockSpec((None, tr, cs), lambda i: (l, i, 0))],
        out_specs=pl.BlockSpec((None, tr, cs), lambda i: (_my_chip(), i, 0)),
        compiler_params=_params(("parallel",), 2 * _nbytes((tr, cs), F32)),
    )(w)


def _add_half(part, theirs, chip_sum, l, name):
    nl, _, rh, cs = theirs.shape
    tr = _ew_rows(rh, cs, 3)
    nrb = rh // tr

    def body(a_ref, b_ref, *rest):
        rest[-1][...] = (a_ref[...].astype(F32) + b_ref[...].astype(F32)).astype(BF16)

    stacked = pl.BlockSpec((None, None, tr, cs), lambda j, i: (l, j, i, 0))
    in_specs = [pl.BlockSpec((None, tr, cs), lambda j, i: (j, _my_core() * nrb + i, 0)), stacked]
    operands = [part, theirs]
    aliases = {}
    if chip_sum is not None:
        in_specs.append(HBM_SPEC)
        operands.append(chip_sum)
        aliases = {2: 0}
    return pl.pallas_call(
        body, name=name,
        out_shape=jax.ShapeDtypeStruct((nl, N_CHIPS, rh, cs), BF16),
        grid=(N_CHIPS, nrb), in_specs=in_specs, out_specs=stacked,
        input_output_aliases=aliases,
        compiler_params=_params(("parallel", "parallel"), 3 * _nbytes((tr, cs), F32)),
    )(*operands)


def _sum_chips(chip_sum, q, name):
    nl, _, rh, cs = q.shape
    tr = _ew_rows(rh, cs, 5)
    nrb = rh // tr

    def body(own_ref, q1_ref, q2_ref, q3_ref, o_ref):
        acc = own_ref[...].astype(F32)
        for ref in (q1_ref, q2_ref, q3_ref):
            acc = acc + ref[...].astype(F32)
        o_ref[...] = acc

    def slot(k):
        return pl.BlockSpec((None, None, tr, cs), lambda l, i: (l, (_my_chip() + k) % N_CHIPS, i, 0))

    return pl.pallas_call(
        body, name=name,
        out_shape=jax.ShapeDtypeStruct((nl, 2 * rh, cs), F32),
        grid=(nl, nrb),
        in_specs=[slot(0), slot(1), slot(2), slot(3)],
        out_specs=pl.BlockSpec((None, tr, cs), lambda l, i: (l, _my_core() * nrb + i, 0)),
        compiler_params=_params(("parallel", "parallel"), 5 * _nbytes((tr, cs), F32)),
    )(chip_sum, q, q, q)


def _adamw(w, g, m, v, name):
    shape = w.shape
    c1 = 1.0 - ADAM_B1 ** ADAM_STEP
    c2 = 1.0 - ADAM_B2 ** ADAM_STEP

    def fn(wv, gv, mv, vv):
        mn = ADAM_B1 * mv + (1.0 - ADAM_B1) * gv
        vn = ADAM_B2 * vv + (1.0 - ADAM_B2) * (gv * gv)
        delta = -ADAM_LR * ((mn / c1) / (jnp.sqrt(vn / c2) + ADAM_EPS) + ADAM_WD * wv)
        return delta, mn, vn

    outs = _ew_call(fn, [_as3(w), _as3(g), _as3(m), _as3(v)], 3, [F32] * 3, name)
    return [o.reshape(shape) for o in outs]


def _sum_leading(q, name):
    b, nj, r, c = q.shape
    tr = _ew_rows(r, c, nj + 1)

    def body(q_ref, o_ref):
        acc = q_ref[0].astype(F32)
        for j in range(1, nj):
            acc = acc + q_ref[j].astype(F32)
        o_ref[...] = acc

    return pl.pallas_call(
        body, name=name,
        out_shape=jax.ShapeDtypeStruct((b, r, c), F32),
        grid=(b, r // tr),
        in_specs=[pl.BlockSpec((None, nj, tr, c), lambda i, j: (i, 0, j, 0))],
        out_specs=pl.BlockSpec((None, tr, c), lambda i, j: (i, j, 0)),
        compiler_params=_params(("parallel", "parallel"), (nj + 1) * _nbytes((tr, c), F32),
                                2 * _nbytes((tr, c), F32)),
    )(q)


HBM_SPEC = pl.BlockSpec(memory_space=pl.ANY)


def _place():
    x, y, c = lax.axis_index("x"), lax.axis_index("y"), lax.axis_index("c")
    peers = [(1 - x, y), (x, 1 - y), (1 - x, 1 - y)]
    return x, y, c, 2 * x + y, peers


def _remote(src, dst, send_sem, recv_sem, dev):
    return pltpu.make_async_remote_copy(src_ref=src, dst_ref=dst, send_sem=send_sem, recv_sem=recv_sem,
                                        device_id=dev, device_id_type=MESH)


def _gather_weight(slots, name):
    na = len(slots)
    _, r, cs = slots[0].shape
    rh = r // 2
    n = 3 * na

    def body(*refs):
        outs = refs[na:2 * na]
        send, recv, fsend, frecv = refs[2 * na:]
        x, y, c, me, peers = _place()
        mine = pl.ds(c * rh, rh)
        other = pl.ds((1 - c) * rh, rh)
        sends = []
        for l in range(na):
            own = outs[l].at[me, mine, :]
            for j, (px, py) in enumerate(peers):
                k = 3 * l + j
                sends.append(_remote(own, own, send.at[k], recv.at[k], (px, py, c)))
                sends[-1].start()
        forwards = []
        for l in range(na):
            for j, (px, py) in enumerate(peers):
                k = 3 * l + j
                landed = outs[l].at[2 * px + py, mine, :]
                _remote(landed, landed, send.at[k], recv.at[k], (px, py, c)).wait_recv()
                forwards.append(_remote(landed, landed, fsend.at[k], frecv.at[k], (x, y, 1 - c)))
                forwards[-1].start()
        for l in range(na):
            for j, (px, py) in enumerate(peers):
                k = 3 * l + j
                landed = outs[l].at[2 * px + py, other, :]
                _remote(landed, landed, fsend.at[k], frecv.at[k], (x, y, 1 - c)).wait_recv()
        for cp in sends + forwards:
            cp.wait_send()

    return pl.pallas_call(
        body, name=name,
        out_shape=[jax.ShapeDtypeStruct(s.shape, s.dtype) for s in slots],
        in_specs=[HBM_SPEC] * na, out_specs=[HBM_SPEC] * na,
        input_output_aliases={l: l for l in range(na)},
        scratch_shapes=[pltpu.SemaphoreType.DMA((n,))] * 4,
    )(*slots)


def _gather_small(v, name):
    def body(v_ref, o_ref, send, recv, lsem):
        x, y, c, me, peers = _place()
        local = pltpu.make_async_copy(v_ref, o_ref.at[me], lsem)
        local.start()
        sends = [_remote(v_ref, o_ref.at[me], send.at[j], recv.at[j], (px, py, c))
                 for j, (px, py) in enumerate(peers)]
        for cp in sends:
            cp.start()
        for j, (px, py) in enumerate(peers):
            _remote(v_ref, o_ref.at[2 * px + py], send.at[j], recv.at[j], (px, py, c)).wait_recv()
        for cp in sends:
            cp.wait_send()
        local.wait()

    return pl.pallas_call(
        body, name=name,
        out_shape=jax.ShapeDtypeStruct((N_CHIPS,) + v.shape, v.dtype),
        in_specs=[HBM_SPEC], out_specs=HBM_SPEC,
        scratch_shapes=[pltpu.SemaphoreType.DMA((3,)), pltpu.SemaphoreType.DMA((3,)), pltpu.SemaphoreType.DMA],
    )(v)


def _gather_all(v, name):
    def body(v_ref, o_ref, send, recv, lsem):
        x, y, c = lax.axis_index("x"), lax.axis_index("y"), lax.axis_index("c")
        me = 4 * x + 2 * y + c
        local = pltpu.make_async_copy(v_ref, o_ref.at[me], lsem)
        local.start()
        flips = [(fx, fy, fc) for fx in (0, 1) for fy in (0, 1) for fc in (0, 1)][1:]
        peers = [(x ^ fx, y ^ fy, c ^ fc) for fx, fy, fc in flips]
        sends = [_remote(v_ref, o_ref.at[me], send.at[k], recv.at[k], dev) for k, dev in enumerate(peers)]
        for cp in sends:
            cp.start()
        for k, (px, py, pc) in enumerate(peers):
            _remote(v_ref, o_ref.at[4 * px + 2 * py + pc], send.at[k], recv.at[k], (px, py, pc)).wait_recv()
        for cp in sends:
            cp.wait_send()
        local.wait()

    return pl.pallas_call(
        body, name=name,
        out_shape=jax.ShapeDtypeStruct((8,) + v.shape, v.dtype),
        in_specs=[HBM_SPEC], out_specs=HBM_SPEC,
        scratch_shapes=[pltpu.SemaphoreType.DMA((7,)), pltpu.SemaphoreType.DMA((7,)), pltpu.SemaphoreType.DMA],
    )(v)


def _send_halves(parts, name):
    nl = len(parts)
    _, r, cs = parts[0].shape
    rh = r // 2

    def body(*refs):
        ins = refs[:nl]
        theirs_ref, send, recv = refs[nl:]
        x, y, c, _, _ = _place()
        give = pl.ds((1 - c) * rh, rh)
        sends = [_remote(ins[l].at[:, give, :], theirs_ref.at[l], send.at[l], recv.at[l], (x, y, 1 - c))
                 for l in range(nl)]
        for cp in sends:
            cp.start()
        for cp in sends:
            cp.wait()

    return pl.pallas_call(
        body, name=name,
        out_shape=jax.ShapeDtypeStruct((nl, N_CHIPS, rh, cs), parts[0].dtype),
        in_specs=[HBM_SPEC] * nl, out_specs=HBM_SPEC,
        scratch_shapes=[pltpu.SemaphoreType.DMA((nl,)), pltpu.SemaphoreType.DMA((nl,))],
    )(*parts)


def _exchange_chips(chip_sum, name):
    def body(s_ref, q_ref, send, recv):
        x, y, c, me, peers = _place()
        sends = [_remote(s_ref.at[:, 2 * px + py], q_ref.at[:, me], send.at[j], recv.at[j], (px, py, c))
                 for j, (px, py) in enumerate(peers)]
        for cp in sends:
            cp.start()
        for j, (px, py) in enumerate(peers):
            _remote(s_ref.at[:, me], q_ref.at[:, 2 * px + py], send.at[j], recv.at[j], (px, py, c)).wait_recv()
        for cp in sends:
            cp.wait_send()

    return pl.pallas_call(
        body, name=name,
        out_shape=jax.ShapeDtypeStruct(chip_sum.shape, chip_sum.dtype),
        in_specs=[HBM_SPEC], out_specs=HBM_SPEC,
        scratch_shapes=[pltpu.SemaphoreType.DMA((3,)), pltpu.SemaphoreType.DMA((3,))],
    )(chip_sum)


def _join_halves(red, name):
    _, r, _ = red.shape
    rh = r // 2

    def body(r_ref, o_ref, send, recv):
        del r_ref
        x, y, c, _, _ = _place()
        mine = o_ref.at[:, pl.ds(c * rh, rh), :]
        other = o_ref.at[:, pl.ds((1 - c) * rh, rh), :]
        out = _remote(mine, mine, send, recv, (x, y, 1 - c))
        out.start()
        _remote(other, other, send, recv, (x, y, 1 - c)).wait_recv()
        out.wait_send()

    return pl.pallas_call(
        body, name=name,
        out_shape=jax.ShapeDtypeStruct(red.shape, red.dtype),
        in_specs=[HBM_SPEC], out_specs=HBM_SPEC,
        input_output_aliases={0: 0},
        scratch_shapes=[pltpu.SemaphoreType.DMA, pltpu.SemaphoreType.DMA],
    )(red)


def _reduce_scatter(parts, tag):
    theirs = _send_halves(parts, "rs_send_" + tag)
    chip_sum = None
    for l, part in enumerate(parts):
        chip_sum = _add_half(part, theirs, chip_sum, l, "rs_add_" + tag)
    q = _exchange_chips(chip_sum, "rs_xchg_" + tag)
    return _join_halves(_sum_chips(chip_sum, q, "rs_sum_" + tag), "rs_join_" + tag)


COL_SHARDED = ("w_in", "w_a_out", "w_b_out", "w_x_out", "w_up")
ROW_SHARDED = ("w_kv", "w_o", "w_down")
BIG = COL_SHARDED + ROW_SHARDED
SMALL_REPLICATED = ("g_mix_pre", "conv_a_b", "ln_a_g", "ln_a_b", "g_mem", "g_mix_post", "g_mlp_pre", "g_mlp_post")
SMALL_SHARDED = ("conv_a_w", "conv_b_w")
WEIGHT_ORDER = ("g_mix_pre", "w_in", "conv_a_w", "conv_a_b", "ln_a_g", "ln_a_b", "w_a_out", "conv_b_w", "w_b_out",
                "g_mem", "w_kv", "w_x_out", "w_o", "g_mix_post", "g_mlp_pre", "w_up", "w_down", "g_mlp_post")


def _pack_rows(arrays, width):
    rows = []
    for a in arrays:
        r = a.reshape(-1, width)
        rows.append(jnp.pad(r, ((0, (-r.shape[0]) % 8), (0, 0))))
    return jnp.concatenate(rows, axis=0)


def _unpack_rows(packed, like, width):
    out, at = [], 0
    for a in like:
        n = a.size // width
        out.append(packed[at:at + n].reshape(a.shape))
        at += n + (-n) % 8
    return out


def _step(w, m, v, x, mem, target):
    nl = w["w_in"].shape[0]
    d = x.shape[1]
    c = w["conv_a_b"].shape[1]
    ka = w["conv_a_w"].shape[1]

    full = {}
    for name in BIG:
        slots = [_cast_to_slot(w[name], l, "cast_" + name) for l in range(nl)]
        per_layer = _gather_weight(slots, "gather_" + name)
        if name in ROW_SHARDED:
            per_layer = [p.reshape(1, -1, p.shape[-1]) for p in per_layer]
        full[name] = per_layer
    conv_pack = jnp.concatenate([w["conv_a_w"], w["conv_b_w"]], axis=1)
    conv_all = _gather_small(conv_pack, "gather_conv")
    conv_all = jnp.moveaxis(conv_all, 0, 2).reshape(nl, conv_pack.shape[1], c)
    cw_a, cw_b = conv_all[:, :ka], conv_all[:, ka:]

    saved = []
    xl = x
    (h,) = _norm_fwd(x, "norm_first", g_next=w["g_mix_pre"][0])
    dy = loss_cols = None
    for l in range(nl):
        s = {"x": xl, "h": h}
        proj = _mm_nn(h, full["w_in"][l], BF16, "mm_in")
        ca, act_a = _branch_a_fwd(proj, cw_a[l], w["conv_a_b"][l], w["ln_a_g"][l], w["ln_a_b"][l], "branch_a_fwd")
        b_in = _branch_b_fwd(proj, cw_b[l], "branch_b_fwd")
        (mem_n,) = _norm_fwd(mem, "norm_mem", g_next=w["g_mem"][l])
        kv = _mm_nn(mem_n, full["w_kv"][l], BF16, "mm_kv")
        att = _attn_fwd(proj, kv, "attn_fwd")
        ya = _mm_nn(act_a, full["w_a_out"][l], BF16, "mm_a_out")
        yb = _mm_nn(b_in, full["w_b_out"][l], BF16, "mm_b_out")
        yx = _mm_nn(att, full["w_x_out"][l], BF16, "mm_x_out")
        merged = _merge_fwd(proj, ya, yb, yx, "merge_fwd")
        z = _mm_nn(merged, full["w_o"][l], F32, "mm_o")
        x1, h2 = _norm_fwd(xl, "norm_mid", z=z, g_post=w["g_mix_post"][l], g_next=w["g_mlp_pre"][l])
        up = _mm_nn(h2, full["w_up"][l], BF16, "mm_up")
        f = _mm_nn(up, full["w_down"][l], F32, "mm_down", a_act="relu2")
        s.update(proj=proj, ca=ca, act_a=act_a, b_in=b_in, mem_n=mem_n, kv=kv, att=att, ya=ya, yb=yb, yx=yx,
                 merged=merged, z=z, x1=x1, h2=h2, up=up, f=f)
        saved.append(s)
        if l + 1 < nl:
            xl, h = _norm_fwd(x1, "norm_mid", z=f, g_post=w["g_mlp_post"][l], g_next=w["g_mix_pre"][l + 1])
        else:
            dy, loss_cols = _norm_fwd(x1, "norm_loss", z=f, g_post=w["g_mlp_post"][l], target=target)

    part = {name: [None] * nl for name in BIG}
    small = {name: [None] * nl for name in SMALL_REPLICATED + SMALL_SHARDED}
    dxo = dy
    d_f, small["g_mlp_post"][nl - 1] = _norm_bwd("norm_bwd_top", dxo=dy,
                                                 post=(saved[-1]["f"], w["g_mlp_post"][nl - 1]))
    grad_x = None
    for l in reversed(range(nl)):
        s = saved[l]
        d_up = _mm_nt(d_f, full["w_down"][l], BF16, "mm_down_dx", relu2_of=s["up"])
        part["w_down"][l] = _mm_tn(s["up"], d_f, 1, "mm_down_dw", a_act="relu2").reshape(N_CHIPS, -1, d)
        d_h2 = _mm_nt(d_up, full["w_up"][l], F32, "mm_up_dx")
        part["w_up"][l] = _mm_tn(s["h2"], d_up, N_CHIPS, "mm_up_dw")
        dx1, d_z, small["g_mlp_pre"][l], small["g_mix_post"][l] = _norm_bwd(
            "norm_bwd_mid", dxo=dxo, pre=(d_h2, s["x1"], w["g_mlp_pre"][l]), post=(s["z"], w["g_mix_post"][l]))
        d_merged = _mm_nt(d_z, full["w_o"][l], BF16, "mm_o_dx")
        part["w_o"][l] = _mm_tn(s["merged"], d_z, 1, "mm_o_dw").reshape(N_CHIPS, -1, d)
        dproj, d_ya, d_yb, d_yx = _merge_bwd(d_merged, s["proj"], s["ya"], s["yb"], s["yx"], "merge_bwd")
        d_act_a = _mm_nt(d_ya, full["w_a_out"][l], BF16, "mm_a_out_dx")
        part["w_a_out"][l] = _mm_tn(s["act_a"], d_ya, N_CHIPS, "mm_a_out_dw")
        d_b_in = _mm_nt(d_yb, full["w_b_out"][l], BF16, "mm_b_out_dx")
        part["w_b_out"][l] = _mm_tn(s["b_in"], d_yb, N_CHIPS, "mm_b_out_dw")
        d_att = _mm_nt(d_yx, full["w_x_out"][l], BF16, "mm_x_out_dx")
        part["w_x_out"][l] = _mm_tn(s["att"], d_yx, N_CHIPS, "mm_x_out_dw")
        dproj, small["ln_a_g"][l], small["ln_a_b"][l], small["conv_a_b"][l], small["conv_a_w"][l] = _branch_a_bwd(
            dproj, d_act_a, s["ca"], s["proj"], cw_a[l], w["ln_a_g"][l], w["ln_a_b"][l], "branch_a_bwd")
        dproj, small["conv_b_w"][l] = _branch_b_bwd(dproj, d_b_in, s["proj"], cw_b[l], "branch_b_bwd")
        dproj, d_kv = _attn_bwd(dproj, d_att, s["proj"], s["kv"], "attn_bwd")
        part["w_kv"][l] = _mm_tn(s["mem_n"], d_kv, 1, "mm_kv_dw").reshape(N_CHIPS, -1, 2 * c)
        d_mem_n = _mm_nt(d_kv, full["w_kv"][l], F32, "mm_kv_dx")
        (small["g_mem"][l],) = _norm_bwd("norm_bwd_mem", pre=(d_mem_n, mem, w["g_mem"][l]), want_dx=False)
        d_h = _mm_nt(dproj, full["w_in"][l], F32, "mm_in_dx")
        part["w_in"][l] = _mm_tn(s["h"], dproj, N_CHIPS, "mm_in_dw")
        if l > 0:
            dxo, d_f, small["g_mix_pre"][l], small["g_mlp_post"][l - 1] = _norm_bwd(
                "norm_bwd_mid", dxo=dx1, pre=(d_h, s["x"], w["g_mix_pre"][l]),
                post=(saved[l - 1]["f"], w["g_mlp_post"][l - 1]))
        else:
            grad_x, small["g_mix_pre"][0] = _norm_bwd("norm_bwd_last", dxo=dx1,
                                                      pre=(d_h, s["x"], w["g_mix_pre"][0]))

    grads, delta, new_m, new_v = {}, {}, {}, {}
    for name in BIG:
        grads[name] = _reduce_scatter(part[name], name)
        delta[name], new_m[name], new_v[name] = _adamw(w[name], grads[name], m[name], v[name], "adamw_" + name)

    small_names = SMALL_REPLICATED + SMALL_SHARDED
    stacked = [jnp.stack(small[n]) for n in small_names]
    packed = _pack_rows(stacked, c)
    total = _sum_leading(_gather_all(packed, "gather_small_grads")[None], "sum_small_grads")[0]
    reduced = dict(zip(small_names, _unpack_rows(total, stacked, c)))
    chip = 2 * lax.axis_index("x") + lax.axis_index("y")
    cs = c // N_CHIPS
    for name in SMALL_SHARDED:
        grads[name] = lax.dynamic_slice_in_dim(reduced[name], chip * cs, cs, axis=2)
    for name in SMALL_REPLICATED:
        grads[name] = reduced[name].reshape(w[name].shape)
    for group, width in ((SMALL_REPLICATED, c), (SMALL_SHARDED, cs)):
        packs = [_pack_rows([src[n] for n in group], width)[None] for src in (w, grads, m, v)]
        outs = _adamw(*packs, "adamw_small_%d" % width)
        for dst, out in zip((delta, new_m, new_v), outs):
            dst.update(zip(group, _unpack_rows(out[0], [w[n] for n in group], width)))

    loss = lax.psum(0.5 * jnp.sum(loss_cols) / d, ("x", "y", "c"))
    return loss, grad_x, grads, delta, new_m, new_v


def kernel(x, mem, g_mix_pre, w_in, conv_a_w, conv_a_b, ln_a_g, ln_a_b, w_a_out, conv_b_w, w_b_out, g_mem, w_kv, w_x_out, w_o, g_mix_post, g_mlp_pre, w_up, w_down, g_mlp_post, loss_target, m_g_mix_pre, m_w_in, m_conv_a_w, m_conv_a_b, m_ln_a_g, m_ln_a_b, m_w_a_out, m_conv_b_w, m_w_b_out, m_g_mem, m_w_kv, m_w_x_out, m_w_o, m_g_mix_post, m_g_mlp_pre, m_w_up, m_w_down, m_g_mlp_post, v_g_mix_pre, v_w_in, v_conv_a_w, v_conv_a_b, v_ln_a_g, v_ln_a_b, v_w_a_out, v_conv_b_w, v_w_b_out, v_g_mem, v_w_kv, v_w_x_out, v_w_o, v_g_mix_post, v_g_mlp_pre, v_w_up, v_w_down, v_g_mlp_post):
    w = dict(g_mix_pre=g_mix_pre, w_in=w_in, conv_a_w=conv_a_w, conv_a_b=conv_a_b, ln_a_g=ln_a_g, ln_a_b=ln_a_b,
             w_a_out=w_a_out, conv_b_w=conv_b_w, w_b_out=w_b_out, g_mem=g_mem, w_kv=w_kv, w_x_out=w_x_out, w_o=w_o,
             g_mix_post=g_mix_post, g_mlp_pre=g_mlp_pre, w_up=w_up, w_down=w_down, g_mlp_post=g_mlp_post)
    m = dict(g_mix_pre=m_g_mix_pre, w_in=m_w_in, conv_a_w=m_conv_a_w, conv_a_b=m_conv_a_b, ln_a_g=m_ln_a_g,
             ln_a_b=m_ln_a_b, w_a_out=m_w_a_out, conv_b_w=m_conv_b_w, w_b_out=m_w_b_out, g_mem=m_g_mem, w_kv=m_w_kv,
             w_x_out=m_w_x_out, w_o=m_w_o, g_mix_post=m_g_mix_post, g_mlp_pre=m_g_mlp_pre, w_up=m_w_up,
             w_down=m_w_down, g_mlp_post=m_g_mlp_post)
    v = dict(g_mix_pre=v_g_mix_pre, w_in=v_w_in, conv_a_w=v_conv_a_w, conv_a_b=v_conv_a_b, ln_a_g=v_ln_a_g,
             ln_a_b=v_ln_a_b, w_a_out=v_w_a_out, conv_b_w=v_conv_b_w, w_b_out=v_w_b_out, g_mem=v_g_mem, w_kv=v_w_kv,
             w_x_out=v_w_x_out, w_o=v_w_o, g_mix_post=v_g_mix_post, g_mlp_pre=v_g_mlp_pre, w_up=v_w_up,
             w_down=v_w_down, g_mlp_post=v_g_mlp_post)
    loss, grad_x, grads, delta, new_m, new_v = _step(w, m, v, x[0], mem[0], loss_target[0])
    out = [loss, grad_x[None]]
    for group in (grads, delta, new_m, new_v):
        out += [group[n] for n in WEIGHT_ORDER]
    return tuple(out)
```

```python
import functools

import jax
import jax.numpy as jnp
from jax import lax
from jax.experimental import pallas as pl
from jax.experimental.pallas import tpu as pltpu

F32 = jnp.float32
BF16 = jnp.bfloat16
MESH = pl.DeviceIdType.MESH

NORM_EPS = 1e-6
N_HEADS = 4
ADAM_LR = 0.001
ADAM_B1 = 0.9
ADAM_B2 = 0.999
ADAM_EPS = 1e-08
ADAM_WD = 0.01
ADAM_STEP = 10

N_CHIPS = 4
V7X_VMEM_BYTES = 64 * 1024 * 1024
VMEM_CAP = V7X_VMEM_BYTES - 8 * 1024 * 1024
LANE = 128
SUBLANE_BF16 = 16
HALO = 32
ROW_TILE = 256
ATTN_TILE = 512
MM_TM = 1024
MM_TN = 1024
MM_TK = 2048
EW_VMEM_BYTES = 24 * 1024 * 1024


def _tile(n, pref, align):
    if n <= pref:
        return n
    t = (pref // align) * align
    while t >= align:
        if n % t == 0:
            return t
        t -= align
    return n


def _ew_rows(r, c, n_arrays):
    return _tile(r, max(SUBLANE_BF16, EW_VMEM_BYTES // (2 * n_arrays * c * 4)), SUBLANE_BF16)


def _nbytes(shape, dtype):
    n = 1
    for s in shape:
        if s is not None:
            n *= s
    return n * jnp.dtype(dtype).itemsize


def _params(semantics, block_bytes, temp_bytes=0):
    need = 2 * block_bytes + temp_bytes + (4 << 20)
    return pltpu.CompilerParams(dimension_semantics=semantics,
                                vmem_limit_bytes=int(min(max(need, 16 << 20), VMEM_CAP)))


def _sigmoid(v):
    return 1.0 / (1.0 + jnp.exp(-v))


def _mm_nn(a, b3, out_dtype, name, a_act=None, after=None):
    m, k = a.shape
    s, k2, ns = b3.shape
    assert k == k2
    tm = _tile(m, MM_TM, SUBLANE_BF16)
    tn = _tile(ns, MM_TN, LANE)
    tk = _tile(k, MM_TK, LANE)
    q = ns // tn
    nk = k // tk

    n_in = 2 if after is None else 3

    def body(*refs):
        a_ref, b_ref, o_ref = refs[0], refs[1], refs[n_in]
        scratch = refs[n_in + 1:]
        av = a_ref[...]
        if a_act == "relu2":
            r = jnp.maximum(av.astype(F32), 0.0)
            av = r * r
        p = jnp.dot(av.astype(BF16), b_ref[...].astype(BF16), preferred_element_type=F32)
        if nk == 1:
            o_ref[...] = p.astype(o_ref.dtype)
        else:
            acc, = scratch
            kk = pl.program_id(2)

            @pl.when(kk == 0)
            def _():
                acc[...] = p

            @pl.when(kk > 0)
            def _():
                acc[...] += p

            @pl.when(kk == nk - 1)
            def _():
                o_ref[...] = acc[...].astype(o_ref.dtype)

    blocks = (_nbytes((tm, tk), a.dtype) + _nbytes((tk, tn), b3.dtype) + _nbytes((tm, tn), out_dtype))
    return pl.pallas_call(
        body, name=name,
        out_shape=jax.ShapeDtypeStruct((m, s * ns), out_dtype),
        grid=(m // tm, s * q, nk),
        in_specs=[pl.BlockSpec((tm, tk), lambda i, j, c: (i, c)),
                  pl.BlockSpec((None, tk, tn), lambda i, j, c: (j // q, c, j % q))]
        + ([] if after is None else [pl.BlockSpec(memory_space=pl.ANY)]),
        out_specs=pl.BlockSpec((tm, tn), lambda i, j, c: (i, j)),
        scratch_shapes=[pltpu.VMEM((tm, tn), F32)] if nk > 1 else [],
        compiler_params=_params(("parallel", "parallel", "arbitrary"), blocks,
                                3 * _nbytes((tm, tn), F32) + _nbytes((tm, tk), F32)),
    )(*([a, b3] if after is None else [a, b3, after]))


def _mm_nt(a, b3, out_dtype, name, relu2_of=None, after=None):
    m, n = a.shape
    s, kd, ns = b3.shape
    assert n == s * ns
    tm = _tile(m, MM_TM, SUBLANE_BF16)
    tj = _tile(kd, MM_TN, LANE)
    tc = _tile(ns, MM_TK, LANE)
    q = ns // tc
    nc = s * q

    n_in = 2 + (relu2_of is not None) + (after is not None)

    def body(*refs):
        a_ref, b_ref, u_ref, o_ref = refs[0], refs[1], refs[2], refs[n_in]
        scratch = refs[n_in + 1:]

        def finish(p):
            if relu2_of is not None:
                p = p * (2.0 * jnp.maximum(u_ref[...].astype(F32), 0.0))
            o_ref[...] = p.astype(o_ref.dtype)

        p = lax.dot_general(a_ref[...].astype(BF16), b_ref[...].astype(BF16),
                            (((1,), (1,)), ((), ())), preferred_element_type=F32)
        if nc == 1:
            finish(p)
        else:
            acc, = scratch
            cc = pl.program_id(2)

            @pl.when(cc == 0)
            def _():
                acc[...] = p

            @pl.when(cc > 0)
            def _():
                acc[...] += p

            @pl.when(cc == nc - 1)
            def _():
                finish(acc[...])

    in_specs = [pl.BlockSpec((tm, tc), lambda i, j, c: (i, c)),
                pl.BlockSpec((None, tj, tc), lambda i, j, c: (c // q, j, c % q))]
    operands = [a, b3]
    blocks = _nbytes((tm, tc), a.dtype) + _nbytes((tj, tc), b3.dtype) + _nbytes((tm, tj), out_dtype)
    if relu2_of is not None:
        in_specs.append(pl.BlockSpec((tm, tj), lambda i, j, c: (i, j)))
        operands.append(relu2_of)
        blocks += _nbytes((tm, tj), relu2_of.dtype)
    if after is not None:
        in_specs.append(pl.BlockSpec(memory_space=pl.ANY))
        operands.append(after)
    return pl.pallas_call(
        body, name=name,
        out_shape=jax.ShapeDtypeStruct((m, kd), out_dtype),
        grid=(m // tm, kd // tj, nc),
        in_specs=in_specs,
        out_specs=pl.BlockSpec((tm, tj), lambda i, j, c: (i, j)),
        scratch_shapes=[pltpu.VMEM((tm, tj), F32)] if nc > 1 else [],
        compiler_params=_params(("parallel", "parallel", "arbitrary"), blocks,
                                3 * _nbytes((tm, tj), F32)),
    )(*operands)


def _mm_tn(a, g, out_shards, name, a_act=None):
    m, ka = a.shape
    m2, n = g.shape
    assert m == m2
    ns = n // out_shards
    ta = _tile(ka, MM_TM, LANE)
    tn = _tile(ns, MM_TN, LANE)
    tm = _tile(m, MM_TK, SUBLANE_BF16)
    q = ns // tn
    nm = m // tm

    def body(a_ref, g_ref, o_ref, *scratch):
        av = a_ref[...]
        if a_act == "relu2":
            r = jnp.maximum(av.astype(F32), 0.0)
            av = r * r
        p = lax.dot_general(av.astype(BF16), g_ref[...].astype(BF16),
                            (((0,), (0,)), ((), ())), preferred_element_type=F32)
        if nm == 1:
            o_ref[...] = p.astype(o_ref.dtype)
        else:
            acc, = scratch
            cc = pl.program_id(2)

            @pl.when(cc == 0)
            def _():
                acc[...] = p

            @pl.when(cc > 0)
            def _():
                acc[...] += p

            @pl.when(cc == nm - 1)
            def _():
                o_ref[...] = acc[...].astype(o_ref.dtype)

    blocks = _nbytes((tm, ta), a.dtype) + _nbytes((tm, tn), g.dtype) + _nbytes((ta, tn), BF16)
    return pl.pallas_call(
        body, name=name,
        out_shape=jax.ShapeDtypeStruct((out_shards, ka, ns), BF16),
        grid=(ka // ta, out_shards * q, nm),
        in_specs=[pl.BlockSpec((tm, ta), lambda i, j, c: (c, i)),
                  pl.BlockSpec((tm, tn), lambda i, j, c: (c, j))],
        out_specs=pl.BlockSpec((None, ta, tn), lambda i, j, c: (j // q, i, j % q)),
        scratch_shapes=[pltpu.VMEM((ta, tn), F32)] if nm > 1 else [],
        compiler_params=_params(("parallel", "parallel", "arbitrary"), blocks,
                                3 * _nbytes((ta, tn), F32) + _nbytes((tm, ta), F32)),
    )(a, g)


def _rms_scale(v):
    return lax.rsqrt(jnp.mean(v * v, axis=-1, keepdims=True) + NORM_EPS)


def _norm_fwd(x, name, *, z=None, g_post=None, g_next=None, target=None):
    t, d = x.shape
    tr = _tile(t, ROW_TILE, SUBLANE_BF16)
    has_res, has_next, has_loss = z is not None, g_next is not None, target is not None

    def body(*refs):
        it = iter(refs)
        x_ref = next(it)
        z_ref, gp_ref = (next(it), next(it)) if has_res else (None, None)
        gn_ref = next(it) if has_next else None
        t_ref = next(it) if has_loss else None
        xv = x_ref[...]
        if has_res:
            zv = z_ref[...]
            xv = xv + zv * _rms_scale(zv) * gp_ref[...]
            if not has_loss:
                next(it)[...] = xv
        if has_next:
            next(it)[...] = (xv * _rms_scale(xv) * gn_ref[...]).astype(BF16)
        if has_loss:
            e = xv - t_ref[...]
            next(it)[...] = e * (1.0 / d)
            ls_ref = next(it)

            @pl.when(pl.program_id(0) == 0)
            def _():
                ls_ref[...] = jnp.zeros_like(ls_ref)

            ls_ref[...] += jnp.sum(e * e, axis=0, keepdims=True)

    row = pl.BlockSpec((tr, d), lambda i: (i, 0))
    vec = pl.BlockSpec((1, d), lambda i: (0, 0))
    operands, in_specs, out_shape, out_specs = [x], [row], [], []
    if has_res:
        operands += [z, g_post.reshape(1, d)]
        in_specs += [row, vec]
        if not has_loss:
            out_shape.append(jax.ShapeDtypeStruct((t, d), F32))
            out_specs.append(row)
    if has_next:
        operands.append(g_next.reshape(1, d))
        in_specs.append(vec)
        out_shape.append(jax.ShapeDtypeStruct((t, d), BF16))
        out_specs.append(row)
    if has_loss:
        operands.append(target)
        in_specs.append(row)
        out_shape += [jax.ShapeDtypeStruct((t, d), F32), jax.ShapeDtypeStruct((1, d), F32)]
        out_specs += [row, vec]
    return pl.pallas_call(
        body, name=name, out_shape=out_shape, grid=(t // tr,),
        in_specs=in_specs, out_specs=out_specs,
        compiler_params=_params(("arbitrary",), 5 * _nbytes((tr, d), F32), 4 * _nbytes((tr, d), F32)),
    )(*operands)


def _norm_bwd(name, *, dxo=None, pre=None, post=None, want_dx=True):
    ref_arr = dxo if dxo is not None else pre[1]
    t, d = ref_arr.shape
    tr = _tile(t, ROW_TILE, SUBLANE_BF16)
    has_dxo, has_pre, has_post = dxo is not None, pre is not None, post is not None

    def body(*refs):
        it = iter(refs)
        dxo_ref = next(it) if has_dxo else None
        dh_ref, xin_ref, gpre_ref = (next(it), next(it), next(it)) if has_pre else (None,) * 3
        z_ref, gpost_ref = (next(it), next(it)) if has_post else (None, None)
        dx_ref = next(it) if (has_pre and want_dx) else None
        dz_ref = next(it) if has_post else None
        dgpre_ref = next(it) if has_pre else None
        dgpost_ref = next(it) if has_post else None
        first = pl.program_id(0) == 0

        dx = dxo_ref[...] if has_dxo else None
        if has_pre:
            xin = xin_ref[...]
            dh = dh_ref[...].astype(F32)
            r = _rms_scale(xin)
            gy = dh * gpre_ref[...]
            dloc = r * gy - xin * (r * r * r) * jnp.mean(gy * xin, axis=-1, keepdims=True)
            dx = dloc if dx is None else dx + dloc
            if want_dx:
                dx_ref[...] = dx

            @pl.when(first)
            def _():
                dgpre_ref[...] = jnp.zeros_like(dgpre_ref)

            dgpre_ref[...] += jnp.sum(dh * xin * r, axis=0, keepdims=True)
        if has_post:
            zv = z_ref[...]
            r = _rms_scale(zv)
            gy = dx * gpost_ref[...]
            dz = r * gy - zv * (r * r * r) * jnp.mean(gy * zv, axis=-1, keepdims=True)
            dz_ref[...] = dz.astype(BF16)

            @pl.when(first)
            def _():
                dgpost_ref[...] = jnp.zeros_like(dgpost_ref)

            dgpost_ref[...] += jnp.sum(dx * zv * r, axis=0, keepdims=True)

    row = pl.BlockSpec((tr, d), lambda i: (i, 0))
    vec = pl.BlockSpec((1, d), lambda i: (0, 0))
    operands, in_specs, out_shape, out_specs = [], [], [], []
    if has_dxo:
        operands.append(dxo)
        in_specs.append(row)
    if has_pre:
        operands += [pre[0], pre[1], pre[2].reshape(1, d)]
        in_specs += [row, row, vec]
    if has_post:
        operands += [post[0], post[1].reshape(1, d)]
        in_specs += [row, vec]
    if has_pre and want_dx:
        out_shape.append(jax.ShapeDtypeStruct((t, d), F32))
        out_specs.append(row)
    if has_post:
        out_shape.append(jax.ShapeDtypeStruct((t, d), BF16))
        out_specs.append(row)
    if has_pre:
        out_shape.append(jax.ShapeDtypeStruct((1, d), F32))
        out_specs.append(vec)
    if has_post:
        out_shape.append(jax.ShapeDtypeStruct((1, d), F32))
        out_specs.append(vec)
    return pl.pallas_call(
        body, name=name, out_shape=out_shape, grid=(t // tr,),
        in_specs=in_specs, out_specs=out_specs,
        compiler_params=_params(("arbitrary",), 6 * _nbytes((tr, d), F32), 6 * _nbytes((tr, d), F32)),
    )(*operands)


def _seq_tiles(t):
    tr = _tile(t, ROW_TILE, HALO)
    assert tr % HALO == 0 and t % tr == 0
    return tr, t // tr, tr // HALO


def _col(tr, width, cb):
    return pl.BlockSpec((tr, width), lambda i: (i, cb))


def _prev_halo(per, width, cb):
    return pl.BlockSpec((HALO, width), lambda i: (jnp.maximum(i * per - 1, 0), cb))


def _next_halo(per, n_halo, width, cb):
    return pl.BlockSpec((HALO, width), lambda i: (jnp.minimum((i + 1) * per, n_halo - 1), cb))


def _const(shape):
    return pl.BlockSpec(shape, lambda i: (0,) * len(shape))


def _glu(val, gate):
    return val.astype(F32) * _sigmoid(gate.astype(F32))


def _layer_norm_parts(ca):
    mu = jnp.mean(ca, axis=-1, keepdims=True)
    xc = ca - mu
    rs = lax.rsqrt(jnp.mean(xc * xc, axis=-1, keepdims=True) + NORM_EPS)
    return xc * rs, rs


def _branch_a_fwd(proj, cw, cb, lg, lb, name):
    t = proj.shape[0]
    kw, c = cw.shape
    tr, nt, per = _seq_tiles(t)

    def body(av_ref, ag_ref, hv_ref, hg_ref, cw_ref, cb_ref, lg_ref, lb_ref, ca_ref, act_ref, abuf):
        i = pl.program_id(0)
        abuf[pl.ds(0, HALO), :] = jnp.where(i > 0, _glu(hv_ref[...], hg_ref[...]), 0.0)
        abuf[pl.ds(HALO, tr), :] = _glu(av_ref[...], ag_ref[...])
        acc = jnp.zeros((tr, c), F32)
        for k in range(kw):
            acc = acc + cw_ref[pl.ds(k, 1), :] * abuf[pl.ds(HALO - (kw - 1) + k, tr), :]
        ca = acc + cb_ref[...]
        ca_ref[...] = ca
        xh, _ = _layer_norm_parts(ca)
        ln = xh * lg_ref[...] + lb_ref[...]
        act_ref[...] = (ln * _sigmoid(ln)).astype(BF16)

    return pl.pallas_call(
        body, name=name,
        out_shape=[jax.ShapeDtypeStruct((t, c), F32), jax.ShapeDtypeStruct((t, c), BF16)],
        grid=(nt,),
        in_specs=[_col(tr, c, 0), _col(tr, c, 1), _prev_halo(per, c, 0), _prev_halo(per, c, 1),
                  _const((kw, c)), _const((1, c)), _const((1, c)), _const((1, c))],
        out_specs=[_col(tr, c, 0), _col(tr, c, 0)],
        scratch_shapes=[pltpu.VMEM((HALO + tr, c), F32)],
        compiler_params=_params(("arbitrary",), 4 * _nbytes((tr, c), F32), 8 * _nbytes((tr + HALO, c), F32)),
    )(proj, proj, proj, proj, cw, cb.reshape(1, c), lg.reshape(1, c), lb.reshape(1, c))


def _branch_a_bwd(dproj, dact, ca, proj, cw, lg, lb, name):
    t = proj.shape[0]
    kw, c = cw.shape
    tr, nt, per = _seq_tiles(t)
    n_halo = t // HALO

    def body(dproj_in, da_ref, dah_ref, ca_ref, cah_ref, av_ref, ag_ref, hv_ref, hg_ref,
             cw_ref, lg_ref, lb_ref, out_ref, dlg_ref, dlb_ref, dcb_ref, dcw_ref, abuf, dbuf):
        del dproj_in
        i = pl.program_id(0)
        lgv, lbv = lg_ref[...], lb_ref[...]

        def conv_grad(dact_v, ca_v):
            xh, rs = _layer_norm_parts(ca_v)
            ln = xh * lgv + lbv
            sg = _sigmoid(ln)
            dln = dact_v.astype(F32) * (sg * (1.0 + ln * (1.0 - sg)))
            dxh = dln * lgv
            dca = rs * (dxh - jnp.mean(dxh, axis=-1, keepdims=True)
                        - xh * jnp.mean(dxh * xh, axis=-1, keepdims=True))
            return dca, dln, xh

        dca, dln, xh = conv_grad(da_ref[...], ca_ref[...])
        dca_h, _, _ = conv_grad(dah_ref[...], cah_ref[...])
        dbuf[pl.ds(0, tr), :] = dca
        dbuf[pl.ds(tr, HALO), :] = jnp.where(i < nt - 1, dca_h, 0.0)

        @pl.when(i == 0)
        def _():
            dlg_ref[...] = jnp.zeros_like(dlg_ref)
            dlb_ref[...] = jnp.zeros_like(dlb_ref)
            dcb_ref[...] = jnp.zeros_like(dcb_ref)
            dcw_ref[...] = jnp.zeros_like(dcw_ref)

        dlg_ref[...] += jnp.sum(dln * xh, axis=0, keepdims=True)
        dlb_ref[...] += jnp.sum(dln, axis=0, keepdims=True)
        dcb_ref[...] += jnp.sum(dca, axis=0, keepdims=True)

        av = av_ref[...].astype(F32)
        sg = _sigmoid(ag_ref[...].astype(F32))
        abuf[pl.ds(0, HALO), :] = jnp.where(i > 0, _glu(hv_ref[...], hg_ref[...]), 0.0)
        abuf[pl.ds(HALO, tr), :] = av * sg

        d_a = jnp.zeros((tr, c), F32)
        for k in range(kw):
            d_a = d_a + cw_ref[pl.ds(k, 1), :] * dbuf[pl.ds(kw - 1 - k, tr), :]
            dcw_ref[pl.ds(k, 1), :] += jnp.sum(dca * abuf[pl.ds(HALO - (kw - 1) + k, tr), :],
                                               axis=0, keepdims=True)
        out_ref[:, pl.ds(0, c)] = (d_a * sg).astype(BF16)
        out_ref[:, pl.ds(c, c)] = (d_a * av * sg * (1.0 - sg)).astype(BF16)

    vec = _const((1, c))
    return pl.pallas_call(
        body, name=name,
        out_shape=[jax.ShapeDtypeStruct(dproj.shape, BF16)] + [jax.ShapeDtypeStruct((1, c), F32)] * 3
        + [jax.ShapeDtypeStruct((kw, c), F32)],
        grid=(nt,),
        in_specs=[pl.BlockSpec(memory_space=pl.ANY),
                  _col(tr, c, 0), _next_halo(per, n_halo, c, 0),
                  _col(tr, c, 0), _next_halo(per, n_halo, c, 0),
                  _col(tr, c, 0), _col(tr, c, 1), _prev_halo(per, c, 0), _prev_halo(per, c, 1),
                  _const((kw, c)), vec, vec],
        out_specs=[pl.BlockSpec((tr, 2 * c), lambda i: (i, 0)), vec, vec, vec, _const((kw, c))],
        scratch_shapes=[pltpu.VMEM((HALO + tr, c), F32), pltpu.VMEM((HALO + tr, c), F32)],
        input_output_aliases={0: 0},
        compiler_params=_params(("arbitrary",), 6 * _nbytes((tr, c), F32), 12 * _nbytes((tr + HALO, c), F32)),
    )(dproj, dact, dact, ca, ca, proj, proj, proj, proj, cw, lg.reshape(1, c), lb.reshape(1, c))


def _branch_b_fwd(proj, cw, name):
    t = proj.shape[0]
    kw, c = cw.shape
    tr, nt, per = _seq_tiles(t)

    def body(sb_ref, sc_ref, sx_ref, hc_ref, hx_ref, cw_ref, o_ref, pbuf):
        i = pl.program_id(0)
        hp = hc_ref[...].astype(F32) * hx_ref[...].astype(F32)
        pbuf[pl.ds(0, HALO), :] = jnp.where(i > 0, hp, 0.0)
        pbuf[pl.ds(HALO, tr), :] = sc_ref[...].astype(F32) * sx_ref[...].astype(F32)
        u = jnp.zeros((tr, c), F32)
        for k in range(kw):
            u = u + cw_ref[pl.ds(k, 1), :] * pbuf[pl.ds(HALO - (kw - 1) + k, tr), :]
        o_ref[...] = (sb_ref[...].astype(F32) * u).astype(BF16)

    return pl.pallas_call(
        body, name=name,
        out_shape=jax.ShapeDtypeStruct((t, c), BF16),
        grid=(nt,),
        in_specs=[_col(tr, c, 2), _col(tr, c, 3), _col(tr, c, 4),
                  _prev_halo(per, c, 3), _prev_halo(per, c, 4), _const((kw, c))],
        out_specs=_col(tr, c, 0),
        scratch_shapes=[pltpu.VMEM((HALO + tr, c), F32)],
        compiler_params=_params(("arbitrary",), 4 * _nbytes((tr, c), F32), 6 * _nbytes((tr + HALO, c), F32)),
    )(proj, proj, proj, proj, proj, cw)


def _branch_b_bwd(dproj, dbin, proj, cw, name):
    t = proj.shape[0]
    kw, c = cw.shape
    tr, nt, per = _seq_tiles(t)
    n_halo = t // HALO

    def body(dproj_in, db_ref, dbh_ref, sb_ref, sbh_ref, sc_ref, sx_ref, hc_ref, hx_ref, cw_ref,
             out_ref, dcw_ref, pbuf, dubuf):
        del dproj_in
        i = pl.program_id(0)
        j = pl.program_id(1)
        sb = sb_ref[...].astype(F32)
        sc = sc_ref[...].astype(F32)
        sx = sx_ref[...].astype(F32)
        dbin_v = db_ref[...].astype(F32)
        hp = hc_ref[...].astype(F32) * hx_ref[...].astype(F32)
        pbuf[pl.ds(0, HALO), :] = jnp.where(i > 0, hp, 0.0)
        pbuf[pl.ds(HALO, tr), :] = sc * sx
        du = dbin_v * sb
        du_h = dbh_ref[...].astype(F32) * sbh_ref[...].astype(F32)
        dubuf[pl.ds(0, tr), :] = du
        dubuf[pl.ds(tr, HALO), :] = jnp.where(i < nt - 1, du_h, 0.0)

        @pl.when((i == 0) & (j == 0))
        def _():
            dcw_ref[...] = jnp.zeros_like(dcw_ref)

        u = jnp.zeros((tr, c), F32)
        dp = jnp.zeros((tr, c), F32)
        for k in range(kw):
            shifted = pbuf[pl.ds(HALO - (kw - 1) + k, tr), :]
            u = u + cw_ref[pl.ds(k, 1), :] * shifted
            dp = dp + cw_ref[pl.ds(k, 1), :] * dubuf[pl.ds(kw - 1 - k, tr), :]

            @pl.when(j == 0)
            def _():
                dcw_ref[pl.ds(k, 1), :] += jnp.sum(du * shifted, axis=0, keepdims=True)

        res = jnp.where(j == 0, dbin_v * u, jnp.where(j == 1, dp * sx, dp * sc))
        out_ref[...] = res.astype(BF16)

    def colj(cb):
        return pl.BlockSpec((tr, c), lambda i, j: (i, cb))

    def prevj(cb):
        return pl.BlockSpec((HALO, c), lambda i, j: (jnp.maximum(i * per - 1, 0), cb))

    def nextj(cb):
        return pl.BlockSpec((HALO, c), lambda i, j: (jnp.minimum((i + 1) * per, n_halo - 1), cb))

    return pl.pallas_call(
        body, name=name,
        out_shape=[jax.ShapeDtypeStruct(dproj.shape, BF16), jax.ShapeDtypeStruct((kw, c), F32)],
        grid=(nt, 3),
        in_specs=[pl.BlockSpec(memory_space=pl.ANY),
                  colj(0), nextj(0), colj(2), nextj(2), colj(3), colj(4), prevj(3), prevj(4),
                  pl.BlockSpec((kw, c), lambda i, j: (0, 0))],
        out_specs=[pl.BlockSpec((tr, c), lambda i, j: (i, 2 + j)),
                   pl.BlockSpec((kw, c), lambda i, j: (0, 0))],
        scratch_shapes=[pltpu.VMEM((HALO + tr, c), F32), pltpu.VMEM((HALO + tr, c), F32)],
        input_output_aliases={0: 0},
        compiler_params=_params(("arbitrary", "arbitrary"), 6 * _nbytes((tr, c), F32),
                                10 * _nbytes((tr + HALO, c), F32)),
    )(dproj, dbin, dbin, proj, proj, proj, proj, proj, proj, cw)


def _softmax_rows(s):
    e = jnp.exp(s - jnp.max(s, axis=-1, keepdims=True))
    return e / jnp.sum(e, axis=-1, keepdims=True)


def _attn_fwd(proj, kv, name):
    t = proj.shape[0]
    m, c2 = kv.shape
    c = c2 // 2
    hd = c // N_HEADS
    ta = _tile(t, ATTN_TILE, SUBLANE_BF16)
    scale = hd ** -0.5

    def body(q_ref, kv_ref, o_ref):
        for h in range(N_HEADS):
            qh = q_ref[:, pl.ds(h * hd, hd)]
            kh = kv_ref[:, pl.ds(h * hd, hd)]
            vh = kv_ref[:, pl.ds(c + h * hd, hd)]
            s = lax.dot_general(qh, kh, (((1,), (1,)), ((), ())), preferred_element_type=F32) * scale
            p = _softmax_rows(s)
            o_ref[:, pl.ds(h * hd, hd)] = jnp.dot(p.astype(BF16), vh,
                                                  preferred_element_type=F32).astype(BF16)

    return pl.pallas_call(
        body, name=name,
        out_shape=jax.ShapeDtypeStruct((t, c), BF16),
        grid=(t // ta,),
        in_specs=[pl.BlockSpec((ta, c), lambda i: (i, 5)), _const((m, c2))],
        out_specs=pl.BlockSpec((ta, c), lambda i: (i, 0)),
        compiler_params=_params(("parallel",), 2 * _nbytes((ta, c), BF16) + _nbytes((m, c2), BF16),
                                8 * _nbytes((ta, m), F32)),
    )(proj, kv)


def _attn_bwd(dproj, d_o, proj, kv, name):
    t = proj.shape[0]
    m, c2 = kv.shape
    c = c2 // 2
    hd = c // N_HEADS
    ta = _tile(t, ATTN_TILE, SUBLANE_BF16)
    scale = hd ** -0.5

    def body(dproj_in, do_ref, q_ref, kv_ref, dq_ref, dkv_ref):
        del dproj_in

        @pl.when(pl.program_id(0) == 0)
        def _():
            dkv_ref[...] = jnp.zeros_like(dkv_ref)

        for h in range(N_HEADS):
            qh = q_ref[:, pl.ds(h * hd, hd)]
            kh = kv_ref[:, pl.ds(h * hd, hd)]
            vh = kv_ref[:, pl.ds(c + h * hd, hd)]
            doh = do_ref[:, pl.ds(h * hd, hd)]
            s = lax.dot_general(qh, kh, (((1,), (1,)), ((), ())), preferred_element_type=F32) * scale
            p = _softmax_rows(s)
            dp = lax.dot_general(doh, vh, (((1,), (1,)), ((), ())), preferred_element_type=F32)
            ds = (p * (dp - jnp.sum(dp * p, axis=-1, keepdims=True))).astype(BF16)
            dq_ref[:, pl.ds(h * hd, hd)] = (jnp.dot(ds, kh, preferred_element_type=F32) * scale).astype(BF16)
            dkv_ref[:, pl.ds(h * hd, hd)] += lax.dot_general(
                ds, qh, (((0,), (0,)), ((), ())), preferred_element_type=F32) * scale
            dkv_ref[:, pl.ds(c + h * hd, hd)] += lax.dot_general(
                p.astype(BF16), doh, (((0,), (0,)), ((), ())), preferred_element_type=F32)

    return pl.pallas_call(
        body, name=name,
        out_shape=[jax.ShapeDtypeStruct(dproj.shape, BF16), jax.ShapeDtypeStruct((m, c2), F32)],
        grid=(t // ta,),
        in_specs=[pl.BlockSpec(memory_space=pl.ANY),
                  pl.BlockSpec((ta, c), lambda i: (i, 0)), pl.BlockSpec((ta, c), lambda i: (i, 5)),
                  _const((m, c2))],
        out_specs=[pl.BlockSpec((ta, c), lambda i: (i, 5)), _const((m, c2))],
        input_output_aliases={0: 0},
        compiler_params=_params(("arbitrary",), 3 * _nbytes((ta, c), BF16) + 2 * _nbytes((m, c2), F32),
                                10 * _nbytes((ta, m), F32)),
    )(dproj, d_o, proj, kv)


def _merge_fwd(proj, ya, yb, yx, name):
    t, d = ya.shape
    tr = _tile(t, ROW_TILE, SUBLANE_BF16)

    def body(g_ref, ya_ref, yb_ref, yx_ref, o_ref):
        acc = jnp.zeros((tr, d), F32)
        for b, y_ref in enumerate((ya_ref, yb_ref, yx_ref)):
            acc = acc + _sigmoid(g_ref[:, pl.ds(b * d, d)].astype(F32)) * y_ref[...].astype(F32)
        o_ref[...] = acc.astype(BF16)

    row = pl.BlockSpec((tr, d), lambda i: (i, 0))
    return pl.pallas_call(
        body, name=name,
        out_shape=jax.ShapeDtypeStruct((t, d), BF16),
        grid=(t // tr,),
        in_specs=[pl.BlockSpec((tr, 3 * d), lambda i: (i, 1)), row, row, row],
        out_specs=row,
        compiler_params=_params(("parallel",), 7 * _nbytes((tr, d), BF16), 6 * _nbytes((tr, d), F32)),
    )(proj, ya, yb, yx)


def _merge_bwd(dmerged, proj, ya, yb, yx, name):
    t, d = ya.shape
    tr = _tile(t, ROW_TILE, SUBLANE_BF16)

    def body(dm_ref, g_ref, ya_ref, yb_ref, yx_ref, dg_ref, dya_ref, dyb_ref, dyx_ref):
        dm = dm_ref[...].astype(F32)
        for b, (y_ref, dy_ref) in enumerate(((ya_ref, dya_ref), (yb_ref, dyb_ref), (yx_ref, dyx_ref))):
            sg = _sigmoid(g_ref[:, pl.ds(b * d, d)].astype(F32))
            dg_ref[:, pl.ds(b * d, d)] = (dm * y_ref[...].astype(F32) * sg * (1.0 - sg)).astype(BF16)
            dy_ref[...] = (dm * sg).astype(BF16)

    row = pl.BlockSpec((tr, d), lambda i: (i, 0))
    gates = pl.BlockSpec((tr, 3 * d), lambda i: (i, 1))
    return pl.pallas_call(
        body, name=name,
        out_shape=[jax.ShapeDtypeStruct(proj.shape, BF16)] + [jax.ShapeDtypeStruct((t, d), BF16)] * 3,
        grid=(t // tr,),
        in_specs=[row, gates, row, row, row],
        out_specs=[gates, row, row, row],
        compiler_params=_params(("parallel",), 14 * _nbytes((tr, d), BF16), 8 * _nbytes((tr, d), F32)),
    )(dmerged, proj, ya, yb, yx)


def _as3(a):
    return a.reshape((-1,) + a.shape[-2:])


def _ew_call(fn, ins, n_out, out_dtypes, name):
    b, r, c = ins[0].shape
    tr = _ew_rows(r, c, len(ins) + n_out)

    def body(*refs):
        outs = fn(*[ref[...] for ref in refs[:len(ins)]])
        for ref, val in zip(refs[len(ins):], outs):
            ref[...] = val.astype(ref.dtype)

    spec = pl.BlockSpec((None, tr, c), lambda i, j: (i, j, 0))
    return pl.pallas_call(
        body, name=name,
        out_shape=[jax.ShapeDtypeStruct((b, r, c), dt) for dt in out_dtypes],
        grid=(b, r // tr),
        in_specs=[spec] * len(ins), out_specs=[spec] * n_out,
        compiler_params=_params(("parallel", "parallel"), (len(ins) + n_out) * _nbytes((tr, c), F32),
                                6 * _nbytes((tr, c), F32)),
    )(*ins)


def _my_chip():
    return 2 * lax.axis_index("x") + lax.axis_index("y")


def _my_core():
    return lax.axis_index("c")


def _cast_to_slot(w, l, name):
    _, r, cs = w.shape
    tr = _ew_rows(r, cs, 2)

    def body(w_ref, o_ref):
        o_ref[...] = w_ref[...].astype(BF16)

    return pl.pallas_call(
        body, name=name,
        out_shape=jax.ShapeDtypeStruct((N_CHIPS, r, cs), BF16),
        grid=(r // tr,),
        in_specs=[pl.BlockSpec((None, tr, cs), lambda i: (l, i, 0))],
        out_specs=pl.BlockSpec((None, tr, cs), lambda i: (_my_chip(), i, 0)),
        compiler_params=_params(("parallel",), 2 * _nbytes((tr, cs), F32)),
    )(w)


def _add_half(part, theirs, name):
    _, rh, cs = theirs.shape
    tr = _ew_rows(rh, cs, 3)
    nrb = rh // tr

    def body(a_ref, b_ref, o_ref):
        o_ref[...] = (a_ref[...].astype(F32) + b_ref[...].astype(F32)).astype(BF16)

    half = pl.BlockSpec((None, tr, cs), lambda j, i: (j, i, 0))
    return pl.pallas_call(
        body, name=name,
        out_shape=jax.ShapeDtypeStruct((N_CHIPS, rh, cs), BF16),
        grid=(N_CHIPS, nrb),
        in_specs=[pl.BlockSpec((None, tr, cs), lambda j, i: (j, _my_core() * nrb + i, 0)), half],
        out_specs=half,
        compiler_params=_params(("parallel", "parallel"), 3 * _nbytes((tr, cs), F32)),
    )(part, theirs)


def _sum_chips(chip_sum, q, total, l, nl, name):
    _, rh, cs = q.shape
    tr = _ew_rows(rh, cs, 5)
    nrb = rh // tr

    def body(own_ref, q1_ref, q2_ref, q3_ref, *rest):
        acc = own_ref[...].astype(F32)
        for ref in (q1_ref, q2_ref, q3_ref):
            acc = acc + ref[...].astype(F32)
        rest[-1][...] = acc

    def slot(k):
        return pl.BlockSpec((None, tr, cs), lambda i: ((_my_chip() + k) % N_CHIPS, i, 0))

    in_specs = [slot(0), slot(1), slot(2), slot(3)]
    operands = [chip_sum, q, q, q]
    aliases = {}
    if total is not None:
        in_specs.append(HBM_SPEC)
        operands.append(total)
        aliases = {4: 0}
    return pl.pallas_call(
        body, name=name,
        out_shape=jax.ShapeDtypeStruct((nl, 2 * rh, cs), F32),
        grid=(nrb,),
        in_specs=in_specs,
        out_specs=pl.BlockSpec((None, tr, cs), lambda i: (l, _my_core() * nrb + i, 0)),
        input_output_aliases=aliases,
        compiler_params=_params(("parallel",), 5 * _nbytes((tr, cs), F32)),
    )(*operands)


def _adamw(w, g, m, v, name):
    shape = w.shape
    c1 = 1.0 - ADAM_B1 ** ADAM_STEP
    c2 = 1.0 - ADAM_B2 ** ADAM_STEP

    def fn(wv, gv, mv, vv):
        mn = ADAM_B1 * mv + (1.0 - ADAM_B1) * gv
        vn = ADAM_B2 * vv + (1.0 - ADAM_B2) * (gv * gv)
        delta = -ADAM_LR * ((mn / c1) / (jnp.sqrt(vn / c2) + ADAM_EPS) + ADAM_WD * wv)
        return delta, mn, vn

    outs = _ew_call(fn, [_as3(w), _as3(g), _as3(m), _as3(v)], 3, [F32] * 3, name)
    return [o.reshape(shape) for o in outs]


def _sum_leading(q, name):
    b, nj, r, c = q.shape
    tr = _ew_rows(r, c, nj + 1)

    def body(q_ref, o_ref):
        acc = q_ref[0].astype(F32)
        for j in range(1, nj):
            acc = acc + q_ref[j].astype(F32)
        o_ref[...] = acc

    return pl.pallas_call(
        body, name=name,
        out_shape=jax.ShapeDtypeStruct((b, r, c), F32),
        grid=(b, r // tr),
        in_specs=[pl.BlockSpec((None, nj, tr, c), lambda i, j: (i, 0, j, 0))],
        out_specs=pl.BlockSpec((None, tr, c), lambda i, j: (i, j, 0)),
        compiler_params=_params(("parallel", "parallel"), (nj + 1) * _nbytes((tr, c), F32),
                                2 * _nbytes((tr, c), F32)),
    )(q)


HBM_SPEC = pl.BlockSpec(memory_space=pl.ANY)


def _place():
    x, y, c = lax.axis_index("x"), lax.axis_index("y"), lax.axis_index("c")
    peers = [(1 - x, y), (x, 1 - y), (1 - x, 1 - y)]
    return x, y, c, 2 * x + y, peers


def _remote(src, dst, send_sem, recv_sem, dev):
    return pltpu.make_async_remote_copy(src_ref=src, dst_ref=dst, send_sem=send_sem, recv_sem=recv_sem,
                                        device_id=dev, device_id_type=MESH)


IN_HBM = pl.BlockSpec(memory_space=pltpu.HBM)
IN_SEM = pl.BlockSpec(memory_space=pltpu.SEMAPHORE)
DATAFLOW = pltpu.SideEffectType.DATAFLOW_SIDE_EFFECTING
TOKEN = jax.ShapeDtypeStruct((8, LANE), F32)


def _in_hbm(arrays):
    return [pltpu.with_memory_space_constraint(a, pltpu.HBM) for a in arrays]


def _half_rows(ref, c):
    rh = ref.shape[1] // 2
    return pl.ds(c * rh, rh)


def _split_start(body_copies, passed, landing, name):
    n_pass, n_land = len(passed), len(landing)
    n_arr = n_pass + n_land
    n = 3 * n_pass

    def body(*refs):
        ins = refs[:n_arr]
        send, recv = refs[n_arr], refs[n_arr + 1]
        token = refs[-1]
        for src, dst, _, s_sem, r_sem, dev in body_copies(ins[:n_pass], ins[n_pass:], send, recv):
            _remote(src, dst, s_sem, r_sem, dev).start()
        token[...] = jnp.zeros_like(token)

    arrays = list(passed) + list(landing)
    return pl.pallas_call(
        body, name=name,
        out_shape=(pltpu.SemaphoreType.DMA((n,)), pltpu.SemaphoreType.DMA((n,)),
                   *[pltpu.HBM(a.shape, a.dtype) for a in arrays], TOKEN),
        in_specs=[IN_HBM] * n_arr,
        out_specs=(IN_SEM, IN_SEM, *[IN_HBM] * n_arr, pl.BlockSpec(memory_space=pltpu.VMEM)),
        input_output_aliases={i: 2 + i for i in range(n_arr)},
        compiler_params=pltpu.CompilerParams(has_side_effects=DATAFLOW),
    )(*_in_hbm(arrays))


def _split_wait(body_copies, send, recv, passed, landing, after, name):
    n_pass, n_land = len(passed), len(landing)
    n_arr = n_pass + n_land

    def body(*refs):
        ins = refs[:n_arr]
        send_ref, recv_ref = refs[n_arr], refs[n_arr + 1]
        for src, _, landed, s_sem, r_sem, dev in body_copies(ins[:n_pass], ins[n_pass:], send_ref, recv_ref):
            cp = _remote(src, landed, s_sem, r_sem, dev)
            cp.wait_send()
            cp.wait_recv()

    arrays = list(passed) + list(landing)
    return pl.pallas_call(
        body, name=name,
        out_shape=tuple(pltpu.HBM(a.shape, a.dtype) for a in arrays),
        in_specs=[IN_HBM] * n_arr + [IN_SEM, IN_SEM, HBM_SPEC],
        out_specs=tuple([IN_HBM] * n_arr),
        input_output_aliases={i: i for i in range(n_arr)},
        compiler_params=pltpu.CompilerParams(has_side_effects=DATAFLOW),
    )(*arrays, send, recv, after)


def _gather_copies(slots, _, send, recv):
    _, _, c, me, peers = _place()
    copies = []
    for a, ref in enumerate(slots):
        rows = _half_rows(ref, c)
        own = ref.at[me, rows, :]
        for j, (px, py) in enumerate(peers):
            k = 3 * a + j
            copies.append((own, own, ref.at[2 * px + py, rows, :], send.at[k], recv.at[k], (px, py, c)))
    return copies


def _exchange_copies(chip_sums, landing, send, recv):
    _, _, c, me, peers = _place()
    copies = []
    for a, (s_ref, q_ref) in enumerate(zip(chip_sums, landing)):
        for j, (px, py) in enumerate(peers):
            k = 3 * a + j
            copies.append((s_ref.at[2 * px + py], q_ref.at[me], q_ref.at[2 * px + py],
                           send.at[k], recv.at[k], (px, py, c)))
    return copies


def _gather_start(slots, name):
    out = _split_start(_gather_copies, slots, [], name)
    return out[0], out[1], list(out[2:-1]), out[-1]


def _gather_wait(send, recv, slots, after, name):
    return list(_split_wait(_gather_copies, send, recv, slots, [], after, name))


def _gather_forward(slots, name):
    na = len(slots)
    n = 3 * na

    def body(*refs):
        outs = refs[na:2 * na]
        fsend, frecv = refs[2 * na:]
        x, y, c, _, peers = _place()
        forwards = []
        for a, ref in enumerate(outs):
            for j, (px, py) in enumerate(peers):
                k = 3 * a + j
                landed = ref.at[2 * px + py, _half_rows(ref, c), :]
                forwards.append(_remote(landed, landed, fsend.at[k], frecv.at[k], (x, y, 1 - c)))
                forwards[-1].start()
        for a, ref in enumerate(outs):
            for j, (px, py) in enumerate(peers):
                k = 3 * a + j
                landed = ref.at[2 * px + py, _half_rows(ref, 1 - c), :]
                _remote(landed, landed, fsend.at[k], frecv.at[k], (x, y, 1 - c)).wait_recv()
        for cp in forwards:
            cp.wait_send()

    return pl.pallas_call(
        body, name=name,
        out_shape=[jax.ShapeDtypeStruct(s.shape, s.dtype) for s in slots],
        in_specs=[HBM_SPEC] * na, out_specs=[HBM_SPEC] * na,
        input_output_aliases={a: a for a in range(na)},
        scratch_shapes=[pltpu.SemaphoreType.DMA((n,))] * 2,
    )(*slots)


def _gather_small(v, name):
    def body(v_ref, o_ref, send, recv, lsem):
        x, y, c, me, peers = _place()
        local = pltpu.make_async_copy(v_ref, o_ref.at[me], lsem)
        local.start()
        sends = [_remote(v_ref, o_ref.at[me], send.at[j], recv.at[j], (px, py, c))
                 for j, (px, py) in enumerate(peers)]
        for cp in sends:
            cp.start()
        for j, (px, py) in enumerate(peers):
            _remote(v_ref, o_ref.at[2 * px + py], send.at[j], recv.at[j], (px, py, c)).wait_recv()
        for cp in sends:
            cp.wait_send()
        local.wait()

    return pl.pallas_call(
        body, name=name,
        out_shape=jax.ShapeDtypeStruct((N_CHIPS,) + v.shape, v.dtype),
        in_specs=[HBM_SPEC], out_specs=HBM_SPEC,
        scratch_shapes=[pltpu.SemaphoreType.DMA((3,)), pltpu.SemaphoreType.DMA((3,)), pltpu.SemaphoreType.DMA],
    )(v)


def _gather_all(v, name):
    def body(v_ref, o_ref, send, recv, lsem):
        x, y, c = lax.axis_index("x"), lax.axis_index("y"), lax.axis_index("c")
        me = 4 * x + 2 * y + c
        local = pltpu.make_async_copy(v_ref, o_ref.at[me], lsem)
        local.start()
        flips = [(fx, fy, fc) for fx in (0, 1) for fy in (0, 1) for fc in (0, 1)][1:]
        peers = [(x ^ fx, y ^ fy, c ^ fc) for fx, fy, fc in flips]
        sends = [_remote(v_ref, o_ref.at[me], send.at[k], recv.at[k], dev) for k, dev in enumerate(peers)]
        for cp in sends:
            cp.start()
        for k, (px, py, pc) in enumerate(peers):
            _remote(v_ref, o_ref.at[4 * px + 2 * py + pc], send.at[k], recv.at[k], (px, py, pc)).wait_recv()
        for cp in sends:
            cp.wait_send()
        local.wait()

    return pl.pallas_call(
        body, name=name,
        out_shape=jax.ShapeDtypeStruct((8,) + v.shape, v.dtype),
        in_specs=[HBM_SPEC], out_specs=HBM_SPEC,
        scratch_shapes=[pltpu.SemaphoreType.DMA((7,)), pltpu.SemaphoreType.DMA((7,)), pltpu.SemaphoreType.DMA],
    )(v)


def _send_halves(parts, name):
    na = len(parts)

    def body(*refs):
        ins, outs = refs[:na], refs[na:2 * na]
        send, recv = refs[2 * na:]
        x, y, c, _, _ = _place()
        sends = [_remote(ins[a].at[:, _half_rows(ins[a], 1 - c), :], outs[a], send.at[a], recv.at[a],
                         (x, y, 1 - c)) for a in range(na)]
        for cp in sends:
            cp.start()
        for cp in sends:
            cp.wait()

    return pl.pallas_call(
        body, name=name,
        out_shape=[jax.ShapeDtypeStruct((N_CHIPS, p.shape[1] // 2, p.shape[2]), p.dtype) for p in parts],
        in_specs=[HBM_SPEC] * na, out_specs=[HBM_SPEC] * na,
        scratch_shapes=[pltpu.SemaphoreType.DMA((na,)), pltpu.SemaphoreType.DMA((na,))],
    )(*parts)


def _exchange_start(chip_sums, name):
    na = len(chip_sums)
    landing = [lax.empty(s.shape, s.dtype) for s in chip_sums]
    out = _split_start(_exchange_copies, chip_sums, landing, name)
    return out[0], out[1], list(out[2:2 + na]), list(out[2 + na:2 + 2 * na]), out[-1]


def _exchange_wait(send, recv, chip_sums, landing, after, name):
    na = len(chip_sums)
    out = _split_wait(_exchange_copies, send, recv, chip_sums, landing, after, name)
    return list(out[:na]), list(out[na:])


def _join_halves(reds, name):
    na = len(reds)

    def body(*refs):
        outs = refs[na:2 * na]
        send, recv = refs[2 * na:]
        x, y, c, _, _ = _place()
        sends = []
        for a, ref in enumerate(outs):
            rh = ref.shape[1] // 2
            mine = ref.at[:, pl.ds(c * rh, rh), :]
            sends.append(_remote(mine, mine, send.at[a], recv.at[a], (x, y, 1 - c)))
            sends[-1].start()
        for a, ref in enumerate(outs):
            rh = ref.shape[1] // 2
            other = ref.at[:, pl.ds((1 - c) * rh, rh), :]
            _remote(other, other, send.at[a], recv.at[a], (x, y, 1 - c)).wait_recv()
        for cp in sends:
            cp.wait_send()

    return pl.pallas_call(
        body, name=name,
        out_shape=[jax.ShapeDtypeStruct(r.shape, r.dtype) for r in reds],
        in_specs=[HBM_SPEC] * na, out_specs=[HBM_SPEC] * na,
        input_output_aliases={a: a for a in range(na)},
        scratch_shapes=[pltpu.SemaphoreType.DMA((na,)), pltpu.SemaphoreType.DMA((na,))],
    )(*reds)


COL_SHARDED = ("w_in", "w_a_out", "w_b_out", "w_x_out", "w_up")
ROW_SHARDED = ("w_kv", "w_o", "w_down")
BIG = COL_SHARDED + ROW_SHARDED
SMALL_REPLICATED = ("g_mix_pre", "conv_a_b", "ln_a_g", "ln_a_b", "g_mem", "g_mix_post", "g_mlp_pre", "g_mlp_post")
SMALL_SHARDED = ("conv_a_w", "conv_b_w")
WEIGHT_ORDER = ("g_mix_pre", "w_in", "conv_a_w", "conv_a_b", "ln_a_g", "ln_a_b", "w_a_out", "conv_b_w", "w_b_out",
                "g_mem", "w_kv", "w_x_out", "w_o", "g_mix_post", "g_mlp_pre", "w_up", "w_down", "g_mlp_post")


def _pack_rows(arrays, width):
    rows = []
    for a in arrays:
        r = a.reshape(-1, width)
        rows.append(jnp.pad(r, ((0, (-r.shape[0]) % 8), (0, 0))))
    return jnp.concatenate(rows, axis=0)


def _unpack_rows(packed, like, width):
    out, at = [], 0
    for a in like:
        n = a.size // width
        out.append(packed[at:at + n].reshape(a.shape))
        at += n + (-n) % 8
    return out


def _step(w, m, v, x, mem, target):
    nl = w["w_in"].shape[0]
    d = x.shape[1]
    c = w["conv_a_b"].shape[1]
    ka = w["conv_a_w"].shape[1]

    slots = {name: [_cast_to_slot(w[name], l, "cast_" + name) for l in range(nl)] for name in BIG}

    def start_gather(l):
        return _gather_start([slots[name][l] for name in BIG], "gather_start_%d" % l)

    def finish_gather(l, pending, after):
        send, recv, thru, _ = pending
        landed = _gather_wait(send, recv, thru, after, "gather_wait_%d" % l)
        arrays = _gather_forward(landed, "gather_forward")
        return {name: (a.reshape(1, -1, a.shape[-1]) if name in ROW_SHARDED else a)
                for name, a in zip(BIG, arrays)}

    pending = start_gather(0)
    full = [finish_gather(0, pending, pending[3])]
    conv_pack = jnp.concatenate([w["conv_a_w"], w["conv_b_w"]], axis=1)
    conv_all = _gather_small(conv_pack, "gather_conv")
    conv_all = jnp.moveaxis(conv_all, 0, 2).reshape(nl, conv_pack.shape[1], c)
    cw_a, cw_b = conv_all[:, :ka], conv_all[:, ka:]

    saved = []
    xl = x
    (h,) = _norm_fwd(x, "norm_first", g_next=w["g_mix_pre"][0])
    dy = loss_cols = None
    for l in range(nl):
        s = {"x": xl, "h": h}
        fw = full[l]
        token = None
        if l + 1 < nl:
            pending = start_gather(l + 1)
            token = pending[3]
        proj = _mm_nn(h, fw["w_in"], BF16, "mm_in", after=token)
        ca, act_a = _branch_a_fwd(proj, cw_a[l], w["conv_a_b"][l], w["ln_a_g"][l], w["ln_a_b"][l], "branch_a_fwd")
        b_in = _branch_b_fwd(proj, cw_b[l], "branch_b_fwd")
        (mem_n,) = _norm_fwd(mem, "norm_mem", g_next=w["g_mem"][l])
        kv = _mm_nn(mem_n, fw["w_kv"], BF16, "mm_kv")
        att = _attn_fwd(proj, kv, "attn_fwd")
        ya = _mm_nn(act_a, fw["w_a_out"], BF16, "mm_a_out")
        yb = _mm_nn(b_in, fw["w_b_out"], BF16, "mm_b_out")
        yx = _mm_nn(att, fw["w_x_out"], BF16, "mm_x_out")
        merged = _merge_fwd(proj, ya, yb, yx, "merge_fwd")
        z = _mm_nn(merged, fw["w_o"], F32, "mm_o")
        x1, h2 = _norm_fwd(xl, "norm_mid", z=z, g_post=w["g_mix_post"][l], g_next=w["g_mlp_pre"][l])
        up = _mm_nn(h2, fw["w_up"], BF16, "mm_up")
        f = _mm_nn(up, fw["w_down"], F32, "mm_down", a_act="relu2")
        s.update(proj=proj, ca=ca, act_a=act_a, b_in=b_in, mem_n=mem_n, kv=kv, att=att, ya=ya, yb=yb, yx=yx,
                 merged=merged, z=z, x1=x1, h2=h2, up=up, f=f)
        saved.append(s)
        if l + 1 < nl:
            full.append(finish_gather(l + 1, pending, f))
        if l + 1 < nl:
            xl, h = _norm_fwd(x1, "norm_mid", z=f, g_post=w["g_mlp_post"][l], g_next=w["g_mix_pre"][l + 1])
        else:
            dy, loss_cols = _norm_fwd(x1, "norm_loss", z=f, g_post=w["g_mlp_post"][l], target=target)

    part = {name: [None] * nl for name in BIG}
    small = {name: [None] * nl for name in SMALL_REPLICATED + SMALL_SHARDED}
    dxo = dy
    d_f, small["g_mlp_post"][nl - 1] = _norm_bwd("norm_bwd_top", dxo=dy,
                                                 post=(saved[-1]["f"], w["g_mlp_post"][nl - 1]))
    grad_x = None
    totals = {name: None for name in BIG}

    def start_exchange(l):
        parts = [part[name][l] for name in BIG]
        theirs = _send_halves(parts, "rs_send")
        chip_sums = [_add_half(p, t, "rs_add_" + name) for name, p, t in zip(BIG, parts, theirs)]
        return _exchange_start(chip_sums, "rs_xchg_start_%d" % l)

    def finish_exchange(l, pending, after):
        send, recv, chip_sums, landing, _ = pending
        chip_sums, q = _exchange_wait(send, recv, chip_sums, landing, after, "rs_xchg_wait_%d" % l)
        for name, own, got in zip(BIG, chip_sums, q):
            totals[name] = _sum_chips(own, got, totals[name], l, nl, "rs_sum_" + name)

    exchange = None
    for l in reversed(range(nl)):
        s = saved[l]
        fw = full[l]
        d_up = _mm_nt(d_f, fw["w_down"], BF16, "mm_down_dx", relu2_of=s["up"],
                      after=None if exchange is None else exchange[4])
        part["w_down"][l] = _mm_tn(s["up"], d_f, 1, "mm_down_dw", a_act="relu2").reshape(N_CHIPS, -1, d)
        d_h2 = _mm_nt(d_up, fw["w_up"], F32, "mm_up_dx")
        part["w_up"][l] = _mm_tn(s["h2"], d_up, N_CHIPS, "mm_up_dw")
        dx1, d_z, small["g_mlp_pre"][l], small["g_mix_post"][l] = _norm_bwd(
            "norm_bwd_mid", dxo=dxo, pre=(d_h2, s["x1"], w["g_mlp_pre"][l]), post=(s["z"], w["g_mix_post"][l]))
        d_merged = _mm_nt(d_z, fw["w_o"], BF16, "mm_o_dx")
        part["w_o"][l] = _mm_tn(s["merged"], d_z, 1, "mm_o_dw").reshape(N_CHIPS, -1, d)
        dproj, d_ya, d_yb, d_yx = _merge_bwd(d_merged, s["proj"], s["ya"], s["yb"], s["yx"], "merge_bwd")
        d_act_a = _mm_nt(d_ya, fw["w_a_out"], BF16, "mm_a_out_dx")
        part["w_a_out"][l] = _mm_tn(s["act_a"], d_ya, N_CHIPS, "mm_a_out_dw")
        d_b_in = _mm_nt(d_yb, fw["w_b_out"], BF16, "mm_b_out_dx")
        part["w_b_out"][l] = _mm_tn(s["b_in"], d_yb, N_CHIPS, "mm_b_out_dw")
        d_att = _mm_nt(d_yx, fw["w_x_out"], BF16, "mm_x_out_dx")
        part["w_x_out"][l] = _mm_tn(s["att"], d_yx, N_CHIPS, "mm_x_out_dw")
        dproj, small["ln_a_g"][l], small["ln_a_b"][l], small["conv_a_b"][l], small["conv_a_w"][l] = _branch_a_bwd(
            dproj, d_act_a, s["ca"], s["proj"], cw_a[l], w["ln_a_g"][l], w["ln_a_b"][l], "branch_a_bwd")
        dproj, small["conv_b_w"][l] = _branch_b_bwd(dproj, d_b_in, s["proj"], cw_b[l], "branch_b_bwd")
        dproj, d_kv = _attn_bwd(dproj, d_att, s["proj"], s["kv"], "attn_bwd")
        part["w_kv"][l] = _mm_tn(s["mem_n"], d_kv, 1, "mm_kv_dw").reshape(N_CHIPS, -1, 2 * c)
        d_mem_n = _mm_nt(d_kv, fw["w_kv"], F32, "mm_kv_dx")
        (small["g_mem"][l],) = _norm_bwd("norm_bwd_mem", pre=(d_mem_n, mem, w["g_mem"][l]), want_dx=False)
        d_h = _mm_nt(dproj, fw["w_in"], F32, "mm_in_dx")
        part["w_in"][l] = _mm_tn(s["h"], dproj, N_CHIPS, "mm_in_dw")
        if exchange is not None:
            finish_exchange(l + 1, exchange, d_h)
        exchange = start_exchange(l)
        if l > 0:
            dxo, d_f, small["g_mix_pre"][l], small["g_mlp_post"][l - 1] = _norm_bwd(
                "norm_bwd_mid", dxo=dx1, pre=(d_h, s["x"], w["g_mix_pre"][l]),
                post=(saved[l - 1]["f"], w["g_mlp_post"][l - 1]))
        else:
            grad_x, small["g_mix_pre"][0] = _norm_bwd("norm_bwd_last", dxo=dx1,
                                                      pre=(d_h, s["x"], w["g_mix_pre"][0]))
    finish_exchange(0, exchange, grad_x)

    grads, delta, new_m, new_v = {}, {}, {}, {}
    grads.update(zip(BIG, _join_halves([totals[name] for name in BIG], "rs_join")))
    for name in BIG:
        delta[name], new_m[name], new_v[name] = _adamw(w[name], grads[name], m[name], v[name], "adamw_" + name)

    small_names = SMALL_REPLICATED + SMALL_SHARDED
    stacked = [jnp.stack(small[n]) for n in small_names]
    packed = _pack_rows(stacked, c)
    total = _sum_leading(_gather_all(packed, "gather_small_grads")[None], "sum_small_grads")[0]
    reduced = dict(zip(small_names, _unpack_rows(total, stacked, c)))
    chip = 2 * lax.axis_index("x") + lax.axis_index("y")
    cs = c // N_CHIPS
    for name in SMALL_SHARDED:
        grads[name] = lax.dynamic_slice_in_dim(reduced[name], chip * cs, cs, axis=2)
    for name in SMALL_REPLICATED:
        grads[name] = reduced[name].reshape(w[name].shape)
    for group, width in ((SMALL_REPLICATED, c), (SMALL_SHARDED, cs)):
        packs = [_pack_rows([src[n] for n in group], width)[None] for src in (w, grads, m, v)]
        outs = _adamw(*packs, "adamw_small_%d" % width)
        for dst, out in zip((delta, new_m, new_v), outs):
            dst.update(zip(group, _unpack_rows(out[0], [w[n] for n in group], width)))

    loss = lax.psum(0.5 * jnp.sum(loss_cols) / d, ("x", "y", "c"))
    return loss, grad_x, grads, delta, new_m, new_v


def kernel(x, mem, g_mix_pre, w_in, conv_a_w, conv_a_b, ln_a_g, ln_a_b, w_a_out, conv_b_w, w_b_out, g_mem, w_kv, w_x_out, w_o, g_mix_post, g_mlp_pre, w_up, w_down, g_mlp_post, loss_target, m_g_mix_pre, m_w_in, m_conv_a_w, m_conv_a_b, m_ln_a_g, m_ln_a_b, m_w_a_out, m_conv_b_w, m_w_b_out, m_g_mem, m_w_kv, m_w_x_out, m_w_o, m_g_mix_post, m_g_mlp_pre, m_w_up, m_w_down, m_g_mlp_post, v_g_mix_pre, v_w_in, v_conv_a_w, v_conv_a_b, v_ln_a_g, v_ln_a_b, v_w_a_out, v_conv_b_w, v_w_b_out, v_g_mem, v_w_kv, v_w_x_out, v_w_o, v_g_mix_post, v_g_mlp_pre, v_w_up, v_w_down, v_g_mlp_post):
    w = dict(g_mix_pre=g_mix_pre, w_in=w_in, conv_a_w=conv_a_w, conv_a_b=conv_a_b, ln_a_g=ln_a_g, ln_a_b=ln_a_b,
             w_a_out=w_a_out, conv_b_w=conv_b_w, w_b_out=w_b_out, g_mem=g_mem, w_kv=w_kv, w_x_out=w_x_out, w_o=w_o,
             g_mix_post=g_mix_post, g_mlp_pre=g_mlp_pre, w_up=w_up, w_down=w_down, g_mlp_post=g_mlp_post)
    m = dict(g_mix_pre=m_g_mix_pre, w_in=m_w_in, conv_a_w=m_conv_a_w, conv_a_b=m_conv_a_b, ln_a_g=m_ln_a_g,
             ln_a_b=m_ln_a_b, w_a_out=m_w_a_out, conv_b_w=m_conv_b_w, w_b_out=m_w_b_out, g_mem=m_g_mem, w_kv=m_w_kv,
             w_x_out=m_w_x_out, w_o=m_w_o, g_mix_post=m_g_mix_post, g_mlp_pre=m_g_mlp_pre, w_up=m_w_up,
             w_down=m_w_down, g_mlp_post=m_g_mlp_post)
    v = dict(g_mix_pre=v_g_mix_pre, w_in=v_w_in, conv_a_w=v_conv_a_w, conv_a_b=v_conv_a_b, ln_a_g=v_ln_a_g,
             ln_a_b=v_ln_a_b, w_a_out=v_w_a_out, conv_b_w=v_conv_b_w, w_b_out=v_w_b_out, g_mem=v_g_mem, w_kv=v_w_kv,
             w_x_out=v_w_x_out, w_o=v_w_o, g_mix_post=v_g_mix_post, g_mlp_pre=v_g_mlp_pre, w_up=v_w_up,
             w_down=v_w_down, g_mlp_post=v_g_mlp_post)
    loss, grad_x, grads, delta, new_m, new_v = _step(w, m, v, x[0], mem[0], loss_target[0])
    out = [loss, grad_x[None]]
    for group in (grads, delta, new_m, new_v):
        out += [group[n] for n in WEIGHT_ORDER]
    return tuple(out)
```

```python
import functools

import jax
import jax.numpy as jnp
from jax import lax
from jax.experimental import pallas as pl
from jax.experimental.pallas import tpu as pltpu

F32 = jnp.float32
BF16 = jnp.bfloat16
MESH = pl.DeviceIdType.MESH

NORM_EPS = 1e-6
N_HEADS = 4
ADAM_LR = 0.001
ADAM_B1 = 0.9
ADAM_B2 = 0.999
ADAM_EPS = 1e-08
ADAM_WD = 0.01
ADAM_STEP = 10

N_CHIPS = 4
V7X_VMEM_BYTES = 64 * 1024 * 1024
VMEM_CAP = V7X_VMEM_BYTES - 8 * 1024 * 1024
LANE = 128
SUBLANE_BF16 = 16
HALO = 32
ROW_TILE = 256
CONV_ROWS = 32
CONV_LANES = 512
ATTN_TILE = 512
MM_TM = 1024
MM_TN = 1024
MM_TK = 2048
EW_VMEM_BYTES = 24 * 1024 * 1024


def _tile(n, pref, align):
    if n <= pref:
        return n
    t = (pref // align) * align
    while t >= align:
        if n % t == 0:
            return t
        t -= align
    return n


def _ew_rows(r, c, n_arrays):
    return _tile(r, max(SUBLANE_BF16, EW_VMEM_BYTES // (2 * n_arrays * c * 4)), SUBLANE_BF16)


def _nbytes(shape, dtype):
    n = 1
    for s in shape:
        if s is not None:
            n *= s
    return n * jnp.dtype(dtype).itemsize


def _params(semantics, block_bytes, temp_bytes=0):
    need = 2 * block_bytes + temp_bytes + (4 << 20)
    return pltpu.CompilerParams(dimension_semantics=semantics,
                                vmem_limit_bytes=int(min(max(need, 16 << 20), VMEM_CAP)))


def _sigmoid(v):
    return 1.0 / (1.0 + jnp.exp(-v))


def _pcall(body, **kwargs):
    call = pl.pallas_call(body, **kwargs)
    return lambda *operands: call(*[pltpu.with_memory_space_constraint(o, pltpu.HBM) for o in operands])


def _mm_nn(a, b3, out_dtype, name, a_act=None, after=None):
    m, k = a.shape
    s, k2, ns = b3.shape
    assert k == k2
    tm = _tile(m, MM_TM, SUBLANE_BF16)
    tn = _tile(ns, MM_TN, LANE)
    tk = _tile(k, MM_TK, LANE)
    q = ns // tn
    nk = k // tk

    n_in = 2 if after is None else 3

    def body(*refs):
        a_ref, b_ref, o_ref = refs[0], refs[1], refs[n_in]
        scratch = refs[n_in + 1:]
        av = a_ref[...]
        if a_act == "relu2":
            r = jnp.maximum(av.astype(F32), 0.0)
            av = r * r
        p = jnp.dot(av.astype(BF16), b_ref[...].astype(BF16), preferred_element_type=F32)
        if nk == 1:
            o_ref[...] = p.astype(o_ref.dtype)
        else:
            acc, = scratch
            kk = pl.program_id(2)

            @pl.when(kk == 0)
            def _():
                acc[...] = p

            @pl.when(kk > 0)
            def _():
                acc[...] += p

            @pl.when(kk == nk - 1)
            def _():
                o_ref[...] = acc[...].astype(o_ref.dtype)

    blocks = (_nbytes((tm, tk), a.dtype) + _nbytes((tk, tn), b3.dtype) + _nbytes((tm, tn), out_dtype))
    return _pcall(
        body, name=name,
        out_shape=jax.ShapeDtypeStruct((m, s * ns), out_dtype),
        grid=(m // tm, s * q, nk),
        in_specs=[pl.BlockSpec((tm, tk), lambda i, j, c: (i, c)),
                  pl.BlockSpec((None, tk, tn), lambda i, j, c: (j // q, c, j % q))]
        + ([] if after is None else [pl.BlockSpec(memory_space=pl.ANY)]),
        out_specs=pl.BlockSpec((tm, tn), lambda i, j, c: (i, j)),
        scratch_shapes=[pltpu.VMEM((tm, tn), F32)] if nk > 1 else [],
        compiler_params=_params(("parallel", "parallel", "arbitrary"), blocks,
                                3 * _nbytes((tm, tn), F32) + _nbytes((tm, tk), F32)),
    )(*([a, b3] if after is None else [a, b3, after]))


def _mm_nt(a, b3, out_dtype, name, relu2_of=None, after=None):
    m, n = a.shape
    s, kd, ns = b3.shape
    assert n == s * ns
    tm = _tile(m, MM_TM, SUBLANE_BF16)
    tj = _tile(kd, MM_TN, LANE)
    tc = _tile(ns, MM_TK, LANE)
    q = ns // tc
    nc = s * q

    n_in = 2 + (relu2_of is not None) + (after is not None)

    def body(*refs):
        a_ref, b_ref, u_ref, o_ref = refs[0], refs[1], refs[2], refs[n_in]
        scratch = refs[n_in + 1:]

        def finish(p):
            if relu2_of is not None:
                p = p * (2.0 * jnp.maximum(u_ref[...].astype(F32), 0.0))
            o_ref[...] = p.astype(o_ref.dtype)

        p = lax.dot_general(a_ref[...].astype(BF16), b_ref[...].astype(BF16),
                            (((1,), (1,)), ((), ())), preferred_element_type=F32)
        if nc == 1:
            finish(p)
        else:
            acc, = scratch
            cc = pl.program_id(2)

            @pl.when(cc == 0)
            def _():
                acc[...] = p

            @pl.when(cc > 0)
            def _():
                acc[...] += p

            @pl.when(cc == nc - 1)
            def _():
                finish(acc[...])

    in_specs = [pl.BlockSpec((tm, tc), lambda i, j, c: (i, c)),
                pl.BlockSpec((None, tj, tc), lambda i, j, c: (c // q, j, c % q))]
    operands = [a, b3]
    blocks = _nbytes((tm, tc), a.dtype) + _nbytes((tj, tc), b3.dtype) + _nbytes((tm, tj), out_dtype)
    if relu2_of is not None:
        in_specs.append(pl.BlockSpec((tm, tj), lambda i, j, c: (i, j)))
        operands.append(relu2_of)
        blocks += _nbytes((tm, tj), relu2_of.dtype)
    if after is not None:
        in_specs.append(pl.BlockSpec(memory_space=pl.ANY))
        operands.append(after)
    return _pcall(
        body, name=name,
        out_shape=jax.ShapeDtypeStruct((m, kd), out_dtype),
        grid=(m // tm, kd // tj, nc),
        in_specs=in_specs,
        out_specs=pl.BlockSpec((tm, tj), lambda i, j, c: (i, j)),
        scratch_shapes=[pltpu.VMEM((tm, tj), F32)] if nc > 1 else [],
        compiler_params=_params(("parallel", "parallel", "arbitrary"), blocks,
                                3 * _nbytes((tm, tj), F32)),
    )(*operands)


def _mm_tn(a, g, out_shards, name, a_act=None):
    m, ka = a.shape
    m2, n = g.shape
    assert m == m2
    ns = n // out_shards
    ta = _tile(ka, MM_TM, LANE)
    tn = _tile(ns, MM_TN, LANE)
    tm = _tile(m, MM_TK, SUBLANE_BF16)
    q = ns // tn
    nm = m // tm

    def body(a_ref, g_ref, o_ref, *scratch):
        av = a_ref[...]
        if a_act == "relu2":
            r = jnp.maximum(av.astype(F32), 0.0)
            av = r * r
        p = lax.dot_general(av.astype(BF16), g_ref[...].astype(BF16),
                            (((0,), (0,)), ((), ())), preferred_element_type=F32)
        if nm == 1:
            o_ref[...] = p.astype(o_ref.dtype)
        else:
            acc, = scratch
            cc = pl.program_id(2)

            @pl.when(cc == 0)
            def _():
                acc[...] = p

            @pl.when(cc > 0)
            def _():
                acc[...] += p

            @pl.when(cc == nm - 1)
            def _():
                o_ref[...] = acc[...].astype(o_ref.dtype)

    blocks = _nbytes((tm, ta), a.dtype) + _nbytes((tm, tn), g.dtype) + _nbytes((ta, tn), BF16)
    return _pcall(
        body, name=name,
        out_shape=jax.ShapeDtypeStruct((out_shards, ka, ns), BF16),
        grid=(ka // ta, out_shards * q, nm),
        in_specs=[pl.BlockSpec((tm, ta), lambda i, j, c: (c, i)),
                  pl.BlockSpec((tm, tn), lambda i, j, c: (c, j))],
        out_specs=pl.BlockSpec((None, ta, tn), lambda i, j, c: (j // q, i, j % q)),
        scratch_shapes=[pltpu.VMEM((ta, tn), F32)] if nm > 1 else [],
        compiler_params=_params(("parallel", "parallel", "arbitrary"), blocks,
                                3 * _nbytes((ta, tn), F32) + _nbytes((tm, ta), F32)),
    )(a, g)


def _rms_scale(v):
    return lax.rsqrt(jnp.mean(v * v, axis=-1, keepdims=True) + NORM_EPS)


def _norm_fwd(x, name, *, z=None, g_post=None, g_next=None, target=None):
    t, d = x.shape
    tr = _tile(t, ROW_TILE, SUBLANE_BF16)
    has_res, has_next, has_loss = z is not None, g_next is not None, target is not None

    def body(*refs):
        it = iter(refs)
        x_ref = next(it)
        z_ref, gp_ref = (next(it), next(it)) if has_res else (None, None)
        gn_ref = next(it) if has_next else None
        t_ref = next(it) if has_loss else None
        xv = x_ref[...]
        if has_res:
            zv = z_ref[...]
            xv = xv + zv * _rms_scale(zv) * gp_ref[...]
            if not has_loss:
                next(it)[...] = xv
        if has_next:
            next(it)[...] = (xv * _rms_scale(xv) * gn_ref[...]).astype(BF16)
        if has_loss:
            e = xv - t_ref[...]
            next(it)[...] = e * (1.0 / d)
            ls_ref = next(it)

            @pl.when(pl.program_id(0) == 0)
            def _():
                ls_ref[...] = jnp.zeros_like(ls_ref)

            ls_ref[...] += jnp.sum(e * e, axis=0, keepdims=True)

    row = pl.BlockSpec((tr, d), lambda i: (i, 0))
    vec = pl.BlockSpec((1, d), lambda i: (0, 0))
    operands, in_specs, out_shape, out_specs = [x], [row], [], []
    if has_res:
        operands += [z, g_post.reshape(1, d)]
        in_specs += [row, vec]
        if not has_loss:
            out_shape.append(jax.ShapeDtypeStruct((t, d), F32))
            out_specs.append(row)
    if has_next:
        operands.append(g_next.reshape(1, d))
        in_specs.append(vec)
        out_shape.append(jax.ShapeDtypeStruct((t, d), BF16))
        out_specs.append(row)
    if has_loss:
        operands.append(target)
        in_specs.append(row)
        out_shape += [jax.ShapeDtypeStruct((t, d), F32), jax.ShapeDtypeStruct((1, d), F32)]
        out_specs += [row, vec]
    return _pcall(
        body, name=name, out_shape=out_shape, grid=(t // tr,),
        in_specs=in_specs, out_specs=out_specs,
        compiler_params=_params(("arbitrary",), 5 * _nbytes((tr, d), F32), 4 * _nbytes((tr, d), F32)),
    )(*operands)


def _norm_bwd(name, *, dxo=None, pre=None, post=None, want_dx=True):
    ref_arr = dxo if dxo is not None else pre[1]
    t, d = ref_arr.shape
    tr = _tile(t, ROW_TILE, SUBLANE_BF16)
    has_dxo, has_pre, has_post = dxo is not None, pre is not None, post is not None

    def body(*refs):
        it = iter(refs)
        dxo_ref = next(it) if has_dxo else None
        dh_ref, xin_ref, gpre_ref = (next(it), next(it), next(it)) if has_pre else (None,) * 3
        z_ref, gpost_ref = (next(it), next(it)) if has_post else (None, None)
        dx_ref = next(it) if (has_pre and want_dx) else None
        dz_ref = next(it) if has_post else None
        dgpre_ref = next(it) if has_pre else None
        dgpost_ref = next(it) if has_post else None
        first = pl.program_id(0) == 0

        dx = dxo_ref[...] if has_dxo else None
        if has_pre:
            xin = xin_ref[...]
            dh = dh_ref[...].astype(F32)
            r = _rms_scale(xin)
            gy = dh * gpre_ref[...]
            dloc = r * gy - xin * (r * r * r) * jnp.mean(gy * xin, axis=-1, keepdims=True)
            dx = dloc if dx is None else dx + dloc
            if want_dx:
                dx_ref[...] = dx

            @pl.when(first)
            def _():
                dgpre_ref[...] = jnp.zeros_like(dgpre_ref)

            dgpre_ref[...] += jnp.sum(dh * xin * r, axis=0, keepdims=True)
        if has_post:
            zv = z_ref[...]
            r = _rms_scale(zv)
            gy = dx * gpost_ref[...]
            dz = r * gy - zv * (r * r * r) * jnp.mean(gy * zv, axis=-1, keepdims=True)
            dz_ref[...] = dz.astype(BF16)

            @pl.when(first)
            def _():
                dgpost_ref[...] = jnp.zeros_like(dgpost_ref)

            dgpost_ref[...] += jnp.sum(dx * zv * r, axis=0, keepdims=True)

    row = pl.BlockSpec((tr, d), lambda i: (i, 0))
    vec = pl.BlockSpec((1, d), lambda i: (0, 0))
    operands, in_specs, out_shape, out_specs = [], [], [], []
    if has_dxo:
        operands.append(dxo)
        in_specs.append(row)
    if has_pre:
        operands += [pre[0], pre[1], pre[2].reshape(1, d)]
        in_specs += [row, row, vec]
    if has_post:
        operands += [post[0], post[1].reshape(1, d)]
        in_specs += [row, vec]
    if has_pre and want_dx:
        out_shape.append(jax.ShapeDtypeStruct((t, d), F32))
        out_specs.append(row)
    if has_post:
        out_shape.append(jax.ShapeDtypeStruct((t, d), BF16))
        out_specs.append(row)
    if has_pre:
        out_shape.append(jax.ShapeDtypeStruct((1, d), F32))
        out_specs.append(vec)
    if has_post:
        out_shape.append(jax.ShapeDtypeStruct((1, d), F32))
        out_specs.append(vec)
    return _pcall(
        body, name=name, out_shape=out_shape, grid=(t // tr,),
        in_specs=in_specs, out_specs=out_specs,
        compiler_params=_params(("arbitrary",), 6 * _nbytes((tr, d), F32), 6 * _nbytes((tr, d), F32)),
    )(*operands)


def _seq_tiles(t):
    tr = _tile(t, ROW_TILE, HALO)
    assert tr % HALO == 0 and t % tr == 0
    return tr, t // tr, tr // HALO


def _col(tr, width, cb):
    return pl.BlockSpec((tr, width), lambda i: (i, cb))


def _prev_halo(per, width, cb):
    return pl.BlockSpec((HALO, width), lambda i: (jnp.maximum(i * per - 1, 0), cb))


def _next_halo(per, n_halo, width, cb):
    return pl.BlockSpec((HALO, width), lambda i: (jnp.minimum((i + 1) * per, n_halo - 1), cb))


def _const(shape):
    return pl.BlockSpec(shape, lambda i: (0,) * len(shape))


def _glu(val, gate):
    return val.astype(F32) * _sigmoid(gate.astype(F32))


def _conv_chunks(tr, c):
    lanes = min(CONV_LANES, c)
    return [(r0, pl.ds(c0, lanes)) for r0 in range(0, tr, CONV_ROWS) for c0 in range(0, c, lanes)]


def _tap_sum(w_ref, buf, r0, cols, first, step, kw):
    acc = jnp.zeros((CONV_ROWS, cols.size), F32)
    for k in range(kw):
        acc = acc + w_ref[pl.ds(k, 1), cols] * buf[pl.ds(first + step * k + r0, CONV_ROWS), cols]
    return acc


def _layer_norm_parts(ca):
    mu = jnp.mean(ca, axis=-1, keepdims=True)
    xc = ca - mu
    rs = lax.rsqrt(jnp.mean(xc * xc, axis=-1, keepdims=True) + NORM_EPS)
    return xc * rs, rs


def _branch_a_fwd(proj, cw, cb, lg, lb, name):
    t = proj.shape[0]
    kw, c = cw.shape
    tr, nt, per = _seq_tiles(t)

    def body(av_ref, ag_ref, hv_ref, hg_ref, cw_ref, cb_ref, lg_ref, lb_ref, ca_ref, act_ref, abuf):
        i = pl.program_id(0)
        abuf[pl.ds(0, HALO), :] = jnp.where(i > 0, _glu(hv_ref[...], hg_ref[...]), 0.0)
        abuf[pl.ds(HALO, tr), :] = _glu(av_ref[...], ag_ref[...])
        for r0, cols in _conv_chunks(tr, c):
            ca_ref[pl.ds(r0, CONV_ROWS), cols] = (
                _tap_sum(cw_ref, abuf, r0, cols, HALO - (kw - 1), 1, kw) + cb_ref[:, cols])
        xh, _ = _layer_norm_parts(ca_ref[...])
        ln = xh * lg_ref[...] + lb_ref[...]
        act_ref[...] = (ln * _sigmoid(ln)).astype(BF16)

    return _pcall(
        body, name=name,
        out_shape=[jax.ShapeDtypeStruct((t, c), F32), jax.ShapeDtypeStruct((t, c), BF16)],
        grid=(nt,),
        in_specs=[_col(tr, c, 0), _col(tr, c, 1), _prev_halo(per, c, 0), _prev_halo(per, c, 1),
                  _const((kw, c)), _const((1, c)), _const((1, c)), _const((1, c))],
        out_specs=[_col(tr, c, 0), _col(tr, c, 0)],
        scratch_shapes=[pltpu.VMEM((HALO + tr, c), F32)],
        compiler_params=_params(("arbitrary",), 4 * _nbytes((tr, c), F32), 8 * _nbytes((tr + HALO, c), F32)),
    )(proj, proj, proj, proj, cw, cb.reshape(1, c), lg.reshape(1, c), lb.reshape(1, c))


def _branch_a_bwd(dproj, dact, ca, proj, cw, lg, lb, name):
    t = proj.shape[0]
    kw, c = cw.shape
    tr, nt, per = _seq_tiles(t)
    n_halo = t // HALO

    def body(dproj_in, da_ref, dah_ref, ca_ref, cah_ref, av_ref, ag_ref, hv_ref, hg_ref,
             cw_ref, lg_ref, lb_ref, out_ref, dlg_ref, dlb_ref, dcb_ref, dcw_ref, abuf, dbuf, sgbuf):
        del dproj_in
        i = pl.program_id(0)
        lgv, lbv = lg_ref[...], lb_ref[...]

        def conv_grad(dact_v, ca_v):
            xh, rs = _layer_norm_parts(ca_v)
            ln = xh * lgv + lbv
            sg = _sigmoid(ln)
            dln = dact_v.astype(F32) * (sg * (1.0 + ln * (1.0 - sg)))
            dxh = dln * lgv
            dca = rs * (dxh - jnp.mean(dxh, axis=-1, keepdims=True)
                        - xh * jnp.mean(dxh * xh, axis=-1, keepdims=True))
            return dca, dln, xh

        dca, dln, xh = conv_grad(da_ref[...], ca_ref[...])
        dca_h, _, _ = conv_grad(dah_ref[...], cah_ref[...])
        dbuf[pl.ds(0, tr), :] = dca
        dbuf[pl.ds(tr, HALO), :] = jnp.where(i < nt - 1, dca_h, 0.0)

        @pl.when(i == 0)
        def _():
            dlg_ref[...] = jnp.zeros_like(dlg_ref)
            dlb_ref[...] = jnp.zeros_like(dlb_ref)
            dcb_ref[...] = jnp.zeros_like(dcb_ref)
            dcw_ref[...] = jnp.zeros_like(dcw_ref)

        dlg_ref[...] += jnp.sum(dln * xh, axis=0, keepdims=True)
        dlb_ref[...] += jnp.sum(dln, axis=0, keepdims=True)
        dcb_ref[...] += jnp.sum(dca, axis=0, keepdims=True)

        sg = _sigmoid(ag_ref[...].astype(F32))
        sgbuf[...] = sg
        abuf[pl.ds(0, HALO), :] = jnp.where(i > 0, _glu(hv_ref[...], hg_ref[...]), 0.0)
        abuf[pl.ds(HALO, tr), :] = av_ref[...].astype(F32) * sg

        for r0, cols in _conv_chunks(tr, c):
            rows = pl.ds(r0, CONV_ROWS)
            d_a = _tap_sum(cw_ref, dbuf, r0, cols, kw - 1, -1, kw)
            sgc = sgbuf[rows, cols]
            out_ref[rows, cols] = (d_a * sgc).astype(BF16)
            out_ref[rows, pl.ds(c + cols.start, cols.size)] = (
                d_a * abuf[pl.ds(HALO + r0, CONV_ROWS), cols] * (1.0 - sgc)).astype(BF16)
        for _, cols in _conv_chunks(CONV_ROWS, c):
            for k in range(kw):
                acc = jnp.zeros((CONV_ROWS, cols.size), F32)
                for r0 in range(0, tr, CONV_ROWS):
                    acc = acc + (dbuf[pl.ds(r0, CONV_ROWS), cols]
                                 * abuf[pl.ds(HALO - (kw - 1) + k + r0, CONV_ROWS), cols])
                dcw_ref[pl.ds(k, 1), cols] += jnp.sum(acc, axis=0, keepdims=True)

    vec = _const((1, c))
    return _pcall(
        body, name=name,
        out_shape=[jax.ShapeDtypeStruct(dproj.shape, BF16)] + [jax.ShapeDtypeStruct((1, c), F32)] * 3
        + [jax.ShapeDtypeStruct((kw, c), F32)],
        grid=(nt,),
        in_specs=[pl.BlockSpec(memory_space=pl.ANY),
                  _col(tr, c, 0), _next_halo(per, n_halo, c, 0),
                  _col(tr, c, 0), _next_halo(per, n_halo, c, 0),
                  _col(tr, c, 0), _col(tr, c, 1), _prev_halo(per, c, 0), _prev_halo(per, c, 1),
                  _const((kw, c)), vec, vec],
        out_specs=[pl.BlockSpec((tr, 2 * c), lambda i: (i, 0)), vec, vec, vec, _const((kw, c))],
        scratch_shapes=[pltpu.VMEM((HALO + tr, c), F32), pltpu.VMEM((HALO + tr, c), F32),
                        pltpu.VMEM((tr, c), F32)],
        input_output_aliases={0: 0},
        compiler_params=_params(("arbitrary",), 6 * _nbytes((tr, c), F32), 12 * _nbytes((tr + HALO, c), F32)),
    )(dproj, dact, dact, ca, ca, proj, proj, proj, proj, cw, lg.reshape(1, c), lb.reshape(1, c))


def _branch_b_fwd(proj, cw, name):
    t = proj.shape[0]
    kw, c = cw.shape
    tr, nt, per = _seq_tiles(t)

    def body(sb_ref, sc_ref, sx_ref, hc_ref, hx_ref, cw_ref, o_ref, pbuf):
        i = pl.program_id(0)
        hp = hc_ref[...].astype(F32) * hx_ref[...].astype(F32)
        pbuf[pl.ds(0, HALO), :] = jnp.where(i > 0, hp, 0.0)
        pbuf[pl.ds(HALO, tr), :] = sc_ref[...].astype(F32) * sx_ref[...].astype(F32)
        u = jnp.zeros((tr, c), F32)
        for k in range(kw):
            u = u + cw_ref[pl.ds(k, 1), :] * pbuf[pl.ds(HALO - (kw - 1) + k, tr), :]
        o_ref[...] = (sb_ref[...].astype(F32) * u).astype(BF16)

    return _pcall(
        body, name=name,
        out_shape=jax.ShapeDtypeStruct((t, c), BF16),
        grid=(nt,),
        in_specs=[_col(tr, c, 2), _col(tr, c, 3), _col(tr, c, 4),
                  _prev_halo(per, c, 3), _prev_halo(per, c, 4), _const((kw, c))],
        out_specs=_col(tr, c, 0),
        scratch_shapes=[pltpu.VMEM((HALO + tr, c), F32)],
        compiler_params=_params(("arbitrary",), 4 * _nbytes((tr, c), F32), 6 * _nbytes((tr + HALO, c), F32)),
    )(proj, proj, proj, proj, proj, cw)


def _branch_b_bwd(dproj, dbin, proj, cw, name):
    t = proj.shape[0]
    kw, c = cw.shape
    tr, nt, per = _seq_tiles(t)
    n_halo = t // HALO

    def body(dproj_in, db_ref, dbh_ref, sb_ref, sbh_ref, sc_ref, sx_ref, hc_ref, hx_ref, cw_ref,
             out_ref, dcw_ref, pbuf, dubuf):
        del dproj_in
        i = pl.program_id(0)
        j = pl.program_id(1)
        sb = sb_ref[...].astype(F32)
        sc = sc_ref[...].astype(F32)
        sx = sx_ref[...].astype(F32)
        dbin_v = db_ref[...].astype(F32)
        hp = hc_ref[...].astype(F32) * hx_ref[...].astype(F32)
        pbuf[pl.ds(0, HALO), :] = jnp.where(i > 0, hp, 0.0)
        pbuf[pl.ds(HALO, tr), :] = sc * sx
        du = dbin_v * sb
        du_h = dbh_ref[...].astype(F32) * sbh_ref[...].astype(F32)
        dubuf[pl.ds(0, tr), :] = du
        dubuf[pl.ds(tr, HALO), :] = jnp.where(i < nt - 1, du_h, 0.0)

        @pl.when((i == 0) & (j == 0))
        def _():
            dcw_ref[...] = jnp.zeros_like(dcw_ref)

        u = jnp.zeros((tr, c), F32)
        dp = jnp.zeros((tr, c), F32)
        for k in range(kw):
            shifted = pbuf[pl.ds(HALO - (kw - 1) + k, tr), :]
            u = u + cw_ref[pl.ds(k, 1), :] * shifted
            dp = dp + cw_ref[pl.ds(k, 1), :] * dubuf[pl.ds(kw - 1 - k, tr), :]

            @pl.when(j == 0)
            def _():
                dcw_ref[pl.ds(k, 1), :] += jnp.sum(du * shifted, axis=0, keepdims=True)

        res = jnp.where(j == 0, dbin_v * u, jnp.where(j == 1, dp * sx, dp * sc))
        out_ref[...] = res.astype(BF16)

    def colj(cb):
        return pl.BlockSpec((tr, c), lambda i, j: (i, cb))

    def prevj(cb):
        return pl.BlockSpec((HALO, c), lambda i, j: (jnp.maximum(i * per - 1, 0), cb))

    def nextj(cb):
        return pl.BlockSpec((HALO, c), lambda i, j: (jnp.minimum((i + 1) * per, n_halo - 1), cb))

    return _pcall(
        body, name=name,
        out_shape=[jax.ShapeDtypeStruct(dproj.shape, BF16), jax.ShapeDtypeStruct((kw, c), F32)],
        grid=(nt, 3),
        in_specs=[pl.BlockSpec(memory_space=pl.ANY),
                  colj(0), nextj(0), colj(2), nextj(2), colj(3), colj(4), prevj(3), prevj(4),
                  pl.BlockSpec((kw, c), lambda i, j: (0, 0))],
        out_specs=[pl.BlockSpec((tr, c), lambda i, j: (i, 2 + j)),
                   pl.BlockSpec((kw, c), lambda i, j: (0, 0))],
        scratch_shapes=[pltpu.VMEM((HALO + tr, c), F32), pltpu.VMEM((HALO + tr, c), F32)],
        input_output_aliases={0: 0},
        compiler_params=_params(("arbitrary", "arbitrary"), 6 * _nbytes((tr, c), F32),
                                10 * _nbytes((tr + HALO, c), F32)),
    )(dproj, dbin, dbin, proj, proj, proj, proj, proj, proj, cw)


def _softmax_rows(s):
    e = jnp.exp(s - jnp.max(s, axis=-1, keepdims=True))
    return e / jnp.sum(e, axis=-1, keepdims=True)


def _attn_fwd(proj, kv, name):
    t = proj.shape[0]
    m, c2 = kv.shape
    c = c2 // 2
    hd = c // N_HEADS
    ta = _tile(t, ATTN_TILE, SUBLANE_BF16)
    scale = hd ** -0.5

    def body(q_ref, kv_ref, o_ref):
        for h in range(N_HEADS):
            qh = q_ref[:, pl.ds(h * hd, hd)]
            kh = kv_ref[:, pl.ds(h * hd, hd)]
            vh = kv_ref[:, pl.ds(c + h * hd, hd)]
            s = lax.dot_general(qh, kh, (((1,), (1,)), ((), ())), preferred_element_type=F32) * scale
            p = _softmax_rows(s)
            o_ref[:, pl.ds(h * hd, hd)] = jnp.dot(p.astype(BF16), vh,
                                                  preferred_element_type=F32).astype(BF16)

    return _pcall(
        body, name=name,
        out_shape=jax.ShapeDtypeStruct((t, c), BF16),
        grid=(t // ta,),
        in_specs=[pl.BlockSpec((ta, c), lambda i: (i, 5)), _const((m, c2))],
        out_specs=pl.BlockSpec((ta, c), lambda i: (i, 0)),
        compiler_params=_params(("parallel",), 2 * _nbytes((ta, c), BF16) + _nbytes((m, c2), BF16),
                                8 * _nbytes((ta, m), F32)),
    )(proj, kv)


def _attn_bwd(dproj, d_o, proj, kv, name):
    t = proj.shape[0]
    m, c2 = kv.shape
    c = c2 // 2
    hd = c // N_HEADS
    ta = _tile(t, ATTN_TILE, SUBLANE_BF16)
    scale = hd ** -0.5

    def body(dproj_in, do_ref, q_ref, kv_ref, dq_ref, dkv_ref):
        del dproj_in

        @pl.when(pl.program_id(0) == 0)
        def _():
            dkv_ref[...] = jnp.zeros_like(dkv_ref)

        for h in range(N_HEADS):
            qh = q_ref[:, pl.ds(h * hd, hd)]
            kh = kv_ref[:, pl.ds(h * hd, hd)]
            vh = kv_ref[:, pl.ds(c + h * hd, hd)]
            doh = do_ref[:, pl.ds(h * hd, hd)]
            s = lax.dot_general(qh, kh, (((1,), (1,)), ((), ())), preferred_element_type=F32) * scale
            p = _softmax_rows(s)
            dp = lax.dot_general(doh, vh, (((1,), (1,)), ((), ())), preferred_element_type=F32)
            ds = (p * (dp - jnp.sum(dp * p, axis=-1, keepdims=True))).astype(BF16)
            dq_ref[:, pl.ds(h * hd, hd)] = (jnp.dot(ds, kh, preferred_element_type=F32) * scale).astype(BF16)
            dkv_ref[:, pl.ds(h * hd, hd)] += lax.dot_general(
                ds, qh, (((0,), (0,)), ((), ())), preferred_element_type=F32) * scale
            dkv_ref[:, pl.ds(c + h * hd, hd)] += lax.dot_general(
                p.astype(BF16), doh, (((0,), (0,)), ((), ())), preferred_element_type=F32)

    return _pcall(
        body, name=name,
        out_shape=[jax.ShapeDtypeStruct(dproj.shape, BF16), jax.ShapeDtypeStruct((m, c2), F32)],
        grid=(t // ta,),
        in_specs=[pl.BlockSpec(memory_space=pl.ANY),
                  pl.BlockSpec((ta, c), lambda i: (i, 0)), pl.BlockSpec((ta, c), lambda i: (i, 5)),
                  _const((m, c2))],
        out_specs=[pl.BlockSpec((ta, c), lambda i: (i, 5)), _const((m, c2))],
        input_output_aliases={0: 0},
        compiler_params=_params(("arbitrary",), 3 * _nbytes((ta, c), BF16) + 2 * _nbytes((m, c2), F32),
                                10 * _nbytes((ta, m), F32)),
    )(dproj, d_o, proj, kv)


def _merge_fwd(proj, ya, yb, yx, name):
    t, d = ya.shape
    tr = _tile(t, ROW_TILE, SUBLANE_BF16)

    def body(g_ref, ya_ref, yb_ref, yx_ref, o_ref):
        acc = jnp.zeros((tr, d), F32)
        for b, y_ref in enumerate((ya_ref, yb_ref, yx_ref)):
            acc = acc + _sigmoid(g_ref[:, pl.ds(b * d, d)].astype(F32)) * y_ref[...].astype(F32)
        o_ref[...] = acc.astype(BF16)

    row = pl.BlockSpec((tr, d), lambda i: (i, 0))
    return _pcall(
        body, name=name,
        out_shape=jax.ShapeDtypeStruct((t, d), BF16),
        grid=(t // tr,),
        in_specs=[pl.BlockSpec((tr, 3 * d), lambda i: (i, 1)), row, row, row],
        out_specs=row,
        compiler_params=_params(("parallel",), 7 * _nbytes((tr, d), BF16), 6 * _nbytes((tr, d), F32)),
    )(proj, ya, yb, yx)


def _merge_bwd(dmerged, proj, ya, yb, yx, name):
    t, d = ya.shape
    tr = _tile(t, ROW_TILE, SUBLANE_BF16)

    def body(dm_ref, g_ref, ya_ref, yb_ref, yx_ref, dg_ref, dya_ref, dyb_ref, dyx_ref):
        dm = dm_ref[...].astype(F32)
        for b, (y_ref, dy_ref) in enumerate(((ya_ref, dya_ref), (yb_ref, dyb_ref), (yx_ref, dyx_ref))):
            sg = _sigmoid(g_ref[:, pl.ds(b * d, d)].astype(F32))
            dg_ref[:, pl.ds(b * d, d)] = (dm * y_ref[...].astype(F32) * sg * (1.0 - sg)).astype(BF16)
            dy_ref[...] = (dm * sg).astype(BF16)

    row = pl.BlockSpec((tr, d), lambda i: (i, 0))
    gates = pl.BlockSpec((tr, 3 * d), lambda i: (i, 1))
    return _pcall(
        body, name=name,
        out_shape=[jax.ShapeDtypeStruct(proj.shape, BF16)] + [jax.ShapeDtypeStruct((t, d), BF16)] * 3,
        grid=(t // tr,),
        in_specs=[row, gates, row, row, row],
        out_specs=[gates, row, row, row],
        compiler_params=_params(("parallel",), 14 * _nbytes((tr, d), BF16), 8 * _nbytes((tr, d), F32)),
    )(dmerged, proj, ya, yb, yx)


def _as3(a):
    return a.reshape((-1,) + a.shape[-2:])


def _ew_call(fn, ins, n_out, out_dtypes, name, batch=None, prev=()):
    b, r, c = ins[0].shape
    lo, hi = batch if batch is not None else (0, b)
    tr = _ew_rows(r, c, len(ins) + n_out)
    n_in = len(ins) + len(prev)

    def body(*refs):
        outs = fn(*[ref[...] for ref in refs[:len(ins)]])
        for ref, val in zip(refs[n_in:], outs):
            ref[...] = val.astype(ref.dtype)

    spec = pl.BlockSpec((None, tr, c), lambda i, j: (i + lo, j, 0))
    return _pcall(
        body, name=name,
        out_shape=[jax.ShapeDtypeStruct((b, r, c), dt) for dt in out_dtypes],
        grid=(hi - lo, r // tr),
        in_specs=[spec] * len(ins) + [pl.BlockSpec(memory_space=pl.ANY)] * len(prev), out_specs=[spec] * n_out,
        input_output_aliases={len(ins) + k: k for k in range(len(prev))},
        compiler_params=_params(("parallel", "parallel"), (len(ins) + n_out) * _nbytes((tr, c), F32),
                                6 * _nbytes((tr, c), F32)),
    )(*ins, *prev)


def _my_chip():
    return 2 * lax.axis_index("x") + lax.axis_index("y")


def _my_core():
    return lax.axis_index("c")


def _cast_to_slot(w, l, name):
    _, r, cs = w.shape
    tr = _ew_rows(r, cs, 2)

    def body(w_ref, o_ref):
        o_ref[...] = w_ref[...].astype(BF16)

    return _pcall(
        body, name=name,
        out_shape=jax.ShapeDtypeStruct((N_CHIPS, r, cs), BF16),
        grid=(r // tr,),
        in_specs=[pl.BlockSpec((None, tr, cs), lambda i: (l, i, 0))],
        out_specs=pl.BlockSpec((None, tr, cs), lambda i: (_my_chip(), i, 0)),
        compiler_params=_params(("parallel",), 2 * _nbytes((tr, cs), F32)),
    )(w)


def _add_half(part, theirs, name):
    _, rh, cs = theirs.shape
    tr = _ew_rows(rh, cs, 3)
    nrb = rh // tr

    def body(a_ref, b_ref, o_ref):
        o_ref[...] = (a_ref[...].astype(F32) + b_ref[...].astype(F32)).astype(BF16)

    half = pl.BlockSpec((None, tr, cs), lambda j, i: (j, i, 0))
    return _pcall(
        body, name=name,
        out_shape=jax.ShapeDtypeStruct((N_CHIPS, rh, cs), BF16),
        grid=(N_CHIPS, nrb),
        in_specs=[pl.BlockSpec((None, tr, cs), lambda j, i: (j, _my_core() * nrb + i, 0)), half],
        out_specs=half,
        compiler_params=_params(("parallel", "parallel"), 3 * _nbytes((tr, cs), F32)),
    )(part, theirs)


def _sum_chips(chip_sum, q, total, l, nl, name):
    _, rh, cs = q.shape
    tr = _ew_rows(rh, cs, 5)
    nrb = rh // tr

    def body(own_ref, q1_ref, q2_ref, q3_ref, *rest):
        acc = own_ref[...].astype(F32)
        for ref in (q1_ref, q2_ref, q3_ref):
            acc = acc + ref[...].astype(F32)
        rest[-1][...] = acc

    def slot(k):
        return pl.BlockSpec((None, tr, cs), lambda i: ((_my_chip() + k) % N_CHIPS, i, 0))

    in_specs = [slot(0), slot(1), slot(2), slot(3)]
    operands = [chip_sum, q, q, q]
    aliases = {}
    if total is not None:
        in_specs.append(HBM_SPEC)
        operands.append(total)
        aliases = {4: 0}
    return _pcall(
        body, name=name,
        out_shape=jax.ShapeDtypeStruct((nl, 2 * rh, cs), F32),
        grid=(nrb,),
        in_specs=in_specs,
        out_specs=pl.BlockSpec((None, tr, cs), lambda i: (l, _my_core() * nrb + i, 0)),
        input_output_aliases=aliases,
        compiler_params=_params(("parallel",), 5 * _nbytes((tr, cs), F32)),
    )(*operands)


def _adamw(w, g, m, v, name, batch=None, prev=()):
    shape = w.shape
    c1 = 1.0 - ADAM_B1 ** ADAM_STEP
    c2 = 1.0 - ADAM_B2 ** ADAM_STEP

    def fn(wv, gv, mv, vv):
        mn = ADAM_B1 * mv + (1.0 - ADAM_B1) * gv
        vn = ADAM_B2 * vv + (1.0 - ADAM_B2) * (gv * gv)
        delta = -ADAM_LR * ((mn / c1) / (jnp.sqrt(vn / c2) + ADAM_EPS) + ADAM_WD * wv)
        return delta, mn, vn

    outs = _ew_call(fn, [_as3(w), _as3(g), _as3(m), _as3(v)], 3, [F32] * 3, name,
                    batch=batch, prev=[_as3(p) for p in prev])
    return [o.reshape(shape) for o in outs]


def _sum_leading(q, name):
    b, nj, r, c = q.shape
    tr = _ew_rows(r, c, nj + 1)

    def body(q_ref, o_ref):
        acc = q_ref[0].astype(F32)
        for j in range(1, nj):
            acc = acc + q_ref[j].astype(F32)
        o_ref[...] = acc

    return _pcall(
        body, name=name,
        out_shape=jax.ShapeDtypeStruct((b, r, c), F32),
        grid=(b, r // tr),
        in_specs=[pl.BlockSpec((None, nj, tr, c), lambda i, j: (i, 0, j, 0))],
        out_specs=pl.BlockSpec((None, tr, c), lambda i, j: (i, j, 0)),
        compiler_params=_params(("parallel", "parallel"), (nj + 1) * _nbytes((tr, c), F32),
                                2 * _nbytes((tr, c), F32)),
    )(q)


HBM_SPEC = pl.BlockSpec(memory_space=pl.ANY)


def _place():
    x, y, c = lax.axis_index("x"), lax.axis_index("y"), lax.axis_index("c")
    peers = [(1 - x, y), (x, 1 - y), (1 - x, 1 - y)]
    return x, y, c, 2 * x + y, peers


def _remote(src, dst, send_sem, recv_sem, dev):
    return pltpu.make_async_remote_copy(src_ref=src, dst_ref=dst, send_sem=send_sem, recv_sem=recv_sem,
                                        device_id=dev, device_id_type=MESH)


IN_HBM = pl.BlockSpec(memory_space=pltpu.HBM)
IN_SEM = pl.BlockSpec(memory_space=pltpu.SEMAPHORE)
DATAFLOW = pltpu.SideEffectType.DATAFLOW_SIDE_EFFECTING
TOKEN = jax.ShapeDtypeStruct((8, LANE), F32)


def _in_hbm(arrays):
    return [pltpu.with_memory_space_constraint(a, pltpu.HBM) for a in arrays]


def _half_rows(ref, c):
    rh = ref.shape[1] // 2
    return pl.ds(c * rh, rh)


def _split_start(body_copies, passed, landing, name):
    n_pass, n_land = len(passed), len(landing)
    n_arr = n_pass + n_land
    n = 3 * n_pass

    def body(*refs):
        ins = refs[:n_arr]
        send, recv = refs[n_arr], refs[n_arr + 1]
        token = refs[-1]
        for src, dst, _, s_sem, r_sem, dev in body_copies(ins[:n_pass], ins[n_pass:], send, recv):
            _remote(src, dst, s_sem, r_sem, dev).start()
        token[...] = jnp.zeros_like(token)

    arrays = list(passed) + list(landing)
    return pl.pallas_call(
        body, name=name,
        out_shape=(pltpu.SemaphoreType.DMA((n,)), pltpu.SemaphoreType.DMA((n,)),
                   *[pltpu.HBM(a.shape, a.dtype) for a in arrays], TOKEN),
        in_specs=[IN_HBM] * n_arr,
        out_specs=(IN_SEM, IN_SEM, *[IN_HBM] * n_arr, pl.BlockSpec(memory_space=pltpu.VMEM)),
        input_output_aliases={i: 2 + i for i in range(n_arr)},
        compiler_params=pltpu.CompilerParams(has_side_effects=DATAFLOW),
    )(*_in_hbm(arrays))


def _split_wait(body_copies, send, recv, passed, landing, after, name):
    n_pass, n_land = len(passed), len(landing)
    n_arr = n_pass + n_land

    def body(*refs):
        ins = refs[:n_arr]
        send_ref, recv_ref = refs[n_arr], refs[n_arr + 1]
        for src, _, landed, s_sem, r_sem, dev in body_copies(ins[:n_pass], ins[n_pass:], send_ref, recv_ref):
            cp = _remote(src, landed, s_sem, r_sem, dev)
            cp.wait_send()
            cp.wait_recv()

    arrays = list(passed) + list(landing)
    return pl.pallas_call(
        body, name=name,
        out_shape=tuple(pltpu.HBM(a.shape, a.dtype) for a in arrays),
        in_specs=[IN_HBM] * n_arr + [IN_SEM, IN_SEM, HBM_SPEC],
        out_specs=tuple([IN_HBM] * n_arr),
        input_output_aliases={i: i for i in range(n_arr)},
        compiler_params=pltpu.CompilerParams(has_side_effects=DATAFLOW),
    )(*arrays, send, recv, after)


def _gather_copies(slots, _, send, recv):
    _, _, c, me, peers = _place()
    copies = []
    for a, ref in enumerate(slots):
        rows = _half_rows(ref, c)
        own = ref.at[me, rows, :]
        for j, (px, py) in enumerate(peers):
            k = 3 * a + j
            copies.append((own, own, ref.at[2 * px + py, rows, :], send.at[k], recv.at[k], (px, py, c)))
    return copies


def _exchange_copies(chip_sums, landing, send, recv):
    _, _, c, me, peers = _place()
    copies = []
    for a, (s_ref, q_ref) in enumerate(zip(chip_sums, landing)):
        for j, (px, py) in enumerate(peers):
            k = 3 * a + j
            copies.append((s_ref.at[2 * px + py], q_ref.at[me], q_ref.at[2 * px + py],
                           send.at[k], recv.at[k], (px, py, c)))
    return copies


def _gather_start(slots, name):
    out = _split_start(_gather_copies, slots, [], name)
    return out[0], out[1], list(out[2:-1]), out[-1]


def _gather_wait(send, recv, slots, after, name):
    return list(_split_wait(_gather_copies, send, recv, slots, [], after, name))


def _gather_forward(slots, name):
    na = len(slots)
    n = 3 * na

    def body(*refs):
        outs = refs[na:2 * na]
        fsend, frecv = refs[2 * na:]
        x, y, c, _, peers = _place()
        forwards = []
        for a, ref in enumerate(outs):
            for j, (px, py) in enumerate(peers):
                k = 3 * a + j
                landed = ref.at[2 * px + py, _half_rows(ref, c), :]
                forwards.append(_remote(landed, landed, fsend.at[k], frecv.at[k], (x, y, 1 - c)))
                forwards[-1].start()
        for a, ref in enumerate(outs):
            for j, (px, py) in enumerate(peers):
                k = 3 * a + j
                landed = ref.at[2 * px + py, _half_rows(ref, 1 - c), :]
                _remote(landed, landed, fsend.at[k], frecv.at[k], (x, y, 1 - c)).wait_recv()
        for cp in forwards:
            cp.wait_send()

    return _pcall(
        body, name=name,
        out_shape=[jax.ShapeDtypeStruct(s.shape, s.dtype) for s in slots],
        in_specs=[HBM_SPEC] * na, out_specs=[HBM_SPEC] * na,
        input_output_aliases={a: a for a in range(na)},
        scratch_shapes=[pltpu.SemaphoreType.DMA((n,))] * 2,
    )(*slots)


def _gather_small(v, name):
    def body(v_ref, o_ref, send, recv, lsem):
        x, y, c, me, peers = _place()
        local = pltpu.make_async_copy(v_ref, o_ref.at[me], lsem)
        local.start()
        sends = [_remote(v_ref, o_ref.at[me], send.at[j], recv.at[j], (px, py, c))
                 for j, (px, py) in enumerate(peers)]
        for cp in sends:
            cp.start()
        for j, (px, py) in enumerate(peers):
            _remote(v_ref, o_ref.at[2 * px + py], send.at[j], recv.at[j], (px, py, c)).wait_recv()
        for cp in sends:
            cp.wait_send()
        local.wait()

    return _pcall(
        body, name=name,
        out_shape=jax.ShapeDtypeStruct((N_CHIPS,) + v.shape, v.dtype),
        in_specs=[HBM_SPEC], out_specs=HBM_SPEC,
        scratch_shapes=[pltpu.SemaphoreType.DMA((3,)), pltpu.SemaphoreType.DMA((3,)), pltpu.SemaphoreType.DMA],
    )(v)


def _gather_all(v, name):
    def body(v_ref, o_ref, send, recv, lsem):
        x, y, c = lax.axis_index("x"), lax.axis_index("y"), lax.axis_index("c")
        me = 4 * x + 2 * y + c
        local = pltpu.make_async_copy(v_ref, o_ref.at[me], lsem)
        local.start()
        flips = [(fx, fy, fc) for fx in (0, 1) for fy in (0, 1) for fc in (0, 1)][1:]
        peers = [(x ^ fx, y ^ fy, c ^ fc) for fx, fy, fc in flips]
        sends = [_remote(v_ref, o_ref.at[me], send.at[k], recv.at[k], dev) for k, dev in enumerate(peers)]
        for cp in sends:
            cp.start()
        for k, (px, py, pc) in enumerate(peers):
            _remote(v_ref, o_ref.at[4 * px + 2 * py + pc], send.at[k], recv.at[k], (px, py, pc)).wait_recv()
        for cp in sends:
            cp.wait_send()
        local.wait()

    return _pcall(
        body, name=name,
        out_shape=jax.ShapeDtypeStruct((8,) + v.shape, v.dtype),
        in_specs=[HBM_SPEC], out_specs=HBM_SPEC,
        scratch_shapes=[pltpu.SemaphoreType.DMA((7,)), pltpu.SemaphoreType.DMA((7,)), pltpu.SemaphoreType.DMA],
    )(v)


def _send_halves(parts, name):
    na = len(parts)

    def body(*refs):
        ins, outs = refs[:na], refs[na:2 * na]
        send, recv = refs[2 * na:]
        x, y, c, _, _ = _place()
        sends = [_remote(ins[a].at[:, _half_rows(ins[a], 1 - c), :], outs[a], send.at[a], recv.at[a],
                         (x, y, 1 - c)) for a in range(na)]
        for cp in sends:
            cp.start()
        for cp in sends:
            cp.wait()

    return _pcall(
        body, name=name,
        out_shape=[jax.ShapeDtypeStruct((N_CHIPS, p.shape[1] // 2, p.shape[2]), p.dtype) for p in parts],
        in_specs=[HBM_SPEC] * na, out_specs=[HBM_SPEC] * na,
        scratch_shapes=[pltpu.SemaphoreType.DMA((na,)), pltpu.SemaphoreType.DMA((na,))],
    )(*parts)


def _exchange_start(chip_sums, name):
    na = len(chip_sums)
    landing = [lax.empty(s.shape, s.dtype) for s in chip_sums]
    out = _split_start(_exchange_copies, chip_sums, landing, name)
    return out[0], out[1], list(out[2:2 + na]), list(out[2 + na:2 + 2 * na]), out[-1]


def _exchange_wait(send, recv, chip_sums, landing, after, name):
    na = len(chip_sums)
    out = _split_wait(_exchange_copies, send, recv, chip_sums, landing, after, name)
    return list(out[:na]), list(out[na:])


def _join_halves(reds, lo, hi, name):
    na = len(reds)
    layers = pl.ds(lo, hi - lo)

    def body(*refs):
        outs = refs[na:2 * na]
        send, recv = refs[2 * na:]
        x, y, c, _, _ = _place()
        sends = []
        for a, ref in enumerate(outs):
            rh = ref.shape[1] // 2
            mine = ref.at[layers, pl.ds(c * rh, rh), :]
            sends.append(_remote(mine, mine, send.at[a], recv.at[a], (x, y, 1 - c)))
            sends[-1].start()
        for a, ref in enumerate(outs):
            rh = ref.shape[1] // 2
            other = ref.at[layers, pl.ds((1 - c) * rh, rh), :]
            _remote(other, other, send.at[a], recv.at[a], (x, y, 1 - c)).wait_recv()
        for cp in sends:
            cp.wait_send()

    return _pcall(
        body, name=name,
        out_shape=[jax.ShapeDtypeStruct(r.shape, r.dtype) for r in reds],
        in_specs=[HBM_SPEC] * na, out_specs=[HBM_SPEC] * na,
        input_output_aliases={a: a for a in range(na)},
        scratch_shapes=[pltpu.SemaphoreType.DMA((na,)), pltpu.SemaphoreType.DMA((na,))],
    )(*reds)


COL_SHARDED = ("w_in", "w_a_out", "w_b_out", "w_x_out", "w_up")
ROW_SHARDED = ("w_kv", "w_o", "w_down")
BIG = COL_SHARDED + ROW_SHARDED
SMALL_REPLICATED = ("g_mix_pre", "conv_a_b", "ln_a_g", "ln_a_b", "g_mem", "g_mix_post", "g_mlp_pre", "g_mlp_post")
SMALL_SHARDED = ("conv_a_w", "conv_b_w")
WEIGHT_ORDER = ("g_mix_pre", "w_in", "conv_a_w", "conv_a_b", "ln_a_g", "ln_a_b", "w_a_out", "conv_b_w", "w_b_out",
                "g_mem", "w_kv", "w_x_out", "w_o", "g_mix_post", "g_mlp_pre", "w_up", "w_down", "g_mlp_post")


def _pack_rows(arrays, width):
    rows = []
    for a in arrays:
        r = a.reshape(-1, width)
        rows.append(jnp.pad(r, ((0, (-r.shape[0]) % 8), (0, 0))))
    return jnp.concatenate(rows, axis=0)


def _unpack_rows(packed, like, width):
    out, at = [], 0
    for a in like:
        n = a.size // width
        out.append(packed[at:at + n].reshape(a.shape))
        at += n + (-n) % 8
    return out


def _step(w, m, v, x, mem, target):
    nl = w["w_in"].shape[0]
    d = x.shape[1]
    c = w["conv_a_b"].shape[1]
    ka = w["conv_a_w"].shape[1]

    slots = {name: [_cast_to_slot(w[name], 0, "cast_" + name)] for name in BIG}

    def start_gather(l):
        return _gather_start([slots[name][l] for name in BIG], "gather_start_%d" % l)

    def finish_gather(l, pending, after):
        send, recv, thru, _ = pending
        landed = _gather_wait(send, recv, thru, after, "gather_wait_%d" % l)
        arrays = _gather_forward(landed, "gather_forward")
        return {name: (a.reshape(1, -1, a.shape[-1]) if name in ROW_SHARDED else a)
                for name, a in zip(BIG, arrays)}

    pending = start_gather(0)
    for name in BIG:
        slots[name] += [_cast_to_slot(w[name], l, "cast_" + name) for l in range(1, nl)]
    conv_pack = jnp.concatenate([w["conv_a_w"], w["conv_b_w"]], axis=1)
    conv_rows = conv_pack.shape[1]
    conv_pack = jnp.pad(conv_pack, ((0, 0), (0, (-conv_rows) % 8), (0, 0)))
    conv_all = _gather_small(conv_pack, "gather_conv")
    conv_all = jnp.moveaxis(conv_all, 0, 2).reshape(nl, conv_pack.shape[1], c)
    cw_a, cw_b = conv_all[:, :ka], conv_all[:, ka:conv_rows]
    (h,) = _norm_fwd(x, "norm_first", g_next=w["g_mix_pre"][0])
    full = [finish_gather(0, pending, h)]

    saved = []
    xl = x
    dy = loss_cols = None
    for l in range(nl):
        s = {"x": xl, "h": h}
        fw = full[l]
        token = None
        if l + 1 < nl:
            pending = start_gather(l + 1)
            token = pending[3]
        proj = _mm_nn(h, fw["w_in"], BF16, "mm_in", after=token)
        ca, act_a = _branch_a_fwd(proj, cw_a[l], w["conv_a_b"][l], w["ln_a_g"][l], w["ln_a_b"][l], "branch_a_fwd")
        b_in = _branch_b_fwd(proj, cw_b[l], "branch_b_fwd")
        (mem_n,) = _norm_fwd(mem, "norm_mem", g_next=w["g_mem"][l])
        kv = _mm_nn(mem_n, fw["w_kv"], BF16, "mm_kv")
        att = _attn_fwd(proj, kv, "attn_fwd")
        ya = _mm_nn(act_a, fw["w_a_out"], BF16, "mm_a_out")
        yb = _mm_nn(b_in, fw["w_b_out"], BF16, "mm_b_out")
        yx = _mm_nn(att, fw["w_x_out"], BF16, "mm_x_out")
        merged = _merge_fwd(proj, ya, yb, yx, "merge_fwd")
        z = _mm_nn(merged, fw["w_o"], F32, "mm_o")
        x1, h2 = _norm_fwd(xl, "norm_mid", z=z, g_post=w["g_mix_post"][l], g_next=w["g_mlp_pre"][l])
        up = _mm_nn(h2, fw["w_up"], BF16, "mm_up")
        f = _mm_nn(up, fw["w_down"], F32, "mm_down", a_act="relu2")
        s.update(proj=proj, ca=ca, act_a=act_a, b_in=b_in, mem_n=mem_n, kv=kv, att=att, ya=ya, yb=yb, yx=yx,
                 merged=merged, z=z, x1=x1, h2=h2, up=up, f=f)
        saved.append(s)
        if l + 1 < nl:
            full.append(finish_gather(l + 1, pending, f))
        if l + 1 < nl:
            xl, h = _norm_fwd(x1, "norm_mid", z=f, g_post=w["g_mlp_post"][l], g_next=w["g_mix_pre"][l + 1])
        else:
            dy, loss_cols = _norm_fwd(x1, "norm_loss", z=f, g_post=w["g_mlp_post"][l], target=target)

    part = {name: [None] * nl for name in BIG}
    small = {name: [None] * nl for name in SMALL_REPLICATED + SMALL_SHARDED}
    dxo = dy
    d_f, small["g_mlp_post"][nl - 1] = _norm_bwd("norm_bwd_top", dxo=dy,
                                                 post=(saved[-1]["f"], w["g_mlp_post"][nl - 1]))
    grad_x = None
    totals = {name: None for name in BIG}

    def start_exchange(l):
        parts = [part[name][l] for name in BIG]
        theirs = _send_halves(parts, "rs_send")
        chip_sums = [_add_half(p, t, "rs_add_" + name) for name, p, t in zip(BIG, parts, theirs)]
        return _exchange_start(chip_sums, "rs_xchg_start_%d" % l)

    def finish_exchange(l, pending, after):
        send, recv, chip_sums, landing, _ = pending
        chip_sums, q = _exchange_wait(send, recv, chip_sums, landing, after, "rs_xchg_wait_%d" % l)
        for name, own, got in zip(BIG, chip_sums, q):
            totals[name] = _sum_chips(own, got, totals[name], l, nl, "rs_sum_" + name)

    exchange = None
    for l in reversed(range(nl)):
        s = saved[l]
        fw = full[l]
        d_up = _mm_nt(d_f, fw["w_down"], BF16, "mm_down_dx", relu2_of=s["up"],
                      after=None if exchange is None else exchange[4])
        part["w_down"][l] = _mm_tn(s["up"], d_f, 1, "mm_down_dw", a_act="relu2").reshape(N_CHIPS, -1, d)
        d_h2 = _mm_nt(d_up, fw["w_up"], F32, "mm_up_dx")
        part["w_up"][l] = _mm_tn(s["h2"], d_up, N_CHIPS, "mm_up_dw")
        dx1, d_z, small["g_mlp_pre"][l], small["g_mix_post"][l] = _norm_bwd(
            "norm_bwd_mid", dxo=dxo, pre=(d_h2, s["x1"], w["g_mlp_pre"][l]), post=(s["z"], w["g_mix_post"][l]))
        d_merged = _mm_nt(d_z, fw["w_o"], BF16, "mm_o_dx")
        part["w_o"][l] = _mm_tn(s["merged"], d_z, 1, "mm_o_dw").reshape(N_CHIPS, -1, d)
        dproj, d_ya, d_yb, d_yx = _merge_bwd(d_merged, s["proj"], s["ya"], s["yb"], s["yx"], "merge_bwd")
        d_act_a = _mm_nt(d_ya, fw["w_a_out"], BF16, "mm_a_out_dx")
        part["w_a_out"][l] = _mm_tn(s["act_a"], d_ya, N_CHIPS, "mm_a_out_dw")
        d_b_in = _mm_nt(d_yb, fw["w_b_out"], BF16, "mm_b_out_dx")
        part["w_b_out"][l] = _mm_tn(s["b_in"], d_yb, N_CHIPS, "mm_b_out_dw")
        d_att = _mm_nt(d_yx, fw["w_x_out"], BF16, "mm_x_out_dx")
        part["w_x_out"][l] = _mm_tn(s["att"], d_yx, N_CHIPS, "mm_x_out_dw")
        dproj, small["ln_a_g"][l], small["ln_a_b"][l], small["conv_a_b"][l], small["conv_a_w"][l] = _branch_a_bwd(
            dproj, d_act_a, s["ca"], s["proj"], cw_a[l], w["ln_a_g"][l], w["ln_a_b"][l], "branch_a_bwd")
        dproj, small["conv_b_w"][l] = _branch_b_bwd(dproj, d_b_in, s["proj"], cw_b[l], "branch_b_bwd")
        dproj, d_kv = _attn_bwd(dproj, d_att, s["proj"], s["kv"], "attn_bwd")
        part["w_kv"][l] = _mm_tn(s["mem_n"], d_kv, 1, "mm_kv_dw").reshape(N_CHIPS, -1, 2 * c)
        d_mem_n = _mm_nt(d_kv, fw["w_kv"], F32, "mm_kv_dx")
        (small["g_mem"][l],) = _norm_bwd("norm_bwd_mem", pre=(d_mem_n, mem, w["g_mem"][l]), want_dx=False)
        d_h = _mm_nt(dproj, fw["w_in"], F32, "mm_in_dx")
        part["w_in"][l] = _mm_tn(s["h"], dproj, N_CHIPS, "mm_in_dw")
        if exchange is not None:
            finish_exchange(l + 1, exchange, d_h)
        exchange = start_exchange(l)
        if l > 0:
            dxo, d_f, small["g_mix_pre"][l], small["g_mlp_post"][l - 1] = _norm_bwd(
                "norm_bwd_mid", dxo=dx1, pre=(d_h, s["x"], w["g_mix_pre"][l]),
                post=(saved[l - 1]["f"], w["g_mlp_post"][l - 1]))
        else:
            grad_x, small["g_mix_pre"][0] = _norm_bwd("norm_bwd_last", dxo=dx1,
                                                      pre=(d_h, s["x"], w["g_mix_pre"][0]))

    grads, delta, new_m, new_v = {}, {}, {}, {}
    small_names = SMALL_REPLICATED + SMALL_SHARDED
    stacked = [jnp.stack(small[n]) for n in small_names]
    packed = _pack_rows(stacked, c)
    total = _sum_leading(_gather_all(packed, "gather_small_grads")[None], "sum_small_grads")[0]
    reduced = dict(zip(small_names, _unpack_rows(total, stacked, c)))
    chip = 2 * lax.axis_index("x") + lax.axis_index("y")
    cs = c // N_CHIPS
    for name in SMALL_SHARDED:
        grads[name] = lax.dynamic_slice_in_dim(reduced[name], chip * cs, cs, axis=2)
    for name in SMALL_REPLICATED:
        grads[name] = reduced[name].reshape(w[name].shape)
    for group, width in ((SMALL_REPLICATED, c), (SMALL_SHARDED, cs)):
        packs = [_pack_rows([src[n] for n in group], width)[None] for src in (w, grads, m, v)]
        outs = _adamw(*packs, "adamw_small_%d" % width)
        for dst, out in zip((delta, new_m, new_v), outs):
            dst.update(zip(group, _unpack_rows(out[0], [w[n] for n in group], width)))

    state = {name: () for name in BIG}
    if nl > 1:
        totals.update(zip(BIG, _join_halves([totals[name] for name in BIG], 1, nl, "rs_join_upper")))
        for name in BIG:
            state[name] = _adamw(w[name], totals[name], m[name], v[name], "adamw_" + name, batch=(1, nl))
    finish_exchange(0, exchange, state[BIG[-1]][0] if nl > 1 else grad_x)
    grads.update(zip(BIG, _join_halves([totals[name] for name in BIG], 0, 1, "rs_join_first")))
    for name in BIG:
        delta[name], new_m[name], new_v[name] = _adamw(w[name], grads[name], m[name], v[name], "adamw_" + name,
                                                       batch=(0, 1), prev=state[name])

    loss = lax.psum(0.5 * jnp.sum(loss_cols) / d, ("x", "y", "c"))
    return loss, grad_x, grads, delta, new_m, new_v


def kernel(x, mem, g_mix_pre, w_in, conv_a_w, conv_a_b, ln_a_g, ln_a_b, w_a_out, conv_b_w, w_b_out, g_mem, w_kv, w_x_out, w_o, g_mix_post, g_mlp_pre, w_up, w_down, g_mlp_post, loss_target, m_g_mix_pre, m_w_in, m_conv_a_w, m_conv_a_b, m_ln_a_g, m_ln_a_b, m_w_a_out, m_conv_b_w, m_w_b_out, m_g_mem, m_w_kv, m_w_x_out, m_w_o, m_g_mix_post, m_g_mlp_pre, m_w_up, m_w_down, m_g_mlp_post, v_g_mix_pre, v_w_in, v_conv_a_w, v_conv_a_b, v_ln_a_g, v_ln_a_b, v_w_a_out, v_conv_b_w, v_w_b_out, v_g_mem, v_w_kv, v_w_x_out, v_w_o, v_g_mix_post, v_g_mlp_pre, v_w_up, v_w_down, v_g_mlp_post):
    w = dict(g_mix_pre=g_mix_pre, w_in=w_in, conv_a_w=conv_a_w, conv_a_b=conv_a_b, ln_a_g=ln_a_g, ln_a_b=ln_a_b,
             w_a_out=w_a_out, conv_b_w=conv_b_w, w_b_out=w_b_out, g_mem=g_mem, w_kv=w_kv, w_x_out=w_x_out, w_o=w_o,
             g_mix_post=g_mix_post, g_mlp_pre=g_mlp_pre, w_up=w_up, w_down=w_down, g_mlp_post=g_mlp_post)
    m = dict(g_mix_pre=m_g_mix_pre, w_in=m_w_in, conv_a_w=m_conv_a_w, conv_a_b=m_conv_a_b, ln_a_g=m_ln_a_g,
             ln_a_b=m_ln_a_b, w_a_out=m_w_a_out, conv_b_w=m_conv_b_w, w_b_out=m_w_b_out, g_mem=m_g_mem, w_kv=m_w_kv,
             w_x_out=m_w_x_out, w_o=m_w_o, g_mix_post=m_g_mix_post, g_mlp_pre=m_g_mlp_pre, w_up=m_w_up,
             w_down=m_w_down, g_mlp_post=m_g_mlp_post)
    v = dict(g_mix_pre=v_g_mix_pre, w_in=v_w_in, conv_a_w=v_conv_a_w, conv_a_b=v_conv_a_b, ln_a_g=v_ln_a_g,
             ln_a_b=v_ln_a_b, w_a_out=v_w_a_out, conv_b_w=v_conv_b_w, w_b_out=v_w_b_out, g_mem=v_g_mem, w_kv=v_w_kv,
             w_x_out=v_w_x_out, w_o=v_w_o, g_mix_post=v_g_mix_post, g_mlp_pre=v_g_mlp_pre, w_up=v_w_up,
             w_down=v_w_down, g_mlp_post=v_g_mlp_post)
    loss, grad_x, grads, delta, new_m, new_v = _step(w, m, v, x[0], mem[0], loss_target[0])
    out = [loss, grad_x[None]]
    for group in (grads, delta, new_m, new_v):
        out += [group[n] for n in WEIGHT_ORDER]
    return tuple(out)
```

```python
import functools

import jax
import jax.numpy as jnp
from jax import lax
from jax.experimental import pallas as pl
from jax.experimental.pallas import tpu as pltpu

F32 = jnp.float32
BF16 = jnp.bfloat16
MESH = pl.DeviceIdType.MESH

NORM_EPS = 1e-6
N_HEADS = 4
ADAM_LR = 0.001
ADAM_B1 = 0.9
ADAM_B2 = 0.999
ADAM_EPS = 1e-08
ADAM_WD = 0.01
ADAM_STEP = 10

N_CHIPS = 4
V7X_VMEM_BYTES = 64 * 1024 * 1024
VMEM_CAP = V7X_VMEM_BYTES - 8 * 1024 * 1024
LANE = 128
SUBLANE_BF16 = 16
HALO = 32
ROW_TILE = 256
CONV_ROWS = 32
CONV_LANES = 512
ATTN_TILE = 512
MM_TM = 1024
MM_TN = 1024
MM_TK = 2048
MM_TC = 3072
EW_VMEM_BYTES = 24 * 1024 * 1024


def _tile(n, pref, align):
    if n <= pref:
        return n
    t = (pref // align) * align
    while t >= align:
        if n % t == 0:
            return t
        t -= align
    return n


def _ew_rows(r, c, n_arrays):
    return _tile(r, max(SUBLANE_BF16, EW_VMEM_BYTES // (2 * n_arrays * c * 4)), SUBLANE_BF16)


def _nbytes(shape, dtype):
    n = 1
    for s in shape:
        if s is not None:
            n *= s
    return n * jnp.dtype(dtype).itemsize


def _params(semantics, block_bytes, temp_bytes=0):
    need = 2 * block_bytes + temp_bytes + (4 << 20)
    return pltpu.CompilerParams(dimension_semantics=semantics,
                                vmem_limit_bytes=int(min(max(need, 16 << 20), VMEM_CAP)))


def _sigmoid(v):
    return 1.0 / (1.0 + jnp.exp(-v))


def _pcall(body, **kwargs):
    call = pl.pallas_call(body, **kwargs)
    return lambda *operands: call(*[pltpu.with_memory_space_constraint(o, pltpu.HBM) for o in operands])


def _mm_nn(a, b3, out_dtype, name, a_act=None, after=None):
    m, k = a.shape
    s, k2, ns = b3.shape
    assert k == k2
    tm = _tile(m, MM_TM, SUBLANE_BF16)
    tn = _tile(ns, MM_TN, LANE)
    tk = _tile(k, MM_TK, LANE)
    q = ns // tn
    nk = k // tk

    n_in = 2 if after is None else 3

    def body(*refs):
        a_ref, b_ref, o_ref = refs[0], refs[1], refs[n_in]
        scratch = refs[n_in + 1:]
        av = a_ref[...]
        if a_act == "relu2":
            r = jnp.maximum(av.astype(BF16), 0.0)
            av = r * r
        p = jnp.dot(av.astype(BF16), b_ref[...].astype(BF16), preferred_element_type=F32)
        if nk == 1:
            o_ref[...] = p.astype(o_ref.dtype)
        else:
            acc, = scratch
            kk = pl.program_id(2)

            @pl.when(kk == 0)
            def _():
                acc[...] = p

            @pl.when(kk > 0)
            def _():
                acc[...] += p

            @pl.when(kk == nk - 1)
            def _():
                o_ref[...] = acc[...].astype(o_ref.dtype)

    blocks = (_nbytes((tm, tk), a.dtype) + _nbytes((tk, tn), b3.dtype) + _nbytes((tm, tn), out_dtype))
    return _pcall(
        body, name=name,
        out_shape=jax.ShapeDtypeStruct((m, s * ns), out_dtype),
        grid=(m // tm, s * q, nk),
        in_specs=[pl.BlockSpec((tm, tk), lambda i, j, c: (i, c)),
                  pl.BlockSpec((None, tk, tn), lambda i, j, c: (j // q, c, j % q))]
        + ([] if after is None else [pl.BlockSpec(memory_space=pl.ANY)]),
        out_specs=pl.BlockSpec((tm, tn), lambda i, j, c: (i, j)),
        scratch_shapes=[pltpu.VMEM((tm, tn), F32)] if nk > 1 else [],
        compiler_params=_params(("parallel", "parallel", "arbitrary"), blocks,
                                3 * _nbytes((tm, tn), F32) + _nbytes((tm, tk), F32)),
    )(*([a, b3] if after is None else [a, b3, after]))


def _mm_nt(a, b3, out_dtype, name, relu2_of=None, after=None):
    m, n = a.shape
    s, kd, ns = b3.shape
    assert n == s * ns
    tm = _tile(m, MM_TM, SUBLANE_BF16)
    tj = _tile(kd, MM_TN, LANE)
    tc = _tile(ns, MM_TC, LANE)
    q = ns // tc
    nc = s * q

    n_in = 2 + (relu2_of is not None) + (after is not None)

    def body(*refs):
        a_ref, b_ref, u_ref, o_ref = refs[0], refs[1], refs[2], refs[n_in]
        scratch = refs[n_in + 1:]

        def finish(p):
            if relu2_of is not None:
                p = p * (2.0 * jnp.maximum(u_ref[...].astype(F32), 0.0))
            o_ref[...] = p.astype(o_ref.dtype)

        p = lax.dot_general(a_ref[...].astype(BF16), b_ref[...].astype(BF16),
                            (((1,), (1,)), ((), ())), preferred_element_type=F32)
        if nc == 1:
            finish(p)
        else:
            acc, = scratch
            cc = pl.program_id(2)

            @pl.when(cc == 0)
            def _():
                acc[...] = p

            @pl.when(cc > 0)
            def _():
                acc[...] += p

            @pl.when(cc == nc - 1)
            def _():
                finish(acc[...])

    in_specs = [pl.BlockSpec((tm, tc), lambda i, j, c: (i, c)),
                pl.BlockSpec((None, tj, tc), lambda i, j, c: (c // q, j, c % q))]
    operands = [a, b3]
    blocks = _nbytes((tm, tc), a.dtype) + _nbytes((tj, tc), b3.dtype) + _nbytes((tm, tj), out_dtype)
    if relu2_of is not None:
        in_specs.append(pl.BlockSpec((tm, tj), lambda i, j, c: (i, j)))
        operands.append(relu2_of)
        blocks += _nbytes((tm, tj), relu2_of.dtype)
    if after is not None:
        in_specs.append(pl.BlockSpec(memory_space=pl.ANY))
        operands.append(after)
    return _pcall(
        body, name=name,
        out_shape=jax.ShapeDtypeStruct((m, kd), out_dtype),
        grid=(m // tm, kd // tj, nc),
        in_specs=in_specs,
        out_specs=pl.BlockSpec((tm, tj), lambda i, j, c: (i, j)),
        scratch_shapes=[pltpu.VMEM((tm, tj), F32)] if nc > 1 else [],
        compiler_params=_params(("parallel", "parallel", "arbitrary"), blocks,
                                3 * _nbytes((tm, tj), F32)),
    )(*operands)


def _mm_tn(a, g, out_shards, name, a_act=None):
    m, ka = a.shape
    m2, n = g.shape
    assert m == m2
    ns = n // out_shards
    ta = _tile(ka, MM_TM, LANE)
    tn = _tile(ns, MM_TN, LANE)
    tm = _tile(m, MM_TK, SUBLANE_BF16)
    q = ns // tn
    nm = m // tm

    def body(a_ref, g_ref, o_ref, *scratch):
        av = a_ref[...]
        if a_act == "relu2":
            r = jnp.maximum(av.astype(BF16), 0.0)
            av = r * r
        p = lax.dot_general(av.astype(BF16), g_ref[...].astype(BF16),
                            (((0,), (0,)), ((), ())), preferred_element_type=F32)
        if nm == 1:
            o_ref[...] = p.astype(o_ref.dtype)
        else:
            acc, = scratch
            cc = pl.program_id(2)

            @pl.when(cc == 0)
            def _():
                acc[...] = p

            @pl.when(cc > 0)
            def _():
                acc[...] += p

            @pl.when(cc == nm - 1)
            def _():
                o_ref[...] = acc[...].astype(o_ref.dtype)

    blocks = _nbytes((tm, ta), a.dtype) + _nbytes((tm, tn), g.dtype) + _nbytes((ta, tn), BF16)
    return _pcall(
        body, name=name,
        out_shape=jax.ShapeDtypeStruct((out_shards, ka, ns), BF16),
        grid=(ka // ta, out_shards * q, nm),
        in_specs=[pl.BlockSpec((tm, ta), lambda i, j, c: (c, i)),
                  pl.BlockSpec((tm, tn), lambda i, j, c: (c, j))],
        out_specs=pl.BlockSpec((None, ta, tn), lambda i, j, c: (j // q, i, j % q)),
        scratch_shapes=[pltpu.VMEM((ta, tn), F32)] if nm > 1 else [],
        compiler_params=_params(("parallel", "parallel", "arbitrary"), blocks,
                                3 * _nbytes((ta, tn), F32) + _nbytes((tm, ta), F32)),
    )(a, g)


def _rms_scale(v):
    return lax.rsqrt(jnp.mean(v * v, axis=-1, keepdims=True) + NORM_EPS)


def _norm_fwd(x, name, *, z=None, g_post=None, g_next=None, target=None):
    t, d = x.shape
    tr = _tile(t, ROW_TILE, SUBLANE_BF16)
    has_res, has_next, has_loss = z is not None, g_next is not None, target is not None

    def body(*refs):
        it = iter(refs)
        x_ref = next(it)
        z_ref, gp_ref = (next(it), next(it)) if has_res else (None, None)
        gn_ref = next(it) if has_next else None
        t_ref = next(it) if has_loss else None
        xv = x_ref[...]
        if has_res:
            zv = z_ref[...]
            xv = xv + zv * _rms_scale(zv) * gp_ref[...]
            if not has_loss:
                next(it)[...] = xv
        if has_next:
            next(it)[...] = (xv * _rms_scale(xv) * gn_ref[...]).astype(BF16)
        if has_loss:
            e = xv - t_ref[...]
            next(it)[...] = e * (1.0 / d)
            ls_ref = next(it)

            @pl.when(pl.program_id(0) == 0)
            def _():
                ls_ref[...] = jnp.zeros_like(ls_ref)

            ls_ref[...] += jnp.sum(e * e, axis=0, keepdims=True)

    row = pl.BlockSpec((tr, d), lambda i: (i, 0))
    vec = pl.BlockSpec((1, d), lambda i: (0, 0))
    operands, in_specs, out_shape, out_specs = [x], [row], [], []
    if has_res:
        operands += [z, g_post.reshape(1, d)]
        in_specs += [row, vec]
        if not has_loss:
            out_shape.append(jax.ShapeDtypeStruct((t, d), F32))
            out_specs.append(row)
    if has_next:
        operands.append(g_next.reshape(1, d))
        in_specs.append(vec)
        out_shape.append(jax.ShapeDtypeStruct((t, d), BF16))
        out_specs.append(row)
    if has_loss:
        operands.append(target)
        in_specs.append(row)
        out_shape += [jax.ShapeDtypeStruct((t, d), F32), jax.ShapeDtypeStruct((1, d), F32)]
        out_specs += [row, vec]
    return _pcall(
        body, name=name, out_shape=out_shape, grid=(t // tr,),
        in_specs=in_specs, out_specs=out_specs,
        compiler_params=_params(("arbitrary",), 5 * _nbytes((tr, d), F32), 4 * _nbytes((tr, d), F32)),
    )(*operands)


def _norm_bwd(name, *, dxo=None, pre=None, post=None, want_dx=True):
    ref_arr = dxo if dxo is not None else pre[1]
    t, d = ref_arr.shape
    tr = _tile(t, ROW_TILE, SUBLANE_BF16)
    has_dxo, has_pre, has_post = dxo is not None, pre is not None, post is not None

    def body(*refs):
        it = iter(refs)
        dxo_ref = next(it) if has_dxo else None
        dh_ref, xin_ref, gpre_ref = (next(it), next(it), next(it)) if has_pre else (None,) * 3
        z_ref, gpost_ref = (next(it), next(it)) if has_post else (None, None)
        dx_ref = next(it) if (has_pre and want_dx) else None
        dz_ref = next(it) if has_post else None
        dgpre_ref = next(it) if has_pre else None
        dgpost_ref = next(it) if has_post else None
        first = pl.program_id(0) == 0

        dx = dxo_ref[...] if has_dxo else None
        if has_pre:
            xin = xin_ref[...]
            dh = dh_ref[...].astype(F32)
            r = _rms_scale(xin)
            gy = dh * gpre_ref[...]
            dloc = r * gy - xin * (r * r * r) * jnp.mean(gy * xin, axis=-1, keepdims=True)
            dx = dloc if dx is None else dx + dloc
            if want_dx:
                dx_ref[...] = dx

            @pl.when(first)
            def _():
                dgpre_ref[...] = jnp.zeros_like(dgpre_ref)

            dgpre_ref[...] += jnp.sum(dh * xin * r, axis=0, keepdims=True)
        if has_post:
            zv = z_ref[...]
            r = _rms_scale(zv)
            gy = dx * gpost_ref[...]
            dz = r * gy - zv * (r * r * r) * jnp.mean(gy * zv, axis=-1, keepdims=True)
            dz_ref[...] = dz.astype(BF16)

            @pl.when(first)
            def _():
                dgpost_ref[...] = jnp.zeros_like(dgpost_ref)

            dgpost_ref[...] += jnp.sum(dx * zv * r, axis=0, keepdims=True)

    row = pl.BlockSpec((tr, d), lambda i: (i, 0))
    vec = pl.BlockSpec((1, d), lambda i: (0, 0))
    operands, in_specs, out_shape, out_specs = [], [], [], []
    if has_dxo:
        operands.append(dxo)
        in_specs.append(row)
    if has_pre:
        operands += [pre[0], pre[1], pre[2].reshape(1, d)]
        in_specs += [row, row, vec]
    if has_post:
        operands += [post[0], post[1].reshape(1, d)]
        in_specs += [row, vec]
    if has_pre and want_dx:
        out_shape.append(jax.ShapeDtypeStruct((t, d), F32))
        out_specs.append(row)
    if has_post:
        out_shape.append(jax.ShapeDtypeStruct((t, d), BF16))
        out_specs.append(row)
    if has_pre:
        out_shape.append(jax.ShapeDtypeStruct((1, d), F32))
        out_specs.append(vec)
    if has_post:
        out_shape.append(jax.ShapeDtypeStruct((1, d), F32))
        out_specs.append(vec)
    return _pcall(
        body, name=name, out_shape=out_shape, grid=(t // tr,),
        in_specs=in_specs, out_specs=out_specs,
        compiler_params=_params(("arbitrary",), 6 * _nbytes((tr, d), F32), 6 * _nbytes((tr, d), F32)),
    )(*operands)


def _seq_tiles(t):
    tr = _tile(t, ROW_TILE, HALO)
    assert tr % HALO == 0 and t % tr == 0
    return tr, t // tr, tr // HALO


def _col(tr, width, cb):
    return pl.BlockSpec((tr, width), lambda i: (i, cb))


def _prev_halo(per, width, cb):
    return pl.BlockSpec((HALO, width), lambda i: (jnp.maximum(i * per - 1, 0), cb))


def _next_halo(per, n_halo, width, cb):
    return pl.BlockSpec((HALO, width), lambda i: (jnp.minimum((i + 1) * per, n_halo - 1), cb))


def _const(shape):
    return pl.BlockSpec(shape, lambda i: (0,) * len(shape))


def _glu(val, gate):
    return val.astype(F32) * _sigmoid(gate.astype(F32))


def _conv_chunks(tr, c):
    lanes = min(CONV_LANES, c)
    return [(r0, pl.ds(c0, lanes)) for r0 in range(0, tr, CONV_ROWS) for c0 in range(0, c, lanes)]


def _tap_sum(w_ref, buf, r0, cols, first, step, kw):
    acc = jnp.zeros((CONV_ROWS, cols.size), F32)
    for k in range(kw):
        acc = acc + w_ref[pl.ds(k, 1), cols] * buf[pl.ds(first + step * k + r0, CONV_ROWS), cols]
    return acc


def _layer_norm_parts(ca):
    mu = jnp.mean(ca, axis=-1, keepdims=True)
    xc = ca - mu
    rs = lax.rsqrt(jnp.mean(xc * xc, axis=-1, keepdims=True) + NORM_EPS)
    return xc * rs, rs


def _branch_a_fwd(proj, cw, cb, lg, lb, name):
    t = proj.shape[0]
    kw, c = cw.shape
    tr, nt, per = _seq_tiles(t)

    def body(av_ref, ag_ref, hv_ref, hg_ref, cw_ref, cb_ref, lg_ref, lb_ref, ca_ref, act_ref, abuf):
        i = pl.program_id(0)
        abuf[pl.ds(0, HALO), :] = jnp.where(i > 0, _glu(hv_ref[...], hg_ref[...]), 0.0)
        abuf[pl.ds(HALO, tr), :] = _glu(av_ref[...], ag_ref[...])
        for r0, cols in _conv_chunks(tr, c):
            ca_ref[pl.ds(r0, CONV_ROWS), cols] = (
                _tap_sum(cw_ref, abuf, r0, cols, HALO - (kw - 1), 1, kw) + cb_ref[:, cols])
        xh, _ = _layer_norm_parts(ca_ref[...])
        ln = xh * lg_ref[...] + lb_ref[...]
        act_ref[...] = (ln * _sigmoid(ln)).astype(BF16)

    return _pcall(
        body, name=name,
        out_shape=[jax.ShapeDtypeStruct((t, c), F32), jax.ShapeDtypeStruct((t, c), BF16)],
        grid=(nt,),
        in_specs=[_col(tr, c, 0), _col(tr, c, 1), _prev_halo(per, c, 0), _prev_halo(per, c, 1),
                  _const((kw, c)), _const((1, c)), _const((1, c)), _const((1, c))],
        out_specs=[_col(tr, c, 0), _col(tr, c, 0)],
        scratch_shapes=[pltpu.VMEM((HALO + tr, c), F32)],
        compiler_params=_params(("arbitrary",), 4 * _nbytes((tr, c), F32), 8 * _nbytes((tr + HALO, c), F32)),
    )(proj, proj, proj, proj, cw, cb.reshape(1, c), lg.reshape(1, c), lb.reshape(1, c))


def _branch_a_bwd(dproj, dact, ca, proj, cw, lg, lb, name):
    t = proj.shape[0]
    kw, c = cw.shape
    tr, nt, per = _seq_tiles(t)
    n_halo = t // HALO

    def body(dproj_in, da_ref, dah_ref, ca_ref, cah_ref, av_ref, ag_ref, hv_ref, hg_ref,
             cw_ref, lg_ref, lb_ref, out_ref, dlg_ref, dlb_ref, dcb_ref, dcw_ref, abuf, dbuf, sgbuf):
        del dproj_in
        i = pl.program_id(0)
        lgv, lbv = lg_ref[...], lb_ref[...]

        def conv_grad(dact_v, ca_v):
            xh, rs = _layer_norm_parts(ca_v)
            ln = xh * lgv + lbv
            sg = _sigmoid(ln)
            dln = dact_v.astype(F32) * (sg * (1.0 + ln * (1.0 - sg)))
            dxh = dln * lgv
            dca = rs * (dxh - jnp.mean(dxh, axis=-1, keepdims=True)
                        - xh * jnp.mean(dxh * xh, axis=-1, keepdims=True))
            return dca, dln, xh

        dca, dln, xh = conv_grad(da_ref[...], ca_ref[...])
        dca_h, _, _ = conv_grad(dah_ref[...], cah_ref[...])
        dbuf[pl.ds(0, tr), :] = dca
        dbuf[pl.ds(tr, HALO), :] = jnp.where(i < nt - 1, dca_h, 0.0)

        @pl.when(i == 0)
        def _():
            dlg_ref[...] = jnp.zeros_like(dlg_ref)
            dlb_ref[...] = jnp.zeros_like(dlb_ref)
            dcb_ref[...] = jnp.zeros_like(dcb_ref)
            dcw_ref[...] = jnp.zeros_like(dcw_ref)

        dlg_ref[...] += jnp.sum(dln * xh, axis=0, keepdims=True)
        dlb_ref[...] += jnp.sum(dln, axis=0, keepdims=True)
        dcb_ref[...] += jnp.sum(dca, axis=0, keepdims=True)

        sg = _sigmoid(ag_ref[...].astype(F32))
        sgbuf[...] = sg
        abuf[pl.ds(0, HALO), :] = jnp.where(i > 0, _glu(hv_ref[...], hg_ref[...]), 0.0)
        abuf[pl.ds(HALO, tr), :] = av_ref[...].astype(F32) * sg

        for r0, cols in _conv_chunks(tr, c):
            rows = pl.ds(r0, CONV_ROWS)
            d_a = _tap_sum(cw_ref, dbuf, r0, cols, kw - 1, -1, kw)
            sgc = sgbuf[rows, cols]
            out_ref[rows, cols] = (d_a * sgc).astype(BF16)
            out_ref[rows, pl.ds(c + cols.start, cols.size)] = (
                d_a * abuf[pl.ds(HALO + r0, CONV_ROWS), cols] * (1.0 - sgc)).astype(BF16)
        for _, cols in _conv_chunks(CONV_ROWS, c):
            for k in range(kw):
                acc = jnp.zeros((CONV_ROWS, cols.size), F32)
                for r0 in range(0, tr, CONV_ROWS):
                    acc = acc + (dbuf[pl.ds(r0, CONV_ROWS), cols]
                                 * abuf[pl.ds(HALO - (kw - 1) + k + r0, CONV_ROWS), cols])
                dcw_ref[pl.ds(k, 1), cols] += jnp.sum(acc, axis=0, keepdims=True)

    vec = _const((1, c))
    return _pcall(
        body, name=name,
        out_shape=[jax.ShapeDtypeStruct(dproj.shape, BF16)] + [jax.ShapeDtypeStruct((1, c), F32)] * 3
        + [jax.ShapeDtypeStruct((kw, c), F32)],
        grid=(nt,),
        in_specs=[pl.BlockSpec(memory_space=pl.ANY),
                  _col(tr, c, 0), _next_halo(per, n_halo, c, 0),
                  _col(tr, c, 0), _next_halo(per, n_halo, c, 0),
                  _col(tr, c, 0), _col(tr, c, 1), _prev_halo(per, c, 0), _prev_halo(per, c, 1),
                  _const((kw, c)), vec, vec],
        out_specs=[pl.BlockSpec((tr, 2 * c), lambda i: (i, 0)), vec, vec, vec, _const((kw, c))],
        scratch_shapes=[pltpu.VMEM((HALO + tr, c), F32), pltpu.VMEM((HALO + tr, c), F32),
                        pltpu.VMEM((tr, c), F32)],
        input_output_aliases={0: 0},
        compiler_params=_params(("arbitrary",), 6 * _nbytes((tr, c), F32), 12 * _nbytes((tr + HALO, c), F32)),
    )(dproj, dact, dact, ca, ca, proj, proj, proj, proj, cw, lg.reshape(1, c), lb.reshape(1, c))


def _branch_b_fwd(proj, cw, name):
    t = proj.shape[0]
    kw, c = cw.shape
    tr, nt, per = _seq_tiles(t)

    def body(sb_ref, sc_ref, sx_ref, hc_ref, hx_ref, cw_ref, o_ref, pbuf):
        i = pl.program_id(0)
        hp = hc_ref[...].astype(F32) * hx_ref[...].astype(F32)
        pbuf[pl.ds(0, HALO), :] = jnp.where(i > 0, hp, 0.0)
        pbuf[pl.ds(HALO, tr), :] = sc_ref[...].astype(F32) * sx_ref[...].astype(F32)
        u = jnp.zeros((tr, c), F32)
        for k in range(kw):
            u = u + cw_ref[pl.ds(k, 1), :] * pbuf[pl.ds(HALO - (kw - 1) + k, tr), :]
        o_ref[...] = (sb_ref[...].astype(F32) * u).astype(BF16)

    return _pcall(
        body, name=name,
        out_shape=jax.ShapeDtypeStruct((t, c), BF16),
        grid=(nt,),
        in_specs=[_col(tr, c, 2), _col(tr, c, 3), _col(tr, c, 4),
                  _prev_halo(per, c, 3), _prev_halo(per, c, 4), _const((kw, c))],
        out_specs=_col(tr, c, 0),
        scratch_shapes=[pltpu.VMEM((HALO + tr, c), F32)],
        compiler_params=_params(("arbitrary",), 4 * _nbytes((tr, c), F32), 6 * _nbytes((tr + HALO, c), F32)),
    )(proj, proj, proj, proj, proj, cw)


def _branch_b_bwd(dproj, dbin, proj, cw, name):
    t = proj.shape[0]
    kw, c = cw.shape
    tr, nt, per = _seq_tiles(t)
    n_halo = t // HALO

    def body(dproj_in, db_ref, dbh_ref, sb_ref, sbh_ref, sc_ref, sx_ref, hc_ref, hx_ref, cw_ref,
             out_ref, dcw_ref, pbuf, dubuf, res):
        del dproj_in
        i = pl.program_id(0)
        j = pl.program_id(1)

        @pl.when(j == 0)
        def _():
            sb = sb_ref[...].astype(F32)
            sc = sc_ref[...].astype(F32)
            sx = sx_ref[...].astype(F32)
            dbin_v = db_ref[...].astype(F32)
            hp = hc_ref[...].astype(F32) * hx_ref[...].astype(F32)
            pbuf[pl.ds(0, HALO), :] = jnp.where(i > 0, hp, 0.0)
            pbuf[pl.ds(HALO, tr), :] = sc * sx
            du = dbin_v * sb
            du_h = dbh_ref[...].astype(F32) * sbh_ref[...].astype(F32)
            dubuf[pl.ds(0, tr), :] = du
            dubuf[pl.ds(tr, HALO), :] = jnp.where(i < nt - 1, du_h, 0.0)

            @pl.when(i == 0)
            def _():
                dcw_ref[...] = jnp.zeros_like(dcw_ref)

            u = jnp.zeros((tr, c), F32)
            dp = jnp.zeros((tr, c), F32)
            for k in range(kw):
                shifted = pbuf[pl.ds(HALO - (kw - 1) + k, tr), :]
                u = u + cw_ref[pl.ds(k, 1), :] * shifted
                dp = dp + cw_ref[pl.ds(k, 1), :] * dubuf[pl.ds(kw - 1 - k, tr), :]
                dcw_ref[pl.ds(k, 1), :] += jnp.sum(du * shifted, axis=0, keepdims=True)
            res[0] = (dbin_v * u).astype(BF16)
            res[1] = (dp * sx).astype(BF16)
            res[2] = (dp * sc).astype(BF16)

        out_ref[...] = res[j]

    def colj(cb):
        return pl.BlockSpec((tr, c), lambda i, j: (i, cb))

    def prevj(cb):
        return pl.BlockSpec((HALO, c), lambda i, j: (jnp.maximum(i * per - 1, 0), cb))

    def nextj(cb):
        return pl.BlockSpec((HALO, c), lambda i, j: (jnp.minimum((i + 1) * per, n_halo - 1), cb))

    return _pcall(
        body, name=name,
        out_shape=[jax.ShapeDtypeStruct(dproj.shape, BF16), jax.ShapeDtypeStruct((kw, c), F32)],
        grid=(nt, 3),
        in_specs=[pl.BlockSpec(memory_space=pl.ANY),
                  colj(0), nextj(0), colj(2), nextj(2), colj(3), colj(4), prevj(3), prevj(4),
                  pl.BlockSpec((kw, c), lambda i, j: (0, 0))],
        out_specs=[pl.BlockSpec((tr, c), lambda i, j: (i, 2 + j)),
                   pl.BlockSpec((kw, c), lambda i, j: (0, 0))],
        scratch_shapes=[pltpu.VMEM((HALO + tr, c), F32), pltpu.VMEM((HALO + tr, c), F32),
                        pltpu.VMEM((3, tr, c), BF16)],
        input_output_aliases={0: 0},
        compiler_params=_params(("arbitrary", "arbitrary"), 6 * _nbytes((tr, c), F32),
                                10 * _nbytes((tr + HALO, c), F32)),
    )(dproj, dbin, dbin, proj, proj, proj, proj, proj, proj, cw)


def _softmax_rows(s):
    e = jnp.exp(s - jnp.max(s, axis=-1, keepdims=True))
    return e / jnp.sum(e, axis=-1, keepdims=True)


def _attn_fwd(proj, kv, name):
    t = proj.shape[0]
    m, c2 = kv.shape
    c = c2 // 2
    hd = c // N_HEADS
    ta = _tile(t, ATTN_TILE, SUBLANE_BF16)
    scale = hd ** -0.5

    def body(q_ref, kv_ref, o_ref):
        for h in range(N_HEADS):
            qh = q_ref[:, pl.ds(h * hd, hd)]
            kh = kv_ref[:, pl.ds(h * hd, hd)]
            vh = kv_ref[:, pl.ds(c + h * hd, hd)]
            s = lax.dot_general(qh, kh, (((1,), (1,)), ((), ())), preferred_element_type=F32) * scale
            p = _softmax_rows(s)
            o_ref[:, pl.ds(h * hd, hd)] = jnp.dot(p.astype(BF16), vh,
                                                  preferred_element_type=F32).astype(BF16)

    return _pcall(
        body, name=name,
        out_shape=jax.ShapeDtypeStruct((t, c), BF16),
        grid=(t // ta,),
        in_specs=[pl.BlockSpec((ta, c), lambda i: (i, 5)), _const((m, c2))],
        out_specs=pl.BlockSpec((ta, c), lambda i: (i, 0)),
        compiler_params=_params(("parallel",), 2 * _nbytes((ta, c), BF16) + _nbytes((m, c2), BF16),
                                8 * _nbytes((ta, m), F32)),
    )(proj, kv)


def _attn_bwd(dproj, d_o, proj, kv, name):
    t = proj.shape[0]
    m, c2 = kv.shape
    c = c2 // 2
    hd = c // N_HEADS
    ta = _tile(t, ATTN_TILE, SUBLANE_BF16)
    scale = hd ** -0.5

    def body(dproj_in, do_ref, q_ref, kv_ref, dq_ref, dkv_ref):
        del dproj_in

        @pl.when(pl.program_id(0) == 0)
        def _():
            dkv_ref[...] = jnp.zeros_like(dkv_ref)

        for h in range(N_HEADS):
            qh = q_ref[:, pl.ds(h * hd, hd)]
            kh = kv_ref[:, pl.ds(h * hd, hd)]
            vh = kv_ref[:, pl.ds(c + h * hd, hd)]
            doh = do_ref[:, pl.ds(h * hd, hd)]
            s = lax.dot_general(qh, kh, (((1,), (1,)), ((), ())), preferred_element_type=F32) * scale
            p = _softmax_rows(s)
            dp = lax.dot_general(doh, vh, (((1,), (1,)), ((), ())), preferred_element_type=F32)
            ds = (p * (dp - jnp.sum(dp * p, axis=-1, keepdims=True))).astype(BF16)
            dq_ref[:, pl.ds(h * hd, hd)] = (jnp.dot(ds, kh, preferred_element_type=F32) * scale).astype(BF16)
            dkv_ref[:, pl.ds(h * hd, hd)] += lax.dot_general(
                ds, qh, (((0,), (0,)), ((), ())), preferred_element_type=F32) * scale
            dkv_ref[:, pl.ds(c + h * hd, hd)] += lax.dot_general(
                p.astype(BF16), doh, (((0,), (0,)), ((), ())), preferred_element_type=F32)

    return _pcall(
        body, name=name,
        out_shape=[jax.ShapeDtypeStruct(dproj.shape, BF16), jax.ShapeDtypeStruct((m, c2), F32)],
        grid=(t // ta,),
        in_specs=[pl.BlockSpec(memory_space=pl.ANY),
                  pl.BlockSpec((ta, c), lambda i: (i, 0)), pl.BlockSpec((ta, c), lambda i: (i, 5)),
                  _const((m, c2))],
        out_specs=[pl.BlockSpec((ta, c), lambda i: (i, 5)), _const((m, c2))],
        input_output_aliases={0: 0},
        compiler_params=_params(("arbitrary",), 3 * _nbytes((ta, c), BF16) + 2 * _nbytes((m, c2), F32),
                                10 * _nbytes((ta, m), F32)),
    )(dproj, d_o, proj, kv)


def _merge_fwd(proj, ya, yb, yx, name):
    t, d = ya.shape
    tr = _tile(t, ROW_TILE, SUBLANE_BF16)

    def body(g_ref, ya_ref, yb_ref, yx_ref, o_ref):
        acc = jnp.zeros((tr, d), F32)
        for b, y_ref in enumerate((ya_ref, yb_ref, yx_ref)):
            acc = acc + _sigmoid(g_ref[:, pl.ds(b * d, d)].astype(F32)) * y_ref[...].astype(F32)
        o_ref[...] = acc.astype(BF16)

    row = pl.BlockSpec((tr, d), lambda i: (i, 0))
    return _pcall(
        body, name=name,
        out_shape=jax.ShapeDtypeStruct((t, d), BF16),
        grid=(t // tr,),
        in_specs=[pl.BlockSpec((tr, 3 * d), lambda i: (i, 1)), row, row, row],
        out_specs=row,
        compiler_params=_params(("parallel",), 7 * _nbytes((tr, d), BF16), 6 * _nbytes((tr, d), F32)),
    )(proj, ya, yb, yx)


def _merge_bwd(dmerged, proj, ya, yb, yx, name):
    t, d = ya.shape
    tr = _tile(t, ROW_TILE, SUBLANE_BF16)

    def body(dm_ref, g_ref, ya_ref, yb_ref, yx_ref, dg_ref, dya_ref, dyb_ref, dyx_ref):
        dm = dm_ref[...].astype(F32)
        for b, (y_ref, dy_ref) in enumerate(((ya_ref, dya_ref), (yb_ref, dyb_ref), (yx_ref, dyx_ref))):
            sg = _sigmoid(g_ref[:, pl.ds(b * d, d)].astype(F32))
            dg_ref[:, pl.ds(b * d, d)] = (dm * y_ref[...].astype(F32) * sg * (1.0 - sg)).astype(BF16)
            dy_ref[...] = (dm * sg).astype(BF16)

    row = pl.BlockSpec((tr, d), lambda i: (i, 0))
    gates = pl.BlockSpec((tr, 3 * d), lambda i: (i, 1))
    return _pcall(
        body, name=name,
        out_shape=[jax.ShapeDtypeStruct(proj.shape, BF16)] + [jax.ShapeDtypeStruct((t, d), BF16)] * 3,
        grid=(t // tr,),
        in_specs=[row, gates, row, row, row],
        out_specs=[gates, row, row, row],
        compiler_params=_params(("parallel",), 14 * _nbytes((tr, d), BF16), 8 * _nbytes((tr, d), F32)),
    )(dmerged, proj, ya, yb, yx)


def _as3(a):
    return a.reshape((-1,) + a.shape[-2:])


def _ew_call(fn, ins, n_out, out_dtypes, name, batch=None, prev=()):
    b, r, c = ins[0].shape
    lo, hi = batch if batch is not None else (0, b)
    tr = _ew_rows(r, c, len(ins) + n_out)
    n_in = len(ins) + len(prev)

    def body(*refs):
        outs = fn(*[ref[...] for ref in refs[:len(ins)]])
        for ref, val in zip(refs[n_in:], outs):
            ref[...] = val.astype(ref.dtype)

    spec = pl.BlockSpec((None, tr, c), lambda i, j: (i + lo, j, 0))
    return _pcall(
        body, name=name,
        out_shape=[jax.ShapeDtypeStruct((b, r, c), dt) for dt in out_dtypes],
        grid=(hi - lo, r // tr),
        in_specs=[spec] * len(ins) + [pl.BlockSpec(memory_space=pl.ANY)] * len(prev), out_specs=[spec] * n_out,
        input_output_aliases={len(ins) + k: k for k in range(len(prev))},
        compiler_params=_params(("parallel", "parallel"), (len(ins) + n_out) * _nbytes((tr, c), F32),
                                6 * _nbytes((tr, c), F32)),
    )(*ins, *prev)


def _my_chip():
    return 2 * lax.axis_index("x") + lax.axis_index("y")


def _my_core():
    return lax.axis_index("c")


def _cast_to_slot(w, l, name, after=()):
    _, r, cs = w.shape
    tr = _ew_rows(r, cs, 2)

    def body(w_ref, *rest):
        rest[-1][...] = w_ref[...].astype(BF16)

    return _pcall(
        body, name=name,
        out_shape=jax.ShapeDtypeStruct((N_CHIPS, r, cs), BF16),
        grid=(r // tr,),
        in_specs=[pl.BlockSpec((None, tr, cs), lambda i: (l, i, 0))]
        + [pl.BlockSpec(memory_space=pl.ANY)] * len(after),
        out_specs=pl.BlockSpec((None, tr, cs), lambda i: (_my_chip(), i, 0)),
        compiler_params=_params(("parallel",), 2 * _nbytes((tr, cs), F32)),
    )(w, *after)


def _add_half(part, theirs, name):
    _, rh, cs = theirs.shape
    tr = _ew_rows(rh, cs, 3)
    nrb = rh // tr

    def body(a_ref, b_ref, o_ref):
        o_ref[...] = (a_ref[...].astype(F32) + b_ref[...].astype(F32)).astype(BF16)

    half = pl.BlockSpec((None, tr, cs), lambda j, i: (j, i, 0))
    return _pcall(
        body, name=name,
        out_shape=jax.ShapeDtypeStruct((N_CHIPS, rh, cs), BF16),
        grid=(N_CHIPS, nrb),
        in_specs=[pl.BlockSpec((None, tr, cs), lambda j, i: (j, _my_core() * nrb + i, 0)), half],
        out_specs=half,
        compiler_params=_params(("parallel", "parallel"), 3 * _nbytes((tr, cs), F32)),
    )(part, theirs)


def _sum_chips(chip_sum, q, total, l, nl, name):
    _, rh, cs = q.shape
    tr = _ew_rows(rh, cs, 5)
    nrb = rh // tr

    def body(own_ref, q1_ref, q2_ref, q3_ref, *rest):
        acc = own_ref[...].astype(F32)
        for ref in (q1_ref, q2_ref, q3_ref):
            acc = acc + ref[...].astype(F32)
        rest[-1][...] = acc

    def slot(k):
        return pl.BlockSpec((None, tr, cs), lambda i: ((_my_chip() + k) % N_CHIPS, i, 0))

    in_specs = [slot(0), slot(1), slot(2), slot(3)]
    operands = [chip_sum, q, q, q]
    aliases = {}
    if total is not None:
        in_specs.append(HBM_SPEC)
        operands.append(total)
        aliases = {4: 0}
    return _pcall(
        body, name=name,
        out_shape=jax.ShapeDtypeStruct((nl, 2 * rh, cs), F32),
        grid=(nrb,),
        in_specs=in_specs,
        out_specs=pl.BlockSpec((None, tr, cs), lambda i: (l, _my_core() * nrb + i, 0)),
        input_output_aliases=aliases,
        compiler_params=_params(("parallel",), 5 * _nbytes((tr, cs), F32)),
    )(*operands)


def _adamw(w, g, m, v, name, batch=None, prev=()):
    shape = w.shape
    c1 = 1.0 - ADAM_B1 ** ADAM_STEP
    c2 = 1.0 - ADAM_B2 ** ADAM_STEP

    def fn(wv, gv, mv, vv):
        mn = ADAM_B1 * mv + (1.0 - ADAM_B1) * gv
        vn = ADAM_B2 * vv + (1.0 - ADAM_B2) * (gv * gv)
        delta = -ADAM_LR * ((mn / c1) / (jnp.sqrt(vn / c2) + ADAM_EPS) + ADAM_WD * wv)
        return delta, mn, vn

    outs = _ew_call(fn, [_as3(w), _as3(g), _as3(m), _as3(v)], 3, [F32] * 3, name,
                    batch=batch, prev=[_as3(p) for p in prev])
    return [o.reshape(shape) for o in outs]


def _sum_leading(q, name):
    b, nj, r, c = q.shape
    tr = _ew_rows(r, c, nj + 1)

    def body(q_ref, o_ref):
        acc = q_ref[0].astype(F32)
        for j in range(1, nj):
            acc = acc + q_ref[j].astype(F32)
        o_ref[...] = acc

    return _pcall(
        body, name=name,
        out_shape=jax.ShapeDtypeStruct((b, r, c), F32),
        grid=(b, r // tr),
        in_specs=[pl.BlockSpec((None, nj, tr, c), lambda i, j: (i, 0, j, 0))],
        out_specs=pl.BlockSpec((None, tr, c), lambda i, j: (i, j, 0)),
        compiler_params=_params(("parallel", "parallel"), (nj + 1) * _nbytes((tr, c), F32),
                                2 * _nbytes((tr, c), F32)),
    )(q)


HBM_SPEC = pl.BlockSpec(memory_space=pl.ANY)


def _place():
    x, y, c = lax.axis_index("x"), lax.axis_index("y"), lax.axis_index("c")
    peers = [(1 - x, y), (x, 1 - y), (1 - x, 1 - y)]
    return x, y, c, 2 * x + y, peers


def _remote(src, dst, send_sem, recv_sem, dev):
    return pltpu.make_async_remote_copy(src_ref=src, dst_ref=dst, send_sem=send_sem, recv_sem=recv_sem,
                                        device_id=dev, device_id_type=MESH)


IN_HBM = pl.BlockSpec(memory_space=pltpu.HBM)
IN_SEM = pl.BlockSpec(memory_space=pltpu.SEMAPHORE)
DATAFLOW = pltpu.SideEffectType.DATAFLOW_SIDE_EFFECTING
TOKEN = jax.ShapeDtypeStruct((8, LANE), F32)


def _in_hbm(arrays):
    return [pltpu.with_memory_space_constraint(a, pltpu.HBM) for a in arrays]


def _half_rows(ref, c):
    rh = ref.shape[1] // 2
    return pl.ds(c * rh, rh)


def _split_start(body_copies, passed, landing, name):
    n_pass, n_land = len(passed), len(landing)
    n_arr = n_pass + n_land
    n = 3 * n_pass

    def body(*refs):
        ins = refs[:n_arr]
        send, recv = refs[n_arr], refs[n_arr + 1]
        token = refs[-1]
        for src, dst, _, s_sem, r_sem, dev in body_copies(ins[:n_pass], ins[n_pass:], send, recv):
            _remote(src, dst, s_sem, r_sem, dev).start()
        token[...] = jnp.zeros_like(token)

    arrays = list(passed) + list(landing)
    return pl.pallas_call(
        body, name=name,
        out_shape=(pltpu.SemaphoreType.DMA((n,)), pltpu.SemaphoreType.DMA((n,)),
                   *[pltpu.HBM(a.shape, a.dtype) for a in arrays], TOKEN),
        in_specs=[IN_HBM] * n_arr,
        out_specs=(IN_SEM, IN_SEM, *[IN_HBM] * n_arr, pl.BlockSpec(memory_space=pltpu.VMEM)),
        input_output_aliases={i: 2 + i for i in range(n_arr)},
        compiler_params=pltpu.CompilerParams(has_side_effects=DATAFLOW),
    )(*_in_hbm(arrays))


def _split_wait(body_copies, send, recv, passed, landing, after, name):
    n_pass, n_land = len(passed), len(landing)
    n_arr = n_pass + n_land

    def body(*refs):
        ins = refs[:n_arr]
        send_ref, recv_ref = refs[n_arr], refs[n_arr + 1]
        for src, _, landed, s_sem, r_sem, dev in body_copies(ins[:n_pass], ins[n_pass:], send_ref, recv_ref):
            cp = _remote(src, landed, s_sem, r_sem, dev)
            cp.wait_send()
            cp.wait_recv()

    arrays = list(passed) + list(landing)
    return pl.pallas_call(
        body, name=name,
        out_shape=tuple(pltpu.HBM(a.shape, a.dtype) for a in arrays),
        in_specs=[IN_HBM] * n_arr + [IN_SEM, IN_SEM] + [HBM_SPEC] * len(after),
        out_specs=tuple([IN_HBM] * n_arr),
        input_output_aliases={i: i for i in range(n_arr)},
        compiler_params=pltpu.CompilerParams(has_side_effects=DATAFLOW),
    )(*arrays, send, recv, *after)


def _gather_copies(slots, _, send, recv):
    _, _, c, me, peers = _place()
    copies = []
    for a, ref in enumerate(slots):
        rows = _half_rows(ref, c)
        own = ref.at[me, rows, :]
        for j, (px, py) in enumerate(peers):
            k = 3 * a + j
            copies.append((own, own, ref.at[2 * px + py, rows, :], send.at[k], recv.at[k], (px, py, c)))
    return copies


def _exchange_copies(chip_sums, landing, send, recv):
    _, _, c, me, peers = _place()
    copies = []
    for a, (s_ref, q_ref) in enumerate(zip(chip_sums, landing)):
        for j, (px, py) in enumerate(peers):
            k = 3 * a + j
            copies.append((s_ref.at[2 * px + py], q_ref.at[me], q_ref.at[2 * px + py],
                           send.at[k], recv.at[k], (px, py, c)))
    return copies


def _gather_start(slots, name):
    out = _split_start(_gather_copies, slots, [], name)
    return out[0], out[1], list(out[2:-1]), out[-1]


def _gather_wait(send, recv, slots, after, name):
    return list(_split_wait(_gather_copies, send, recv, slots, [], after, name))


def _gather_forward(slots, name):
    na = len(slots)
    n = 3 * na

    def body(*refs):
        outs = refs[na:2 * na]
        fsend, frecv = refs[2 * na:]
        x, y, c, _, peers = _place()
        forwards = []
        for a, ref in enumerate(outs):
            for j, (px, py) in enumerate(peers):
                k = 3 * a + j
                landed = ref.at[2 * px + py, _half_rows(ref, c), :]
                forwards.append(_remote(landed, landed, fsend.at[k], frecv.at[k], (x, y, 1 - c)))
                forwards[-1].start()
        for a, ref in enumerate(outs):
            for j, (px, py) in enumerate(peers):
                k = 3 * a + j
                landed = ref.at[2 * px + py, _half_rows(ref, 1 - c), :]
                _remote(landed, landed, fsend.at[k], frecv.at[k], (x, y, 1 - c)).wait_recv()
        for cp in forwards:
            cp.wait_send()

    return _pcall(
        body, name=name,
        out_shape=[jax.ShapeDtypeStruct(s.shape, s.dtype) for s in slots],
        in_specs=[HBM_SPEC] * na, out_specs=[HBM_SPEC] * na,
        input_output_aliases={a: a for a in range(na)},
        scratch_shapes=[pltpu.SemaphoreType.DMA((n,))] * 2,
    )(*slots)


def _gather_small(v, name):
    def body(v_ref, o_ref, send, recv, lsem):
        x, y, c, me, peers = _place()
        local = pltpu.make_async_copy(v_ref, o_ref.at[me], lsem)
        local.start()
        sends = [_remote(v_ref, o_ref.at[me], send.at[j], recv.at[j], (px, py, c))
                 for j, (px, py) in enumerate(peers)]
        for cp in sends:
            cp.start()
        for j, (px, py) in enumerate(peers):
            _remote(v_ref, o_ref.at[2 * px + py], send.at[j], recv.at[j], (px, py, c)).wait_recv()
        for cp in sends:
            cp.wait_send()
        local.wait()

    return _pcall(
        body, name=name,
        out_shape=jax.ShapeDtypeStruct((N_CHIPS,) + v.shape, v.dtype),
        in_specs=[HBM_SPEC], out_specs=HBM_SPEC,
        scratch_shapes=[pltpu.SemaphoreType.DMA((3,)), pltpu.SemaphoreType.DMA((3,)), pltpu.SemaphoreType.DMA],
    )(v)


def _gather_all(v, name):
    def body(v_ref, o_ref, send, recv, lsem):
        x, y, c = lax.axis_index("x"), lax.axis_index("y"), lax.axis_index("c")
        me = 4 * x + 2 * y + c
        local = pltpu.make_async_copy(v_ref, o_ref.at[me], lsem)
        local.start()
        flips = [(fx, fy, fc) for fx in (0, 1) for fy in (0, 1) for fc in (0, 1)][1:]
        peers = [(x ^ fx, y ^ fy, c ^ fc) for fx, fy, fc in flips]
        sends = [_remote(v_ref, o_ref.at[me], send.at[k], recv.at[k], dev) for k, dev in enumerate(peers)]
        for cp in sends:
            cp.start()
        for k, (px, py, pc) in enumerate(peers):
            _remote(v_ref, o_ref.at[4 * px + 2 * py + pc], send.at[k], recv.at[k], (px, py, pc)).wait_recv()
        for cp in sends:
            cp.wait_send()
        local.wait()

    return _pcall(
        body, name=name,
        out_shape=jax.ShapeDtypeStruct((8,) + v.shape, v.dtype),
        in_specs=[HBM_SPEC], out_specs=HBM_SPEC,
        scratch_shapes=[pltpu.SemaphoreType.DMA((7,)), pltpu.SemaphoreType.DMA((7,)), pltpu.SemaphoreType.DMA],
    )(v)


def _send_halves(parts, name):
    na = len(parts)

    def body(*refs):
        ins, outs = refs[:na], refs[na:2 * na]
        send, recv = refs[2 * na:]
        x, y, c, _, _ = _place()
        sends = [_remote(ins[a].at[:, _half_rows(ins[a], 1 - c), :], outs[a], send.at[a], recv.at[a],
                         (x, y, 1 - c)) for a in range(na)]
        for cp in sends:
            cp.start()
        for cp in sends:
            cp.wait()

    return _pcall(
        body, name=name,
        out_shape=[jax.ShapeDtypeStruct((N_CHIPS, p.shape[1] // 2, p.shape[2]), p.dtype) for p in parts],
        in_specs=[HBM_SPEC] * na, out_specs=[HBM_SPEC] * na,
        scratch_shapes=[pltpu.SemaphoreType.DMA((na,)), pltpu.SemaphoreType.DMA((na,))],
    )(*parts)


def _exchange_start(chip_sums, name):
    na = len(chip_sums)
    landing = [lax.empty(s.shape, s.dtype) for s in chip_sums]
    out = _split_start(_exchange_copies, chip_sums, landing, name)
    return out[0], out[1], list(out[2:2 + na]), list(out[2 + na:2 + 2 * na]), out[-1]


def _exchange_wait(send, recv, chip_sums, landing, after, name):
    na = len(chip_sums)
    out = _split_wait(_exchange_copies, send, recv, chip_sums, landing, after, name)
    return list(out[:na]), list(out[na:])


def _join_halves(reds, lo, hi, name):
    na = len(reds)
    layers = pl.ds(lo, hi - lo)

    def body(*refs):
        outs = refs[na:2 * na]
        send, recv = refs[2 * na:]
        x, y, c, _, _ = _place()
        sends = []
        for a, ref in enumerate(outs):
            rh = ref.shape[1] // 2
            mine = ref.at[layers, pl.ds(c * rh, rh), :]
            sends.append(_remote(mine, mine, send.at[a], recv.at[a], (x, y, 1 - c)))
            sends[-1].start()
        for a, ref in enumerate(outs):
            rh = ref.shape[1] // 2
            other = ref.at[layers, pl.ds((1 - c) * rh, rh), :]
            _remote(other, other, send.at[a], recv.at[a], (x, y, 1 - c)).wait_recv()
        for cp in sends:
            cp.wait_send()

    return _pcall(
        body, name=name,
        out_shape=[jax.ShapeDtypeStruct(r.shape, r.dtype) for r in reds],
        in_specs=[HBM_SPEC] * na, out_specs=[HBM_SPEC] * na,
        input_output_aliases={a: a for a in range(na)},
        scratch_shapes=[pltpu.SemaphoreType.DMA((na,)), pltpu.SemaphoreType.DMA((na,))],
    )(*reds)


COL_SHARDED = ("w_in", "w_a_out", "w_b_out", "w_x_out", "w_up")
ROW_SHARDED = ("w_kv", "w_o", "w_down")
BIG = COL_SHARDED + ROW_SHARDED
SMALL_REPLICATED = ("g_mix_pre", "conv_a_b", "ln_a_g", "ln_a_b", "g_mem", "g_mix_post", "g_mlp_pre", "g_mlp_post")
SMALL_SHARDED = ("conv_a_w", "conv_b_w")
WEIGHT_ORDER = ("g_mix_pre", "w_in", "conv_a_w", "conv_a_b", "ln_a_g", "ln_a_b", "w_a_out", "conv_b_w", "w_b_out",
                "g_mem", "w_kv", "w_x_out", "w_o", "g_mix_post", "g_mlp_pre", "w_up", "w_down", "g_mlp_post")


def _pack_rows(arrays, width):
    rows = []
    for a in arrays:
        r = a.reshape(-1, width)
        rows.append(jnp.pad(r, ((0, (-r.shape[0]) % 8), (0, 0))))
    return jnp.concatenate(rows, axis=0)


def _unpack_rows(packed, like, width):
    out, at = [], 0
    for a in like:
        n = a.size // width
        out.append(packed[at:at + n].reshape(a.shape))
        at += n + (-n) % 8
    return out


def _step(w, m, v, x, mem, target):
    nl = w["w_in"].shape[0]
    d = x.shape[1]
    c = w["conv_a_b"].shape[1]
    ka = w["conv_a_w"].shape[1]

    slots = {name: [_cast_to_slot(w[name], 0, "cast_" + name)] for name in BIG}

    def start_gather(l):
        return _gather_start([slots[name][l] for name in BIG], "gather_start_%d" % l)

    def finish_gather(l, pending, after):
        send, recv, thru, _ = pending
        landed = _gather_wait(send, recv, thru, after, "gather_wait_%d" % l)
        arrays = _gather_forward(landed, "gather_forward")
        return {name: (a.reshape(1, -1, a.shape[-1]) if name in ROW_SHARDED else a)
                for name, a in zip(BIG, arrays)}

    pending = start_gather(0)
    for name in BIG:
        slots[name] += [_cast_to_slot(w[name], l, "cast_" + name, after=[pending[3]]) for l in range(1, nl)]
    conv_pack = jnp.concatenate([w["conv_a_w"], w["conv_b_w"]], axis=1)
    conv_rows = conv_pack.shape[1]
    conv_pack = jnp.pad(conv_pack, ((0, 0), (0, (-conv_rows) % 8), (0, 0)))
    conv_all = _gather_small(conv_pack, "gather_conv")
    conv_all = jnp.moveaxis(conv_all, 0, 2).reshape(nl, conv_pack.shape[1], c)
    cw_a, cw_b = conv_all[:, :ka], conv_all[:, ka:conv_rows]
    (h,) = _norm_fwd(x, "norm_first", g_next=w["g_mix_pre"][0])
    full = [finish_gather(0, pending, [h, conv_all] + [s for name in BIG for s in slots[name][1:]])]

    saved = []
    xl = x
    dy = loss_cols = None
    for l in range(nl):
        s = {"x": xl, "h": h}
        fw = full[l]
        token = None
        if l + 1 < nl:
            pending = start_gather(l + 1)
            token = pending[3]
        proj = _mm_nn(h, fw["w_in"], BF16, "mm_in", after=token)
        ca, act_a = _branch_a_fwd(proj, cw_a[l], w["conv_a_b"][l], w["ln_a_g"][l], w["ln_a_b"][l], "branch_a_fwd")
        b_in = _branch_b_fwd(proj, cw_b[l], "branch_b_fwd")
        (mem_n,) = _norm_fwd(mem, "norm_mem", g_next=w["g_mem"][l])
        kv = _mm_nn(mem_n, fw["w_kv"], BF16, "mm_kv")
        att = _attn_fwd(proj, kv, "attn_fwd")
        ya = _mm_nn(act_a, fw["w_a_out"], BF16, "mm_a_out")
        yb = _mm_nn(b_in, fw["w_b_out"], BF16, "mm_b_out")
        yx = _mm_nn(att, fw["w_x_out"], BF16, "mm_x_out")
        merged = _merge_fwd(proj, ya, yb, yx, "merge_fwd")
        z = _mm_nn(merged, fw["w_o"], F32, "mm_o")
        x1, h2 = _norm_fwd(xl, "norm_mid", z=z, g_post=w["g_mix_post"][l], g_next=w["g_mlp_pre"][l])
        up = _mm_nn(h2, fw["w_up"], BF16, "mm_up")
        f = _mm_nn(up, fw["w_down"], F32, "mm_down", a_act="relu2")
        s.update(proj=proj, ca=ca, act_a=act_a, b_in=b_in, mem_n=mem_n, kv=kv, att=att, ya=ya, yb=yb, yx=yx,
                 merged=merged, z=z, x1=x1, h2=h2, up=up, f=f)
        saved.append(s)
        if l + 1 < nl:
            full.append(finish_gather(l + 1, pending, [f]))
        if l + 1 < nl:
            xl, h = _norm_fwd(x1, "norm_mid", z=f, g_post=w["g_mlp_post"][l], g_next=w["g_mix_pre"][l + 1])
        else:
            dy, loss_cols = _norm_fwd(x1, "norm_loss", z=f, g_post=w["g_mlp_post"][l], target=target)

    part = {name: [None] * nl for name in BIG}
    small = {name: [None] * nl for name in SMALL_REPLICATED + SMALL_SHARDED}
    dxo = dy
    d_f, small["g_mlp_post"][nl - 1] = _norm_bwd("norm_bwd_top", dxo=dy,
                                                 post=(saved[-1]["f"], w["g_mlp_post"][nl - 1]))
    grad_x = None
    totals = {name: None for name in BIG}

    def start_exchange(l):
        parts = [part[name][l] for name in BIG]
        theirs = _send_halves(parts, "rs_send")
        chip_sums = [_add_half(p, t, "rs_add_" + name) for name, p, t in zip(BIG, parts, theirs)]
        return _exchange_start(chip_sums, "rs_xchg_start_%d" % l)

    def finish_exchange(l, pending, after):
        send, recv, chip_sums, landing, _ = pending
        chip_sums, q = _exchange_wait(send, recv, chip_sums, landing, after, "rs_xchg_wait_%d" % l)
        for name, own, got in zip(BIG, chip_sums, q):
            totals[name] = _sum_chips(own, got, totals[name], l, nl, "rs_sum_" + name)

    exchange = None
    for l in reversed(range(nl)):
        s = saved[l]
        fw = full[l]
        d_up = _mm_nt(d_f, fw["w_down"], BF16, "mm_down_dx", relu2_of=s["up"],
                      after=None if exchange is None else exchange[4])
        part["w_down"][l] = _mm_tn(s["up"], d_f, 1, "mm_down_dw", a_act="relu2").reshape(N_CHIPS, -1, d)
        d_h2 = _mm_nt(d_up, fw["w_up"], F32, "mm_up_dx")
        part["w_up"][l] = _mm_tn(s["h2"], d_up, N_CHIPS, "mm_up_dw")
        dx1, d_z, small["g_mlp_pre"][l], small["g_mix_post"][l] = _norm_bwd(
            "norm_bwd_mid", dxo=dxo, pre=(d_h2, s["x1"], w["g_mlp_pre"][l]), post=(s["z"], w["g_mix_post"][l]))
        d_merged = _mm_nt(d_z, fw["w_o"], BF16, "mm_o_dx")
        part["w_o"][l] = _mm_tn(s["merged"], d_z, 1, "mm_o_dw").reshape(N_CHIPS, -1, d)
        dproj, d_ya, d_yb, d_yx = _merge_bwd(d_merged, s["proj"], s["ya"], s["yb"], s["yx"], "merge_bwd")
        d_act_a = _mm_nt(d_ya, fw["w_a_out"], BF16, "mm_a_out_dx")
        part["w_a_out"][l] = _mm_tn(s["act_a"], d_ya, N_CHIPS, "mm_a_out_dw")
        d_b_in = _mm_nt(d_yb, fw["w_b_out"], BF16, "mm_b_out_dx")
        part["w_b_out"][l] = _mm_tn(s["b_in"], d_yb, N_CHIPS, "mm_b_out_dw")
        d_att = _mm_nt(d_yx, fw["w_x_out"], BF16, "mm_x_out_dx")
        part["w_x_out"][l] = _mm_tn(s["att"], d_yx, N_CHIPS, "mm_x_out_dw")
        dproj, small["ln_a_g"][l], small["ln_a_b"][l], small["conv_a_b"][l], small["conv_a_w"][l] = _branch_a_bwd(
            dproj, d_act_a, s["ca"], s["proj"], cw_a[l], w["ln_a_g"][l], w["ln_a_b"][l], "branch_a_bwd")
        dproj, small["conv_b_w"][l] = _branch_b_bwd(dproj, d_b_in, s["proj"], cw_b[l], "branch_b_bwd")
        dproj, d_kv = _attn_bwd(dproj, d_att, s["proj"], s["kv"], "attn_bwd")
        part["w_kv"][l] = _mm_tn(s["mem_n"], d_kv, 1, "mm_kv_dw").reshape(N_CHIPS, -1, 2 * c)
        d_mem_n = _mm_nt(d_kv, fw["w_kv"], F32, "mm_kv_dx")
        (small["g_mem"][l],) = _norm_bwd("norm_bwd_mem", pre=(d_mem_n, mem, w["g_mem"][l]), want_dx=False)
        d_h = _mm_nt(dproj, fw["w_in"], F32, "mm_in_dx")
        part["w_in"][l] = _mm_tn(s["h"], dproj, N_CHIPS, "mm_in_dw")
        if exchange is not None:
            finish_exchange(l + 1, exchange, [d_h])
        exchange = start_exchange(l)
        if l > 0:
            dxo, d_f, small["g_mix_pre"][l], small["g_mlp_post"][l - 1] = _norm_bwd(
                "norm_bwd_mid", dxo=dx1, pre=(d_h, s["x"], w["g_mix_pre"][l]),
                post=(saved[l - 1]["f"], w["g_mlp_post"][l - 1]))
        else:
            grad_x, small["g_mix_pre"][0] = _norm_bwd("norm_bwd_last", dxo=dx1,
                                                      pre=(d_h, s["x"], w["g_mix_pre"][0]))

    grads, delta, new_m, new_v = {}, {}, {}, {}
    small_names = SMALL_REPLICATED + SMALL_SHARDED
    stacked = [jnp.stack(small[n]) for n in small_names]
    packed = _pack_rows(stacked, c)
    total = _sum_leading(_gather_all(packed, "gather_small_grads")[None], "sum_small_grads")[0]
    reduced = dict(zip(small_names, _unpack_rows(total, stacked, c)))
    chip = 2 * lax.axis_index("x") + lax.axis_index("y")
    cs = c // N_CHIPS
    for name in SMALL_SHARDED:
        grads[name] = lax.dynamic_slice_in_dim(reduced[name], chip * cs, cs, axis=2)
    for name in SMALL_REPLICATED:
        grads[name] = reduced[name].reshape(w[name].shape)
    for group, width in ((SMALL_REPLICATED, c), (SMALL_SHARDED, cs)):
        packs = [_pack_rows([src[n] for n in group], width)[None] for src in (w, grads, m, v)]
        outs = _adamw(*packs, "adamw_small_%d" % width)
        for dst, out in zip((delta, new_m, new_v), outs):
            dst.update(zip(group, _unpack_rows(out[0], [w[n] for n in group], width)))

    state = {name: () for name in BIG}
    if nl > 1:
        totals.update(zip(BIG, _join_halves([totals[name] for name in BIG], 1, nl, "rs_join_upper")))
        for name in BIG:
            state[name] = _adamw(w[name], totals[name], m[name], v[name], "adamw_" + name, batch=(1, nl))
    finish_exchange(0, exchange, [state[name][0] for name in BIG] if nl > 1 else [grad_x])
    grads.update(zip(BIG, _join_halves([totals[name] for name in BIG], 0, 1, "rs_join_first")))
    for name in BIG:
        delta[name], new_m[name], new_v[name] = _adamw(w[name], grads[name], m[name], v[name], "adamw_" + name,
                                                       batch=(0, 1), prev=state[name])

    loss = lax.psum(0.5 * jnp.sum(loss_cols) / d, ("x", "y", "c"))
    return loss, grad_x, grads, delta, new_m, new_v


def kernel(x, mem, g_mix_pre, w_in, conv_a_w, conv_a_b, ln_a_g, ln_a_b, w_a_out, conv_b_w, w_b_out, g_mem, w_kv, w_x_out, w_o, g_mix_post, g_mlp_pre, w_up, w_down, g_mlp_post, loss_target, m_g_mix_pre, m_w_in, m_conv_a_w, m_conv_a_b, m_ln_a_g, m_ln_a_b, m_w_a_out, m_conv_b_w, m_w_b_out, m_g_mem, m_w_kv, m_w_x_out, m_w_o, m_g_mix_post, m_g_mlp_pre, m_w_up, m_w_down, m_g_mlp_post, v_g_mix_pre, v_w_in, v_conv_a_w, v_conv_a_b, v_ln_a_g, v_ln_a_b, v_w_a_out, v_conv_b_w, v_w_b_out, v_g_mem, v_w_kv, v_w_x_out, v_w_o, v_g_mix_post, v_g_mlp_pre, v_w_up, v_w_down, v_g_mlp_post):
    w = dict(g_mix_pre=g_mix_pre, w_in=w_in, conv_a_w=conv_a_w, conv_a_b=conv_a_b, ln_a_g=ln_a_g, ln_a_b=ln_a_b,
             w_a_out=w_a_out, conv_b_w=conv_b_w, w_b_out=w_b_out, g_mem=g_mem, w_kv=w_kv, w_x_out=w_x_out, w_o=w_o,
             g_mix_post=g_mix_post, g_mlp_pre=g_mlp_pre, w_up=w_up, w_down=w_down, g_mlp_post=g_mlp_post)
    m = dict(g_mix_pre=m_g_mix_pre, w_in=m_w_in, conv_a_w=m_conv_a_w, conv_a_b=m_conv_a_b, ln_a_g=m_ln_a_g,
             ln_a_b=m_ln_a_b, w_a_out=m_w_a_out, conv_b_w=m_conv_b_w, w_b_out=m_w_b_out, g_mem=m_g_mem, w_kv=m_w_kv,
             w_x_out=m_w_x_out, w_o=m_w_o, g_mix_post=m_g_mix_post, g_mlp_pre=m_g_mlp_pre, w_up=m_w_up,
             w_down=m_w_down, g_mlp_post=m_g_mlp_post)
    v = dict(g_mix_pre=v_g_mix_pre, w_in=v_w_in, conv_a_w=v_conv_a_w, conv_a_b=v_conv_a_b, ln_a_g=v_ln_a_g,
             ln_a_b=v_ln_a_b, w_a_out=v_w_a_out, conv_b_w=v_conv_b_w, w_b_out=v_w_b_out, g_mem=v_g_mem, w_kv=v_w_kv,
             w_x_out=v_w_x_out, w_o=v_w_o, g_mix_post=v_g_mix_post, g_mlp_pre=v_g_mlp_pre, w_up=v_w_up,
             w_down=v_w_down, g_mlp_post=v_g_mlp_post)
    loss, grad_x, grads, delta, new_m, new_v = _step(w, m, v, x[0], mem[0], loss_target[0])
    out = [loss, grad_x[None]]
    for group in (grads, delta, new_m, new_v):
        out += [group[n] for n in WEIGHT_ORDER]
    return tuple(out)
```

```python
import functools

import jax
import jax.numpy as jnp
from jax import lax
from jax.experimental import pallas as pl
from jax.experimental.pallas import tpu as pltpu

F32 = jnp.float32
BF16 = jnp.bfloat16
MESH = pl.DeviceIdType.MESH

NORM_EPS = 1e-6
N_HEADS = 4
ADAM_LR = 0.001
ADAM_B1 = 0.9
ADAM_B2 = 0.999
ADAM_EPS = 1e-08
ADAM_WD = 0.01
ADAM_STEP = 10

N_CHIPS = 4
V7X_VMEM_BYTES = 64 * 1024 * 1024
VMEM_CAP = V7X_VMEM_BYTES - 8 * 1024 * 1024
LANE = 128
SUBLANES = 8
SUBLANE_BF16 = 16
HALO = 32
ROW_TILE = 256
CONV_ROWS = 32
CONV_LANES = 512
ATTN_TILE = 512
MM_TM = 1024
MM_TN = 1024
MM_TK = 2048
MM_TC = 3072
EW_VMEM_BYTES = 24 * 1024 * 1024


def _tile(n, pref, align):
    if n <= pref:
        return n
    t = (pref // align) * align
    while t >= align:
        if n % t == 0:
            return t
        t -= align
    return n


def _ew_rows(r, c, n_arrays):
    return _tile(r, max(SUBLANE_BF16, EW_VMEM_BYTES // (2 * n_arrays * c * 4)), SUBLANE_BF16)


def _nbytes(shape, dtype):
    n = 1
    for s in shape:
        if s is not None:
            n *= s
    return n * jnp.dtype(dtype).itemsize


def _params(semantics, block_bytes, temp_bytes=0):
    need = 2 * block_bytes + temp_bytes + (4 << 20)
    return pltpu.CompilerParams(dimension_semantics=semantics,
                                vmem_limit_bytes=int(min(max(need, 16 << 20), VMEM_CAP)))


def _sigmoid(v):
    return 1.0 / (1.0 + jnp.exp(-v))


def _pcall(body, **kwargs):
    call = pl.pallas_call(body, **kwargs)
    return lambda *operands: call(*[pltpu.with_memory_space_constraint(o, pltpu.HBM) for o in operands])


def _mm_nn(a, b3, out_dtype, name, a_act=None, after=None):
    m, k = a.shape
    s, k2, ns = b3.shape
    assert k == k2
    tm = _tile(m, MM_TM, SUBLANE_BF16)
    tn = _tile(ns, MM_TN, LANE)
    tk = _tile(k, MM_TK, LANE)
    q = ns // tn
    nk = k // tk

    n_in = 2 if after is None else 3

    def body(*refs):
        a_ref, b_ref, o_ref = refs[0], refs[1], refs[n_in]
        scratch = refs[n_in + 1:]
        av = a_ref[...]
        if a_act == "relu2":
            r = jnp.maximum(av.astype(BF16), 0.0)
            av = r * r
        p = jnp.dot(av.astype(BF16), b_ref[...].astype(BF16), preferred_element_type=F32)
        if nk == 1:
            o_ref[...] = p.astype(o_ref.dtype)
        else:
            acc, = scratch
            kk = pl.program_id(2)

            @pl.when(kk == 0)
            def _():
                acc[...] = p

            @pl.when(kk > 0)
            def _():
                acc[...] += p

            @pl.when(kk == nk - 1)
            def _():
                o_ref[...] = acc[...].astype(o_ref.dtype)

    blocks = (_nbytes((tm, tk), a.dtype) + _nbytes((tk, tn), b3.dtype) + _nbytes((tm, tn), out_dtype))
    return _pcall(
        body, name=name,
        out_shape=jax.ShapeDtypeStruct((m, s * ns), out_dtype),
        grid=(m // tm, s * q, nk),
        in_specs=[pl.BlockSpec((tm, tk), lambda i, j, c: (i, c)),
                  pl.BlockSpec((None, tk, tn), lambda i, j, c: (j // q, c, j % q))]
        + ([] if after is None else [pl.BlockSpec(memory_space=pl.ANY)]),
        out_specs=pl.BlockSpec((tm, tn), lambda i, j, c: (i, j)),
        scratch_shapes=[pltpu.VMEM((tm, tn), F32)] if nk > 1 else [],
        compiler_params=_params(("parallel", "parallel", "arbitrary"), blocks,
                                3 * _nbytes((tm, tn), F32) + _nbytes((tm, tk), F32)),
    )(*([a, b3] if after is None else [a, b3, after]))


def _mm_nt(a, b3, out_dtype, name, relu2_of=None, after=None):
    m, n = a.shape
    s, kd, ns = b3.shape
    assert n == s * ns
    tm = _tile(m, MM_TM, SUBLANE_BF16)
    tj = _tile(kd, MM_TN, LANE)
    tc = _tile(ns, MM_TC, LANE)
    q = ns // tc
    nc = s * q

    n_in = 2 + (relu2_of is not None) + (after is not None)

    def body(*refs):
        a_ref, b_ref, u_ref, o_ref = refs[0], refs[1], refs[2], refs[n_in]
        scratch = refs[n_in + 1:]

        def finish(p):
            if relu2_of is not None:
                p = p * (2.0 * jnp.maximum(u_ref[...].astype(F32), 0.0))
            o_ref[...] = p.astype(o_ref.dtype)

        p = lax.dot_general(a_ref[...].astype(BF16), b_ref[...].astype(BF16),
                            (((1,), (1,)), ((), ())), preferred_element_type=F32)
        if nc == 1:
            finish(p)
        else:
            acc, = scratch
            cc = pl.program_id(2)

            @pl.when(cc == 0)
            def _():
                acc[...] = p

            @pl.when(cc > 0)
            def _():
                acc[...] += p

            @pl.when(cc == nc - 1)
            def _():
                finish(acc[...])

    in_specs = [pl.BlockSpec((tm, tc), lambda i, j, c: (i, c)),
                pl.BlockSpec((None, tj, tc), lambda i, j, c: (c // q, j, c % q))]
    operands = [a, b3]
    blocks = _nbytes((tm, tc), a.dtype) + _nbytes((tj, tc), b3.dtype) + _nbytes((tm, tj), out_dtype)
    if relu2_of is not None:
        in_specs.append(pl.BlockSpec((tm, tj), lambda i, j, c: (i, j)))
        operands.append(relu2_of)
        blocks += _nbytes((tm, tj), relu2_of.dtype)
    if after is not None:
        in_specs.append(pl.BlockSpec(memory_space=pl.ANY))
        operands.append(after)
    return _pcall(
        body, name=name,
        out_shape=jax.ShapeDtypeStruct((m, kd), out_dtype),
        grid=(m // tm, kd // tj, nc),
        in_specs=in_specs,
        out_specs=pl.BlockSpec((tm, tj), lambda i, j, c: (i, j)),
        scratch_shapes=[pltpu.VMEM((tm, tj), F32)] if nc > 1 else [],
        compiler_params=_params(("parallel", "parallel", "arbitrary"), blocks,
                                3 * _nbytes((tm, tj), F32)),
    )(*operands)


def _mm_tn(a, g, out_shards, name, a_act=None):
    m, ka = a.shape
    m2, n = g.shape
    assert m == m2
    ns = n // out_shards
    ta = _tile(ka, MM_TM, LANE)
    tn = _tile(ns, MM_TN, LANE)
    tm = _tile(m, MM_TK, SUBLANE_BF16)
    q = ns // tn
    nm = m // tm

    def body(a_ref, g_ref, o_ref, *scratch):
        av = a_ref[...]
        if a_act == "relu2":
            r = jnp.maximum(av.astype(BF16), 0.0)
            av = r * r
        p = lax.dot_general(av.astype(BF16), g_ref[...].astype(BF16),
                            (((0,), (0,)), ((), ())), preferred_element_type=F32)
        if nm == 1:
            o_ref[...] = p.astype(o_ref.dtype)
        else:
            acc, = scratch
            cc = pl.program_id(2)

            @pl.when(cc == 0)
            def _():
                acc[...] = p

            @pl.when(cc > 0)
            def _():
                acc[...] += p

            @pl.when(cc == nm - 1)
            def _():
                o_ref[...] = acc[...].astype(o_ref.dtype)

    blocks = _nbytes((tm, ta), a.dtype) + _nbytes((tm, tn), g.dtype) + _nbytes((ta, tn), BF16)
    return _pcall(
        body, name=name,
        out_shape=jax.ShapeDtypeStruct((out_shards, ka, ns), BF16),
        grid=(ka // ta, out_shards * q, nm),
        in_specs=[pl.BlockSpec((tm, ta), lambda i, j, c: (c, i)),
                  pl.BlockSpec((tm, tn), lambda i, j, c: (c, j))],
        out_specs=pl.BlockSpec((None, ta, tn), lambda i, j, c: (j // q, i, j % q)),
        scratch_shapes=[pltpu.VMEM((ta, tn), F32)] if nm > 1 else [],
        compiler_params=_params(("parallel", "parallel", "arbitrary"), blocks,
                                3 * _nbytes((ta, tn), F32) + _nbytes((tm, ta), F32)),
    )(a, g)


def _rms_scale(v):
    return lax.rsqrt(jnp.mean(v * v, axis=-1, keepdims=True) + NORM_EPS)


def _norm_fwd(x, name, *, z=None, g_post=None, g_next=None, target=None):
    t, d = x.shape
    tr = _tile(t, ROW_TILE, SUBLANE_BF16)
    has_res, has_next, has_loss = z is not None, g_next is not None, target is not None

    def body(*refs):
        it = iter(refs)
        x_ref = next(it)
        z_ref, gp_ref = (next(it), next(it)) if has_res else (None, None)
        gn_ref = next(it) if has_next else None
        t_ref = next(it) if has_loss else None
        xv = x_ref[...]
        if has_res:
            zv = z_ref[...]
            xv = xv + zv * _rms_scale(zv) * gp_ref[...]
            if not has_loss:
                next(it)[...] = xv
        if has_next:
            next(it)[...] = (xv * _rms_scale(xv) * gn_ref[...]).astype(BF16)
        if has_loss:
            e = xv - t_ref[...]
            next(it)[...] = e * (1.0 / d)
            ls_ref = next(it)

            @pl.when(pl.program_id(0) == 0)
            def _():
                ls_ref[...] = jnp.zeros_like(ls_ref)

            ls_ref[...] += jnp.sum(e * e, axis=0, keepdims=True)

    row = pl.BlockSpec((tr, d), lambda i: (i, 0))
    vec = pl.BlockSpec((1, d), lambda i: (0, 0))
    operands, in_specs, out_shape, out_specs = [x], [row], [], []
    if has_res:
        operands += [z, g_post.reshape(1, d)]
        in_specs += [row, vec]
        if not has_loss:
            out_shape.append(jax.ShapeDtypeStruct((t, d), F32))
            out_specs.append(row)
    if has_next:
        operands.append(g_next.reshape(1, d))
        in_specs.append(vec)
        out_shape.append(jax.ShapeDtypeStruct((t, d), BF16))
        out_specs.append(row)
    if has_loss:
        operands.append(target)
        in_specs.append(row)
        out_shape += [jax.ShapeDtypeStruct((t, d), F32), jax.ShapeDtypeStruct((1, d), F32)]
        out_specs += [row, vec]
    return _pcall(
        body, name=name, out_shape=out_shape, grid=(t // tr,),
        in_specs=in_specs, out_specs=out_specs,
        compiler_params=_params(("arbitrary",), 5 * _nbytes((tr, d), F32), 4 * _nbytes((tr, d), F32)),
    )(*operands)


def _norm_bwd(name, *, dxo=None, pre=None, post=None, want_dx=True):
    ref_arr = dxo if dxo is not None else pre[1]
    t, d = ref_arr.shape
    tr = _tile(t, ROW_TILE, SUBLANE_BF16)
    has_dxo, has_pre, has_post = dxo is not None, pre is not None, post is not None

    def body(*refs):
        it = iter(refs)
        dxo_ref = next(it) if has_dxo else None
        dh_ref, xin_ref, gpre_ref = (next(it), next(it), next(it)) if has_pre else (None,) * 3
        z_ref, gpost_ref = (next(it), next(it)) if has_post else (None, None)
        dx_ref = next(it) if (has_pre and want_dx) else None
        dz_ref = next(it) if has_post else None
        dgpre_ref = next(it) if has_pre else None
        dgpost_ref = next(it) if has_post else None
        first = pl.program_id(0) == 0

        dx = dxo_ref[...] if has_dxo else None
        if has_pre:
            xin = xin_ref[...]
            dh = dh_ref[...].astype(F32)
            r = _rms_scale(xin)
            gy = dh * gpre_ref[...]
            dloc = r * gy - xin * (r * r * r) * jnp.mean(gy * xin, axis=-1, keepdims=True)
            dx = dloc if dx is None else dx + dloc
            if want_dx:
                dx_ref[...] = dx

            @pl.when(first)
            def _():
                dgpre_ref[...] = jnp.zeros_like(dgpre_ref)

            dgpre_ref[...] += jnp.sum(dh * xin * r, axis=0, keepdims=True)
        if has_post:
            zv = z_ref[...]
            r = _rms_scale(zv)
            gy = dx * gpost_ref[...]
            dz = r * gy - zv * (r * r * r) * jnp.mean(gy * zv, axis=-1, keepdims=True)
            dz_ref[...] = dz.astype(BF16)

            @pl.when(first)
            def _():
                dgpost_ref[...] = jnp.zeros_like(dgpost_ref)

            dgpost_ref[...] += jnp.sum(dx * zv * r, axis=0, keepdims=True)

    row = pl.BlockSpec((tr, d), lambda i: (i, 0))
    vec = pl.BlockSpec((1, d), lambda i: (0, 0))
    operands, in_specs, out_shape, out_specs = [], [], [], []
    if has_dxo:
        operands.append(dxo)
        in_specs.append(row)
    if has_pre:
        operands += [pre[0], pre[1], pre[2].reshape(1, d)]
        in_specs += [row, row, vec]
    if has_post:
        operands += [post[0], post[1].reshape(1, d)]
        in_specs += [row, vec]
    if has_pre and want_dx:
        out_shape.append(jax.ShapeDtypeStruct((t, d), F32))
        out_specs.append(row)
    if has_post:
        out_shape.append(jax.ShapeDtypeStruct((t, d), BF16))
        out_specs.append(row)
    if has_pre:
        out_shape.append(jax.ShapeDtypeStruct((1, d), F32))
        out_specs.append(vec)
    if has_post:
        out_shape.append(jax.ShapeDtypeStruct((1, d), F32))
        out_specs.append(vec)
    return _pcall(
        body, name=name, out_shape=out_shape, grid=(t // tr,),
        in_specs=in_specs, out_specs=out_specs,
        compiler_params=_params(("arbitrary",), 6 * _nbytes((tr, d), F32), 6 * _nbytes((tr, d), F32)),
    )(*operands)


def _seq_tiles(t):
    tr = _tile(t, ROW_TILE, HALO)
    assert tr % HALO == 0 and t % tr == 0
    return tr, t // tr, tr // HALO


def _col(tr, width, cb):
    return pl.BlockSpec((tr, width), lambda i: (i, cb))


def _prev_halo(per, width, cb):
    return pl.BlockSpec((HALO, width), lambda i: (jnp.maximum(i * per - 1, 0), cb))


def _next_halo(per, n_halo, width, cb):
    return pl.BlockSpec((HALO, width), lambda i: (jnp.minimum((i + 1) * per, n_halo - 1), cb))


def _const(shape):
    return pl.BlockSpec(shape, lambda i: (0,) * len(shape))


def _glu(val, gate):
    return val.astype(F32) * _sigmoid(gate.astype(F32))


def _conv_chunks(tr, c):
    lanes = min(CONV_LANES, c)
    return [(r0, pl.ds(c0, lanes)) for r0 in range(0, tr, CONV_ROWS) for c0 in range(0, c, lanes)]


def _shift_copies(buf, shifted):
    rows = shifted.shape[1]
    buf[pl.ds(rows, SUBLANES), :] = jnp.zeros((SUBLANES, buf.shape[1]), F32)
    for s in range(SUBLANES):
        shifted[s] = buf[pl.ds(s, rows), :]


def _shifted_rows(shifted, offset, r0, cols):
    return shifted[offset % SUBLANES, pl.ds(offset - offset % SUBLANES + r0, CONV_ROWS), cols]


def _tap_sum(w_ref, shifted, r0, cols, first, step, kw):
    acc = jnp.zeros((CONV_ROWS, cols.size), F32)
    for k in range(kw):
        acc = acc + w_ref[pl.ds(k, 1), cols] * _shifted_rows(shifted, first + step * k, r0, cols)
    return acc


def _layer_norm_parts(ca):
    mu = jnp.mean(ca, axis=-1, keepdims=True)
    xc = ca - mu
    rs = lax.rsqrt(jnp.mean(xc * xc, axis=-1, keepdims=True) + NORM_EPS)
    return xc * rs, rs


def _branch_a_fwd(proj, cw, cb, lg, lb, name):
    t = proj.shape[0]
    kw, c = cw.shape
    tr, nt, per = _seq_tiles(t)

    def body(av_ref, ag_ref, hv_ref, hg_ref, cw_ref, cb_ref, lg_ref, lb_ref, ca_ref, act_ref, abuf, ash):
        i = pl.program_id(0)
        abuf[pl.ds(0, HALO), :] = jnp.where(i > 0, _glu(hv_ref[...], hg_ref[...]), 0.0)
        abuf[pl.ds(HALO, tr), :] = _glu(av_ref[...], ag_ref[...])
        _shift_copies(abuf, ash)
        for r0, cols in _conv_chunks(tr, c):
            ca_ref[pl.ds(r0, CONV_ROWS), cols] = (
                _tap_sum(cw_ref, ash, r0, cols, HALO - (kw - 1), 1, kw) + cb_ref[:, cols])
        xh, _ = _layer_norm_parts(ca_ref[...])
        ln = xh * lg_ref[...] + lb_ref[...]
        act_ref[...] = (ln * _sigmoid(ln)).astype(BF16)

    return _pcall(
        body, name=name,
        out_shape=[jax.ShapeDtypeStruct((t, c), F32), jax.ShapeDtypeStruct((t, c), BF16)],
        grid=(nt,),
        in_specs=[_col(tr, c, 0), _col(tr, c, 1), _prev_halo(per, c, 0), _prev_halo(per, c, 1),
                  _const((kw, c)), _const((1, c)), _const((1, c)), _const((1, c))],
        out_specs=[_col(tr, c, 0), _col(tr, c, 0)],
        scratch_shapes=[pltpu.VMEM((HALO + tr + SUBLANES, c), F32), pltpu.VMEM((SUBLANES, HALO + tr, c), F32)],
        compiler_params=_params(("arbitrary",), 4 * _nbytes((tr, c), F32),
                                (8 + SUBLANES) * _nbytes((tr + HALO, c), F32)),
    )(proj, proj, proj, proj, cw, cb.reshape(1, c), lg.reshape(1, c), lb.reshape(1, c))


def _branch_a_bwd(dproj, dact, ca, proj, cw, lg, lb, name):
    t = proj.shape[0]
    kw, c = cw.shape
    tr, nt, per = _seq_tiles(t)
    n_halo = t // HALO

    def body(dproj_in, da_ref, dah_ref, ca_ref, cah_ref, av_ref, ag_ref, hv_ref, hg_ref,
             cw_ref, lg_ref, lb_ref, out_ref, dlg_ref, dlb_ref, dcb_ref, dcw_ref, abuf, dbuf, sgbuf, ash, dsh):
        del dproj_in
        i = pl.program_id(0)
        lgv, lbv = lg_ref[...], lb_ref[...]

        def conv_grad(dact_v, ca_v):
            xh, rs = _layer_norm_parts(ca_v)
            ln = xh * lgv + lbv
            sg = _sigmoid(ln)
            dln = dact_v.astype(F32) * (sg * (1.0 + ln * (1.0 - sg)))
            dxh = dln * lgv
            dca = rs * (dxh - jnp.mean(dxh, axis=-1, keepdims=True)
                        - xh * jnp.mean(dxh * xh, axis=-1, keepdims=True))
            return dca, dln, xh

        dca, dln, xh = conv_grad(da_ref[...], ca_ref[...])
        dca_h, _, _ = conv_grad(dah_ref[...], cah_ref[...])
        dbuf[pl.ds(0, tr), :] = dca
        dbuf[pl.ds(tr, HALO), :] = jnp.where(i < nt - 1, dca_h, 0.0)

        @pl.when(i == 0)
        def _():
            dlg_ref[...] = jnp.zeros_like(dlg_ref)
            dlb_ref[...] = jnp.zeros_like(dlb_ref)
            dcb_ref[...] = jnp.zeros_like(dcb_ref)
            dcw_ref[...] = jnp.zeros_like(dcw_ref)

        dlg_ref[...] += jnp.sum(dln * xh, axis=0, keepdims=True)
        dlb_ref[...] += jnp.sum(dln, axis=0, keepdims=True)
        dcb_ref[...] += jnp.sum(dca, axis=0, keepdims=True)

        sg = _sigmoid(ag_ref[...].astype(F32))
        sgbuf[...] = sg
        abuf[pl.ds(0, HALO), :] = jnp.where(i > 0, _glu(hv_ref[...], hg_ref[...]), 0.0)
        abuf[pl.ds(HALO, tr), :] = av_ref[...].astype(F32) * sg
        _shift_copies(abuf, ash)
        _shift_copies(dbuf, dsh)

        for r0, cols in _conv_chunks(tr, c):
            rows = pl.ds(r0, CONV_ROWS)
            d_a = _tap_sum(cw_ref, dsh, r0, cols, kw - 1, -1, kw)
            sgc = sgbuf[rows, cols]
            out_ref[rows, cols] = (d_a * sgc).astype(BF16)
            out_ref[rows, pl.ds(c + cols.start, cols.size)] = (
                d_a * abuf[pl.ds(HALO + r0, CONV_ROWS), cols] * (1.0 - sgc)).astype(BF16)
        for _, cols in _conv_chunks(CONV_ROWS, c):
            for k in range(kw):
                acc = jnp.zeros((CONV_ROWS, cols.size), F32)
                for r0 in range(0, tr, CONV_ROWS):
                    acc = acc + (dbuf[pl.ds(r0, CONV_ROWS), cols]
                                 * _shifted_rows(ash, HALO - (kw - 1) + k, r0, cols))
                dcw_ref[pl.ds(k, 1), cols] += jnp.sum(acc, axis=0, keepdims=True)

    vec = _const((1, c))
    return _pcall(
        body, name=name,
        out_shape=[jax.ShapeDtypeStruct(dproj.shape, BF16)] + [jax.ShapeDtypeStruct((1, c), F32)] * 3
        + [jax.ShapeDtypeStruct((kw, c), F32)],
        grid=(nt,),
        in_specs=[pl.BlockSpec(memory_space=pl.ANY),
                  _col(tr, c, 0), _next_halo(per, n_halo, c, 0),
                  _col(tr, c, 0), _next_halo(per, n_halo, c, 0),
                  _col(tr, c, 0), _col(tr, c, 1), _prev_halo(per, c, 0), _prev_halo(per, c, 1),
                  _const((kw, c)), vec, vec],
        out_specs=[pl.BlockSpec((tr, 2 * c), lambda i: (i, 0)), vec, vec, vec, _const((kw, c))],
        scratch_shapes=[pltpu.VMEM((HALO + tr + SUBLANES, c), F32), pltpu.VMEM((HALO + tr + SUBLANES, c), F32),
                        pltpu.VMEM((tr, c), F32),
                        pltpu.VMEM((SUBLANES, HALO + tr, c), F32), pltpu.VMEM((SUBLANES, HALO + tr, c), F32)],
        input_output_aliases={0: 0},
        compiler_params=_params(("arbitrary",), 6 * _nbytes((tr, c), F32),
                                (12 + 2 * SUBLANES) * _nbytes((tr + HALO, c), F32)),
    )(dproj, dact, dact, ca, ca, proj, proj, proj, proj, cw, lg.reshape(1, c), lb.reshape(1, c))


def _branch_b_fwd(proj, cw, name):
    t = proj.shape[0]
    kw, c = cw.shape
    tr, nt, per = _seq_tiles(t)

    def body(sb_ref, sc_ref, sx_ref, hc_ref, hx_ref, cw_ref, o_ref, pbuf):
        i = pl.program_id(0)
        hp = hc_ref[...].astype(F32) * hx_ref[...].astype(F32)
        pbuf[pl.ds(0, HALO), :] = jnp.where(i > 0, hp, 0.0)
        pbuf[pl.ds(HALO, tr), :] = sc_ref[...].astype(F32) * sx_ref[...].astype(F32)
        u = jnp.zeros((tr, c), F32)
        for k in range(kw):
            u = u + cw_ref[pl.ds(k, 1), :] * pbuf[pl.ds(HALO - (kw - 1) + k, tr), :]
        o_ref[...] = (sb_ref[...].astype(F32) * u).astype(BF16)

    return _pcall(
        body, name=name,
        out_shape=jax.ShapeDtypeStruct((t, c), BF16),
        grid=(nt,),
        in_specs=[_col(tr, c, 2), _col(tr, c, 3), _col(tr, c, 4),
                  _prev_halo(per, c, 3), _prev_halo(per, c, 4), _const((kw, c))],
        out_specs=_col(tr, c, 0),
        scratch_shapes=[pltpu.VMEM((HALO + tr, c), F32)],
        compiler_params=_params(("arbitrary",), 4 * _nbytes((tr, c), F32), 6 * _nbytes((tr + HALO, c), F32)),
    )(proj, proj, proj, proj, proj, cw)


def _branch_b_bwd(dproj, dbin, proj, cw, name):
    t = proj.shape[0]
    kw, c = cw.shape
    tr, nt, per = _seq_tiles(t)
    n_halo = t // HALO

    def body(dproj_in, db_ref, dbh_ref, sb_ref, sbh_ref, sc_ref, sx_ref, hc_ref, hx_ref, cw_ref,
             out_ref, dcw_ref, pbuf, dubuf, res):
        del dproj_in
        i = pl.program_id(0)
        j = pl.program_id(1)

        @pl.when(j == 0)
        def _():
            sb = sb_ref[...].astype(F32)
            sc = sc_ref[...].astype(F32)
            sx = sx_ref[...].astype(F32)
            dbin_v = db_ref[...].astype(F32)
            hp = hc_ref[...].astype(F32) * hx_ref[...].astype(F32)
            pbuf[pl.ds(0, HALO), :] = jnp.where(i > 0, hp, 0.0)
            pbuf[pl.ds(HALO, tr), :] = sc * sx
            du = dbin_v * sb
            du_h = dbh_ref[...].astype(F32) * sbh_ref[...].astype(F32)
            dubuf[pl.ds(0, tr), :] = du
            dubuf[pl.ds(tr, HALO), :] = jnp.where(i < nt - 1, du_h, 0.0)

            @pl.when(i == 0)
            def _():
                dcw_ref[...] = jnp.zeros_like(dcw_ref)

            u = jnp.zeros((tr, c), F32)
            dp = jnp.zeros((tr, c), F32)
            for k in range(kw):
                shifted = pbuf[pl.ds(HALO - (kw - 1) + k, tr), :]
                u = u + cw_ref[pl.ds(k, 1), :] * shifted
                dp = dp + cw_ref[pl.ds(k, 1), :] * dubuf[pl.ds(kw - 1 - k, tr), :]
                dcw_ref[pl.ds(k, 1), :] += jnp.sum(du * shifted, axis=0, keepdims=True)
            res[0] = (dbin_v * u).astype(BF16)
            res[1] = (dp * sx).astype(BF16)
            res[2] = (dp * sc).astype(BF16)

        out_ref[...] = res[j]

    def colj(cb):
        return pl.BlockSpec((tr, c), lambda i, j: (i, cb))

    def prevj(cb):
        return pl.BlockSpec((HALO, c), lambda i, j: (jnp.maximum(i * per - 1, 0), cb))

    def nextj(cb):
        return pl.BlockSpec((HALO, c), lambda i, j: (jnp.minimum((i + 1) * per, n_halo - 1), cb))

    return _pcall(
        body, name=name,
        out_shape=[jax.ShapeDtypeStruct(dproj.shape, BF16), jax.ShapeDtypeStruct((kw, c), F32)],
        grid=(nt, 3),
        in_specs=[pl.BlockSpec(memory_space=pl.ANY),
                  colj(0), nextj(0), colj(2), nextj(2), colj(3), colj(4), prevj(3), prevj(4),
                  pl.BlockSpec((kw, c), lambda i, j: (0, 0))],
        out_specs=[pl.BlockSpec((tr, c), lambda i, j: (i, 2 + j)),
                   pl.BlockSpec((kw, c), lambda i, j: (0, 0))],
        scratch_shapes=[pltpu.VMEM((HALO + tr, c), F32), pltpu.VMEM((HALO + tr, c), F32),
                        pltpu.VMEM((3, tr, c), BF16)],
        input_output_aliases={0: 0},
        compiler_params=_params(("arbitrary", "arbitrary"), 6 * _nbytes((tr, c), F32),
                                10 * _nbytes((tr + HALO, c), F32)),
    )(dproj, dbin, dbin, proj, proj, proj, proj, proj, proj, cw)


def _softmax_rows(s):
    e = jnp.exp(s - jnp.max(s, axis=-1, keepdims=True))
    return e / jnp.sum(e, axis=-1, keepdims=True)


def _attn_fwd(proj, kv, name):
    t = proj.shape[0]
    m, c2 = kv.shape
    c = c2 // 2
    hd = c // N_HEADS
    ta = _tile(t, ATTN_TILE, SUBLANE_BF16)
    scale = hd ** -0.5

    def body(q_ref, kv_ref, o_ref):
        for h in range(N_HEADS):
            qh = q_ref[:, pl.ds(h * hd, hd)]
            kh = kv_ref[:, pl.ds(h * hd, hd)]
            vh = kv_ref[:, pl.ds(c + h * hd, hd)]
            s = lax.dot_general(qh, kh, (((1,), (1,)), ((), ())), preferred_element_type=F32) * scale
            p = _softmax_rows(s)
            o_ref[:, pl.ds(h * hd, hd)] = jnp.dot(p.astype(BF16), vh,
                                                  preferred_element_type=F32).astype(BF16)

    return _pcall(
        body, name=name,
        out_shape=jax.ShapeDtypeStruct((t, c), BF16),
        grid=(t // ta,),
        in_specs=[pl.BlockSpec((ta, c), lambda i: (i, 5)), _const((m, c2))],
        out_specs=pl.BlockSpec((ta, c), lambda i: (i, 0)),
        compiler_params=_params(("parallel",), 2 * _nbytes((ta, c), BF16) + _nbytes((m, c2), BF16),
                                8 * _nbytes((ta, m), F32)),
    )(proj, kv)


def _attn_bwd(dproj, d_o, proj, kv, name):
    t = proj.shape[0]
    m, c2 = kv.shape
    c = c2 // 2
    hd = c // N_HEADS
    ta = _tile(t, ATTN_TILE, SUBLANE_BF16)
    scale = hd ** -0.5

    def body(dproj_in, do_ref, q_ref, kv_ref, dq_ref, dkv_ref):
        del dproj_in

        @pl.when(pl.program_id(0) == 0)
        def _():
            dkv_ref[...] = jnp.zeros_like(dkv_ref)

        for h in range(N_HEADS):
            qh = q_ref[:, pl.ds(h * hd, hd)]
            kh = kv_ref[:, pl.ds(h * hd, hd)]
            vh = kv_ref[:, pl.ds(c + h * hd, hd)]
            doh = do_ref[:, pl.ds(h * hd, hd)]
            s = lax.dot_general(qh, kh, (((1,), (1,)), ((), ())), preferred_element_type=F32) * scale
            p = _softmax_rows(s)
            dp = lax.dot_general(doh, vh, (((1,), (1,)), ((), ())), preferred_element_type=F32)
            ds = (p * (dp - jnp.sum(dp * p, axis=-1, keepdims=True))).astype(BF16)
            dq_ref[:, pl.ds(h * hd, hd)] = (jnp.dot(ds, kh, preferred_element_type=F32) * scale).astype(BF16)
            dkv_ref[:, pl.ds(h * hd, hd)] += lax.dot_general(
                ds, qh, (((0,), (0,)), ((), ())), preferred_element_type=F32) * scale
            dkv_ref[:, pl.ds(c + h * hd, hd)] += lax.dot_general(
                p.astype(BF16), doh, (((0,), (0,)), ((), ())), preferred_element_type=F32)

    return _pcall(
        body, name=name,
        out_shape=[jax.ShapeDtypeStruct(dproj.shape, BF16), jax.ShapeDtypeStruct((m, c2), F32)],
        grid=(t // ta,),
        in_specs=[pl.BlockSpec(memory_space=pl.ANY),
                  pl.BlockSpec((ta, c), lambda i: (i, 0)), pl.BlockSpec((ta, c), lambda i: (i, 5)),
                  _const((m, c2))],
        out_specs=[pl.BlockSpec((ta, c), lambda i: (i, 5)), _const((m, c2))],
        input_output_aliases={0: 0},
        compiler_params=_params(("arbitrary",), 3 * _nbytes((ta, c), BF16) + 2 * _nbytes((m, c2), F32),
                                10 * _nbytes((ta, m), F32)),
    )(dproj, d_o, proj, kv)


def _merge_fwd(proj, ya, yb, yx, name):
    t, d = ya.shape
    tr = _tile(t, ROW_TILE, SUBLANE_BF16)

    def body(g_ref, ya_ref, yb_ref, yx_ref, o_ref):
        acc = jnp.zeros((tr, d), F32)
        for b, y_ref in enumerate((ya_ref, yb_ref, yx_ref)):
            acc = acc + _sigmoid(g_ref[:, pl.ds(b * d, d)].astype(F32)) * y_ref[...].astype(F32)
        o_ref[...] = acc.astype(BF16)

    row = pl.BlockSpec((tr, d), lambda i: (i, 0))
    return _pcall(
        body, name=name,
        out_shape=jax.ShapeDtypeStruct((t, d), BF16),
        grid=(t // tr,),
        in_specs=[pl.BlockSpec((tr, 3 * d), lambda i: (i, 1)), row, row, row],
        out_specs=row,
        compiler_params=_params(("parallel",), 7 * _nbytes((tr, d), BF16), 6 * _nbytes((tr, d), F32)),
    )(proj, ya, yb, yx)


def _merge_bwd(dmerged, proj, ya, yb, yx, name):
    t, d = ya.shape
    tr = _tile(t, ROW_TILE, SUBLANE_BF16)

    def body(dm_ref, g_ref, ya_ref, yb_ref, yx_ref, dg_ref, dya_ref, dyb_ref, dyx_ref):
        dm = dm_ref[...].astype(F32)
        for b, (y_ref, dy_ref) in enumerate(((ya_ref, dya_ref), (yb_ref, dyb_ref), (yx_ref, dyx_ref))):
            sg = _sigmoid(g_ref[:, pl.ds(b * d, d)].astype(F32))
            dg_ref[:, pl.ds(b * d, d)] = (dm * y_ref[...].astype(F32) * sg * (1.0 - sg)).astype(BF16)
            dy_ref[...] = (dm * sg).astype(BF16)

    row = pl.BlockSpec((tr, d), lambda i: (i, 0))
    gates = pl.BlockSpec((tr, 3 * d), lambda i: (i, 1))
    return _pcall(
        body, name=name,
        out_shape=[jax.ShapeDtypeStruct(proj.shape, BF16)] + [jax.ShapeDtypeStruct((t, d), BF16)] * 3,
        grid=(t // tr,),
        in_specs=[row, gates, row, row, row],
        out_specs=[gates, row, row, row],
        compiler_params=_params(("parallel",), 14 * _nbytes((tr, d), BF16), 8 * _nbytes((tr, d), F32)),
    )(dmerged, proj, ya, yb, yx)


def _as3(a):
    return a.reshape((-1,) + a.shape[-2:])


def _ew_call(fn, ins, n_out, out_dtypes, name, batch=None, prev=()):
    b, r, c = ins[0].shape
    lo, hi = batch if batch is not None else (0, b)
    tr = _ew_rows(r, c, len(ins) + n_out)
    n_in = len(ins) + len(prev)

    def body(*refs):
        outs = fn(*[ref[...] for ref in refs[:len(ins)]])
        for ref, val in zip(refs[n_in:], outs):
            ref[...] = val.astype(ref.dtype)

    spec = pl.BlockSpec((None, tr, c), lambda i, j: (i + lo, j, 0))
    return _pcall(
        body, name=name,
        out_shape=[jax.ShapeDtypeStruct((b, r, c), dt) for dt in out_dtypes],
        grid=(hi - lo, r // tr),
        in_specs=[spec] * len(ins) + [pl.BlockSpec(memory_space=pl.ANY)] * len(prev), out_specs=[spec] * n_out,
        input_output_aliases={len(ins) + k: k for k in range(len(prev))},
        compiler_params=_params(("parallel", "parallel"), (len(ins) + n_out) * _nbytes((tr, c), F32),
                                6 * _nbytes((tr, c), F32)),
    )(*ins, *prev)


def _my_chip():
    return 2 * lax.axis_index("x") + lax.axis_index("y")


def _my_core():
    return lax.axis_index("c")


def _cast_to_slot(w, l, name, after=()):
    _, r, cs = w.shape
    tr = _ew_rows(r, cs, 2)

    def body(w_ref, *rest):
        rest[-1][...] = w_ref[...].astype(BF16)

    return _pcall(
        body, name=name,
        out_shape=jax.ShapeDtypeStruct((N_CHIPS, r, cs), BF16),
        grid=(r // tr,),
        in_specs=[pl.BlockSpec((None, tr, cs), lambda i: (l, i, 0))]
        + [pl.BlockSpec(memory_space=pl.ANY)] * len(after),
        out_specs=pl.BlockSpec((None, tr, cs), lambda i: (_my_chip(), i, 0)),
        compiler_params=_params(("parallel",), 2 * _nbytes((tr, cs), F32)),
    )(w, *after)


def _add_half(part, theirs, name):
    _, rh, cs = theirs.shape
    tr = _ew_rows(rh, cs, 3)
    nrb = rh // tr

    def body(a_ref, b_ref, o_ref):
        o_ref[...] = (a_ref[...].astype(F32) + b_ref[...].astype(F32)).astype(BF16)

    half = pl.BlockSpec((None, tr, cs), lambda j, i: (j, i, 0))
    return _pcall(
        body, name=name,
        out_shape=jax.ShapeDtypeStruct((N_CHIPS, rh, cs), BF16),
        grid=(N_CHIPS, nrb),
        in_specs=[pl.BlockSpec((None, tr, cs), lambda j, i: (j, _my_core() * nrb + i, 0)), half],
        out_specs=half,
        compiler_params=_params(("parallel", "parallel"), 3 * _nbytes((tr, cs), F32)),
    )(part, theirs)


def _sum_chips(chip_sum, q, total, l, nl, name):
    _, rh, cs = q.shape
    tr = _ew_rows(rh, cs, 5)
    nrb = rh // tr

    def body(own_ref, q1_ref, q2_ref, q3_ref, *rest):
        acc = own_ref[...].astype(F32)
        for ref in (q1_ref, q2_ref, q3_ref):
            acc = acc + ref[...].astype(F32)
        rest[-1][...] = acc

    def slot(k):
        return pl.BlockSpec((None, tr, cs), lambda i: ((_my_chip() + k) % N_CHIPS, i, 0))

    in_specs = [slot(0), slot(1), slot(2), slot(3)]
    operands = [chip_sum, q, q, q]
    aliases = {}
    if total is not None:
        in_specs.append(HBM_SPEC)
        operands.append(total)
        aliases = {4: 0}
    return _pcall(
        body, name=name,
        out_shape=jax.ShapeDtypeStruct((nl, 2 * rh, cs), F32),
        grid=(nrb,),
        in_specs=in_specs,
        out_specs=pl.BlockSpec((None, tr, cs), lambda i: (l, _my_core() * nrb + i, 0)),
        input_output_aliases=aliases,
        compiler_params=_params(("parallel",), 5 * _nbytes((tr, cs), F32)),
    )(*operands)


def _adamw(w, g, m, v, name, batch=None, prev=()):
    shape = w.shape
    c1 = 1.0 - ADAM_B1 ** ADAM_STEP
    c2 = 1.0 - ADAM_B2 ** ADAM_STEP

    def fn(wv, gv, mv, vv):
        mn = ADAM_B1 * mv + (1.0 - ADAM_B1) * gv
        vn = ADAM_B2 * vv + (1.0 - ADAM_B2) * (gv * gv)
        delta = -ADAM_LR * ((mn / c1) / (jnp.sqrt(vn / c2) + ADAM_EPS) + ADAM_WD * wv)
        return delta, mn, vn

    outs = _ew_call(fn, [_as3(w), _as3(g), _as3(m), _as3(v)], 3, [F32] * 3, name,
                    batch=batch, prev=[_as3(p) for p in prev])
    return [o.reshape(shape) for o in outs]


def _sum_leading(q, name):
    b, nj, r, c = q.shape
    tr = _ew_rows(r, c, nj + 1)

    def body(q_ref, o_ref):
        acc = q_ref[0].astype(F32)
        for j in range(1, nj):
            acc = acc + q_ref[j].astype(F32)
        o_ref[...] = acc

    return _pcall(
        body, name=name,
        out_shape=jax.ShapeDtypeStruct((b, r, c), F32),
        grid=(b, r // tr),
        in_specs=[pl.BlockSpec((None, nj, tr, c), lambda i, j: (i, 0, j, 0))],
        out_specs=pl.BlockSpec((None, tr, c), lambda i, j: (i, j, 0)),
        compiler_params=_params(("parallel", "parallel"), (nj + 1) * _nbytes((tr, c), F32),
                                2 * _nbytes((tr, c), F32)),
    )(q)


HBM_SPEC = pl.BlockSpec(memory_space=pl.ANY)


def _place():
    x, y, c = lax.axis_index("x"), lax.axis_index("y"), lax.axis_index("c")
    peers = [(1 - x, y), (x, 1 - y), (1 - x, 1 - y)]
    return x, y, c, 2 * x + y, peers


def _remote(src, dst, send_sem, recv_sem, dev):
    return pltpu.make_async_remote_copy(src_ref=src, dst_ref=dst, send_sem=send_sem, recv_sem=recv_sem,
                                        device_id=dev, device_id_type=MESH)


IN_HBM = pl.BlockSpec(memory_space=pltpu.HBM)
IN_SEM = pl.BlockSpec(memory_space=pltpu.SEMAPHORE)
DATAFLOW = pltpu.SideEffectType.DATAFLOW_SIDE_EFFECTING
TOKEN = jax.ShapeDtypeStruct((8, LANE), F32)


def _in_hbm(arrays):
    return [pltpu.with_memory_space_constraint(a, pltpu.HBM) for a in arrays]


def _half_rows(ref, c):
    rh = ref.shape[1] // 2
    return pl.ds(c * rh, rh)


def _split_start(body_copies, passed, landing, name):
    n_pass, n_land = len(passed), len(landing)
    n_arr = n_pass + n_land
    n = 3 * n_pass

    def body(*refs):
        ins = refs[:n_arr]
        send, recv = refs[n_arr], refs[n_arr + 1]
        token = refs[-1]
        for src, dst, _, s_sem, r_sem, dev in body_copies(ins[:n_pass], ins[n_pass:], send, recv):
            _remote(src, dst, s_sem, r_sem, dev).start()
        token[...] = jnp.zeros_like(token)

    arrays = list(passed) + list(landing)
    return pl.pallas_call(
        body, name=name,
        out_shape=(pltpu.SemaphoreType.DMA((n,)), pltpu.SemaphoreType.DMA((n,)),
                   *[pltpu.HBM(a.shape, a.dtype) for a in arrays], TOKEN),
        in_specs=[IN_HBM] * n_arr,
        out_specs=(IN_SEM, IN_SEM, *[IN_HBM] * n_arr, pl.BlockSpec(memory_space=pltpu.VMEM)),
        input_output_aliases={i: 2 + i for i in range(n_arr)},
        compiler_params=pltpu.CompilerParams(has_side_effects=DATAFLOW),
    )(*_in_hbm(arrays))


def _split_wait(body_copies, send, recv, passed, landing, after, name):
    n_pass, n_land = len(passed), len(landing)
    n_arr = n_pass + n_land

    def body(*refs):
        ins = refs[:n_arr]
        send_ref, recv_ref = refs[n_arr], refs[n_arr + 1]
        for src, _, landed, s_sem, r_sem, dev in body_copies(ins[:n_pass], ins[n_pass:], send_ref, recv_ref):
            cp = _remote(src, landed, s_sem, r_sem, dev)
            cp.wait_send()
            cp.wait_recv()

    arrays = list(passed) + list(landing)
    return pl.pallas_call(
        body, name=name,
        out_shape=tuple(pltpu.HBM(a.shape, a.dtype) for a in arrays),
        in_specs=[IN_HBM] * n_arr + [IN_SEM, IN_SEM] + [HBM_SPEC] * len(after),
        out_specs=tuple([IN_HBM] * n_arr),
        input_output_aliases={i: i for i in range(n_arr)},
        compiler_params=pltpu.CompilerParams(has_side_effects=DATAFLOW),
    )(*arrays, send, recv, *after)


def _gather_copies(slots, _, send, recv):
    _, _, c, me, peers = _place()
    copies = []
    for a, ref in enumerate(slots):
        rows = _half_rows(ref, c)
        own = ref.at[me, rows, :]
        for j, (px, py) in enumerate(peers):
            k = 3 * a + j
            copies.append((own, own, ref.at[2 * px + py, rows, :], send.at[k], recv.at[k], (px, py, c)))
    return copies


def _exchange_copies(chip_sums, landing, send, recv):
    _, _, c, me, peers = _place()
    copies = []
    for a, (s_ref, q_ref) in enumerate(zip(chip_sums, landing)):
        for j, (px, py) in enumerate(peers):
            k = 3 * a + j
            copies.append((s_ref.at[2 * px + py], q_ref.at[me], q_ref.at[2 * px + py],
                           send.at[k], recv.at[k], (px, py, c)))
    return copies


def _gather_start(slots, name):
    out = _split_start(_gather_copies, slots, [], name)
    return out[0], out[1], list(out[2:-1]), out[-1]


def _gather_wait(send, recv, slots, after, name):
    return list(_split_wait(_gather_copies, send, recv, slots, [], after, name))


def _gather_forward(slots, name):
    na = len(slots)
    n = 3 * na

    def body(*refs):
        outs = refs[na:2 * na]
        fsend, frecv = refs[2 * na:]
        x, y, c, _, peers = _place()
        forwards = []
        for a, ref in enumerate(outs):
            for j, (px, py) in enumerate(peers):
                k = 3 * a + j
                landed = ref.at[2 * px + py, _half_rows(ref, c), :]
                forwards.append(_remote(landed, landed, fsend.at[k], frecv.at[k], (x, y, 1 - c)))
                forwards[-1].start()
        for a, ref in enumerate(outs):
            for j, (px, py) in enumerate(peers):
                k = 3 * a + j
                landed = ref.at[2 * px + py, _half_rows(ref, 1 - c), :]
                _remote(landed, landed, fsend.at[k], frecv.at[k], (x, y, 1 - c)).wait_recv()
        for cp in forwards:
            cp.wait_send()

    return _pcall(
        body, name=name,
        out_shape=[jax.ShapeDtypeStruct(s.shape, s.dtype) for s in slots],
        in_specs=[HBM_SPEC] * na, out_specs=[HBM_SPEC] * na,
        input_output_aliases={a: a for a in range(na)},
        scratch_shapes=[pltpu.SemaphoreType.DMA((n,))] * 2,
    )(*slots)


def _gather_small(v, name):
    def body(v_ref, o_ref, send, recv, lsem):
        x, y, c, me, peers = _place()
        local = pltpu.make_async_copy(v_ref, o_ref.at[me], lsem)
        local.start()
        sends = [_remote(v_ref, o_ref.at[me], send.at[j], recv.at[j], (px, py, c))
                 for j, (px, py) in enumerate(peers)]
        for cp in sends:
            cp.start()
        for j, (px, py) in enumerate(peers):
            _remote(v_ref, o_ref.at[2 * px + py], send.at[j], recv.at[j], (px, py, c)).wait_recv()
        for cp in sends:
            cp.wait_send()
        local.wait()

    return _pcall(
        body, name=name,
        out_shape=jax.ShapeDtypeStruct((N_CHIPS,) + v.shape, v.dtype),
        in_specs=[HBM_SPEC], out_specs=HBM_SPEC,
        scratch_shapes=[pltpu.SemaphoreType.DMA((3,)), pltpu.SemaphoreType.DMA((3,)), pltpu.SemaphoreType.DMA],
    )(v)


def _gather_all(v, name):
    def body(v_ref, o_ref, send, recv, lsem):
        x, y, c = lax.axis_index("x"), lax.axis_index("y"), lax.axis_index("c")
        me = 4 * x + 2 * y + c
        local = pltpu.make_async_copy(v_ref, o_ref.at[me], lsem)
        local.start()
        flips = [(fx, fy, fc) for fx in (0, 1) for fy in (0, 1) for fc in (0, 1)][1:]
        peers = [(x ^ fx, y ^ fy, c ^ fc) for fx, fy, fc in flips]
        sends = [_remote(v_ref, o_ref.at[me], send.at[k], recv.at[k], dev) for k, dev in enumerate(peers)]
        for cp in sends:
            cp.start()
        for k, (px, py, pc) in enumerate(peers):
            _remote(v_ref, o_ref.at[4 * px + 2 * py + pc], send.at[k], recv.at[k], (px, py, pc)).wait_recv()
        for cp in sends:
            cp.wait_send()
        local.wait()

    return _pcall(
        body, name=name,
        out_shape=jax.ShapeDtypeStruct((8,) + v.shape, v.dtype),
        in_specs=[HBM_SPEC], out_specs=HBM_SPEC,
        scratch_shapes=[pltpu.SemaphoreType.DMA((7,)), pltpu.SemaphoreType.DMA((7,)), pltpu.SemaphoreType.DMA],
    )(v)


def _send_halves(parts, name):
    na = len(parts)

    def body(*refs):
        ins, outs = refs[:na], refs[na:2 * na]
        send, recv = refs[2 * na:]
        x, y, c, _, _ = _place()
        sends = [_remote(ins[a].at[:, _half_rows(ins[a], 1 - c), :], outs[a], send.at[a], recv.at[a],
                         (x, y, 1 - c)) for a in range(na)]
        for cp in sends:
            cp.start()
        for cp in sends:
            cp.wait()

    return _pcall(
        body, name=name,
        out_shape=[jax.ShapeDtypeStruct((N_CHIPS, p.shape[1] // 2, p.shape[2]), p.dtype) for p in parts],
        in_specs=[HBM_SPEC] * na, out_specs=[HBM_SPEC] * na,
        scratch_shapes=[pltpu.SemaphoreType.DMA((na,)), pltpu.SemaphoreType.DMA((na,))],
    )(*parts)


def _exchange_start(chip_sums, name):
    na = len(chip_sums)
    landing = [lax.empty(s.shape, s.dtype) for s in chip_sums]
    out = _split_start(_exchange_copies, chip_sums, landing, name)
    return out[0], out[1], list(out[2:2 + na]), list(out[2 + na:2 + 2 * na]), out[-1]


def _exchange_wait(send, recv, chip_sums, landing, after, name):
    na = len(chip_sums)
    out = _split_wait(_exchange_copies, send, recv, chip_sums, landing, after, name)
    return list(out[:na]), list(out[na:])


def _join_halves(reds, lo, hi, name, after=()):
    na = len(reds)
    n_in = na + len(after)
    layers = pl.ds(lo, hi - lo)

    def body(*refs):
        outs = refs[n_in:n_in + na]
        send, recv = refs[n_in + na:]
        x, y, c, _, _ = _place()
        sends = []
        for a, ref in enumerate(outs):
            rh = ref.shape[1] // 2
            mine = ref.at[layers, pl.ds(c * rh, rh), :]
            sends.append(_remote(mine, mine, send.at[a], recv.at[a], (x, y, 1 - c)))
            sends[-1].start()
        for a, ref in enumerate(outs):
            rh = ref.shape[1] // 2
            other = ref.at[layers, pl.ds((1 - c) * rh, rh), :]
            _remote(other, other, send.at[a], recv.at[a], (x, y, 1 - c)).wait_recv()
        for cp in sends:
            cp.wait_send()

    return _pcall(
        body, name=name,
        out_shape=[jax.ShapeDtypeStruct(r.shape, r.dtype) for r in reds],
        in_specs=[HBM_SPEC] * n_in, out_specs=[HBM_SPEC] * na,
        input_output_aliases={a: a for a in range(na)},
        scratch_shapes=[pltpu.SemaphoreType.DMA((na,)), pltpu.SemaphoreType.DMA((na,))],
    )(*reds, *after)


COL_SHARDED = ("w_in", "w_a_out", "w_b_out", "w_x_out", "w_up")
ROW_SHARDED = ("w_kv", "w_o", "w_down")
BIG = COL_SHARDED + ROW_SHARDED
SMALL_REPLICATED = ("g_mix_pre", "conv_a_b", "ln_a_g", "ln_a_b", "g_mem", "g_mix_post", "g_mlp_pre", "g_mlp_post")
SMALL_SHARDED = ("conv_a_w", "conv_b_w")
WEIGHT_ORDER = ("g_mix_pre", "w_in", "conv_a_w", "conv_a_b", "ln_a_g", "ln_a_b", "w_a_out", "conv_b_w", "w_b_out",
                "g_mem", "w_kv", "w_x_out", "w_o", "g_mix_post", "g_mlp_pre", "w_up", "w_down", "g_mlp_post")


def _pack_rows(arrays, width):
    rows = []
    for a in arrays:
        r = a.reshape(-1, width)
        rows.append(jnp.pad(r, ((0, (-r.shape[0]) % 8), (0, 0))))
    return jnp.concatenate(rows, axis=0)


def _unpack_rows(packed, like, width):
    out, at = [], 0
    for a in like:
        n = a.size // width
        out.append(packed[at:at + n].reshape(a.shape))
        at += n + (-n) % 8
    return out


def _step(w, m, v, x, mem, target):
    nl = w["w_in"].shape[0]
    d = x.shape[1]
    c = w["conv_a_b"].shape[1]
    ka = w["conv_a_w"].shape[1]

    conv_pack = jnp.concatenate([w["conv_a_w"], w["conv_b_w"]], axis=1)
    conv_rows = conv_pack.shape[1]
    conv_pack = jnp.pad(conv_pack, ((0, 0), (0, (-conv_rows) % 8), (0, 0)))
    conv_all = _gather_small(conv_pack, "gather_conv")
    slots = {name: [_cast_to_slot(w[name], 0, "cast_" + name, after=[conv_all])] for name in BIG}
    conv_all = jnp.moveaxis(conv_all, 0, 2).reshape(nl, conv_pack.shape[1], c)
    cw_a, cw_b = conv_all[:, :ka], conv_all[:, ka:conv_rows]

    def start_gather(l):
        return _gather_start([slots[name][l] for name in BIG], "gather_start_%d" % l)

    def finish_gather(l, pending, after):
        send, recv, thru, _ = pending
        landed = _gather_wait(send, recv, thru, after, "gather_wait_%d" % l)
        arrays = _gather_forward(landed, "gather_forward")
        return {name: (a.reshape(1, -1, a.shape[-1]) if name in ROW_SHARDED else a)
                for name, a in zip(BIG, arrays)}

    pending = start_gather(0)
    for name in BIG:
        slots[name] += [_cast_to_slot(w[name], l, "cast_" + name, after=[pending[3]]) for l in range(1, nl)]
    (h,) = _norm_fwd(x, "norm_first", g_next=w["g_mix_pre"][0])
    full = [finish_gather(0, pending, [h] + [s for name in BIG for s in slots[name][1:]])]

    saved = []
    xl = x
    dy = loss_cols = None
    for l in range(nl):
        s = {"x": xl, "h": h}
        fw = full[l]
        token = None
        if l + 1 < nl:
            pending = start_gather(l + 1)
            token = pending[3]
        proj = _mm_nn(h, fw["w_in"], BF16, "mm_in", after=token)
        ca, act_a = _branch_a_fwd(proj, cw_a[l], w["conv_a_b"][l], w["ln_a_g"][l], w["ln_a_b"][l], "branch_a_fwd")
        b_in = _branch_b_fwd(proj, cw_b[l], "branch_b_fwd")
        (mem_n,) = _norm_fwd(mem, "norm_mem", g_next=w["g_mem"][l])
        kv = _mm_nn(mem_n, fw["w_kv"], BF16, "mm_kv")
        att = _attn_fwd(proj, kv, "attn_fwd")
        ya = _mm_nn(act_a, fw["w_a_out"], BF16, "mm_a_out")
        yb = _mm_nn(b_in, fw["w_b_out"], BF16, "mm_b_out")
        yx = _mm_nn(att, fw["w_x_out"], BF16, "mm_x_out")
        merged = _merge_fwd(proj, ya, yb, yx, "merge_fwd")
        z = _mm_nn(merged, fw["w_o"], F32, "mm_o")
        x1, h2 = _norm_fwd(xl, "norm_mid", z=z, g_post=w["g_mix_post"][l], g_next=w["g_mlp_pre"][l])
        up = _mm_nn(h2, fw["w_up"], BF16, "mm_up")
        f = _mm_nn(up, fw["w_down"], F32, "mm_down", a_act="relu2")
        s.update(proj=proj, ca=ca, act_a=act_a, b_in=b_in, mem_n=mem_n, kv=kv, att=att, ya=ya, yb=yb, yx=yx,
                 merged=merged, z=z, x1=x1, h2=h2, up=up, f=f)
        saved.append(s)
        if l + 1 < nl:
            full.append(finish_gather(l + 1, pending, [f]))
        if l + 1 < nl:
            xl, h = _norm_fwd(x1, "norm_mid", z=f, g_post=w["g_mlp_post"][l], g_next=w["g_mix_pre"][l + 1])
        else:
            dy, loss_cols = _norm_fwd(x1, "norm_loss", z=f, g_post=w["g_mlp_post"][l], target=target)

    part = {name: [None] * nl for name in BIG}
    small = {name: [None] * nl for name in SMALL_REPLICATED + SMALL_SHARDED}
    dxo = dy
    d_f, small["g_mlp_post"][nl - 1] = _norm_bwd("norm_bwd_top", dxo=dy,
                                                 post=(saved[-1]["f"], w["g_mlp_post"][nl - 1]))
    grad_x = None
    totals = {name: None for name in BIG}

    def start_exchange(l):
        parts = [part[name][l] for name in BIG]
        theirs = _send_halves(parts, "rs_send")
        chip_sums = [_add_half(p, t, "rs_add_" + name) for name, p, t in zip(BIG, parts, theirs)]
        return _exchange_start(chip_sums, "rs_xchg_start_%d" % l)

    def finish_exchange(l, pending, after):
        send, recv, chip_sums, landing, _ = pending
        chip_sums, q = _exchange_wait(send, recv, chip_sums, landing, after, "rs_xchg_wait_%d" % l)
        for name, own, got in zip(BIG, chip_sums, q):
            totals[name] = _sum_chips(own, got, totals[name], l, nl, "rs_sum_" + name)

    exchange = None
    for l in reversed(range(nl)):
        s = saved[l]
        fw = full[l]
        d_up = _mm_nt(d_f, fw["w_down"], BF16, "mm_down_dx", relu2_of=s["up"],
                      after=None if exchange is None else exchange[4])
        part["w_down"][l] = _mm_tn(s["up"], d_f, 1, "mm_down_dw", a_act="relu2").reshape(N_CHIPS, -1, d)
        d_h2 = _mm_nt(d_up, fw["w_up"], F32, "mm_up_dx")
        part["w_up"][l] = _mm_tn(s["h2"], d_up, N_CHIPS, "mm_up_dw")
        dx1, d_z, small["g_mlp_pre"][l], small["g_mix_post"][l] = _norm_bwd(
            "norm_bwd_mid", dxo=dxo, pre=(d_h2, s["x1"], w["g_mlp_pre"][l]), post=(s["z"], w["g_mix_post"][l]))
        d_merged = _mm_nt(d_z, fw["w_o"], BF16, "mm_o_dx")
        part["w_o"][l] = _mm_tn(s["merged"], d_z, 1, "mm_o_dw").reshape(N_CHIPS, -1, d)
        dproj, d_ya, d_yb, d_yx = _merge_bwd(d_merged, s["proj"], s["ya"], s["yb"], s["yx"], "merge_bwd")
        d_act_a = _mm_nt(d_ya, fw["w_a_out"], BF16, "mm_a_out_dx")
        part["w_a_out"][l] = _mm_tn(s["act_a"], d_ya, N_CHIPS, "mm_a_out_dw")
        d_b_in = _mm_nt(d_yb, fw["w_b_out"], BF16, "mm_b_out_dx")
        part["w_b_out"][l] = _mm_tn(s["b_in"], d_yb, N_CHIPS, "mm_b_out_dw")
        d_att = _mm_nt(d_yx, fw["w_x_out"], BF16, "mm_x_out_dx")
        part["w_x_out"][l] = _mm_tn(s["att"], d_yx, N_CHIPS, "mm_x_out_dw")
        dproj, small["ln_a_g"][l], small["ln_a_b"][l], small["conv_a_b"][l], small["conv_a_w"][l] = _branch_a_bwd(
            dproj, d_act_a, s["ca"], s["proj"], cw_a[l], w["ln_a_g"][l], w["ln_a_b"][l], "branch_a_bwd")
        dproj, small["conv_b_w"][l] = _branch_b_bwd(dproj, d_b_in, s["proj"], cw_b[l], "branch_b_bwd")
        dproj, d_kv = _attn_bwd(dproj, d_att, s["proj"], s["kv"], "attn_bwd")
        part["w_kv"][l] = _mm_tn(s["mem_n"], d_kv, 1, "mm_kv_dw").reshape(N_CHIPS, -1, 2 * c)
        d_mem_n = _mm_nt(d_kv, fw["w_kv"], F32, "mm_kv_dx")
        (small["g_mem"][l],) = _norm_bwd("norm_bwd_mem", pre=(d_mem_n, mem, w["g_mem"][l]), want_dx=False)
        d_h = _mm_nt(dproj, fw["w_in"], F32, "mm_in_dx")
        part["w_in"][l] = _mm_tn(s["h"], dproj, N_CHIPS, "mm_in_dw")
        if exchange is not None:
            finish_exchange(l + 1, exchange, [d_h])
        exchange = start_exchange(l)
        if l > 0:
            dxo, d_f, small["g_mix_pre"][l], small["g_mlp_post"][l - 1] = _norm_bwd(
                "norm_bwd_mid", dxo=dx1, pre=(d_h, s["x"], w["g_mix_pre"][l]),
                post=(saved[l - 1]["f"], w["g_mlp_post"][l - 1]))
        else:
            grad_x, small["g_mix_pre"][0] = _norm_bwd("norm_bwd_last", dxo=dx1,
                                                      pre=(d_h, s["x"], w["g_mix_pre"][0]))

    grads, delta, new_m, new_v = {}, {}, {}, {}
    small_names = SMALL_REPLICATED + SMALL_SHARDED
    stacked = [jnp.stack(small[n]) for n in small_names]
    packed = _pack_rows(stacked, c)
    total = _sum_leading(_gather_all(packed, "gather_small_grads")[None], "sum_small_grads")[0]
    reduced = dict(zip(small_names, _unpack_rows(total, stacked, c)))
    chip = 2 * lax.axis_index("x") + lax.axis_index("y")
    cs = c // N_CHIPS
    for name in SMALL_SHARDED:
        grads[name] = lax.dynamic_slice_in_dim(reduced[name], chip * cs, cs, axis=2)
    for name in SMALL_REPLICATED:
        grads[name] = reduced[name].reshape(w[name].shape)
    for group, width in ((SMALL_REPLICATED, c), (SMALL_SHARDED, cs)):
        packs = [_pack_rows([src[n] for n in group], width)[None] for src in (w, grads, m, v)]
        outs = _adamw(*packs, "adamw_small_%d" % width)
        for dst, out in zip((delta, new_m, new_v), outs):
            dst.update(zip(group, _unpack_rows(out[0], [w[n] for n in group], width)))

    state = {name: () for name in BIG}
    if nl > 1:
        totals.update(zip(BIG, _join_halves([totals[name] for name in BIG], 1, nl, "rs_join_upper",
                                            after=[exchange[4]])))
        for name in BIG:
            state[name] = _adamw(w[name], totals[name], m[name], v[name], "adamw_" + name, batch=(1, nl))
    finish_exchange(0, exchange, [state[name][0] for name in BIG] if nl > 1 else [grad_x])
    grads.update(zip(BIG, _join_halves([totals[name] for name in BIG], 0, 1, "rs_join_first")))
    for name in BIG:
        delta[name], new_m[name], new_v[name] = _adamw(w[name], grads[name], m[name], v[name], "adamw_" + name,
                                                       batch=(0, 1), prev=state[name])

    loss = lax.psum(0.5 * jnp.sum(loss_cols) / d, ("x", "y", "c"))
    return loss, grad_x, grads, delta, new_m, new_v


def kernel(x, mem, g_mix_pre, w_in, conv_a_w, conv_a_b, ln_a_g, ln_a_b, w_a_out, conv_b_w, w_b_out, g_mem, w_kv, w_x_out, w_o, g_mix_post, g_mlp_pre, w_up, w_down, g_mlp_post, loss_target, m_g_mix_pre, m_w_in, m_conv_a_w, m_conv_a_b, m_ln_a_g, m_ln_a_b, m_w_a_out, m_conv_b_w, m_w_b_out, m_g_mem, m_w_kv, m_w_x_out, m_w_o, m_g_mix_post, m_g_mlp_pre, m_w_up, m_w_down, m_g_mlp_post, v_g_mix_pre, v_w_in, v_conv_a_w, v_conv_a_b, v_ln_a_g, v_ln_a_b, v_w_a_out, v_conv_b_w, v_w_b_out, v_g_mem, v_w_kv, v_w_x_out, v_w_o, v_g_mix_post, v_g_mlp_pre, v_w_up, v_w_down, v_g_mlp_post):
    w = dict(g_mix_pre=g_mix_pre, w_in=w_in, conv_a_w=conv_a_w, conv_a_b=conv_a_b, ln_a_g=ln_a_g, ln_a_b=ln_a_b,
             w_a_out=w_a_out, conv_b_w=conv_b_w, w_b_out=w_b_out, g_mem=g_mem, w_kv=w_kv, w_x_out=w_x_out, w_o=w_o,
             g_mix_post=g_mix_post, g_mlp_pre=g_mlp_pre, w_up=w_up, w_down=w_down, g_mlp_post=g_mlp_post)
    m = dict(g_mix_pre=m_g_mix_pre, w_in=m_w_in, conv_a_w=m_conv_a_w, conv_a_b=m_conv_a_b, ln_a_g=m_ln_a_g,
             ln_a_b=m_ln_a_b, w_a_out=m_w_a_out, conv_b_w=m_conv_b_w, w_b_out=m_w_b_out, g_mem=m_g_mem, w_kv=m_w_kv,
             w_x_out=m_w_x_out, w_o=m_w_o, g_mix_post=m_g_mix_post, g_mlp_pre=m_g_mlp_pre, w_up=m_w_up,
             w_down=m_w_down, g_mlp_post=m_g_mlp_post)
    v = dict(g_mix_pre=v_g_mix_pre, w_in=v_w_in, conv_a_w=v_conv_a_w, conv_a_b=v_conv_a_b, ln_a_g=v_ln_a_g,
             ln_a_b=v_ln_a_b, w_a_out=v_w_a_out, conv_b_w=v_conv_b_w, w_b_out=v_w_b_out, g_mem=v_g_mem, w_kv=v_w_kv,
             w_x_out=v_w_x_out, w_o=v_w_o, g_mix_post=v_g_mix_post, g_mlp_pre=v_g_mlp_pre, w_up=v_w_up,
             w_down=v_w_down, g_mlp_post=v_g_mlp_post)
    loss, grad_x, grads, delta, new_m, new_v = _step(w, m, v, x[0], mem[0], loss_target[0])
    out = [loss, grad_x[None]]
    for group in (grads, delta, new_m, new_v):
        out += [group[n] for n in WEIGHT_ORDER]
    return tuple(out)
```

```python
import functools

import jax
import jax.numpy as jnp
from jax import lax
from jax.experimental import pallas as pl
from jax.experimental.pallas import tpu as pltpu

F32 = jnp.float32
BF16 = jnp.bfloat16
MESH = pl.DeviceIdType.MESH

NORM_EPS = 1e-6
N_HEADS = 4
ADAM_LR = 0.001
ADAM_B1 = 0.9
ADAM_B2 = 0.999
ADAM_EPS = 1e-08
ADAM_WD = 0.01
ADAM_STEP = 10

N_CHIPS = 4
V7X_VMEM_BYTES = 64 * 1024 * 1024
VMEM_CAP = V7X_VMEM_BYTES - 8 * 1024 * 1024
LANE = 128
SUBLANES = 8
SUBLANE_BF16 = 16
HALO = 32
ROW_TILE = 256
CONV_ROWS = 32
CONV_LANES = 512
ATTN_TILE = 512
MM_TM = 1024
MM_TN = 1024
MM_TK = 2048
MM_TC = 3072
EW_VMEM_BYTES = 24 * 1024 * 1024


def _tile(n, pref, align):
    if n <= pref:
        return n
    t = (pref // align) * align
    while t >= align:
        if n % t == 0:
            return t
        t -= align
    return n


def _ew_rows(r, c, n_arrays):
    return _tile(r, max(SUBLANE_BF16, EW_VMEM_BYTES // (2 * n_arrays * c * 4)), SUBLANE_BF16)


def _nbytes(shape, dtype):
    n = 1
    for s in shape:
        if s is not None:
            n *= s
    return n * jnp.dtype(dtype).itemsize


def _params(semantics, block_bytes, temp_bytes=0):
    need = 2 * block_bytes + temp_bytes + (4 << 20)
    return pltpu.CompilerParams(dimension_semantics=semantics,
                                vmem_limit_bytes=int(min(max(need, 16 << 20), VMEM_CAP)))


def _sigmoid(v):
    return 1.0 / (1.0 + jnp.exp(-v))


def _pcall(body, **kwargs):
    call = pl.pallas_call(body, **kwargs)
    return lambda *operands: call(*[pltpu.with_memory_space_constraint(o, pltpu.HBM) for o in operands])


def _mm_nn(a, b3, out_dtype, name, a_act=None, after=None):
    m, k = a.shape
    s, k2, ns = b3.shape
    assert k == k2
    tm = _tile(m, MM_TM, SUBLANE_BF16)
    tn = _tile(ns, MM_TN, LANE)
    tk = _tile(k, MM_TK, LANE)
    q = ns // tn
    nk = k // tk

    n_in = 2 if after is None else 3

    def body(*refs):
        a_ref, b_ref, o_ref = refs[0], refs[1], refs[n_in]
        scratch = refs[n_in + 1:]
        av = a_ref[...]
        if a_act == "relu2":
            r = jnp.maximum(av.astype(BF16), 0.0)
            av = r * r
        p = jnp.dot(av.astype(BF16), b_ref[...].astype(BF16), preferred_element_type=F32)
        if nk == 1:
            o_ref[...] = p.astype(o_ref.dtype)
        else:
            acc, = scratch
            kk = pl.program_id(2)

            @pl.when(kk == 0)
            def _():
                acc[...] = p

            @pl.when(kk > 0)
            def _():
                acc[...] += p

            @pl.when(kk == nk - 1)
            def _():
                o_ref[...] = acc[...].astype(o_ref.dtype)

    blocks = (_nbytes((tm, tk), a.dtype) + _nbytes((tk, tn), b3.dtype) + _nbytes((tm, tn), out_dtype))
    return _pcall(
        body, name=name,
        out_shape=jax.ShapeDtypeStruct((m, s * ns), out_dtype),
        grid=(m // tm, s * q, nk),
        in_specs=[pl.BlockSpec((tm, tk), lambda i, j, c: (i, c)),
                  pl.BlockSpec((None, tk, tn), lambda i, j, c: (j // q, c, j % q))]
        + ([] if after is None else [pl.BlockSpec(memory_space=pl.ANY)]),
        out_specs=pl.BlockSpec((tm, tn), lambda i, j, c: (i, j)),
        scratch_shapes=[pltpu.VMEM((tm, tn), F32)] if nk > 1 else [],
        compiler_params=_params(("parallel", "parallel", "arbitrary"), blocks,
                                3 * _nbytes((tm, tn), F32) + _nbytes((tm, tk), F32)),
    )(*([a, b3] if after is None else [a, b3, after]))


def _mm_nt(a, b3, out_dtype, name, relu2_of=None, after=None):
    m, n = a.shape
    s, kd, ns = b3.shape
    assert n == s * ns
    tm = _tile(m, MM_TM, SUBLANE_BF16)
    tj = _tile(kd, MM_TN, LANE)
    tc = _tile(ns, MM_TC, LANE)
    q = ns // tc
    nc = s * q

    n_in = 2 + (relu2_of is not None) + (after is not None)

    def body(*refs):
        a_ref, b_ref, u_ref, o_ref = refs[0], refs[1], refs[2], refs[n_in]
        scratch = refs[n_in + 1:]

        def finish(p):
            if relu2_of is not None:
                p = p * (2.0 * jnp.maximum(u_ref[...].astype(F32), 0.0))
            o_ref[...] = p.astype(o_ref.dtype)

        p = lax.dot_general(a_ref[...].astype(BF16), b_ref[...].astype(BF16),
                            (((1,), (1,)), ((), ())), preferred_element_type=F32)
        if nc == 1:
            finish(p)
        else:
            acc, = scratch
            cc = pl.program_id(2)

            @pl.when(cc == 0)
            def _():
                acc[...] = p

            @pl.when(cc > 0)
            def _():
                acc[...] += p

            @pl.when(cc == nc - 1)
            def _():
                finish(acc[...])

    in_specs = [pl.BlockSpec((tm, tc), lambda i, j, c: (i, c)),
                pl.BlockSpec((None, tj, tc), lambda i, j, c: (c // q, j, c % q))]
    operands = [a, b3]
    blocks = _nbytes((tm, tc), a.dtype) + _nbytes((tj, tc), b3.dtype) + _nbytes((tm, tj), out_dtype)
    if relu2_of is not None:
        in_specs.append(pl.BlockSpec((tm, tj), lambda i, j, c: (i, j)))
        operands.append(relu2_of)
        blocks += _nbytes((tm, tj), relu2_of.dtype)
    if after is not None:
        in_specs.append(pl.BlockSpec(memory_space=pl.ANY))
        operands.append(after)
    return _pcall(
        body, name=name,
        out_shape=jax.ShapeDtypeStruct((m, kd), out_dtype),
        grid=(m // tm, kd // tj, nc),
        in_specs=in_specs,
        out_specs=pl.BlockSpec((tm, tj), lambda i, j, c: (i, j)),
        scratch_shapes=[pltpu.VMEM((tm, tj), F32)] if nc > 1 else [],
        compiler_params=_params(("parallel", "parallel", "arbitrary"), blocks,
                                3 * _nbytes((tm, tj), F32)),
    )(*operands)


def _mm_tn(a, g, out_shards, name, a_act=None):
    m, ka = a.shape
    m2, n = g.shape
    assert m == m2
    ns = n // out_shards
    ta = _tile(ka, MM_TM, LANE)
    tn = _tile(ns, MM_TN, LANE)
    tm = _tile(m, MM_TK, SUBLANE_BF16)
    q = ns // tn
    nm = m // tm

    def body(a_ref, g_ref, o_ref, *scratch):
        av = a_ref[...]
        if a_act == "relu2":
            r = jnp.maximum(av.astype(BF16), 0.0)
            av = r * r
        p = lax.dot_general(av.astype(BF16), g_ref[...].astype(BF16),
                            (((0,), (0,)), ((), ())), preferred_element_type=F32)
        if nm == 1:
            o_ref[...] = p.astype(o_ref.dtype)
        else:
            acc, = scratch
            cc = pl.program_id(2)

            @pl.when(cc == 0)
            def _():
                acc[...] = p

            @pl.when(cc > 0)
            def _():
                acc[...] += p

            @pl.when(cc == nm - 1)
            def _():
                o_ref[...] = acc[...].astype(o_ref.dtype)

    blocks = _nbytes((tm, ta), a.dtype) + _nbytes((tm, tn), g.dtype) + _nbytes((ta, tn), BF16)
    return _pcall(
        body, name=name,
        out_shape=jax.ShapeDtypeStruct((out_shards, ka, ns), BF16),
        grid=(ka // ta, out_shards * q, nm),
        in_specs=[pl.BlockSpec((tm, ta), lambda i, j, c: (c, i)),
                  pl.BlockSpec((tm, tn), lambda i, j, c: (c, j))],
        out_specs=pl.BlockSpec((None, ta, tn), lambda i, j, c: (j // q, i, j % q)),
        scratch_shapes=[pltpu.VMEM((ta, tn), F32)] if nm > 1 else [],
        compiler_params=_params(("parallel", "parallel", "arbitrary"), blocks,
                                3 * _nbytes((ta, tn), F32) + _nbytes((tm, ta), F32)),
    )(a, g)


def _rms_scale(v):
    return lax.rsqrt(jnp.mean(v * v, axis=-1, keepdims=True) + NORM_EPS)


def _norm_fwd(x, name, *, z=None, g_post=None, g_next=None, target=None):
    t, d = x.shape
    tr = _tile(t, ROW_TILE, SUBLANE_BF16)
    has_res, has_next, has_loss = z is not None, g_next is not None, target is not None

    def body(*refs):
        it = iter(refs)
        x_ref = next(it)
        z_ref, gp_ref = (next(it), next(it)) if has_res else (None, None)
        gn_ref = next(it) if has_next else None
        t_ref = next(it) if has_loss else None
        xv = x_ref[...]
        if has_res:
            zv = z_ref[...].astype(F32)
            xv = xv + zv * _rms_scale(zv) * gp_ref[...]
            if not has_loss:
                next(it)[...] = xv
        if has_next:
            next(it)[...] = (xv * _rms_scale(xv) * gn_ref[...]).astype(BF16)
        if has_loss:
            e = xv - t_ref[...]
            next(it)[...] = e * (1.0 / d)
            ls_ref = next(it)

            @pl.when(pl.program_id(0) == 0)
            def _():
                ls_ref[...] = jnp.zeros_like(ls_ref)

            ls_ref[...] += jnp.sum(e * e, axis=0, keepdims=True)

    row = pl.BlockSpec((tr, d), lambda i: (i, 0))
    vec = pl.BlockSpec((1, d), lambda i: (0, 0))
    operands, in_specs, out_shape, out_specs = [x], [row], [], []
    if has_res:
        operands += [z, g_post.reshape(1, d)]
        in_specs += [row, vec]
        if not has_loss:
            out_shape.append(jax.ShapeDtypeStruct((t, d), F32))
            out_specs.append(row)
    if has_next:
        operands.append(g_next.reshape(1, d))
        in_specs.append(vec)
        out_shape.append(jax.ShapeDtypeStruct((t, d), BF16))
        out_specs.append(row)
    if has_loss:
        operands.append(target)
        in_specs.append(row)
        out_shape += [jax.ShapeDtypeStruct((t, d), F32), jax.ShapeDtypeStruct((1, d), F32)]
        out_specs += [row, vec]
    return _pcall(
        body, name=name, out_shape=out_shape, grid=(t // tr,),
        in_specs=in_specs, out_specs=out_specs,
        compiler_params=_params(("arbitrary",), 5 * _nbytes((tr, d), F32), 4 * _nbytes((tr, d), F32)),
    )(*operands)


def _norm_bwd(name, *, dxo=None, pre=None, post=None, want_dx=True):
    ref_arr = dxo if dxo is not None else pre[1]
    t, d = ref_arr.shape
    tr = _tile(t, ROW_TILE, SUBLANE_BF16)
    has_dxo, has_pre, has_post = dxo is not None, pre is not None, post is not None

    def body(*refs):
        it = iter(refs)
        dxo_ref = next(it) if has_dxo else None
        dh_ref, xin_ref, gpre_ref = (next(it), next(it), next(it)) if has_pre else (None,) * 3
        z_ref, gpost_ref = (next(it), next(it)) if has_post else (None, None)
        dx_ref = next(it) if (has_pre and want_dx) else None
        dz_ref = next(it) if has_post else None
        dgpre_ref = next(it) if has_pre else None
        dgpost_ref = next(it) if has_post else None
        first = pl.program_id(0) == 0

        dx = dxo_ref[...] if has_dxo else None
        if has_pre:
            xin = xin_ref[...]
            dh = dh_ref[...].astype(F32)
            r = _rms_scale(xin)
            gy = dh * gpre_ref[...]
            dloc = r * gy - xin * (r * r * r) * jnp.mean(gy * xin, axis=-1, keepdims=True)
            dx = dloc if dx is None else dx + dloc
            if want_dx:
                dx_ref[...] = dx

            @pl.when(first)
            def _():
                dgpre_ref[...] = jnp.zeros_like(dgpre_ref)

            dgpre_ref[...] += jnp.sum(dh * xin * r, axis=0, keepdims=True)
        if has_post:
            zv = z_ref[...].astype(F32)
            r = _rms_scale(zv)
            gy = dx * gpost_ref[...]
            dz = r * gy - zv * (r * r * r) * jnp.mean(gy * zv, axis=-1, keepdims=True)
            dz_ref[...] = dz.astype(BF16)

            @pl.when(first)
            def _():
                dgpost_ref[...] = jnp.zeros_like(dgpost_ref)

            dgpost_ref[...] += jnp.sum(dx * zv * r, axis=0, keepdims=True)

    row = pl.BlockSpec((tr, d), lambda i: (i, 0))
    vec = pl.BlockSpec((1, d), lambda i: (0, 0))
    operands, in_specs, out_shape, out_specs = [], [], [], []
    if has_dxo:
        operands.append(dxo)
        in_specs.append(row)
    if has_pre:
        operands += [pre[0], pre[1], pre[2].reshape(1, d)]
        in_specs += [row, row, vec]
    if has_post:
        operands += [post[0], post[1].reshape(1, d)]
        in_specs += [row, vec]
    if has_pre and want_dx:
        out_shape.append(jax.ShapeDtypeStruct((t, d), F32))
        out_specs.append(row)
    if has_post:
        out_shape.append(jax.ShapeDtypeStruct((t, d), BF16))
        out_specs.append(row)
    if has_pre:
        out_shape.append(jax.ShapeDtypeStruct((1, d), F32))
        out_specs.append(vec)
    if has_post:
        out_shape.append(jax.ShapeDtypeStruct((1, d), F32))
        out_specs.append(vec)
    return _pcall(
        body, name=name, out_shape=out_shape, grid=(t // tr,),
        in_specs=in_specs, out_specs=out_specs,
        compiler_params=_params(("arbitrary",), 6 * _nbytes((tr, d), F32), 6 * _nbytes((tr, d), F32)),
    )(*operands)


def _seq_tiles(t):
    tr = _tile(t, ROW_TILE, HALO)
    assert tr % HALO == 0 and t % tr == 0
    return tr, t // tr, tr // HALO


def _col(tr, width, cb):
    return pl.BlockSpec((tr, width), lambda i: (i, cb))


def _prev_halo(per, width, cb):
    return pl.BlockSpec((HALO, width), lambda i: (jnp.maximum(i * per - 1, 0), cb))


def _next_halo(per, n_halo, width, cb):
    return pl.BlockSpec((HALO, width), lambda i: (jnp.minimum((i + 1) * per, n_halo - 1), cb))


def _const(shape):
    return pl.BlockSpec(shape, lambda i: (0,) * len(shape))


def _glu(val, gate):
    return val.astype(F32) * _sigmoid(gate.astype(F32))


def _conv_chunks(tr, c):
    lanes = min(CONV_LANES, c)
    return [(r0, pl.ds(c0, lanes)) for r0 in range(0, tr, CONV_ROWS) for c0 in range(0, c, lanes)]


def _shift_copies(buf, shifted):
    rows = shifted.shape[1]
    buf[pl.ds(rows, SUBLANES), :] = jnp.zeros((SUBLANES, buf.shape[1]), F32)
    for s in range(SUBLANES):
        shifted[s] = buf[pl.ds(s, rows), :]


def _shifted_rows(shifted, offset, r0, cols):
    return shifted[offset % SUBLANES, pl.ds(offset - offset % SUBLANES + r0, CONV_ROWS), cols]


def _tap_sum(w_ref, shifted, r0, cols, first, step, kw):
    acc = jnp.zeros((CONV_ROWS, cols.size), F32)
    for k in range(kw):
        acc = acc + w_ref[pl.ds(k, 1), cols] * _shifted_rows(shifted, first + step * k, r0, cols)
    return acc


def _layer_norm_parts(ca):
    mu = jnp.mean(ca, axis=-1, keepdims=True)
    xc = ca - mu
    rs = lax.rsqrt(jnp.mean(xc * xc, axis=-1, keepdims=True) + NORM_EPS)
    return xc * rs, rs


def _branch_a_fwd(proj, cw, cb, lg, lb, name):
    t = proj.shape[0]
    kw, c = cw.shape
    tr, nt, per = _seq_tiles(t)

    def body(av_ref, ag_ref, hv_ref, hg_ref, cw_ref, cb_ref, lg_ref, lb_ref, ca_ref, act_ref, abuf, ash):
        i = pl.program_id(0)
        abuf[pl.ds(0, HALO), :] = jnp.where(i > 0, _glu(hv_ref[...], hg_ref[...]), 0.0)
        abuf[pl.ds(HALO, tr), :] = _glu(av_ref[...], ag_ref[...])
        _shift_copies(abuf, ash)
        for r0, cols in _conv_chunks(tr, c):
            ca_ref[pl.ds(r0, CONV_ROWS), cols] = (
                _tap_sum(cw_ref, ash, r0, cols, HALO - (kw - 1), 1, kw) + cb_ref[:, cols])
        xh, _ = _layer_norm_parts(ca_ref[...])
        ln = xh * lg_ref[...] + lb_ref[...]
        act_ref[...] = (ln * _sigmoid(ln)).astype(BF16)

    return _pcall(
        body, name=name,
        out_shape=[jax.ShapeDtypeStruct((t, c), F32), jax.ShapeDtypeStruct((t, c), BF16)],
        grid=(nt,),
        in_specs=[_col(tr, c, 0), _col(tr, c, 1), _prev_halo(per, c, 0), _prev_halo(per, c, 1),
                  _const((kw, c)), _const((1, c)), _const((1, c)), _const((1, c))],
        out_specs=[_col(tr, c, 0), _col(tr, c, 0)],
        scratch_shapes=[pltpu.VMEM((HALO + tr + SUBLANES, c), F32), pltpu.VMEM((SUBLANES, HALO + tr, c), F32)],
        compiler_params=_params(("arbitrary",), 4 * _nbytes((tr, c), F32),
                                (8 + SUBLANES) * _nbytes((tr + HALO, c), F32)),
    )(proj, proj, proj, proj, cw, cb.reshape(1, c), lg.reshape(1, c), lb.reshape(1, c))


def _branch_a_bwd(dproj, dact, ca, proj, cw, lg, lb, name):
    t = proj.shape[0]
    kw, c = cw.shape
    tr, nt, per = _seq_tiles(t)
    n_halo = t // HALO

    def body(dproj_in, da_ref, dah_ref, ca_ref, cah_ref, av_ref, ag_ref, hv_ref, hg_ref,
             cw_ref, lg_ref, lb_ref, out_ref, dlg_ref, dlb_ref, dcb_ref, dcw_ref, abuf, dbuf, sgbuf, ash, dsh):
        del dproj_in
        i = pl.program_id(0)
        lgv, lbv = lg_ref[...], lb_ref[...]

        def conv_grad(dact_v, ca_v):
            xh, rs = _layer_norm_parts(ca_v)
            ln = xh * lgv + lbv
            sg = _sigmoid(ln)
            dln = dact_v.astype(F32) * (sg * (1.0 + ln * (1.0 - sg)))
            dxh = dln * lgv
            dca = rs * (dxh - jnp.mean(dxh, axis=-1, keepdims=True)
                        - xh * jnp.mean(dxh * xh, axis=-1, keepdims=True))
            return dca, dln, xh

        dca, dln, xh = conv_grad(da_ref[...], ca_ref[...])
        dca_h, _, _ = conv_grad(dah_ref[...], cah_ref[...])
        dbuf[pl.ds(0, tr), :] = dca
        dbuf[pl.ds(tr, HALO), :] = jnp.where(i < nt - 1, dca_h, 0.0)

        @pl.when(i == 0)
        def _():
            dlg_ref[...] = jnp.zeros_like(dlg_ref)
            dlb_ref[...] = jnp.zeros_like(dlb_ref)
            dcb_ref[...] = jnp.zeros_like(dcb_ref)
            dcw_ref[...] = jnp.zeros_like(dcw_ref)

        dlg_ref[...] += jnp.sum(dln * xh, axis=0, keepdims=True)
        dlb_ref[...] += jnp.sum(dln, axis=0, keepdims=True)
        dcb_ref[...] += jnp.sum(dca, axis=0, keepdims=True)

        sg = _sigmoid(ag_ref[...].astype(F32))
        sgbuf[...] = sg
        abuf[pl.ds(0, HALO), :] = jnp.where(i > 0, _glu(hv_ref[...], hg_ref[...]), 0.0)
        abuf[pl.ds(HALO, tr), :] = av_ref[...].astype(F32) * sg
        _shift_copies(abuf, ash)
        _shift_copies(dbuf, dsh)

        for r0, cols in _conv_chunks(tr, c):
            rows = pl.ds(r0, CONV_ROWS)
            d_a = _tap_sum(cw_ref, dsh, r0, cols, kw - 1, -1, kw)
            sgc = sgbuf[rows, cols]
            out_ref[rows, cols] = (d_a * sgc).astype(BF16)
            out_ref[rows, pl.ds(c + cols.start, cols.size)] = (
                d_a * abuf[pl.ds(HALO + r0, CONV_ROWS), cols] * (1.0 - sgc)).astype(BF16)
        for _, cols in _conv_chunks(CONV_ROWS, c):
            for k in range(kw):
                acc = jnp.zeros((CONV_ROWS, cols.size), F32)
                for r0 in range(0, tr, CONV_ROWS):
                    acc = acc + (dbuf[pl.ds(r0, CONV_ROWS), cols]
                                 * _shifted_rows(ash, HALO - (kw - 1) + k, r0, cols))
                dcw_ref[pl.ds(k, 1), cols] += jnp.sum(acc, axis=0, keepdims=True)

    vec = _const((1, c))
    return _pcall(
        body, name=name,
        out_shape=[jax.ShapeDtypeStruct(dproj.shape, BF16)] + [jax.ShapeDtypeStruct((1, c), F32)] * 3
        + [jax.ShapeDtypeStruct((kw, c), F32)],
        grid=(nt,),
        in_specs=[pl.BlockSpec(memory_space=pl.ANY),
                  _col(tr, c, 0), _next_halo(per, n_halo, c, 0),
                  _col(tr, c, 0), _next_halo(per, n_halo, c, 0),
                  _col(tr, c, 0), _col(tr, c, 1), _prev_halo(per, c, 0), _prev_halo(per, c, 1),
                  _const((kw, c)), vec, vec],
        out_specs=[pl.BlockSpec((tr, 2 * c), lambda i: (i, 0)), vec, vec, vec, _const((kw, c))],
        scratch_shapes=[pltpu.VMEM((HALO + tr + SUBLANES, c), F32), pltpu.VMEM((HALO + tr + SUBLANES, c), F32),
                        pltpu.VMEM((tr, c), F32),
                        pltpu.VMEM((SUBLANES, HALO + tr, c), F32), pltpu.VMEM((SUBLANES, HALO + tr, c), F32)],
        input_output_aliases={0: 0},
        compiler_params=_params(("arbitrary",), 6 * _nbytes((tr, c), F32),
                                (12 + 2 * SUBLANES) * _nbytes((tr + HALO, c), F32)),
    )(dproj, dact, dact, ca, ca, proj, proj, proj, proj, cw, lg.reshape(1, c), lb.reshape(1, c))


def _branch_b_fwd(proj, cw, name):
    t = proj.shape[0]
    kw, c = cw.shape
    tr, nt, per = _seq_tiles(t)

    def body(sb_ref, sc_ref, sx_ref, hc_ref, hx_ref, cw_ref, o_ref, pbuf):
        i = pl.program_id(0)
        hp = hc_ref[...].astype(F32) * hx_ref[...].astype(F32)
        pbuf[pl.ds(0, HALO), :] = jnp.where(i > 0, hp, 0.0)
        pbuf[pl.ds(HALO, tr), :] = sc_ref[...].astype(F32) * sx_ref[...].astype(F32)
        u = jnp.zeros((tr, c), F32)
        for k in range(kw):
            u = u + cw_ref[pl.ds(k, 1), :] * pbuf[pl.ds(HALO - (kw - 1) + k, tr), :]
        o_ref[...] = (sb_ref[...].astype(F32) * u).astype(BF16)

    return _pcall(
        body, name=name,
        out_shape=jax.ShapeDtypeStruct((t, c), BF16),
        grid=(nt,),
        in_specs=[_col(tr, c, 2), _col(tr, c, 3), _col(tr, c, 4),
                  _prev_halo(per, c, 3), _prev_halo(per, c, 4), _const((kw, c))],
        out_specs=_col(tr, c, 0),
        scratch_shapes=[pltpu.VMEM((HALO + tr, c), F32)],
        compiler_params=_params(("arbitrary",), 4 * _nbytes((tr, c), F32), 6 * _nbytes((tr + HALO, c), F32)),
    )(proj, proj, proj, proj, proj, cw)


def _branch_b_bwd(dproj, dbin, proj, cw, name):
    t = proj.shape[0]
    kw, c = cw.shape
    tr, nt, per = _seq_tiles(t)
    n_halo = t // HALO

    def body(dproj_in, db_ref, dbh_ref, sb_ref, sbh_ref, sc_ref, sx_ref, hc_ref, hx_ref, cw_ref,
             out_ref, dcw_ref, pbuf, dubuf, res):
        del dproj_in
        i = pl.program_id(0)
        j = pl.program_id(1)

        @pl.when(j == 0)
        def _():
            sb = sb_ref[...].astype(F32)
            sc = sc_ref[...].astype(F32)
            sx = sx_ref[...].astype(F32)
            dbin_v = db_ref[...].astype(F32)
            hp = hc_ref[...].astype(F32) * hx_ref[...].astype(F32)
            pbuf[pl.ds(0, HALO), :] = jnp.where(i > 0, hp, 0.0)
            pbuf[pl.ds(HALO, tr), :] = sc * sx
            du = dbin_v * sb
            du_h = dbh_ref[...].astype(F32) * sbh_ref[...].astype(F32)
            dubuf[pl.ds(0, tr), :] = du
            dubuf[pl.ds(tr, HALO), :] = jnp.where(i < nt - 1, du_h, 0.0)

            @pl.when(i == 0)
            def _():
                dcw_ref[...] = jnp.zeros_like(dcw_ref)

            u = jnp.zeros((tr, c), F32)
            dp = jnp.zeros((tr, c), F32)
            for k in range(kw):
                shifted = pbuf[pl.ds(HALO - (kw - 1) + k, tr), :]
                u = u + cw_ref[pl.ds(k, 1), :] * shifted
                dp = dp + cw_ref[pl.ds(k, 1), :] * dubuf[pl.ds(kw - 1 - k, tr), :]
                dcw_ref[pl.ds(k, 1), :] += jnp.sum(du * shifted, axis=0, keepdims=True)
            res[0] = (dbin_v * u).astype(BF16)
            res[1] = (dp * sx).astype(BF16)
            res[2] = (dp * sc).astype(BF16)

        out_ref[...] = res[j]

    def colj(cb):
        return pl.BlockSpec((tr, c), lambda i, j: (i, cb))

    def prevj(cb):
        return pl.BlockSpec((HALO, c), lambda i, j: (jnp.maximum(i * per - 1, 0), cb))

    def nextj(cb):
        return pl.BlockSpec((HALO, c), lambda i, j: (jnp.minimum((i + 1) * per, n_halo - 1), cb))

    return _pcall(
        body, name=name,
        out_shape=[jax.ShapeDtypeStruct(dproj.shape, BF16), jax.ShapeDtypeStruct((kw, c), F32)],
        grid=(nt, 3),
        in_specs=[pl.BlockSpec(memory_space=pl.ANY),
                  colj(0), nextj(0), colj(2), nextj(2), colj(3), colj(4), prevj(3), prevj(4),
                  pl.BlockSpec((kw, c), lambda i, j: (0, 0))],
        out_specs=[pl.BlockSpec((tr, c), lambda i, j: (i, 2 + j)),
                   pl.BlockSpec((kw, c), lambda i, j: (0, 0))],
        scratch_shapes=[pltpu.VMEM((HALO + tr, c), F32), pltpu.VMEM((HALO + tr, c), F32),
                        pltpu.VMEM((3, tr, c), BF16)],
        input_output_aliases={0: 0},
        compiler_params=_params(("arbitrary", "arbitrary"), 6 * _nbytes((tr, c), F32),
                                10 * _nbytes((tr + HALO, c), F32)),
    )(dproj, dbin, dbin, proj, proj, proj, proj, proj, proj, cw)


def _softmax_rows(s):
    e = jnp.exp(s - jnp.max(s, axis=-1, keepdims=True))
    return e / jnp.sum(e, axis=-1, keepdims=True)


def _attn_fwd(proj, kv, name):
    t = proj.shape[0]
    m, c2 = kv.shape
    c = c2 // 2
    hd = c // N_HEADS
    ta = _tile(t, ATTN_TILE, SUBLANE_BF16)
    scale = hd ** -0.5

    def body(q_ref, kv_ref, o_ref):
        for h in range(N_HEADS):
            qh = q_ref[:, pl.ds(h * hd, hd)]
            kh = kv_ref[:, pl.ds(h * hd, hd)]
            vh = kv_ref[:, pl.ds(c + h * hd, hd)]
            s = lax.dot_general(qh, kh, (((1,), (1,)), ((), ())), preferred_element_type=F32) * scale
            p = _softmax_rows(s)
            o_ref[:, pl.ds(h * hd, hd)] = jnp.dot(p.astype(BF16), vh,
                                                  preferred_element_type=F32).astype(BF16)

    return _pcall(
        body, name=name,
        out_shape=jax.ShapeDtypeStruct((t, c), BF16),
        grid=(t // ta,),
        in_specs=[pl.BlockSpec((ta, c), lambda i: (i, 5)), _const((m, c2))],
        out_specs=pl.BlockSpec((ta, c), lambda i: (i, 0)),
        compiler_params=_params(("parallel",), 2 * _nbytes((ta, c), BF16) + _nbytes((m, c2), BF16),
                                8 * _nbytes((ta, m), F32)),
    )(proj, kv)


def _attn_bwd(dproj, d_o, proj, kv, name):
    t = proj.shape[0]
    m, c2 = kv.shape
    c = c2 // 2
    hd = c // N_HEADS
    ta = _tile(t, ATTN_TILE, SUBLANE_BF16)
    scale = hd ** -0.5

    def body(dproj_in, do_ref, q_ref, kv_ref, dq_ref, dkv_ref):
        del dproj_in

        @pl.when(pl.program_id(0) == 0)
        def _():
            dkv_ref[...] = jnp.zeros_like(dkv_ref)

        for h in range(N_HEADS):
            qh = q_ref[:, pl.ds(h * hd, hd)]
            kh = kv_ref[:, pl.ds(h * hd, hd)]
            vh = kv_ref[:, pl.ds(c + h * hd, hd)]
            doh = do_ref[:, pl.ds(h * hd, hd)]
            s = lax.dot_general(qh, kh, (((1,), (1,)), ((), ())), preferred_element_type=F32) * scale
            p = _softmax_rows(s)
            dp = lax.dot_general(doh, vh, (((1,), (1,)), ((), ())), preferred_element_type=F32)
            ds = (p * (dp - jnp.sum(dp * p, axis=-1, keepdims=True))).astype(BF16)
            dq_ref[:, pl.ds(h * hd, hd)] = (jnp.dot(ds, kh, preferred_element_type=F32) * scale).astype(BF16)
            dkv_ref[:, pl.ds(h * hd, hd)] += lax.dot_general(
                ds, qh, (((0,), (0,)), ((), ())), preferred_element_type=F32) * scale
            dkv_ref[:, pl.ds(c + h * hd, hd)] += lax.dot_general(
                p.astype(BF16), doh, (((0,), (0,)), ((), ())), preferred_element_type=F32)

    return _pcall(
        body, name=name,
        out_shape=[jax.ShapeDtypeStruct(dproj.shape, BF16), jax.ShapeDtypeStruct((m, c2), F32)],
        grid=(t // ta,),
        in_specs=[pl.BlockSpec(memory_space=pl.ANY),
                  pl.BlockSpec((ta, c), lambda i: (i, 0)), pl.BlockSpec((ta, c), lambda i: (i, 5)),
                  _const((m, c2))],
        out_specs=[pl.BlockSpec((ta, c), lambda i: (i, 5)), _const((m, c2))],
        input_output_aliases={0: 0},
        compiler_params=_params(("arbitrary",), 3 * _nbytes((ta, c), BF16) + 2 * _nbytes((m, c2), F32),
                                10 * _nbytes((ta, m), F32)),
    )(dproj, d_o, proj, kv)


def _merge_fwd(proj, ya, yb, yx, name):
    t, d = ya.shape
    tr = _tile(t, ROW_TILE, SUBLANE_BF16)

    def body(g_ref, ya_ref, yb_ref, yx_ref, o_ref):
        acc = jnp.zeros((tr, d), F32)
        for b, y_ref in enumerate((ya_ref, yb_ref, yx_ref)):
            acc = acc + _sigmoid(g_ref[:, pl.ds(b * d, d)].astype(F32)) * y_ref[...].astype(F32)
        o_ref[...] = acc.astype(BF16)

    row = pl.BlockSpec((tr, d), lambda i: (i, 0))
    return _pcall(
        body, name=name,
        out_shape=jax.ShapeDtypeStruct((t, d), BF16),
        grid=(t // tr,),
        in_specs=[pl.BlockSpec((tr, 3 * d), lambda i: (i, 1)), row, row, row],
        out_specs=row,
        compiler_params=_params(("parallel",), 7 * _nbytes((tr, d), BF16), 6 * _nbytes((tr, d), F32)),
    )(proj, ya, yb, yx)


def _merge_bwd(dmerged, proj, ya, yb, yx, name):
    t, d = ya.shape
    tr = _tile(t, ROW_TILE, SUBLANE_BF16)

    def body(dm_ref, g_ref, ya_ref, yb_ref, yx_ref, dg_ref, dya_ref, dyb_ref, dyx_ref):
        dm = dm_ref[...].astype(F32)
        for b, (y_ref, dy_ref) in enumerate(((ya_ref, dya_ref), (yb_ref, dyb_ref), (yx_ref, dyx_ref))):
            sg = _sigmoid(g_ref[:, pl.ds(b * d, d)].astype(F32))
            dg_ref[:, pl.ds(b * d, d)] = (dm * y_ref[...].astype(F32) * sg * (1.0 - sg)).astype(BF16)
            dy_ref[...] = (dm * sg).astype(BF16)

    row = pl.BlockSpec((tr, d), lambda i: (i, 0))
    gates = pl.BlockSpec((tr, 3 * d), lambda i: (i, 1))
    return _pcall(
        body, name=name,
        out_shape=[jax.ShapeDtypeStruct(proj.shape, BF16)] + [jax.ShapeDtypeStruct((t, d), BF16)] * 3,
        grid=(t // tr,),
        in_specs=[row, gates, row, row, row],
        out_specs=[gates, row, row, row],
        compiler_params=_params(("parallel",), 14 * _nbytes((tr, d), BF16), 8 * _nbytes((tr, d), F32)),
    )(dmerged, proj, ya, yb, yx)


def _as3(a):
    return a.reshape((-1,) + a.shape[-2:])


def _ew_call(fn, ins, n_out, out_dtypes, name, batch=None, prev=()):
    b, r, c = ins[0].shape
    lo, hi = batch if batch is not None else (0, b)
    tr = _ew_rows(r, c, len(ins) + n_out)
    n_in = len(ins) + len(prev)

    def body(*refs):
        outs = fn(*[ref[...] for ref in refs[:len(ins)]])
        for ref, val in zip(refs[n_in:], outs):
            ref[...] = val.astype(ref.dtype)

    spec = pl.BlockSpec((None, tr, c), lambda i, j: (i + lo, j, 0))
    return _pcall(
        body, name=name,
        out_shape=[jax.ShapeDtypeStruct((b, r, c), dt) for dt in out_dtypes],
        grid=(hi - lo, r // tr),
        in_specs=[spec] * len(ins) + [pl.BlockSpec(memory_space=pl.ANY)] * len(prev), out_specs=[spec] * n_out,
        input_output_aliases={len(ins) + k: k for k in range(len(prev))},
        compiler_params=_params(("parallel", "parallel"), (len(ins) + n_out) * _nbytes((tr, c), F32),
                                6 * _nbytes((tr, c), F32)),
    )(*ins, *prev)


def _my_chip():
    return 2 * lax.axis_index("x") + lax.axis_index("y")


def _my_core():
    return lax.axis_index("c")


def _cast_to_slot(w, l, name, after=()):
    _, r, cs = w.shape
    tr = _ew_rows(r, cs, 2)

    def body(w_ref, *rest):
        rest[-1][...] = w_ref[...].astype(BF16)

    return _pcall(
        body, name=name,
        out_shape=jax.ShapeDtypeStruct((N_CHIPS, r, cs), BF16),
        grid=(r // tr,),
        in_specs=[pl.BlockSpec((None, tr, cs), lambda i: (l, i, 0))]
        + [pl.BlockSpec(memory_space=pl.ANY)] * len(after),
        out_specs=pl.BlockSpec((None, tr, cs), lambda i: (_my_chip(), i, 0)),
        compiler_params=_params(("parallel",), 2 * _nbytes((tr, cs), F32)),
    )(w, *after)


def _add_half(part, theirs, name):
    _, rh, cs = theirs.shape
    tr = _ew_rows(rh, cs, 3)
    nrb = rh // tr

    def body(a_ref, b_ref, o_ref):
        o_ref[...] = (a_ref[...].astype(F32) + b_ref[...].astype(F32)).astype(BF16)

    half = pl.BlockSpec((None, tr, cs), lambda j, i: (j, i, 0))
    return _pcall(
        body, name=name,
        out_shape=jax.ShapeDtypeStruct((N_CHIPS, rh, cs), BF16),
        grid=(N_CHIPS, nrb),
        in_specs=[pl.BlockSpec((None, tr, cs), lambda j, i: (j, _my_core() * nrb + i, 0)), half],
        out_specs=half,
        compiler_params=_params(("parallel", "parallel"), 3 * _nbytes((tr, cs), F32)),
    )(part, theirs)


def _sum_chips(chip_sum, q, total, l, nl, name):
    _, rh, cs = q.shape
    tr = _ew_rows(rh, cs, 5)
    nrb = rh // tr

    def body(own_ref, q1_ref, q2_ref, q3_ref, *rest):
        acc = own_ref[...].astype(F32)
        for ref in (q1_ref, q2_ref, q3_ref):
            acc = acc + ref[...].astype(F32)
        rest[-1][...] = acc

    def slot(k):
        return pl.BlockSpec((None, tr, cs), lambda i: ((_my_chip() + k) % N_CHIPS, i, 0))

    in_specs = [slot(0), slot(1), slot(2), slot(3)]
    operands = [chip_sum, q, q, q]
    aliases = {}
    if total is not None:
        in_specs.append(HBM_SPEC)
        operands.append(total)
        aliases = {4: 0}
    return _pcall(
        body, name=name,
        out_shape=jax.ShapeDtypeStruct((nl, 2 * rh, cs), F32),
        grid=(nrb,),
        in_specs=in_specs,
        out_specs=pl.BlockSpec((None, tr, cs), lambda i: (l, _my_core() * nrb + i, 0)),
        input_output_aliases=aliases,
        compiler_params=_params(("parallel",), 5 * _nbytes((tr, cs), F32)),
    )(*operands)


def _adamw(w, g, m, v, name, batch=None, prev=()):
    shape = w.shape
    c1 = 1.0 - ADAM_B1 ** ADAM_STEP
    c2 = 1.0 - ADAM_B2 ** ADAM_STEP

    def fn(wv, gv, mv, vv):
        mn = ADAM_B1 * mv + (1.0 - ADAM_B1) * gv
        vn = ADAM_B2 * vv + (1.0 - ADAM_B2) * (gv * gv)
        delta = -ADAM_LR * ((mn / c1) / (jnp.sqrt(vn / c2) + ADAM_EPS) + ADAM_WD * wv)
        return delta, mn, vn

    outs = _ew_call(fn, [_as3(w), _as3(g), _as3(m), _as3(v)], 3, [F32] * 3, name,
                    batch=batch, prev=[_as3(p) for p in prev])
    return [o.reshape(shape) for o in outs]


def _sum_leading(q, name):
    b, nj, r, c = q.shape
    tr = _ew_rows(r, c, nj + 1)

    def body(q_ref, o_ref):
        acc = q_ref[0].astype(F32)
        for j in range(1, nj):
            acc = acc + q_ref[j].astype(F32)
        o_ref[...] = acc

    return _pcall(
        body, name=name,
        out_shape=jax.ShapeDtypeStruct((b, r, c), F32),
        grid=(b, r // tr),
        in_specs=[pl.BlockSpec((None, nj, tr, c), lambda i, j: (i, 0, j, 0))],
        out_specs=pl.BlockSpec((None, tr, c), lambda i, j: (i, j, 0)),
        compiler_params=_params(("parallel", "parallel"), (nj + 1) * _nbytes((tr, c), F32),
                                2 * _nbytes((tr, c), F32)),
    )(q)


HBM_SPEC = pl.BlockSpec(memory_space=pl.ANY)


def _place():
    x, y, c = lax.axis_index("x"), lax.axis_index("y"), lax.axis_index("c")
    peers = [(1 - x, y), (x, 1 - y), (1 - x, 1 - y)]
    return x, y, c, 2 * x + y, peers


def _remote(src, dst, send_sem, recv_sem, dev):
    return pltpu.make_async_remote_copy(src_ref=src, dst_ref=dst, send_sem=send_sem, recv_sem=recv_sem,
                                        device_id=dev, device_id_type=MESH)


IN_HBM = pl.BlockSpec(memory_space=pltpu.HBM)
IN_SEM = pl.BlockSpec(memory_space=pltpu.SEMAPHORE)
DATAFLOW = pltpu.SideEffectType.DATAFLOW_SIDE_EFFECTING
TOKEN = jax.ShapeDtypeStruct((8, LANE), F32)


def _in_hbm(arrays):
    return [pltpu.with_memory_space_constraint(a, pltpu.HBM) for a in arrays]


def _half_rows(ref, c):
    rh = ref.shape[1] // 2
    return pl.ds(c * rh, rh)


def _split_start(body_copies, passed, landing, name, per_array=3):
    n_pass, n_land = len(passed), len(landing)
    n_arr = n_pass + n_land
    n = per_array * n_pass

    def body(*refs):
        ins = refs[:n_arr]
        send, recv = refs[n_arr], refs[n_arr + 1]
        token = refs[-1]
        for src, dst, _, s_sem, r_sem, dev in body_copies(ins[:n_pass], ins[n_pass:], send, recv):
            _remote(src, dst, s_sem, r_sem, dev).start()
        token[...] = jnp.zeros_like(token)

    arrays = list(passed) + list(landing)
    return pl.pallas_call(
        body, name=name,
        out_shape=(pltpu.SemaphoreType.DMA((n,)), pltpu.SemaphoreType.DMA((n,)),
                   *[pltpu.HBM(a.shape, a.dtype) for a in arrays], TOKEN),
        in_specs=[IN_HBM] * n_arr,
        out_specs=(IN_SEM, IN_SEM, *[IN_HBM] * n_arr, pl.BlockSpec(memory_space=pltpu.VMEM)),
        input_output_aliases={i: 2 + i for i in range(n_arr)},
        compiler_params=pltpu.CompilerParams(has_side_effects=DATAFLOW),
    )(*_in_hbm(arrays))


def _split_wait(body_copies, send, recv, passed, landing, after, name):
    n_pass, n_land = len(passed), len(landing)
    n_arr = n_pass + n_land

    def body(*refs):
        ins = refs[:n_arr]
        send_ref, recv_ref = refs[n_arr], refs[n_arr + 1]
        for src, _, landed, s_sem, r_sem, dev in body_copies(ins[:n_pass], ins[n_pass:], send_ref, recv_ref):
            cp = _remote(src, landed, s_sem, r_sem, dev)
            cp.wait_send()
            cp.wait_recv()

    arrays = list(passed) + list(landing)
    return pl.pallas_call(
        body, name=name,
        out_shape=tuple(pltpu.HBM(a.shape, a.dtype) for a in arrays),
        in_specs=[IN_HBM] * n_arr + [IN_SEM, IN_SEM] + [HBM_SPEC] * len(after),
        out_specs=tuple([IN_HBM] * n_arr),
        input_output_aliases={i: i for i in range(n_arr)},
        compiler_params=pltpu.CompilerParams(has_side_effects=DATAFLOW),
    )(*arrays, send, recv, *after)


def _gather_copies(slots, _, send, recv):
    _, _, c, me, peers = _place()
    copies = []
    for a, ref in enumerate(slots):
        rows = _half_rows(ref, c)
        own = ref.at[me, rows, :]
        for j, (px, py) in enumerate(peers):
            k = 3 * a + j
            copies.append((own, own, ref.at[2 * px + py, rows, :], send.at[k], recv.at[k], (px, py, c)))
    return copies


def _exchange_copies(chip_sums, landing, send, recv):
    _, _, c, me, peers = _place()
    copies = []
    for a, (s_ref, q_ref) in enumerate(zip(chip_sums, landing)):
        for j, (px, py) in enumerate(peers):
            k = 3 * a + j
            copies.append((s_ref.at[2 * px + py], q_ref.at[me], q_ref.at[2 * px + py],
                           send.at[k], recv.at[k], (px, py, c)))
    return copies


def _gather_start(slots, name):
    out = _split_start(_gather_copies, slots, [], name)
    return out[0], out[1], list(out[2:-1]), out[-1]


def _gather_wait(send, recv, slots, after, name):
    return list(_split_wait(_gather_copies, send, recv, slots, [], after, name))


def _forward_copies(slots, _, send, recv):
    x, y, c, _, peers = _place()
    copies = []
    for a, ref in enumerate(slots):
        for j, (px, py) in enumerate(peers):
            k = 3 * a + j
            mine = ref.at[2 * px + py, _half_rows(ref, c), :]
            theirs = ref.at[2 * px + py, _half_rows(ref, 1 - c), :]
            copies.append((mine, mine, theirs, send.at[k], recv.at[k], (x, y, 1 - c)))
    return copies


def _forward_start(slots, name):
    out = _split_start(_forward_copies, slots, [], name)
    return out[0], out[1], list(out[2:-1]), out[-1]


def _forward_wait(send, recv, slots, after, name):
    return list(_split_wait(_forward_copies, send, recv, slots, [], after, name))


def _gather_small(v, name):
    def body(v_ref, o_ref, send, recv, lsem):
        x, y, c, me, peers = _place()
        local = pltpu.make_async_copy(v_ref, o_ref.at[me], lsem)
        local.start()
        sends = [_remote(v_ref, o_ref.at[me], send.at[j], recv.at[j], (px, py, c))
                 for j, (px, py) in enumerate(peers)]
        for cp in sends:
            cp.start()
        for j, (px, py) in enumerate(peers):
            _remote(v_ref, o_ref.at[2 * px + py], send.at[j], recv.at[j], (px, py, c)).wait_recv()
        for cp in sends:
            cp.wait_send()
        local.wait()

    return _pcall(
        body, name=name,
        out_shape=jax.ShapeDtypeStruct((N_CHIPS,) + v.shape, v.dtype),
        in_specs=[HBM_SPEC], out_specs=HBM_SPEC,
        scratch_shapes=[pltpu.SemaphoreType.DMA((3,)), pltpu.SemaphoreType.DMA((3,)), pltpu.SemaphoreType.DMA],
    )(v)


def _gather_all(v, name):
    def body(v_ref, o_ref, send, recv, lsem):
        x, y, c = lax.axis_index("x"), lax.axis_index("y"), lax.axis_index("c")
        me = 4 * x + 2 * y + c
        local = pltpu.make_async_copy(v_ref, o_ref.at[me], lsem)
        local.start()
        flips = [(fx, fy, fc) for fx in (0, 1) for fy in (0, 1) for fc in (0, 1)][1:]
        peers = [(x ^ fx, y ^ fy, c ^ fc) for fx, fy, fc in flips]
        sends = [_remote(v_ref, o_ref.at[me], send.at[k], recv.at[k], dev) for k, dev in enumerate(peers)]
        for cp in sends:
            cp.start()
        for k, (px, py, pc) in enumerate(peers):
            _remote(v_ref, o_ref.at[4 * px + 2 * py + pc], send.at[k], recv.at[k], (px, py, pc)).wait_recv()
        for cp in sends:
            cp.wait_send()
        local.wait()

    return _pcall(
        body, name=name,
        out_shape=jax.ShapeDtypeStruct((8,) + v.shape, v.dtype),
        in_specs=[HBM_SPEC], out_specs=HBM_SPEC,
        scratch_shapes=[pltpu.SemaphoreType.DMA((7,)), pltpu.SemaphoreType.DMA((7,)), pltpu.SemaphoreType.DMA],
    )(v)


def _send_copies(parts, landing, send, recv):
    x, y, c, _, _ = _place()
    return [(p.at[:, _half_rows(p, 1 - c), :], t, t, send.at[a], recv.at[a], (x, y, 1 - c))
            for a, (p, t) in enumerate(zip(parts, landing))]


def _send_start(parts, name):
    na = len(parts)
    landing = [lax.empty((N_CHIPS, p.shape[1] // 2, p.shape[2]), p.dtype) for p in parts]
    out = _split_start(_send_copies, parts, landing, name, per_array=1)
    return out[0], out[1], list(out[2:2 + na]), list(out[2 + na:2 + 2 * na]), out[-1]


def _send_wait(send, recv, parts, landing, after, name):
    na = len(parts)
    out = _split_wait(_send_copies, send, recv, parts, landing, after, name)
    return list(out[:na]), list(out[na:])


def _exchange_start(chip_sums, name):
    na = len(chip_sums)
    landing = [lax.empty(s.shape, s.dtype) for s in chip_sums]
    out = _split_start(_exchange_copies, chip_sums, landing, name)
    return out[0], out[1], list(out[2:2 + na]), list(out[2 + na:2 + 2 * na]), out[-1]


def _exchange_wait(send, recv, chip_sums, landing, after, name):
    na = len(chip_sums)
    out = _split_wait(_exchange_copies, send, recv, chip_sums, landing, after, name)
    return list(out[:na]), list(out[na:])


def _join_halves(reds, lo, hi, name, after=()):
    na = len(reds)
    n_in = na + len(after)
    layers = pl.ds(lo, hi - lo)

    def body(*refs):
        outs = refs[n_in:n_in + na]
        send, recv = refs[n_in + na:]
        x, y, c, _, _ = _place()
        sends = []
        for a, ref in enumerate(outs):
            rh = ref.shape[1] // 2
            mine = ref.at[layers, pl.ds(c * rh, rh), :]
            sends.append(_remote(mine, mine, send.at[a], recv.at[a], (x, y, 1 - c)))
            sends[-1].start()
        for a, ref in enumerate(outs):
            rh = ref.shape[1] // 2
            other = ref.at[layers, pl.ds((1 - c) * rh, rh), :]
            _remote(other, other, send.at[a], recv.at[a], (x, y, 1 - c)).wait_recv()
        for cp in sends:
            cp.wait_send()

    return _pcall(
        body, name=name,
        out_shape=[jax.ShapeDtypeStruct(r.shape, r.dtype) for r in reds],
        in_specs=[HBM_SPEC] * n_in, out_specs=[HBM_SPEC] * na,
        input_output_aliases={a: a for a in range(na)},
        scratch_shapes=[pltpu.SemaphoreType.DMA((na,)), pltpu.SemaphoreType.DMA((na,))],
    )(*reds, *after)


COL_SHARDED = ("w_in", "w_a_out", "w_b_out", "w_x_out", "w_up")
ROW_SHARDED = ("w_kv", "w_o", "w_down")
BIG = COL_SHARDED + ROW_SHARDED
SMALL_REPLICATED = ("g_mix_pre", "conv_a_b", "ln_a_g", "ln_a_b", "g_mem", "g_mix_post", "g_mlp_pre", "g_mlp_post")
SMALL_SHARDED = ("conv_a_w", "conv_b_w")
WEIGHT_ORDER = ("g_mix_pre", "w_in", "conv_a_w", "conv_a_b", "ln_a_g", "ln_a_b", "w_a_out", "conv_b_w", "w_b_out",
                "g_mem", "w_kv", "w_x_out", "w_o", "g_mix_post", "g_mlp_pre", "w_up", "w_down", "g_mlp_post")


def _pack_rows(arrays, width):
    rows = []
    for a in arrays:
        r = a.reshape(-1, width)
        rows.append(jnp.pad(r, ((0, (-r.shape[0]) % 8), (0, 0))))
    return jnp.concatenate(rows, axis=0)


def _unpack_rows(packed, like, width):
    out, at = [], 0
    for a in like:
        n = a.size // width
        out.append(packed[at:at + n].reshape(a.shape))
        at += n + (-n) % 8
    return out


def _step(w, m, v, x, mem, target):
    nl = w["w_in"].shape[0]
    d = x.shape[1]
    c = w["conv_a_b"].shape[1]
    ka = w["conv_a_w"].shape[1]

    conv_pack = jnp.concatenate([w["conv_a_w"], w["conv_b_w"]], axis=1)
    conv_rows = conv_pack.shape[1]
    conv_pack = jnp.pad(conv_pack, ((0, 0), (0, (-conv_rows) % 8), (0, 0)))
    conv_all = _gather_small(conv_pack, "gather_conv")
    slots = {name: [_cast_to_slot(w[name], 0, "cast_" + name, after=[conv_all])] for name in BIG}
    conv_all = jnp.moveaxis(conv_all, 0, 2).reshape(nl, conv_pack.shape[1], c)
    cw_a, cw_b = conv_all[:, :ka], conv_all[:, ka:conv_rows]

    def start_gather(l):
        return _gather_start([slots[name][l] for name in BIG], "gather_start_%d" % l)

    def start_forward(l, pending, after):
        send, recv, thru, _ = pending
        landed = _gather_wait(send, recv, thru, after, "gather_wait_%d" % l)
        return _forward_start(landed, "gather_fwd_start_%d" % l)

    def finish_forward(l, passing, after):
        send, recv, thru, _ = passing
        arrays = _forward_wait(send, recv, thru, after, "gather_fwd_wait_%d" % l)
        return {name: (a.reshape(1, -1, a.shape[-1]) if name in ROW_SHARDED else a)
                for name, a in zip(BIG, arrays)}

    pending = start_gather(0)
    for name in BIG:
        slots[name] += [_cast_to_slot(w[name], l, "cast_" + name, after=[pending[3]]) for l in range(1, nl)]
    (h,) = _norm_fwd(x, "norm_first", g_next=w["g_mix_pre"][0])
    passing = start_forward(0, pending, [h] + [s for name in BIG for s in slots[name][1:]])
    full = [finish_forward(0, passing, [passing[3]])]

    saved = []
    xl = x
    dy = loss_cols = None
    for l in range(nl):
        s = {"x": xl, "h": h}
        fw = full[l]
        token = None
        if l + 1 < nl:
            pending = start_gather(l + 1)
            token = pending[3]
        proj = _mm_nn(h, fw["w_in"], BF16, "mm_in", after=token)
        ca, act_a = _branch_a_fwd(proj, cw_a[l], w["conv_a_b"][l], w["ln_a_g"][l], w["ln_a_b"][l], "branch_a_fwd")
        b_in = _branch_b_fwd(proj, cw_b[l], "branch_b_fwd")
        (mem_n,) = _norm_fwd(mem, "norm_mem", g_next=w["g_mem"][l])
        kv = _mm_nn(mem_n, fw["w_kv"], BF16, "mm_kv")
        att = _attn_fwd(proj, kv, "attn_fwd")
        ya = _mm_nn(act_a, fw["w_a_out"], BF16, "mm_a_out")
        yb = _mm_nn(b_in, fw["w_b_out"], BF16, "mm_b_out")
        yx = _mm_nn(att, fw["w_x_out"], BF16, "mm_x_out")
        merged = _merge_fwd(proj, ya, yb, yx, "merge_fwd")
        z = _mm_nn(merged, fw["w_o"], BF16, "mm_o")
        x1, h2 = _norm_fwd(xl, "norm_mid", z=z, g_post=w["g_mix_post"][l], g_next=w["g_mlp_pre"][l])
        token = None
        if l + 1 < nl:
            passing = start_forward(l + 1, pending, [h2])
            token = passing[3]
        up = _mm_nn(h2, fw["w_up"], BF16, "mm_up", after=token)
        f = _mm_nn(up, fw["w_down"], BF16, "mm_down", a_act="relu2")
        s.update(proj=proj, ca=ca, act_a=act_a, b_in=b_in, mem_n=mem_n, kv=kv, att=att, ya=ya, yb=yb, yx=yx,
                 merged=merged, z=z, x1=x1, h2=h2, up=up, f=f)
        saved.append(s)
        if l + 1 < nl:
            full.append(finish_forward(l + 1, passing, [f]))
        if l + 1 < nl:
            xl, h = _norm_fwd(x1, "norm_mid", z=f, g_post=w["g_mlp_post"][l], g_next=w["g_mix_pre"][l + 1])
        else:
            dy, loss_cols = _norm_fwd(x1, "norm_loss", z=f, g_post=w["g_mlp_post"][l], target=target)

    part = {name: [None] * nl for name in BIG}
    small = {name: [None] * nl for name in SMALL_REPLICATED + SMALL_SHARDED}
    dxo = dy
    d_f, small["g_mlp_post"][nl - 1] = _norm_bwd("norm_bwd_top", dxo=dy,
                                                 post=(saved[-1]["f"], w["g_mlp_post"][nl - 1]))
    grad_x = None
    totals = {name: None for name in BIG}

    def start_send(l):
        return _send_start([part[name][l] for name in BIG], "rs_send_start_%d" % l)

    def start_exchange(l, sending, after):
        send, recv, parts, landing, _ = sending
        parts, theirs = _send_wait(send, recv, parts, landing, after, "rs_send_wait_%d" % l)
        chip_sums = [_add_half(p, t, "rs_add_" + name) for name, p, t in zip(BIG, parts, theirs)]
        return _exchange_start(chip_sums, "rs_xchg_start_%d" % l)

    def finish_exchange(l, pending, after):
        send, recv, chip_sums, landing, _ = pending
        chip_sums, q = _exchange_wait(send, recv, chip_sums, landing, after, "rs_xchg_wait_%d" % l)
        for name, own, got in zip(BIG, chip_sums, q):
            totals[name] = _sum_chips(own, got, totals[name], l, nl, "rs_sum_" + name)

    sending = exchange = None
    for l in reversed(range(nl)):
        s = saved[l]
        fw = full[l]
        d_up = _mm_nt(d_f, fw["w_down"], BF16, "mm_down_dx", relu2_of=s["up"],
                      after=None if sending is None else sending[4])
        part["w_down"][l] = _mm_tn(s["up"], d_f, 1, "mm_down_dw", a_act="relu2").reshape(N_CHIPS, -1, d)
        d_h2 = _mm_nt(d_up, fw["w_up"], BF16, "mm_up_dx")
        part["w_up"][l] = _mm_tn(s["h2"], d_up, N_CHIPS, "mm_up_dw")
        dx1, d_z, small["g_mlp_pre"][l], small["g_mix_post"][l] = _norm_bwd(
            "norm_bwd_mid", dxo=dxo, pre=(d_h2, s["x1"], w["g_mlp_pre"][l]), post=(s["z"], w["g_mix_post"][l]))
        if sending is not None:
            exchange = start_exchange(l + 1, sending, [d_z])
        d_merged = _mm_nt(d_z, fw["w_o"], BF16, "mm_o_dx", after=None if exchange is None else exchange[4])
        part["w_o"][l] = _mm_tn(s["merged"], d_z, 1, "mm_o_dw").reshape(N_CHIPS, -1, d)
        dproj, d_ya, d_yb, d_yx = _merge_bwd(d_merged, s["proj"], s["ya"], s["yb"], s["yx"], "merge_bwd")
        d_act_a = _mm_nt(d_ya, fw["w_a_out"], BF16, "mm_a_out_dx")
        part["w_a_out"][l] = _mm_tn(s["act_a"], d_ya, N_CHIPS, "mm_a_out_dw")
        d_b_in = _mm_nt(d_yb, fw["w_b_out"], BF16, "mm_b_out_dx")
        part["w_b_out"][l] = _mm_tn(s["b_in"], d_yb, N_CHIPS, "mm_b_out_dw")
        d_att = _mm_nt(d_yx, fw["w_x_out"], BF16, "mm_x_out_dx")
        part["w_x_out"][l] = _mm_tn(s["att"], d_yx, N_CHIPS, "mm_x_out_dw")
        dproj, small["ln_a_g"][l], small["ln_a_b"][l], small["conv_a_b"][l], small["conv_a_w"][l] = _branch_a_bwd(
            dproj, d_act_a, s["ca"], s["proj"], cw_a[l], w["ln_a_g"][l], w["ln_a_b"][l], "branch_a_bwd")
        dproj, small["conv_b_w"][l] = _branch_b_bwd(dproj, d_b_in, s["proj"], cw_b[l], "branch_b_bwd")
        dproj, d_kv = _attn_bwd(dproj, d_att, s["proj"], s["kv"], "attn_bwd")
        part["w_kv"][l] = _mm_tn(s["mem_n"], d_kv, 1, "mm_kv_dw").reshape(N_CHIPS, -1, 2 * c)
        d_mem_n = _mm_nt(d_kv, fw["w_kv"], F32, "mm_kv_dx")
        (small["g_mem"][l],) = _norm_bwd("norm_bwd_mem", pre=(d_mem_n, mem, w["g_mem"][l]), want_dx=False)
        d_h = _mm_nt(dproj, fw["w_in"], BF16, "mm_in_dx")
        part["w_in"][l] = _mm_tn(s["h"], dproj, N_CHIPS, "mm_in_dw")
        if exchange is not None:
            finish_exchange(l + 1, exchange, [d_h])
        sending = start_send(l)
        if l > 0:
            dxo, d_f, small["g_mix_pre"][l], small["g_mlp_post"][l - 1] = _norm_bwd(
                "norm_bwd_mid", dxo=dx1, pre=(d_h, s["x"], w["g_mix_pre"][l]),
                post=(saved[l - 1]["f"], w["g_mlp_post"][l - 1]))
        else:
            grad_x, small["g_mix_pre"][0] = _norm_bwd("norm_bwd_last", dxo=dx1,
                                                      pre=(d_h, s["x"], w["g_mix_pre"][0]))
    exchange = start_exchange(0, sending, [grad_x])

    grads, delta, new_m, new_v = {}, {}, {}, {}
    small_names = SMALL_REPLICATED + SMALL_SHARDED
    stacked = [jnp.stack(small[n]) for n in small_names]
    packed = _pack_rows(stacked, c)
    total = _sum_leading(_gather_all(packed, "gather_small_grads")[None], "sum_small_grads")[0]
    reduced = dict(zip(small_names, _unpack_rows(total, stacked, c)))
    chip = 2 * lax.axis_index("x") + lax.axis_index("y")
    cs = c // N_CHIPS
    for name in SMALL_SHARDED:
        grads[name] = lax.dynamic_slice_in_dim(reduced[name], chip * cs, cs, axis=2)
    for name in SMALL_REPLICATED:
        grads[name] = reduced[name].reshape(w[name].shape)
    for group, width in ((SMALL_REPLICATED, c), (SMALL_SHARDED, cs)):
        packs = [_pack_rows([src[n] for n in group], width)[None] for src in (w, grads, m, v)]
        outs = _adamw(*packs, "adamw_small_%d" % width)
        for dst, out in zip((delta, new_m, new_v), outs):
            dst.update(zip(group, _unpack_rows(out[0], [w[n] for n in group], width)))

    state = {name: () for name in BIG}
    if nl > 1:
        totals.update(zip(BIG, _join_halves([totals[name] for name in BIG], 1, nl, "rs_join_upper",
                                            after=[exchange[4]])))
        for name in BIG:
            state[name] = _adamw(w[name], totals[name], m[name], v[name], "adamw_" + name, batch=(1, nl))
    finish_exchange(0, exchange, [state[name][0] for name in BIG] if nl > 1 else [grad_x])
    grads.update(zip(BIG, _join_halves([totals[name] for name in BIG], 0, 1, "rs_join_first")))
    for name in BIG:
        delta[name], new_m[name], new_v[name] = _adamw(w[name], grads[name], m[name], v[name], "adamw_" + name,
                                                       batch=(0, 1), prev=state[name])

    loss = lax.psum(0.5 * jnp.sum(loss_cols) / d, ("x", "y", "c"))
    return loss, grad_x, grads, delta, new_m, new_v


def kernel(x, mem, g_mix_pre, w_in, conv_a_w, conv_a_b, ln_a_g, ln_a_b, w_a_out, conv_b_w, w_b_out, g_mem, w_kv, w_x_out, w_o, g_mix_post, g_mlp_pre, w_up, w_down, g_mlp_post, loss_target, m_g_mix_pre, m_w_in, m_conv_a_w, m_conv_a_b, m_ln_a_g, m_ln_a_b, m_w_a_out, m_conv_b_w, m_w_b_out, m_g_mem, m_w_kv, m_w_x_out, m_w_o, m_g_mix_post, m_g_mlp_pre, m_w_up, m_w_down, m_g_mlp_post, v_g_mix_pre, v_w_in, v_conv_a_w, v_conv_a_b, v_ln_a_g, v_ln_a_b, v_w_a_out, v_conv_b_w, v_w_b_out, v_g_mem, v_w_kv, v_w_x_out, v_w_o, v_g_mix_post, v_g_mlp_pre, v_w_up, v_w_down, v_g_mlp_post):
    w = dict(g_mix_pre=g_mix_pre, w_in=w_in, conv_a_w=conv_a_w, conv_a_b=conv_a_b, ln_a_g=ln_a_g, ln_a_b=ln_a_b,
             w_a_out=w_a_out, conv_b_w=conv_b_w, w_b_out=w_b_out, g_mem=g_mem, w_kv=w_kv, w_x_out=w_x_out, w_o=w_o,
             g_mix_post=g_mix_post, g_mlp_pre=g_mlp_pre, w_up=w_up, w_down=w_down, g_mlp_post=g_mlp_post)
    m = dict(g_mix_pre=m_g_mix_pre, w_in=m_w_in, conv_a_w=m_conv_a_w, conv_a_b=m_conv_a_b, ln_a_g=m_ln_a_g,
             ln_a_b=m_ln_a_b, w_a_out=m_w_a_out, conv_b_w=m_conv_b_w, w_b_out=m_w_b_out, g_mem=m_g_mem, w_kv=m_w_kv,
             w_x_out=m_w_x_out, w_o=m_w_o, g_mix_post=m_g_mix_post, g_mlp_pre=m_g_mlp_pre, w_up=m_w_up,
             w_down=m_w_down, g_mlp_post=m_g_mlp_post)
    v = dict(g_mix_pre=v_g_mix_pre, w_in=v_w_in, conv_a_w=v_conv_a_w, conv_a_b=v_conv_a_b, ln_a_g=v_ln_a_g,
             ln_a_b=v_ln_a_b, w_a_out=v_w_a_out, conv_b_w=v_conv_b_w, w_b_out=v_w_b_out, g_mem=v_g_mem, w_kv=v_w_kv,
             w_x_out=v_w_x_out, w_o=v_w_o, g_mix_post=v_g_mix_post, g_mlp_pre=v_g_mlp_pre, w_up=v_w_up,
             w_down=v_w_down, g_mlp_post=v_g_mlp_post)
    loss, grad_x, grads, delta, new_m, new_v = _step(w, m, v, x[0], mem[0], loss_target[0])
    out = [loss, grad_x[None]]
    for group in (grads, delta, new_m, new_v):
        out += [group[n] for n in WEIGHT_ORDER]
    return tuple(out)
```

```python
import functools

import jax
import jax.numpy as jnp
from jax import lax
from jax.experimental import pallas as pl
from jax.experimental.pallas import tpu as pltpu

F32 = jnp.float32
BF16 = jnp.bfloat16
MESH = pl.DeviceIdType.MESH

NORM_EPS = 1e-6
N_HEADS = 4
ADAM_LR = 0.001
ADAM_B1 = 0.9
ADAM_B2 = 0.999
ADAM_EPS = 1e-08
ADAM_WD = 0.01
ADAM_STEP = 10

N_CHIPS = 4
V7X_VMEM_BYTES = 64 * 1024 * 1024
VMEM_CAP = V7X_VMEM_BYTES - 8 * 1024 * 1024
LANE = 128
SUBLANES = 8
SUBLANE_BF16 = 16
HALO = 32
ROW_TILE = 256
CONV_ROWS = 32
CONV_LANES = 512
ATTN_TILE = 512
MM_TM = 1024
MM_TN = 1024
MM_TK = 2048
MM_TC = 3072
MM_WHOLE_TILE = 512
MM_WHOLE_VMEM = 34 * 1024 * 1024
EW_VMEM_BYTES = 24 * 1024 * 1024


def _tile(n, pref, align):
    if n <= pref:
        return n
    t = (pref // align) * align
    while t >= align:
        if n % t == 0:
            return t
        t -= align
    return n


def _ew_rows(r, c, n_arrays):
    return _tile(r, max(SUBLANE_BF16, EW_VMEM_BYTES // (2 * n_arrays * c * 4)), SUBLANE_BF16)


def _nbytes(shape, dtype):
    n = 1
    for s in shape:
        if s is not None:
            n *= s
    return n * jnp.dtype(dtype).itemsize


def _params(semantics, block_bytes, temp_bytes=0):
    need = 2 * block_bytes + temp_bytes + (4 << 20)
    return pltpu.CompilerParams(dimension_semantics=semantics,
                                vmem_limit_bytes=int(min(max(need, 16 << 20), VMEM_CAP)))


def _sigmoid(v):
    return 1.0 / (1.0 + jnp.exp(-v))


def _pcall(body, **kwargs):
    call = pl.pallas_call(body, **kwargs)
    return lambda *operands: call(*[pltpu.with_memory_space_constraint(o, pltpu.HBM) for o in operands])


def _whole_contraction_fits(k):
    return 4 * k * MM_WHOLE_TILE * 2 <= MM_WHOLE_VMEM


def _mm_nn(a, b3, out_dtype, name, a_act=None, after=None):
    m, k = a.shape
    s, k2, ns = b3.shape
    assert k == k2
    tm = _tile(m, MM_TM, SUBLANE_BF16)
    tn = _tile(ns, MM_TN, LANE)
    tk = _tile(k, MM_TK, LANE)
    if k > tk and _whole_contraction_fits(k):
        tm, tn, tk = _tile(m, MM_WHOLE_TILE, SUBLANE_BF16), _tile(ns, MM_WHOLE_TILE, LANE), k
    q = ns // tn
    nk = k // tk

    n_in = 2 if after is None else 3

    def body(*refs):
        a_ref, b_ref, o_ref = refs[0], refs[1], refs[n_in]
        scratch = refs[n_in + 1:]
        av = a_ref[...]
        if a_act == "relu2":
            r = jnp.maximum(av.astype(BF16), 0.0)
            av = r * r
        p = jnp.dot(av.astype(BF16), b_ref[...].astype(BF16), preferred_element_type=F32)
        if nk == 1:
            o_ref[...] = p.astype(o_ref.dtype)
        else:
            acc, = scratch
            kk = pl.program_id(2)

            @pl.when(kk == 0)
            def _():
                acc[...] = p

            @pl.when(kk > 0)
            def _():
                acc[...] += p

            @pl.when(kk == nk - 1)
            def _():
                o_ref[...] = acc[...].astype(o_ref.dtype)

    blocks = (_nbytes((tm, tk), a.dtype) + _nbytes((tk, tn), b3.dtype) + _nbytes((tm, tn), out_dtype))
    return _pcall(
        body, name=name,
        out_shape=jax.ShapeDtypeStruct((m, s * ns), out_dtype),
        grid=(m // tm, s * q, nk),
        in_specs=[pl.BlockSpec((tm, tk), lambda i, j, c: (i, c)),
                  pl.BlockSpec((None, tk, tn), lambda i, j, c: (j // q, c, j % q))]
        + ([] if after is None else [pl.BlockSpec(memory_space=pl.ANY)]),
        out_specs=pl.BlockSpec((tm, tn), lambda i, j, c: (i, j)),
        scratch_shapes=[pltpu.VMEM((tm, tn), F32)] if nk > 1 else [],
        compiler_params=_params(("parallel", "parallel", "arbitrary"), blocks,
                                3 * _nbytes((tm, tn), F32) + _nbytes((tm, tk), F32)),
    )(*([a, b3] if after is None else [a, b3, after]))


def _mm_nt(a, b3, out_dtype, name, relu2_of=None, after=None):
    m, n = a.shape
    s, kd, ns = b3.shape
    assert n == s * ns
    tm = _tile(m, MM_TM, SUBLANE_BF16)
    tj = _tile(kd, MM_TN, LANE)
    tc = _tile(ns, MM_TC, LANE)
    q = ns // tc
    nc = s * q

    n_in = 2 + (relu2_of is not None) + (after is not None)

    def body(*refs):
        a_ref, b_ref, u_ref, o_ref = refs[0], refs[1], refs[2], refs[n_in]
        scratch = refs[n_in + 1:]

        def finish(p):
            if relu2_of is not None:
                p = p * (2.0 * jnp.maximum(u_ref[...].astype(F32), 0.0))
            o_ref[...] = p.astype(o_ref.dtype)

        p = lax.dot_general(a_ref[...].astype(BF16), b_ref[...].astype(BF16),
                            (((1,), (1,)), ((), ())), preferred_element_type=F32)
        if nc == 1:
            finish(p)
        else:
            acc, = scratch
            cc = pl.program_id(2)

            @pl.when(cc == 0)
            def _():
                acc[...] = p

            @pl.when(cc > 0)
            def _():
                acc[...] += p

            @pl.when(cc == nc - 1)
            def _():
                finish(acc[...])

    in_specs = [pl.BlockSpec((tm, tc), lambda i, j, c: (i, c)),
                pl.BlockSpec((None, tj, tc), lambda i, j, c: (c // q, j, c % q))]
    operands = [a, b3]
    blocks = _nbytes((tm, tc), a.dtype) + _nbytes((tj, tc), b3.dtype) + _nbytes((tm, tj), out_dtype)
    if relu2_of is not None:
        in_specs.append(pl.BlockSpec((tm, tj), lambda i, j, c: (i, j)))
        operands.append(relu2_of)
        blocks += _nbytes((tm, tj), relu2_of.dtype)
    if after is not None:
        in_specs.append(pl.BlockSpec(memory_space=pl.ANY))
        operands.append(after)
    return _pcall(
        body, name=name,
        out_shape=jax.ShapeDtypeStruct((m, kd), out_dtype),
        grid=(m // tm, kd // tj, nc),
        in_specs=in_specs,
        out_specs=pl.BlockSpec((tm, tj), lambda i, j, c: (i, j)),
        scratch_shapes=[pltpu.VMEM((tm, tj), F32)] if nc > 1 else [],
        compiler_params=_params(("parallel", "parallel", "arbitrary"), blocks,
                                3 * _nbytes((tm, tj), F32)),
    )(*operands)


def _mm_tn(a, g, out_shards, name, a_act=None):
    m, ka = a.shape
    m2, n = g.shape
    assert m == m2
    ns = n // out_shards
    ta = _tile(ka, MM_TM, LANE)
    tn = _tile(ns, MM_TN, LANE)
    tm = _tile(m, MM_TK, SUBLANE_BF16)
    if m > tm and _whole_contraction_fits(m):
        ta, tn, tm = _tile(ka, MM_WHOLE_TILE, LANE), _tile(ns, MM_WHOLE_TILE, LANE), m
    q = ns // tn
    nm = m // tm

    def body(a_ref, g_ref, o_ref, *scratch):
        av = a_ref[...]
        if a_act == "relu2":
            r = jnp.maximum(av.astype(BF16), 0.0)
            av = r * r
        p = lax.dot_general(av.astype(BF16), g_ref[...].astype(BF16),
                            (((0,), (0,)), ((), ())), preferred_element_type=F32)
        if nm == 1:
            o_ref[...] = p.astype(o_ref.dtype)
        else:
            acc, = scratch
            cc = pl.program_id(2)

            @pl.when(cc == 0)
            def _():
                acc[...] = p

            @pl.when(cc > 0)
            def _():
                acc[...] += p

            @pl.when(cc == nm - 1)
            def _():
                o_ref[...] = acc[...].astype(o_ref.dtype)

    blocks = _nbytes((tm, ta), a.dtype) + _nbytes((tm, tn), g.dtype) + _nbytes((ta, tn), BF16)
    return _pcall(
        body, name=name,
        out_shape=jax.ShapeDtypeStruct((out_shards, ka, ns), BF16),
        grid=(ka // ta, out_shards * q, nm),
        in_specs=[pl.BlockSpec((tm, ta), lambda i, j, c: (c, i)),
                  pl.BlockSpec((tm, tn), lambda i, j, c: (c, j))],
        out_specs=pl.BlockSpec((None, ta, tn), lambda i, j, c: (j // q, i, j % q)),
        scratch_shapes=[pltpu.VMEM((ta, tn), F32)] if nm > 1 else [],
        compiler_params=_params(("parallel", "parallel", "arbitrary"), blocks,
                                3 * _nbytes((ta, tn), F32) + _nbytes((tm, ta), F32)),
    )(a, g)


def _rms_scale(v):
    return lax.rsqrt(jnp.mean(v * v, axis=-1, keepdims=True) + NORM_EPS)


def _norm_fwd(x, name, *, z=None, g_post=None, g_next=None, target=None):
    t, d = x.shape
    tr = _tile(t, ROW_TILE, SUBLANE_BF16)
    has_res, has_next, has_loss = z is not None, g_next is not None, target is not None

    def body(*refs):
        it = iter(refs)
        x_ref = next(it)
        z_ref, gp_ref = (next(it), next(it)) if has_res else (None, None)
        gn_ref = next(it) if has_next else None
        t_ref = next(it) if has_loss else None
        xv = x_ref[...]
        if has_res:
            zv = z_ref[...].astype(F32)
            xv = xv + zv * _rms_scale(zv) * gp_ref[...]
            if not has_loss:
                next(it)[...] = xv
        if has_next:
            next(it)[...] = (xv * _rms_scale(xv) * gn_ref[...]).astype(BF16)
        if has_loss:
            e = xv - t_ref[...]
            next(it)[...] = e * (1.0 / d)
            ls_ref = next(it)

            @pl.when(pl.program_id(0) == 0)
            def _():
                ls_ref[...] = jnp.zeros_like(ls_ref)

            ls_ref[...] += jnp.sum(e * e, axis=0, keepdims=True)

    row = pl.BlockSpec((tr, d), lambda i: (i, 0))
    vec = pl.BlockSpec((1, d), lambda i: (0, 0))
    operands, in_specs, out_shape, out_specs = [x], [row], [], []
    if has_res:
        operands += [z, g_post.reshape(1, d)]
        in_specs += [row, vec]
        if not has_loss:
            out_shape.append(jax.ShapeDtypeStruct((t, d), F32))
            out_specs.append(row)
    if has_next:
        operands.append(g_next.reshape(1, d))
        in_specs.append(vec)
        out_shape.append(jax.ShapeDtypeStruct((t, d), BF16))
        out_specs.append(row)
    if has_loss:
        operands.append(target)
        in_specs.append(row)
        out_shape += [jax.ShapeDtypeStruct((t, d), F32), jax.ShapeDtypeStruct((1, d), F32)]
        out_specs += [row, vec]
    return _pcall(
        body, name=name, out_shape=out_shape, grid=(t // tr,),
        in_specs=in_specs, out_specs=out_specs,
        compiler_params=_params(("arbitrary",), 5 * _nbytes((tr, d), F32), 4 * _nbytes((tr, d), F32)),
    )(*operands)


def _norm_bwd(name, *, dxo=None, pre=None, post=None, want_dx=True):
    ref_arr = dxo if dxo is not None else pre[1]
    t, d = ref_arr.shape
    tr = _tile(t, ROW_TILE, SUBLANE_BF16)
    has_dxo, has_pre, has_post = dxo is not None, pre is not None, post is not None

    def body(*refs):
        it = iter(refs)
        dxo_ref = next(it) if has_dxo else None
        dh_ref, xin_ref, gpre_ref = (next(it), next(it), next(it)) if has_pre else (None,) * 3
        z_ref, gpost_ref = (next(it), next(it)) if has_post else (None, None)
        dx_ref = next(it) if (has_pre and want_dx) else None
        dz_ref = next(it) if has_post else None
        dgpre_ref = next(it) if has_pre else None
        dgpost_ref = next(it) if has_post else None
        first = pl.program_id(0) == 0

        dx = dxo_ref[...] if has_dxo else None
        if has_pre:
            xin = xin_ref[...]
            dh = dh_ref[...].astype(F32)
            r = _rms_scale(xin)
            gy = dh * gpre_ref[...]
            dloc = r * gy - xin * (r * r * r) * jnp.mean(gy * xin, axis=-1, keepdims=True)
            dx = dloc if dx is None else dx + dloc
            if want_dx:
                dx_ref[...] = dx

            @pl.when(first)
            def _():
                dgpre_ref[...] = jnp.zeros_like(dgpre_ref)

            dgpre_ref[...] += jnp.sum(dh * xin * r, axis=0, keepdims=True)
        if has_post:
            zv = z_ref[...].astype(F32)
            r = _rms_scale(zv)
            gy = dx * gpost_ref[...]
            dz = r * gy - zv * (r * r * r) * jnp.mean(gy * zv, axis=-1, keepdims=True)
            dz_ref[...] = dz.astype(BF16)

            @pl.when(first)
            def _():
                dgpost_ref[...] = jnp.zeros_like(dgpost_ref)

            dgpost_ref[...] += jnp.sum(dx * zv * r, axis=0, keepdims=True)

    row = pl.BlockSpec((tr, d), lambda i: (i, 0))
    vec = pl.BlockSpec((1, d), lambda i: (0, 0))
    operands, in_specs, out_shape, out_specs = [], [], [], []
    if has_dxo:
        operands.append(dxo)
        in_specs.append(row)
    if has_pre:
        operands += [pre[0], pre[1], pre[2].reshape(1, d)]
        in_specs += [row, row, vec]
    if has_post:
        operands += [post[0], post[1].reshape(1, d)]
        in_specs += [row, vec]
    if has_pre and want_dx:
        out_shape.append(jax.ShapeDtypeStruct((t, d), F32))
        out_specs.append(row)
    if has_post:
        out_shape.append(jax.ShapeDtypeStruct((t, d), BF16))
        out_specs.append(row)
    if has_pre:
        out_shape.append(jax.ShapeDtypeStruct((1, d), F32))
        out_specs.append(vec)
    if has_post:
        out_shape.append(jax.ShapeDtypeStruct((1, d), F32))
        out_specs.append(vec)
    return _pcall(
        body, name=name, out_shape=out_shape, grid=(t // tr,),
        in_specs=in_specs, out_specs=out_specs,
        compiler_params=_params(("arbitrary",), 6 * _nbytes((tr, d), F32), 6 * _nbytes((tr, d), F32)),
    )(*operands)


def _seq_tiles(t):
    tr = _tile(t, ROW_TILE, HALO)
    assert tr % HALO == 0 and t % tr == 0
    return tr, t // tr, tr // HALO


def _col(tr, width, cb):
    return pl.BlockSpec((tr, width), lambda i: (i, cb))


def _prev_halo(per, width, cb):
    return pl.BlockSpec((HALO, width), lambda i: (jnp.maximum(i * per - 1, 0), cb))


def _next_halo(per, n_halo, width, cb):
    return pl.BlockSpec((HALO, width), lambda i: (jnp.minimum((i + 1) * per, n_halo - 1), cb))


def _const(shape):
    return pl.BlockSpec(shape, lambda i: (0,) * len(shape))


def _glu(val, gate):
    return val.astype(F32) * _sigmoid(gate.astype(F32))


def _conv_chunks(tr, c):
    lanes = min(CONV_LANES, c)
    return [(r0, pl.ds(c0, lanes)) for r0 in range(0, tr, CONV_ROWS) for c0 in range(0, c, lanes)]


def _shift_copies(buf, shifted):
    rows = shifted.shape[1]
    buf[pl.ds(rows, SUBLANES), :] = jnp.zeros((SUBLANES, buf.shape[1]), F32)
    for s in range(SUBLANES):
        shifted[s] = buf[pl.ds(s, rows), :]


def _shifted_rows(shifted, offset, r0, cols):
    return shifted[offset % SUBLANES, pl.ds(offset - offset % SUBLANES + r0, CONV_ROWS), cols]


def _tap_sum(w_ref, shifted, r0, cols, first, step, kw):
    acc = jnp.zeros((CONV_ROWS, cols.size), F32)
    for k in range(kw):
        acc = acc + w_ref[pl.ds(k, 1), cols] * _shifted_rows(shifted, first + step * k, r0, cols)
    return acc


def _layer_norm_parts(ca):
    mu = jnp.mean(ca, axis=-1, keepdims=True)
    xc = ca - mu
    rs = lax.rsqrt(jnp.mean(xc * xc, axis=-1, keepdims=True) + NORM_EPS)
    return xc * rs, rs


def _branch_a_fwd(proj, cw, cb, lg, lb, name):
    t = proj.shape[0]
    kw, c = cw.shape
    tr, nt, per = _seq_tiles(t)

    def body(av_ref, ag_ref, hv_ref, hg_ref, cw_ref, cb_ref, lg_ref, lb_ref, ca_ref, act_ref, abuf, ash):
        i = pl.program_id(0)
        abuf[pl.ds(0, HALO), :] = jnp.where(i > 0, _glu(hv_ref[...], hg_ref[...]), 0.0)
        abuf[pl.ds(HALO, tr), :] = _glu(av_ref[...], ag_ref[...])
        _shift_copies(abuf, ash)
        for r0, cols in _conv_chunks(tr, c):
            ca_ref[pl.ds(r0, CONV_ROWS), cols] = (
                _tap_sum(cw_ref, ash, r0, cols, HALO - (kw - 1), 1, kw) + cb_ref[:, cols])
        xh, _ = _layer_norm_parts(ca_ref[...])
        ln = xh * lg_ref[...] + lb_ref[...]
        act_ref[...] = (ln * _sigmoid(ln)).astype(BF16)

    return _pcall(
        body, name=name,
        out_shape=[jax.ShapeDtypeStruct((t, c), F32), jax.ShapeDtypeStruct((t, c), BF16)],
        grid=(nt,),
        in_specs=[_col(tr, c, 0), _col(tr, c, 1), _prev_halo(per, c, 0), _prev_halo(per, c, 1),
                  _const((kw, c)), _const((1, c)), _const((1, c)), _const((1, c))],
        out_specs=[_col(tr, c, 0), _col(tr, c, 0)],
        scratch_shapes=[pltpu.VMEM((HALO + tr + SUBLANES, c), F32), pltpu.VMEM((SUBLANES, HALO + tr, c), F32)],
        compiler_params=_params(("arbitrary",), 4 * _nbytes((tr, c), F32),
                                (8 + SUBLANES) * _nbytes((tr + HALO, c), F32)),
    )(proj, proj, proj, proj, cw, cb.reshape(1, c), lg.reshape(1, c), lb.reshape(1, c))


def _branch_a_bwd(dproj, dact, ca, proj, cw, lg, lb, name):
    t = proj.shape[0]
    kw, c = cw.shape
    tr, nt, per = _seq_tiles(t)
    n_halo = t // HALO

    def body(dproj_in, da_ref, dah_ref, ca_ref, cah_ref, av_ref, ag_ref, hv_ref, hg_ref,
             cw_ref, lg_ref, lb_ref, out_ref, dlg_ref, dlb_ref, dcb_ref, dcw_ref, abuf, dbuf, sgbuf, ash, dsh):
        del dproj_in
        i = pl.program_id(0)
        lgv, lbv = lg_ref[...], lb_ref[...]

        def conv_grad(dact_v, ca_v):
            xh, rs = _layer_norm_parts(ca_v)
            ln = xh * lgv + lbv
            sg = _sigmoid(ln)
            dln = dact_v.astype(F32) * (sg * (1.0 + ln * (1.0 - sg)))
            dxh = dln * lgv
            dca = rs * (dxh - jnp.mean(dxh, axis=-1, keepdims=True)
                        - xh * jnp.mean(dxh * xh, axis=-1, keepdims=True))
            return dca, dln, xh

        dca, dln, xh = conv_grad(da_ref[...], ca_ref[...])
        dca_h, _, _ = conv_grad(dah_ref[...], cah_ref[...])
        dbuf[pl.ds(0, tr), :] = dca
        dbuf[pl.ds(tr, HALO), :] = jnp.where(i < nt - 1, dca_h, 0.0)

        @pl.when(i == 0)
        def _():
            dlg_ref[...] = jnp.zeros_like(dlg_ref)
            dlb_ref[...] = jnp.zeros_like(dlb_ref)
            dcb_ref[...] = jnp.zeros_like(dcb_ref)
            dcw_ref[...] = jnp.zeros_like(dcw_ref)

        dlg_ref[...] += jnp.sum(dln * xh, axis=0, keepdims=True)
        dlb_ref[...] += jnp.sum(dln, axis=0, keepdims=True)
        dcb_ref[...] += jnp.sum(dca, axis=0, keepdims=True)

        sg = _sigmoid(ag_ref[...].astype(F32))
        sgbuf[...] = sg
        abuf[pl.ds(0, HALO), :] = jnp.where(i > 0, _glu(hv_ref[...], hg_ref[...]), 0.0)
        abuf[pl.ds(HALO, tr), :] = av_ref[...].astype(F32) * sg
        _shift_copies(abuf, ash)
        _shift_copies(dbuf, dsh)

        for r0, cols in _conv_chunks(tr, c):
            rows = pl.ds(r0, CONV_ROWS)
            d_a = _tap_sum(cw_ref, dsh, r0, cols, kw - 1, -1, kw)
            sgc = sgbuf[rows, cols]
            out_ref[rows, cols] = (d_a * sgc).astype(BF16)
            out_ref[rows, pl.ds(c + cols.start, cols.size)] = (
                d_a * abuf[pl.ds(HALO + r0, CONV_ROWS), cols] * (1.0 - sgc)).astype(BF16)
        for _, cols in _conv_chunks(CONV_ROWS, c):
            for k in range(kw):
                acc = jnp.zeros((CONV_ROWS, cols.size), F32)
                for r0 in range(0, tr, CONV_ROWS):
                    acc = acc + (dbuf[pl.ds(r0, CONV_ROWS), cols]
                                 * _shifted_rows(ash, HALO - (kw - 1) + k, r0, cols))
                dcw_ref[pl.ds(k, 1), cols] += jnp.sum(acc, axis=0, keepdims=True)

    vec = _const((1, c))
    return _pcall(
        body, name=name,
        out_shape=[jax.ShapeDtypeStruct(dproj.shape, BF16)] + [jax.ShapeDtypeStruct((1, c), F32)] * 3
        + [jax.ShapeDtypeStruct((kw, c), F32)],
        grid=(nt,),
        in_specs=[pl.BlockSpec(memory_space=pl.ANY),
                  _col(tr, c, 0), _next_halo(per, n_halo, c, 0),
                  _col(tr, c, 0), _next_halo(per, n_halo, c, 0),
                  _col(tr, c, 0), _col(tr, c, 1), _prev_halo(per, c, 0), _prev_halo(per, c, 1),
                  _const((kw, c)), vec, vec],
        out_specs=[pl.BlockSpec((tr, 2 * c), lambda i: (i, 0)), vec, vec, vec, _const((kw, c))],
        scratch_shapes=[pltpu.VMEM((HALO + tr + SUBLANES, c), F32), pltpu.VMEM((HALO + tr + SUBLANES, c), F32),
                        pltpu.VMEM((tr, c), F32),
                        pltpu.VMEM((SUBLANES, HALO + tr, c), F32), pltpu.VMEM((SUBLANES, HALO + tr, c), F32)],
        input_output_aliases={0: 0},
        compiler_params=_params(("arbitrary",), 6 * _nbytes((tr, c), F32),
                                (12 + 2 * SUBLANES) * _nbytes((tr + HALO, c), F32)),
    )(dproj, dact, dact, ca, ca, proj, proj, proj, proj, cw, lg.reshape(1, c), lb.reshape(1, c))


def _branch_b_fwd(proj, cw, name):
    t = proj.shape[0]
    kw, c = cw.shape
    tr, nt, per = _seq_tiles(t)

    def body(sb_ref, sc_ref, sx_ref, hc_ref, hx_ref, cw_ref, o_ref, pbuf):
        i = pl.program_id(0)
        hp = hc_ref[...].astype(F32) * hx_ref[...].astype(F32)
        pbuf[pl.ds(0, HALO), :] = jnp.where(i > 0, hp, 0.0)
        pbuf[pl.ds(HALO, tr), :] = sc_ref[...].astype(F32) * sx_ref[...].astype(F32)
        u = jnp.zeros((tr, c), F32)
        for k in range(kw):
            u = u + cw_ref[pl.ds(k, 1), :] * pbuf[pl.ds(HALO - (kw - 1) + k, tr), :]
        o_ref[...] = (sb_ref[...].astype(F32) * u).astype(BF16)

    return _pcall(
        body, name=name,
        out_shape=jax.ShapeDtypeStruct((t, c), BF16),
        grid=(nt,),
        in_specs=[_col(tr, c, 2), _col(tr, c, 3), _col(tr, c, 4),
                  _prev_halo(per, c, 3), _prev_halo(per, c, 4), _const((kw, c))],
        out_specs=_col(tr, c, 0),
        scratch_shapes=[pltpu.VMEM((HALO + tr, c), F32)],
        compiler_params=_params(("arbitrary",), 4 * _nbytes((tr, c), F32), 6 * _nbytes((tr + HALO, c), F32)),
    )(proj, proj, proj, proj, proj, cw)


def _branch_b_bwd(dproj, dbin, proj, cw, name):
    t = proj.shape[0]
    kw, c = cw.shape
    tr, nt, per = _seq_tiles(t)
    n_halo = t // HALO

    def body(dproj_in, db_ref, dbh_ref, sb_ref, sbh_ref, sc_ref, sx_ref, hc_ref, hx_ref, cw_ref,
             out_ref, dcw_ref, pbuf, dubuf, res):
        del dproj_in
        i = pl.program_id(0)
        j = pl.program_id(1)

        @pl.when(j == 0)
        def _():
            sb = sb_ref[...].astype(F32)
            sc = sc_ref[...].astype(F32)
            sx = sx_ref[...].astype(F32)
            dbin_v = db_ref[...].astype(F32)
            hp = hc_ref[...].astype(F32) * hx_ref[...].astype(F32)
            pbuf[pl.ds(0, HALO), :] = jnp.where(i > 0, hp, 0.0)
            pbuf[pl.ds(HALO, tr), :] = sc * sx
            du = dbin_v * sb
            du_h = dbh_ref[...].astype(F32) * sbh_ref[...].astype(F32)
            dubuf[pl.ds(0, tr), :] = du
            dubuf[pl.ds(tr, HALO), :] = jnp.where(i < nt - 1, du_h, 0.0)

            @pl.when(i == 0)
            def _():
                dcw_ref[...] = jnp.zeros_like(dcw_ref)

            u = jnp.zeros((tr, c), F32)
            dp = jnp.zeros((tr, c), F32)
            for k in range(kw):
                shifted = pbuf[pl.ds(HALO - (kw - 1) + k, tr), :]
                u = u + cw_ref[pl.ds(k, 1), :] * shifted
                dp = dp + cw_ref[pl.ds(k, 1), :] * dubuf[pl.ds(kw - 1 - k, tr), :]
                dcw_ref[pl.ds(k, 1), :] += jnp.sum(du * shifted, axis=0, keepdims=True)
            res[0] = (dbin_v * u).astype(BF16)
            res[1] = (dp * sx).astype(BF16)
            res[2] = (dp * sc).astype(BF16)

        out_ref[...] = res[j]

    def colj(cb):
        return pl.BlockSpec((tr, c), lambda i, j: (i, cb))

    def prevj(cb):
        return pl.BlockSpec((HALO, c), lambda i, j: (jnp.maximum(i * per - 1, 0), cb))

    def nextj(cb):
        return pl.BlockSpec((HALO, c), lambda i, j: (jnp.minimum((i + 1) * per, n_halo - 1), cb))

    return _pcall(
        body, name=name,
        out_shape=[jax.ShapeDtypeStruct(dproj.shape, BF16), jax.ShapeDtypeStruct((kw, c), F32)],
        grid=(nt, 3),
        in_specs=[pl.BlockSpec(memory_space=pl.ANY),
                  colj(0), nextj(0), colj(2), nextj(2), colj(3), colj(4), prevj(3), prevj(4),
                  pl.BlockSpec((kw, c), lambda i, j: (0, 0))],
        out_specs=[pl.BlockSpec((tr, c), lambda i, j: (i, 2 + j)),
                   pl.BlockSpec((kw, c), lambda i, j: (0, 0))],
        scratch_shapes=[pltpu.VMEM((HALO + tr, c), F32), pltpu.VMEM((HALO + tr, c), F32),
                        pltpu.VMEM((3, tr, c), BF16)],
        input_output_aliases={0: 0},
        compiler_params=_params(("arbitrary", "arbitrary"), 6 * _nbytes((tr, c), F32),
                                10 * _nbytes((tr + HALO, c), F32)),
    )(dproj, dbin, dbin, proj, proj, proj, proj, proj, proj, cw)


def _softmax_rows(s):
    e = jnp.exp(s - jnp.max(s, axis=-1, keepdims=True))
    return e / jnp.sum(e, axis=-1, keepdims=True)


def _attn_fwd(proj, kv, name):
    t = proj.shape[0]
    m, c2 = kv.shape
    c = c2 // 2
    hd = c // N_HEADS
    ta = _tile(t, ATTN_TILE, SUBLANE_BF16)
    scale = hd ** -0.5

    def body(q_ref, kv_ref, o_ref):
        for h in range(N_HEADS):
            qh = q_ref[:, pl.ds(h * hd, hd)]
            kh = kv_ref[:, pl.ds(h * hd, hd)]
            vh = kv_ref[:, pl.ds(c + h * hd, hd)]
            s = lax.dot_general(qh, kh, (((1,), (1,)), ((), ())), preferred_element_type=F32) * scale
            p = _softmax_rows(s)
            o_ref[:, pl.ds(h * hd, hd)] = jnp.dot(p.astype(BF16), vh,
                                                  preferred_element_type=F32).astype(BF16)

    return _pcall(
        body, name=name,
        out_shape=jax.ShapeDtypeStruct((t, c), BF16),
        grid=(t // ta,),
        in_specs=[pl.BlockSpec((ta, c), lambda i: (i, 5)), _const((m, c2))],
        out_specs=pl.BlockSpec((ta, c), lambda i: (i, 0)),
        compiler_params=_params(("parallel",), 2 * _nbytes((ta, c), BF16) + _nbytes((m, c2), BF16),
                                8 * _nbytes((ta, m), F32)),
    )(proj, kv)


def _attn_bwd(dproj, d_o, proj, kv, name):
    t = proj.shape[0]
    m, c2 = kv.shape
    c = c2 // 2
    hd = c // N_HEADS
    ta = _tile(t, ATTN_TILE, SUBLANE_BF16)
    scale = hd ** -0.5

    def body(dproj_in, do_ref, q_ref, kv_ref, dq_ref, dkv_ref):
        del dproj_in

        @pl.when(pl.program_id(0) == 0)
        def _():
            dkv_ref[...] = jnp.zeros_like(dkv_ref)

        for h in range(N_HEADS):
            qh = q_ref[:, pl.ds(h * hd, hd)]
            kh = kv_ref[:, pl.ds(h * hd, hd)]
            vh = kv_ref[:, pl.ds(c + h * hd, hd)]
            doh = do_ref[:, pl.ds(h * hd, hd)]
            s = lax.dot_general(qh, kh, (((1,), (1,)), ((), ())), preferred_element_type=F32) * scale
            p = _softmax_rows(s)
            dp = lax.dot_general(doh, vh, (((1,), (1,)), ((), ())), preferred_element_type=F32)
            ds = (p * (dp - jnp.sum(dp * p, axis=-1, keepdims=True))).astype(BF16)
            dq_ref[:, pl.ds(h * hd, hd)] = (jnp.dot(ds, kh, preferred_element_type=F32) * scale).astype(BF16)
            dkv_ref[:, pl.ds(h * hd, hd)] += lax.dot_general(
                ds, qh, (((0,), (0,)), ((), ())), preferred_element_type=F32) * scale
            dkv_ref[:, pl.ds(c + h * hd, hd)] += lax.dot_general(
                p.astype(BF16), doh, (((0,), (0,)), ((), ())), preferred_element_type=F32)

    return _pcall(
        body, name=name,
        out_shape=[jax.ShapeDtypeStruct(dproj.shape, BF16), jax.ShapeDtypeStruct((m, c2), F32)],
        grid=(t // ta,),
        in_specs=[pl.BlockSpec(memory_space=pl.ANY),
                  pl.BlockSpec((ta, c), lambda i: (i, 0)), pl.BlockSpec((ta, c), lambda i: (i, 5)),
                  _const((m, c2))],
        out_specs=[pl.BlockSpec((ta, c), lambda i: (i, 5)), _const((m, c2))],
        input_output_aliases={0: 0},
        compiler_params=_params(("arbitrary",), 3 * _nbytes((ta, c), BF16) + 2 * _nbytes((m, c2), F32),
                                10 * _nbytes((ta, m), F32)),
    )(dproj, d_o, proj, kv)


def _merge_fwd(proj, ya, yb, yx, name):
    t, d = ya.shape
    tr = _tile(t, ROW_TILE, SUBLANE_BF16)

    def body(g_ref, ya_ref, yb_ref, yx_ref, o_ref):
        acc = jnp.zeros((tr, d), F32)
        for b, y_ref in enumerate((ya_ref, yb_ref, yx_ref)):
            acc = acc + _sigmoid(g_ref[:, pl.ds(b * d, d)].astype(F32)) * y_ref[...].astype(F32)
        o_ref[...] = acc.astype(BF16)

    row = pl.BlockSpec((tr, d), lambda i: (i, 0))
    return _pcall(
        body, name=name,
        out_shape=jax.ShapeDtypeStruct((t, d), BF16),
        grid=(t // tr,),
        in_specs=[pl.BlockSpec((tr, 3 * d), lambda i: (i, 1)), row, row, row],
        out_specs=row,
        compiler_params=_params(("parallel",), 7 * _nbytes((tr, d), BF16), 6 * _nbytes((tr, d), F32)),
    )(proj, ya, yb, yx)


def _merge_bwd(dmerged, proj, ya, yb, yx, name):
    t, d = ya.shape
    tr = _tile(t, ROW_TILE, SUBLANE_BF16)

    def body(dm_ref, g_ref, ya_ref, yb_ref, yx_ref, dg_ref, dya_ref, dyb_ref, dyx_ref):
        dm = dm_ref[...].astype(F32)
        for b, (y_ref, dy_ref) in enumerate(((ya_ref, dya_ref), (yb_ref, dyb_ref), (yx_ref, dyx_ref))):
            sg = _sigmoid(g_ref[:, pl.ds(b * d, d)].astype(F32))
            dg_ref[:, pl.ds(b * d, d)] = (dm * y_ref[...].astype(F32) * sg * (1.0 - sg)).astype(BF16)
            dy_ref[...] = (dm * sg).astype(BF16)

    row = pl.BlockSpec((tr, d), lambda i: (i, 0))
    gates = pl.BlockSpec((tr, 3 * d), lambda i: (i, 1))
    return _pcall(
        body, name=name,
        out_shape=[jax.ShapeDtypeStruct(proj.shape, BF16)] + [jax.ShapeDtypeStruct((t, d), BF16)] * 3,
        grid=(t // tr,),
        in_specs=[row, gates, row, row, row],
        out_specs=[gates, row, row, row],
        compiler_params=_params(("parallel",), 14 * _nbytes((tr, d), BF16), 8 * _nbytes((tr, d), F32)),
    )(dmerged, proj, ya, yb, yx)


def _as3(a):
    return a.reshape((-1,) + a.shape[-2:])


def _ew_call(fn, ins, n_out, out_dtypes, name, batch=None, prev=()):
    b, r, c = ins[0].shape
    lo, hi = batch if batch is not None else (0, b)
    tr = _ew_rows(r, c, len(ins) + n_out)
    n_in = len(ins) + len(prev)

    def body(*refs):
        outs = fn(*[ref[...] for ref in refs[:len(ins)]])
        for ref, val in zip(refs[n_in:], outs):
            ref[...] = val.astype(ref.dtype)

    spec = pl.BlockSpec((None, tr, c), lambda i, j: (i + lo, j, 0))
    return _pcall(
        body, name=name,
        out_shape=[jax.ShapeDtypeStruct((b, r, c), dt) for dt in out_dtypes],
        grid=(hi - lo, r // tr),
        in_specs=[spec] * len(ins) + [pl.BlockSpec(memory_space=pl.ANY)] * len(prev), out_specs=[spec] * n_out,
        input_output_aliases={len(ins) + k: k for k in range(len(prev))},
        compiler_params=_params(("parallel", "parallel"), (len(ins) + n_out) * _nbytes((tr, c), F32),
                                6 * _nbytes((tr, c), F32)),
    )(*ins, *prev)


def _my_chip():
    return 2 * lax.axis_index("x") + lax.axis_index("y")


def _my_core():
    return lax.axis_index("c")


def _cast_to_slot(w, l, name, after=()):
    _, r, cs = w.shape
    tr = _ew_rows(r, cs, 2)

    def body(w_ref, *rest):
        rest[-1][...] = w_ref[...].astype(BF16)

    return _pcall(
        body, name=name,
        out_shape=jax.ShapeDtypeStruct((N_CHIPS, r, cs), BF16),
        grid=(r // tr,),
        in_specs=[pl.BlockSpec((None, tr, cs), lambda i: (l, i, 0))]
        + [pl.BlockSpec(memory_space=pl.ANY)] * len(after),
        out_specs=pl.BlockSpec((None, tr, cs), lambda i: (_my_chip(), i, 0)),
        compiler_params=_params(("parallel",), 2 * _nbytes((tr, cs), F32)),
    )(w, *after)


def _add_half(part, theirs, name):
    _, rh, cs = theirs.shape
    tr = _ew_rows(rh, cs, 3)
    nrb = rh // tr

    def body(a_ref, b_ref, o_ref):
        o_ref[...] = (a_ref[...].astype(F32) + b_ref[...].astype(F32)).astype(BF16)

    half = pl.BlockSpec((None, tr, cs), lambda j, i: (j, i, 0))
    return _pcall(
        body, name=name,
        out_shape=jax.ShapeDtypeStruct((N_CHIPS, rh, cs), BF16),
        grid=(N_CHIPS, nrb),
        in_specs=[pl.BlockSpec((None, tr, cs), lambda j, i: (j, _my_core() * nrb + i, 0)), half],
        out_specs=half,
        compiler_params=_params(("parallel", "parallel"), 3 * _nbytes((tr, cs), F32)),
    )(part, theirs)


def _sum_chips(chip_sum, q, total, l, nl, name):
    _, rh, cs = q.shape
    tr = _ew_rows(rh, cs, 5)
    nrb = rh // tr

    def body(own_ref, q1_ref, q2_ref, q3_ref, *rest):
        acc = own_ref[...].astype(F32)
        for ref in (q1_ref, q2_ref, q3_ref):
            acc = acc + ref[...].astype(F32)
        rest[-1][...] = acc

    def slot(k):
        return pl.BlockSpec((None, tr, cs), lambda i: ((_my_chip() + k) % N_CHIPS, i, 0))

    in_specs = [slot(0), slot(1), slot(2), slot(3)]
    operands = [chip_sum, q, q, q]
    aliases = {}
    if total is not None:
        in_specs.append(HBM_SPEC)
        operands.append(total)
        aliases = {4: 0}
    return _pcall(
        body, name=name,
        out_shape=jax.ShapeDtypeStruct((nl, 2 * rh, cs), F32),
        grid=(nrb,),
        in_specs=in_specs,
        out_specs=pl.BlockSpec((None, tr, cs), lambda i: (l, _my_core() * nrb + i, 0)),
        input_output_aliases=aliases,
        compiler_params=_params(("parallel",), 5 * _nbytes((tr, cs), F32)),
    )(*operands)


def _adamw(w, g, m, v, name, batch=None, prev=()):
    shape = w.shape
    c1 = 1.0 - ADAM_B1 ** ADAM_STEP
    c2 = 1.0 - ADAM_B2 ** ADAM_STEP

    def fn(wv, gv, mv, vv):
        mn = ADAM_B1 * mv + (1.0 - ADAM_B1) * gv
        vn = ADAM_B2 * vv + (1.0 - ADAM_B2) * (gv * gv)
        delta = -ADAM_LR * ((mn / c1) / (jnp.sqrt(vn / c2) + ADAM_EPS) + ADAM_WD * wv)
        return delta, mn, vn

    outs = _ew_call(fn, [_as3(w), _as3(g), _as3(m), _as3(v)], 3, [F32] * 3, name,
                    batch=batch, prev=[_as3(p) for p in prev])
    return [o.reshape(shape) for o in outs]


def _sum_leading(q, name):
    b, nj, r, c = q.shape
    tr = _ew_rows(r, c, nj + 1)

    def body(q_ref, o_ref):
        acc = q_ref[0].astype(F32)
        for j in range(1, nj):
            acc = acc + q_ref[j].astype(F32)
        o_ref[...] = acc

    return _pcall(
        body, name=name,
        out_shape=jax.ShapeDtypeStruct((b, r, c), F32),
        grid=(b, r // tr),
        in_specs=[pl.BlockSpec((None, nj, tr, c), lambda i, j: (i, 0, j, 0))],
        out_specs=pl.BlockSpec((None, tr, c), lambda i, j: (i, j, 0)),
        compiler_params=_params(("parallel", "parallel"), (nj + 1) * _nbytes((tr, c), F32),
                                2 * _nbytes((tr, c), F32)),
    )(q)


HBM_SPEC = pl.BlockSpec(memory_space=pl.ANY)


def _place():
    x, y, c = lax.axis_index("x"), lax.axis_index("y"), lax.axis_index("c")
    peers = [(1 - x, y), (x, 1 - y), (1 - x, 1 - y)]
    return x, y, c, 2 * x + y, peers


def _remote(src, dst, send_sem, recv_sem, dev):
    return pltpu.make_async_remote_copy(src_ref=src, dst_ref=dst, send_sem=send_sem, recv_sem=recv_sem,
                                        device_id=dev, device_id_type=MESH)


IN_HBM = pl.BlockSpec(memory_space=pltpu.HBM)
IN_SEM = pl.BlockSpec(memory_space=pltpu.SEMAPHORE)
DATAFLOW = pltpu.SideEffectType.DATAFLOW_SIDE_EFFECTING
TOKEN = jax.ShapeDtypeStruct((8, LANE), F32)


def _in_hbm(arrays):
    return [pltpu.with_memory_space_constraint(a, pltpu.HBM) for a in arrays]


def _half_rows(ref, c):
    rh = ref.shape[1] // 2
    return pl.ds(c * rh, rh)


def _split_start(body_copies, passed, landing, name, per_array=3):
    n_pass, n_land = len(passed), len(landing)
    n_arr = n_pass + n_land
    n = per_array * n_pass

    def body(*refs):
        ins = refs[:n_arr]
        send, recv = refs[n_arr], refs[n_arr + 1]
        token = refs[-1]
        for src, dst, _, s_sem, r_sem, dev in body_copies(ins[:n_pass], ins[n_pass:], send, recv):
            _remote(src, dst, s_sem, r_sem, dev).start()
        token[...] = jnp.zeros_like(token)

    arrays = list(passed) + list(landing)
    return pl.pallas_call(
        body, name=name,
        out_shape=(pltpu.SemaphoreType.DMA((n,)), pltpu.SemaphoreType.DMA((n,)),
                   *[pltpu.HBM(a.shape, a.dtype) for a in arrays], TOKEN),
        in_specs=[IN_HBM] * n_arr,
        out_specs=(IN_SEM, IN_SEM, *[IN_HBM] * n_arr, pl.BlockSpec(memory_space=pltpu.VMEM)),
        input_output_aliases={i: 2 + i for i in range(n_arr)},
        compiler_params=pltpu.CompilerParams(has_side_effects=DATAFLOW),
    )(*_in_hbm(arrays))


def _split_wait(body_copies, send, recv, passed, landing, after, name):
    n_pass, n_land = len(passed), len(landing)
    n_arr = n_pass + n_land

    def body(*refs):
        ins = refs[:n_arr]
        send_ref, recv_ref = refs[n_arr], refs[n_arr + 1]
        for src, _, landed, s_sem, r_sem, dev in body_copies(ins[:n_pass], ins[n_pass:], send_ref, recv_ref):
            cp = _remote(src, landed, s_sem, r_sem, dev)
            cp.wait_send()
            cp.wait_recv()

    arrays = list(passed) + list(landing)
    return pl.pallas_call(
        body, name=name,
        out_shape=tuple(pltpu.HBM(a.shape, a.dtype) for a in arrays),
        in_specs=[IN_HBM] * n_arr + [IN_SEM, IN_SEM] + [HBM_SPEC] * len(after),
        out_specs=tuple([IN_HBM] * n_arr),
        input_output_aliases={i: i for i in range(n_arr)},
        compiler_params=pltpu.CompilerParams(has_side_effects=DATAFLOW),
    )(*arrays, send, recv, *after)


def _gather_copies(slots, _, send, recv):
    _, _, c, me, peers = _place()
    copies = []
    for a, ref in enumerate(slots):
        rows = _half_rows(ref, c)
        own = ref.at[me, rows, :]
        for j, (px, py) in enumerate(peers):
            k = 3 * a + j
            copies.append((own, own, ref.at[2 * px + py, rows, :], send.at[k], recv.at[k], (px, py, c)))
    return copies


def _exchange_copies(chip_sums, landing, send, recv):
    _, _, c, me, peers = _place()
    copies = []
    for a, (s_ref, q_ref) in enumerate(zip(chip_sums, landing)):
        for j, (px, py) in enumerate(peers):
            k = 3 * a + j
            copies.append((s_ref.at[2 * px + py], q_ref.at[me], q_ref.at[2 * px + py],
                           send.at[k], recv.at[k], (px, py, c)))
    return copies


def _gather_start(slots, name):
    out = _split_start(_gather_copies, slots, [], name)
    return out[0], out[1], list(out[2:-1]), out[-1]


def _gather_wait(send, recv, slots, after, name):
    return list(_split_wait(_gather_copies, send, recv, slots, [], after, name))


def _forward_copies(slots, _, send, recv):
    x, y, c, _, peers = _place()
    copies = []
    for a, ref in enumerate(slots):
        for j, (px, py) in enumerate(peers):
            k = 3 * a + j
            mine = ref.at[2 * px + py, _half_rows(ref, c), :]
            theirs = ref.at[2 * px + py, _half_rows(ref, 1 - c), :]
            copies.append((mine, mine, theirs, send.at[k], recv.at[k], (x, y, 1 - c)))
    return copies


def _forward_start(slots, name):
    out = _split_start(_forward_copies, slots, [], name)
    return out[0], out[1], list(out[2:-1]), out[-1]


def _forward_wait(send, recv, slots, after, name):
    return list(_split_wait(_forward_copies, send, recv, slots, [], after, name))


def _gather_small(v, name):
    def body(v_ref, o_ref, send, recv, lsem):
        x, y, c, me, peers = _place()
        local = pltpu.make_async_copy(v_ref, o_ref.at[me], lsem)
        local.start()
        sends = [_remote(v_ref, o_ref.at[me], send.at[j], recv.at[j], (px, py, c))
                 for j, (px, py) in enumerate(peers)]
        for cp in sends:
            cp.start()
        for j, (px, py) in enumerate(peers):
            _remote(v_ref, o_ref.at[2 * px + py], send.at[j], recv.at[j], (px, py, c)).wait_recv()
        for cp in sends:
            cp.wait_send()
        local.wait()

    return _pcall(
        body, name=name,
        out_shape=jax.ShapeDtypeStruct((N_CHIPS,) + v.shape, v.dtype),
        in_specs=[HBM_SPEC], out_specs=HBM_SPEC,
        scratch_shapes=[pltpu.SemaphoreType.DMA((3,)), pltpu.SemaphoreType.DMA((3,)), pltpu.SemaphoreType.DMA],
    )(v)


def _gather_all(v, name):
    def body(v_ref, o_ref, send, recv, lsem):
        x, y, c = lax.axis_index("x"), lax.axis_index("y"), lax.axis_index("c")
        me = 4 * x + 2 * y + c
        local = pltpu.make_async_copy(v_ref, o_ref.at[me], lsem)
        local.start()
        flips = [(fx, fy, fc) for fx in (0, 1) for fy in (0, 1) for fc in (0, 1)][1:]
        peers = [(x ^ fx, y ^ fy, c ^ fc) for fx, fy, fc in flips]
        sends = [_remote(v_ref, o_ref.at[me], send.at[k], recv.at[k], dev) for k, dev in enumerate(peers)]
        for cp in sends:
            cp.start()
        for k, (px, py, pc) in enumerate(peers):
            _remote(v_ref, o_ref.at[4 * px + 2 * py + pc], send.at[k], recv.at[k], (px, py, pc)).wait_recv()
        for cp in sends:
            cp.wait_send()
        local.wait()

    return _pcall(
        body, name=name,
        out_shape=jax.ShapeDtypeStruct((8,) + v.shape, v.dtype),
        in_specs=[HBM_SPEC], out_specs=HBM_SPEC,
        scratch_shapes=[pltpu.SemaphoreType.DMA((7,)), pltpu.SemaphoreType.DMA((7,)), pltpu.SemaphoreType.DMA],
    )(v)


def _send_copies(parts, landing, send, recv):
    x, y, c, _, _ = _place()
    return [(p.at[:, _half_rows(p, 1 - c), :], t, t, send.at[a], recv.at[a], (x, y, 1 - c))
            for a, (p, t) in enumerate(zip(parts, landing))]


def _send_start(parts, name):
    na = len(parts)
    landing = [lax.empty((N_CHIPS, p.shape[1] // 2, p.shape[2]), p.dtype) for p in parts]
    out = _split_start(_send_copies, parts, landing, name, per_array=1)
    return out[0], out[1], list(out[2:2 + na]), list(out[2 + na:2 + 2 * na]), out[-1]


def _send_wait(send, recv, parts, landing, after, name):
    na = len(parts)
    out = _split_wait(_send_copies, send, recv, parts, landing, after, name)
    return list(out[:na]), list(out[na:])


def _exchange_start(chip_sums, name):
    na = len(chip_sums)
    landing = [lax.empty(s.shape, s.dtype) for s in chip_sums]
    out = _split_start(_exchange_copies, chip_sums, landing, name)
    return out[0], out[1], list(out[2:2 + na]), list(out[2 + na:2 + 2 * na]), out[-1]


def _exchange_wait(send, recv, chip_sums, landing, after, name):
    na = len(chip_sums)
    out = _split_wait(_exchange_copies, send, recv, chip_sums, landing, after, name)
    return list(out[:na]), list(out[na:])


def _join_halves(reds, lo, hi, name, after=()):
    na = len(reds)
    n_in = na + len(after)
    layers = pl.ds(lo, hi - lo)

    def body(*refs):
        outs = refs[n_in:n_in + na]
        send, recv = refs[n_in + na:]
        x, y, c, _, _ = _place()
        sends = []
        for a, ref in enumerate(outs):
            rh = ref.shape[1] // 2
            mine = ref.at[layers, pl.ds(c * rh, rh), :]
            sends.append(_remote(mine, mine, send.at[a], recv.at[a], (x, y, 1 - c)))
            sends[-1].start()
        for a, ref in enumerate(outs):
            rh = ref.shape[1] // 2
            other = ref.at[layers, pl.ds((1 - c) * rh, rh), :]
            _remote(other, other, send.at[a], recv.at[a], (x, y, 1 - c)).wait_recv()
        for cp in sends:
            cp.wait_send()

    return _pcall(
        body, name=name,
        out_shape=[jax.ShapeDtypeStruct(r.shape, r.dtype) for r in reds],
        in_specs=[HBM_SPEC] * n_in, out_specs=[HBM_SPEC] * na,
        input_output_aliases={a: a for a in range(na)},
        scratch_shapes=[pltpu.SemaphoreType.DMA((na,)), pltpu.SemaphoreType.DMA((na,))],
    )(*reds, *after)


COL_SHARDED = ("w_in", "w_a_out", "w_b_out", "w_x_out", "w_up")
ROW_SHARDED = ("w_kv", "w_o", "w_down")
BIG = COL_SHARDED + ROW_SHARDED
SMALL_REPLICATED = ("g_mix_pre", "conv_a_b", "ln_a_g", "ln_a_b", "g_mem", "g_mix_post", "g_mlp_pre", "g_mlp_post")
SMALL_SHARDED = ("conv_a_w", "conv_b_w")
WEIGHT_ORDER = ("g_mix_pre", "w_in", "conv_a_w", "conv_a_b", "ln_a_g", "ln_a_b", "w_a_out", "conv_b_w", "w_b_out",
                "g_mem", "w_kv", "w_x_out", "w_o", "g_mix_post", "g_mlp_pre", "w_up", "w_down", "g_mlp_post")


def _pack_rows(arrays, width):
    rows = []
    for a in arrays:
        r = a.reshape(-1, width)
        rows.append(jnp.pad(r, ((0, (-r.shape[0]) % 8), (0, 0))))
    return jnp.concatenate(rows, axis=0)


def _unpack_rows(packed, like, width):
    out, at = [], 0
    for a in like:
        n = a.size // width
        out.append(packed[at:at + n].reshape(a.shape))
        at += n + (-n) % 8
    return out


def _step(w, m, v, x, mem, target):
    nl = w["w_in"].shape[0]
    d = x.shape[1]
    c = w["conv_a_b"].shape[1]
    ka = w["conv_a_w"].shape[1]

    conv_pack = jnp.concatenate([w["conv_a_w"], w["conv_b_w"]], axis=1)
    conv_rows = conv_pack.shape[1]
    conv_pack = jnp.pad(conv_pack, ((0, 0), (0, (-conv_rows) % 8), (0, 0)))
    conv_all = _gather_small(conv_pack, "gather_conv")
    slots = {name: [_cast_to_slot(w[name], 0, "cast_" + name, after=[conv_all])] for name in BIG}
    conv_all = jnp.moveaxis(conv_all, 0, 2).reshape(nl, conv_pack.shape[1], c)
    cw_a, cw_b = conv_all[:, :ka], conv_all[:, ka:conv_rows]

    def start_gather(l):
        return _gather_start([slots[name][l] for name in BIG], "gather_start_%d" % l)

    def start_forward(l, pending, after):
        send, recv, thru, _ = pending
        landed = _gather_wait(send, recv, thru, after, "gather_wait_%d" % l)
        return _forward_start(landed, "gather_fwd_start_%d" % l)

    def finish_forward(l, passing, after):
        send, recv, thru, _ = passing
        arrays = _forward_wait(send, recv, thru, after, "gather_fwd_wait_%d" % l)
        return {name: (a.reshape(1, -1, a.shape[-1]) if name in ROW_SHARDED else a)
                for name, a in zip(BIG, arrays)}

    pending = start_gather(0)
    for name in BIG:
        slots[name] += [_cast_to_slot(w[name], l, "cast_" + name, after=[pending[3]]) for l in range(1, nl)]
    (h,) = _norm_fwd(x, "norm_first", g_next=w["g_mix_pre"][0])
    passing = start_forward(0, pending, [h] + [s for name in BIG for s in slots[name][1:]])
    full = [finish_forward(0, passing, [passing[3]])]

    saved = []
    xl = x
    dy = loss_cols = None
    for l in range(nl):
        s = {"x": xl, "h": h}
        fw = full[l]
        token = None
        if l + 1 < nl:
            pending = start_gather(l + 1)
            token = pending[3]
        proj = _mm_nn(h, fw["w_in"], BF16, "mm_in", after=token)
        ca, act_a = _branch_a_fwd(proj, cw_a[l], w["conv_a_b"][l], w["ln_a_g"][l], w["ln_a_b"][l], "branch_a_fwd")
        b_in = _branch_b_fwd(proj, cw_b[l], "branch_b_fwd")
        (mem_n,) = _norm_fwd(mem, "norm_mem", g_next=w["g_mem"][l])
        kv = _mm_nn(mem_n, fw["w_kv"], BF16, "mm_kv")
        att = _attn_fwd(proj, kv, "attn_fwd")
        ya = _mm_nn(act_a, fw["w_a_out"], BF16, "mm_a_out")
        yb = _mm_nn(b_in, fw["w_b_out"], BF16, "mm_b_out")
        yx = _mm_nn(att, fw["w_x_out"], BF16, "mm_x_out")
        merged = _merge_fwd(proj, ya, yb, yx, "merge_fwd")
        z = _mm_nn(merged, fw["w_o"], BF16, "mm_o")
        x1, h2 = _norm_fwd(xl, "norm_mid", z=z, g_post=w["g_mix_post"][l], g_next=w["g_mlp_pre"][l])
        token = None
        if l + 1 < nl:
            passing = start_forward(l + 1, pending, [h2])
            token = passing[3]
        up = _mm_nn(h2, fw["w_up"], BF16, "mm_up", after=token)
        f = _mm_nn(up, fw["w_down"], BF16, "mm_down", a_act="relu2")
        s.update(proj=proj, ca=ca, act_a=act_a, b_in=b_in, mem_n=mem_n, kv=kv, att=att, ya=ya, yb=yb, yx=yx,
                 merged=merged, z=z, x1=x1, h2=h2, up=up, f=f)
        saved.append(s)
        if l + 1 < nl:
            full.append(finish_forward(l + 1, passing, [f]))
        if l + 1 < nl:
            xl, h = _norm_fwd(x1, "norm_mid", z=f, g_post=w["g_mlp_post"][l], g_next=w["g_mix_pre"][l + 1])
        else:
            dy, loss_cols = _norm_fwd(x1, "norm_loss", z=f, g_post=w["g_mlp_post"][l], target=target)

    part = {name: [None] * nl for name in BIG}
    small = {name: [None] * nl for name in SMALL_REPLICATED + SMALL_SHARDED}
    dxo = dy
    d_f, small["g_mlp_post"][nl - 1] = _norm_bwd("norm_bwd_top", dxo=dy,
                                                 post=(saved[-1]["f"], w["g_mlp_post"][nl - 1]))
    grad_x = None
    totals = {name: None for name in BIG}

    def start_send(l):
        return _send_start([part[name][l] for name in BIG], "rs_send_start_%d" % l)

    def start_exchange(l, sending, after):
        send, recv, parts, landing, _ = sending
        parts, theirs = _send_wait(send, recv, parts, landing, after, "rs_send_wait_%d" % l)
        chip_sums = [_add_half(p, t, "rs_add_" + name) for name, p, t in zip(BIG, parts, theirs)]
        return _exchange_start(chip_sums, "rs_xchg_start_%d" % l)

    def finish_exchange(l, pending, after):
        send, recv, chip_sums, landing, _ = pending
        chip_sums, q = _exchange_wait(send, recv, chip_sums, landing, after, "rs_xchg_wait_%d" % l)
        for name, own, got in zip(BIG, chip_sums, q):
            totals[name] = _sum_chips(own, got, totals[name], l, nl, "rs_sum_" + name)

    sending = exchange = None
    for l in reversed(range(nl)):
        s = saved[l]
        fw = full[l]
        d_up = _mm_nt(d_f, fw["w_down"], BF16, "mm_down_dx", relu2_of=s["up"],
                      after=None if sending is None else sending[4])
        part["w_down"][l] = _mm_tn(s["up"], d_f, 1, "mm_down_dw", a_act="relu2").reshape(N_CHIPS, -1, d)
        d_h2 = _mm_nt(d_up, fw["w_up"], BF16, "mm_up_dx")
        part["w_up"][l] = _mm_tn(s["h2"], d_up, N_CHIPS, "mm_up_dw")
        dx1, d_z, small["g_mlp_pre"][l], small["g_mix_post"][l] = _norm_bwd(
            "norm_bwd_mid", dxo=dxo, pre=(d_h2, s["x1"], w["g_mlp_pre"][l]), post=(s["z"], w["g_mix_post"][l]))
        if sending is not None:
            exchange = start_exchange(l + 1, sending, [d_z])
        d_merged = _mm_nt(d_z, fw["w_o"], BF16, "mm_o_dx", after=None if exchange is None else exchange[4])
        part["w_o"][l] = _mm_tn(s["merged"], d_z, 1, "mm_o_dw").reshape(N_CHIPS, -1, d)
        dproj, d_ya, d_yb, d_yx = _merge_bwd(d_merged, s["proj"], s["ya"], s["yb"], s["yx"], "merge_bwd")
        d_act_a = _mm_nt(d_ya, fw["w_a_out"], BF16, "mm_a_out_dx")
        part["w_a_out"][l] = _mm_tn(s["act_a"], d_ya, N_CHIPS, "mm_a_out_dw")
        d_b_in = _mm_nt(d_yb, fw["w_b_out"], BF16, "mm_b_out_dx")
        part["w_b_out"][l] = _mm_tn(s["b_in"], d_yb, N_CHIPS, "mm_b_out_dw")
        d_att = _mm_nt(d_yx, fw["w_x_out"], BF16, "mm_x_out_dx")
        part["w_x_out"][l] = _mm_tn(s["att"], d_yx, N_CHIPS, "mm_x_out_dw")
        dproj, small["ln_a_g"][l], small["ln_a_b"][l], small["conv_a_b"][l], small["conv_a_w"][l] = _branch_a_bwd(
            dproj, d_act_a, s["ca"], s["proj"], cw_a[l], w["ln_a_g"][l], w["ln_a_b"][l], "branch_a_bwd")
        dproj, small["conv_b_w"][l] = _branch_b_bwd(dproj, d_b_in, s["proj"], cw_b[l], "branch_b_bwd")
        dproj, d_kv = _attn_bwd(dproj, d_att, s["proj"], s["kv"], "attn_bwd")
        part["w_kv"][l] = _mm_tn(s["mem_n"], d_kv, 1, "mm_kv_dw").reshape(N_CHIPS, -1, 2 * c)
        d_mem_n = _mm_nt(d_kv, fw["w_kv"], F32, "mm_kv_dx")
        (small["g_mem"][l],) = _norm_bwd("norm_bwd_mem", pre=(d_mem_n, mem, w["g_mem"][l]), want_dx=False)
        d_h = _mm_nt(dproj, fw["w_in"], BF16, "mm_in_dx")
        part["w_in"][l] = _mm_tn(s["h"], dproj, N_CHIPS, "mm_in_dw")
        if exchange is not None:
            finish_exchange(l + 1, exchange, [d_h])
        sending = start_send(l)
        if l > 0:
            dxo, d_f, small["g_mix_pre"][l], small["g_mlp_post"][l - 1] = _norm_bwd(
                "norm_bwd_mid", dxo=dx1, pre=(d_h, s["x"], w["g_mix_pre"][l]),
                post=(saved[l - 1]["f"], w["g_mlp_post"][l - 1]))
        else:
            grad_x, small["g_mix_pre"][0] = _norm_bwd("norm_bwd_last", dxo=dx1,
                                                      pre=(d_h, s["x"], w["g_mix_pre"][0]))
    exchange = start_exchange(0, sending, [grad_x])

    grads, delta, new_m, new_v = {}, {}, {}, {}
    small_names = SMALL_REPLICATED + SMALL_SHARDED
    stacked = [jnp.stack(small[n]) for n in small_names]
    packed = _pack_rows(stacked, c)
    total = _sum_leading(_gather_all(packed, "gather_small_grads")[None], "sum_small_grads")[0]
    reduced = dict(zip(small_names, _unpack_rows(total, stacked, c)))
    chip = 2 * lax.axis_index("x") + lax.axis_index("y")
    cs = c // N_CHIPS
    for name in SMALL_SHARDED:
        grads[name] = lax.dynamic_slice_in_dim(reduced[name], chip * cs, cs, axis=2)
    for name in SMALL_REPLICATED:
        grads[name] = reduced[name].reshape(w[name].shape)
    for group, width in ((SMALL_REPLICATED, c), (SMALL_SHARDED, cs)):
        packs = [_pack_rows([src[n] for n in group], width)[None] for src in (w, grads, m, v)]
        outs = _adamw(*packs, "adamw_small_%d" % width)
        for dst, out in zip((delta, new_m, new_v), outs):
            dst.update(zip(group, _unpack_rows(out[0], [w[n] for n in group], width)))

    state = {name: () for name in BIG}
    if nl > 1:
        totals.update(zip(BIG, _join_halves([totals[name] for name in BIG], 1, nl, "rs_join_upper",
                                            after=[exchange[4]])))
        for name in BIG:
            state[name] = _adamw(w[name], totals[name], m[name], v[name], "adamw_" + name, batch=(1, nl))
    finish_exchange(0, exchange, [state[name][0] for name in BIG] if nl > 1 else [grad_x])
    grads.update(zip(BIG, _join_halves([totals[name] for name in BIG], 0, 1, "rs_join_first")))
    for name in BIG:
        delta[name], new_m[name], new_v[name] = _adamw(w[name], grads[name], m[name], v[name], "adamw_" + name,
                                                       batch=(0, 1), prev=state[name])

    loss = lax.psum(0.5 * jnp.sum(loss_cols) / d, ("x", "y", "c"))
    return loss, grad_x, grads, delta, new_m, new_v


def kernel(x, mem, g_mix_pre, w_in, conv_a_w, conv_a_b, ln_a_g, ln_a_b, w_a_out, conv_b_w, w_b_out, g_mem, w_kv, w_x_out, w_o, g_mix_post, g_mlp_pre, w_up, w_down, g_mlp_post, loss_target, m_g_mix_pre, m_w_in, m_conv_a_w, m_conv_a_b, m_ln_a_g, m_ln_a_b, m_w_a_out, m_conv_b_w, m_w_b_out, m_g_mem, m_w_kv, m_w_x_out, m_w_o, m_g_mix_post, m_g_mlp_pre, m_w_up, m_w_down, m_g_mlp_post, v_g_mix_pre, v_w_in, v_conv_a_w, v_conv_a_b, v_ln_a_g, v_ln_a_b, v_w_a_out, v_conv_b_w, v_w_b_out, v_g_mem, v_w_kv, v_w_x_out, v_w_o, v_g_mix_post, v_g_mlp_pre, v_w_up, v_w_down, v_g_mlp_post):
    w = dict(g_mix_pre=g_mix_pre, w_in=w_in, conv_a_w=conv_a_w, conv_a_b=conv_a_b, ln_a_g=ln_a_g, ln_a_b=ln_a_b,
             w_a_out=w_a_out, conv_b_w=conv_b_w, w_b_out=w_b_out, g_mem=g_mem, w_kv=w_kv, w_x_out=w_x_out, w_o=w_o,
             g_mix_post=g_mix_post, g_mlp_pre=g_mlp_pre, w_up=w_up, w_down=w_down, g_mlp_post=g_mlp_post)
    m = dict(g_mix_pre=m_g_mix_pre, w_in=m_w_in, conv_a_w=m_conv_a_w, conv_a_b=m_conv_a_b, ln_a_g=m_ln_a_g,
             ln_a_b=m_ln_a_b, w_a_out=m_w_a_out, conv_b_w=m_conv_b_w, w_b_out=m_w_b_out, g_mem=m_g_mem, w_kv=m_w_kv,
             w_x_out=m_w_x_out, w_o=m_w_o, g_mix_post=m_g_mix_post, g_mlp_pre=m_g_mlp_pre, w_up=m_w_up,
             w_down=m_w_down, g_mlp_post=m_g_mlp_post)
    v = dict(g_mix_pre=v_g_mix_pre, w_in=v_w_in, conv_a_w=v_conv_a_w, conv_a_b=v_conv_a_b, ln_a_g=v_ln_a_g,
             ln_a_b=v_ln_a_b, w_a_out=v_w_a_out, conv_b_w=v_conv_b_w, w_b_out=v_w_b_out, g_mem=v_g_mem, w_kv=v_w_kv,
             w_x_out=v_w_x_out, w_o=v_w_o, g_mix_post=v_g_mix_post, g_mlp_pre=v_g_mlp_pre, w_up=v_w_up,
             w_down=v_w_down, g_mlp_post=v_g_mlp_post)
    loss, grad_x, grads, delta, new_m, new_v = _step(w, m, v, x[0], mem[0], loss_target[0])
    out = [loss, grad_x[None]]
    for group in (grads, delta, new_m, new_v):
        out += [group[n] for n in WEIGHT_ORDER]
    return tuple(out)
```

```python
import functools

import jax
import jax.numpy as jnp
from jax import lax
from jax.experimental import pallas as pl
from jax.experimental.pallas import tpu as pltpu

F32 = jnp.float32
BF16 = jnp.bfloat16
MESH = pl.DeviceIdType.MESH

NORM_EPS = 1e-6
N_HEADS = 4
ADAM_LR = 0.001
ADAM_B1 = 0.9
ADAM_B2 = 0.999
ADAM_EPS = 1e-08
ADAM_WD = 0.01
ADAM_STEP = 10

N_CHIPS = 4
V7X_VMEM_BYTES = 64 * 1024 * 1024
VMEM_CAP = V7X_VMEM_BYTES - 8 * 1024 * 1024
LANE = 128
SUBLANES = 8
SUBLANE_BF16 = 16
HALO = 32
ROW_TILE = 256
CONV_ROWS = 32
CONV_LANES = 512
ATTN_TILE = 512
MM_TM = 1024
MM_TN = 1024
MM_TK = 2048
MM_TC = 3072
MM_WHOLE_TILE = 512
MM_WHOLE_VMEM = 34 * 1024 * 1024
EW_VMEM_BYTES = 24 * 1024 * 1024


def _tile(n, pref, align):
    if n <= pref:
        return n
    t = (pref // align) * align
    while t >= align:
        if n % t == 0:
            return t
        t -= align
    return n


def _ew_rows(r, c, n_arrays):
    return _tile(r, max(SUBLANE_BF16, EW_VMEM_BYTES // (2 * n_arrays * c * 4)), SUBLANE_BF16)


def _nbytes(shape, dtype):
    n = 1
    for s in shape:
        if s is not None:
            n *= s
    return n * jnp.dtype(dtype).itemsize


def _params(semantics, block_bytes, temp_bytes=0):
    need = 2 * block_bytes + temp_bytes + (4 << 20)
    return pltpu.CompilerParams(dimension_semantics=semantics,
                                vmem_limit_bytes=int(min(max(need, 16 << 20), VMEM_CAP)))


def _sigmoid(v):
    return 1.0 / (1.0 + jnp.exp(-v))


def _pcall(body, **kwargs):
    call = pl.pallas_call(body, **kwargs)
    return lambda *operands: call(*[pltpu.with_memory_space_constraint(o, pltpu.HBM) for o in operands])


def _whole_contraction_fits(k):
    return 4 * k * MM_WHOLE_TILE * 2 <= MM_WHOLE_VMEM


def _mm_nn(a, b3, out_dtype, name, a_act=None, after=None):
    m, k = a.shape
    s, k2, ns = b3.shape
    assert k == k2
    tm = _tile(m, MM_TM, SUBLANE_BF16)
    tn = _tile(ns, MM_TN, LANE)
    tk = _tile(k, MM_TK, LANE)
    b_resident = k > tk and _whole_contraction_fits(k)
    if b_resident:
        tm, tn, tk = _tile(m, MM_WHOLE_TILE, SUBLANE_BF16), _tile(ns, MM_WHOLE_TILE, LANE), k
    q = ns // tn
    nk = k // tk

    def ij(g0, g1):
        return (g1, g0) if b_resident else (g0, g1)

    n_in = 2 if after is None else 3

    def body(*refs):
        a_ref, b_ref, o_ref = refs[0], refs[1], refs[n_in]
        scratch = refs[n_in + 1:]
        av = a_ref[...]
        if a_act == "relu2":
            r = jnp.maximum(av.astype(BF16), 0.0)
            av = r * r
        p = jnp.dot(av.astype(BF16), b_ref[...].astype(BF16), preferred_element_type=F32)
        if nk == 1:
            o_ref[...] = p.astype(o_ref.dtype)
        else:
            acc, = scratch
            kk = pl.program_id(2)

            @pl.when(kk == 0)
            def _():
                acc[...] = p

            @pl.when(kk > 0)
            def _():
                acc[...] += p

            @pl.when(kk == nk - 1)
            def _():
                o_ref[...] = acc[...].astype(o_ref.dtype)

    blocks = (_nbytes((tm, tk), a.dtype) + _nbytes((tk, tn), b3.dtype) + _nbytes((tm, tn), out_dtype))
    return _pcall(
        body, name=name,
        out_shape=jax.ShapeDtypeStruct((m, s * ns), out_dtype),
        grid=(*ij(m // tm, s * q), nk),
        in_specs=[pl.BlockSpec((tm, tk), lambda g0, g1, c: (ij(g0, g1)[0], c)),
                  pl.BlockSpec((None, tk, tn), lambda g0, g1, c: (ij(g0, g1)[1] // q, c, ij(g0, g1)[1] % q))]
        + ([] if after is None else [pl.BlockSpec(memory_space=pl.ANY)]),
        out_specs=pl.BlockSpec((tm, tn), lambda g0, g1, c: ij(g0, g1)),
        scratch_shapes=[pltpu.VMEM((tm, tn), F32)] if nk > 1 else [],
        compiler_params=_params(("parallel", "parallel", "arbitrary"), blocks,
                                3 * _nbytes((tm, tn), F32) + _nbytes((tm, tk), F32)),
    )(*([a, b3] if after is None else [a, b3, after]))


def _mm_nt(a, b3, out_dtype, name, relu2_of=None, after=None):
    m, n = a.shape
    s, kd, ns = b3.shape
    assert n == s * ns
    tm = _tile(m, MM_TM, SUBLANE_BF16)
    tj = _tile(kd, MM_TN, LANE)
    tc = _tile(ns, MM_TC, LANE)
    q = ns // tc
    nc = s * q

    n_in = 2 + (relu2_of is not None) + (after is not None)

    def body(*refs):
        a_ref, b_ref, u_ref, o_ref = refs[0], refs[1], refs[2], refs[n_in]
        scratch = refs[n_in + 1:]

        def finish(p):
            if relu2_of is not None:
                p = p * (2.0 * jnp.maximum(u_ref[...].astype(F32), 0.0))
            o_ref[...] = p.astype(o_ref.dtype)

        p = lax.dot_general(a_ref[...].astype(BF16), b_ref[...].astype(BF16),
                            (((1,), (1,)), ((), ())), preferred_element_type=F32)
        if nc == 1:
            finish(p)
        else:
            acc, = scratch
            cc = pl.program_id(2)

            @pl.when(cc == 0)
            def _():
                acc[...] = p

            @pl.when(cc > 0)
            def _():
                acc[...] += p

            @pl.when(cc == nc - 1)
            def _():
                finish(acc[...])

    in_specs = [pl.BlockSpec((tm, tc), lambda i, j, c: (i, c)),
                pl.BlockSpec((None, tj, tc), lambda i, j, c: (c // q, j, c % q))]
    operands = [a, b3]
    blocks = _nbytes((tm, tc), a.dtype) + _nbytes((tj, tc), b3.dtype) + _nbytes((tm, tj), out_dtype)
    if relu2_of is not None:
        in_specs.append(pl.BlockSpec((tm, tj), lambda i, j, c: (i, j)))
        operands.append(relu2_of)
        blocks += _nbytes((tm, tj), relu2_of.dtype)
    if after is not None:
        in_specs.append(pl.BlockSpec(memory_space=pl.ANY))
        operands.append(after)
    return _pcall(
        body, name=name,
        out_shape=jax.ShapeDtypeStruct((m, kd), out_dtype),
        grid=(m // tm, kd // tj, nc),
        in_specs=in_specs,
        out_specs=pl.BlockSpec((tm, tj), lambda i, j, c: (i, j)),
        scratch_shapes=[pltpu.VMEM((tm, tj), F32)] if nc > 1 else [],
        compiler_params=_params(("parallel", "parallel", "arbitrary"), blocks,
                                3 * _nbytes((tm, tj), F32)),
    )(*operands)


def _mm_tn(a, g, out_shards, name, a_act=None):
    m, ka = a.shape
    m2, n = g.shape
    assert m == m2
    ns = n // out_shards
    ta = _tile(ka, MM_TM, LANE)
    tn = _tile(ns, MM_TN, LANE)
    tm = _tile(m, MM_TK, SUBLANE_BF16)
    if m > tm and _whole_contraction_fits(m):
        ta, tn, tm = _tile(ka, MM_WHOLE_TILE, LANE), _tile(ns, MM_WHOLE_TILE, LANE), m
    q = ns // tn
    nm = m // tm

    def body(a_ref, g_ref, o_ref, *scratch):
        av = a_ref[...]
        if a_act == "relu2":
            r = jnp.maximum(av.astype(BF16), 0.0)
            av = r * r
        p = lax.dot_general(av.astype(BF16), g_ref[...].astype(BF16),
                            (((0,), (0,)), ((), ())), preferred_element_type=F32)
        if nm == 1:
            o_ref[...] = p.astype(o_ref.dtype)
        else:
            acc, = scratch
            cc = pl.program_id(2)

            @pl.when(cc == 0)
            def _():
                acc[...] = p

            @pl.when(cc > 0)
            def _():
                acc[...] += p

            @pl.when(cc == nm - 1)
            def _():
                o_ref[...] = acc[...].astype(o_ref.dtype)

    blocks = _nbytes((tm, ta), a.dtype) + _nbytes((tm, tn), g.dtype) + _nbytes((ta, tn), BF16)
    return _pcall(
        body, name=name,
        out_shape=jax.ShapeDtypeStruct((out_shards, ka, ns), BF16),
        grid=(ka // ta, out_shards * q, nm),
        in_specs=[pl.BlockSpec((tm, ta), lambda i, j, c: (c, i)),
                  pl.BlockSpec((tm, tn), lambda i, j, c: (c, j))],
        out_specs=pl.BlockSpec((None, ta, tn), lambda i, j, c: (j // q, i, j % q)),
        scratch_shapes=[pltpu.VMEM((ta, tn), F32)] if nm > 1 else [],
        compiler_params=_params(("parallel", "parallel", "arbitrary"), blocks,
                                3 * _nbytes((ta, tn), F32) + _nbytes((tm, ta), F32)),
    )(a, g)


def _rms_scale(v):
    return lax.rsqrt(jnp.mean(v * v, axis=-1, keepdims=True) + NORM_EPS)


def _norm_fwd(x, name, *, z=None, g_post=None, g_next=None, target=None):
    t, d = x.shape
    tr = _tile(t, ROW_TILE, SUBLANE_BF16)
    has_res, has_next, has_loss = z is not None, g_next is not None, target is not None

    def body(*refs):
        it = iter(refs)
        x_ref = next(it)
        z_ref, gp_ref = (next(it), next(it)) if has_res else (None, None)
        gn_ref = next(it) if has_next else None
        t_ref = next(it) if has_loss else None
        xv = x_ref[...]
        if has_res:
            zv = z_ref[...].astype(F32)
            xv = xv + zv * _rms_scale(zv) * gp_ref[...]
            if not has_loss:
                next(it)[...] = xv
        if has_next:
            next(it)[...] = (xv * _rms_scale(xv) * gn_ref[...]).astype(BF16)
        if has_loss:
            e = xv - t_ref[...]
            next(it)[...] = e * (1.0 / d)
            ls_ref = next(it)

            @pl.when(pl.program_id(0) == 0)
            def _():
                ls_ref[...] = jnp.zeros_like(ls_ref)

            ls_ref[...] += jnp.sum(e * e, axis=0, keepdims=True)

    row = pl.BlockSpec((tr, d), lambda i: (i, 0))
    vec = pl.BlockSpec((1, d), lambda i: (0, 0))
    operands, in_specs, out_shape, out_specs = [x], [row], [], []
    if has_res:
        operands += [z, g_post.reshape(1, d)]
        in_specs += [row, vec]
        if not has_loss:
            out_shape.append(jax.ShapeDtypeStruct((t, d), F32))
            out_specs.append(row)
    if has_next:
        operands.append(g_next.reshape(1, d))
        in_specs.append(vec)
        out_shape.append(jax.ShapeDtypeStruct((t, d), BF16))
        out_specs.append(row)
    if has_loss:
        operands.append(target)
        in_specs.append(row)
        out_shape += [jax.ShapeDtypeStruct((t, d), F32), jax.ShapeDtypeStruct((1, d), F32)]
        out_specs += [row, vec]
    return _pcall(
        body, name=name, out_shape=out_shape, grid=(t // tr,),
        in_specs=in_specs, out_specs=out_specs,
        compiler_params=_params(("arbitrary",), 5 * _nbytes((tr, d), F32), 4 * _nbytes((tr, d), F32)),
    )(*operands)


def _norm_bwd(name, *, dxo=None, pre=None, post=None, want_dx=True):
    ref_arr = dxo if dxo is not None else pre[1]
    t, d = ref_arr.shape
    tr = _tile(t, ROW_TILE, SUBLANE_BF16)
    has_dxo, has_pre, has_post = dxo is not None, pre is not None, post is not None

    def body(*refs):
        it = iter(refs)
        dxo_ref = next(it) if has_dxo else None
        dh_ref, xin_ref, gpre_ref = (next(it), next(it), next(it)) if has_pre else (None,) * 3
        z_ref, gpost_ref = (next(it), next(it)) if has_post else (None, None)
        dx_ref = next(it) if (has_pre and want_dx) else None
        dz_ref = next(it) if has_post else None
        dgpre_ref = next(it) if has_pre else None
        dgpost_ref = next(it) if has_post else None
        first = pl.program_id(0) == 0

        dx = dxo_ref[...] if has_dxo else None
        if has_pre:
            xin = xin_ref[...]
            dh = dh_ref[...].astype(F32)
            r = _rms_scale(xin)
            gy = dh * gpre_ref[...]
            dloc = r * gy - xin * (r * r * r) * jnp.mean(gy * xin, axis=-1, keepdims=True)
            dx = dloc if dx is None else dx + dloc
            if want_dx:
                dx_ref[...] = dx

            @pl.when(first)
            def _():
                dgpre_ref[...] = jnp.zeros_like(dgpre_ref)

            dgpre_ref[...] += jnp.sum(dh * xin * r, axis=0, keepdims=True)
        if has_post:
            zv = z_ref[...].astype(F32)
            r = _rms_scale(zv)
            gy = dx * gpost_ref[...]
            dz = r * gy - zv * (r * r * r) * jnp.mean(gy * zv, axis=-1, keepdims=True)
            dz_ref[...] = dz.astype(BF16)

            @pl.when(first)
            def _():
                dgpost_ref[...] = jnp.zeros_like(dgpost_ref)

            dgpost_ref[...] += jnp.sum(dx * zv * r, axis=0, keepdims=True)

    row = pl.BlockSpec((tr, d), lambda i: (i, 0))
    vec = pl.BlockSpec((1, d), lambda i: (0, 0))
    operands, in_specs, out_shape, out_specs = [], [], [], []
    if has_dxo:
        operands.append(dxo)
        in_specs.append(row)
    if has_pre:
        operands += [pre[0], pre[1], pre[2].reshape(1, d)]
        in_specs += [row, row, vec]
    if has_post:
        operands += [post[0], post[1].reshape(1, d)]
        in_specs += [row, vec]
    if has_pre and want_dx:
        out_shape.append(jax.ShapeDtypeStruct((t, d), F32))
        out_specs.append(row)
    if has_post:
        out_shape.append(jax.ShapeDtypeStruct((t, d), BF16))
        out_specs.append(row)
    if has_pre:
        out_shape.append(jax.ShapeDtypeStruct((1, d), F32))
        out_specs.append(vec)
    if has_post:
        out_shape.append(jax.ShapeDtypeStruct((1, d), F32))
        out_specs.append(vec)
    return _pcall(
        body, name=name, out_shape=out_shape, grid=(t // tr,),
        in_specs=in_specs, out_specs=out_specs,
        compiler_params=_params(("arbitrary",), 6 * _nbytes((tr, d), F32), 6 * _nbytes((tr, d), F32)),
    )(*operands)


def _seq_tiles(t):
    tr = _tile(t, ROW_TILE, HALO)
    assert tr % HALO == 0 and t % tr == 0
    return tr, t // tr, tr // HALO


def _col(tr, width, cb):
    return pl.BlockSpec((tr, width), lambda i: (i, cb))


def _prev_halo(per, width, cb):
    return pl.BlockSpec((HALO, width), lambda i: (jnp.maximum(i * per - 1, 0), cb))


def _next_halo(per, n_halo, width, cb):
    return pl.BlockSpec((HALO, width), lambda i: (jnp.minimum((i + 1) * per, n_halo - 1), cb))


def _const(shape):
    return pl.BlockSpec(shape, lambda i: (0,) * len(shape))


def _glu(val, gate):
    return val.astype(F32) * _sigmoid(gate.astype(F32))


def _conv_chunks(tr, c):
    lanes = min(CONV_LANES, c)
    return [(r0, pl.ds(c0, lanes)) for r0 in range(0, tr, CONV_ROWS) for c0 in range(0, c, lanes)]


def _shift_copies(buf, shifted):
    rows = shifted.shape[1]
    buf[pl.ds(rows, SUBLANES), :] = jnp.zeros((SUBLANES, buf.shape[1]), F32)
    for s in range(SUBLANES):
        shifted[s] = buf[pl.ds(s, rows), :]


def _shifted_rows(shifted, offset, r0, cols):
    return shifted[offset % SUBLANES, pl.ds(offset - offset % SUBLANES + r0, CONV_ROWS), cols]


def _tap_sum(w_ref, shifted, r0, cols, first, step, kw):
    acc = jnp.zeros((CONV_ROWS, cols.size), F32)
    for k in range(kw):
        acc = acc + w_ref[pl.ds(k, 1), cols] * _shifted_rows(shifted, first + step * k, r0, cols)
    return acc


def _layer_norm_parts(ca):
    mu = jnp.mean(ca, axis=-1, keepdims=True)
    xc = ca - mu
    rs = lax.rsqrt(jnp.mean(xc * xc, axis=-1, keepdims=True) + NORM_EPS)
    return xc * rs, rs


def _branch_a_fwd(proj, cw, cb, lg, lb, name):
    t = proj.shape[0]
    kw, c = cw.shape
    tr, nt, per = _seq_tiles(t)

    def body(av_ref, ag_ref, hv_ref, hg_ref, cw_ref, cb_ref, lg_ref, lb_ref, ca_ref, act_ref, abuf, ash):
        i = pl.program_id(0)
        abuf[pl.ds(0, HALO), :] = jnp.where(i > 0, _glu(hv_ref[...], hg_ref[...]), 0.0)
        abuf[pl.ds(HALO, tr), :] = _glu(av_ref[...], ag_ref[...])
        _shift_copies(abuf, ash)
        for r0, cols in _conv_chunks(tr, c):
            ca_ref[pl.ds(r0, CONV_ROWS), cols] = (
                _tap_sum(cw_ref, ash, r0, cols, HALO - (kw - 1), 1, kw) + cb_ref[:, cols])
        xh, _ = _layer_norm_parts(ca_ref[...])
        ln = xh * lg_ref[...] + lb_ref[...]
        act_ref[...] = (ln * _sigmoid(ln)).astype(BF16)

    return _pcall(
        body, name=name,
        out_shape=[jax.ShapeDtypeStruct((t, c), F32), jax.ShapeDtypeStruct((t, c), BF16)],
        grid=(nt,),
        in_specs=[_col(tr, c, 0), _col(tr, c, 1), _prev_halo(per, c, 0), _prev_halo(per, c, 1),
                  _const((kw, c)), _const((1, c)), _const((1, c)), _const((1, c))],
        out_specs=[_col(tr, c, 0), _col(tr, c, 0)],
        scratch_shapes=[pltpu.VMEM((HALO + tr + SUBLANES, c), F32), pltpu.VMEM((SUBLANES, HALO + tr, c), F32)],
        compiler_params=_params(("arbitrary",), 4 * _nbytes((tr, c), F32),
                                (8 + SUBLANES) * _nbytes((tr + HALO, c), F32)),
    )(proj, proj, proj, proj, cw, cb.reshape(1, c), lg.reshape(1, c), lb.reshape(1, c))


def _branch_a_bwd(dproj, dact, ca, proj, cw, lg, lb, name):
    t = proj.shape[0]
    kw, c = cw.shape
    tr, nt, per = _seq_tiles(t)
    n_halo = t // HALO

    def body(dproj_in, da_ref, dah_ref, ca_ref, cah_ref, av_ref, ag_ref, hv_ref, hg_ref,
             cw_ref, lg_ref, lb_ref, out_ref, dlg_ref, dlb_ref, dcb_ref, dcw_ref, abuf, dbuf, sgbuf, ash, dsh):
        del dproj_in
        i = pl.program_id(0)
        lgv, lbv = lg_ref[...], lb_ref[...]

        def conv_grad(dact_v, ca_v):
            xh, rs = _layer_norm_parts(ca_v)
            ln = xh * lgv + lbv
            sg = _sigmoid(ln)
            dln = dact_v.astype(F32) * (sg * (1.0 + ln * (1.0 - sg)))
            dxh = dln * lgv
            dca = rs * (dxh - jnp.mean(dxh, axis=-1, keepdims=True)
                        - xh * jnp.mean(dxh * xh, axis=-1, keepdims=True))
            return dca, dln, xh

        dca, dln, xh = conv_grad(da_ref[...], ca_ref[...])
        dca_h, _, _ = conv_grad(dah_ref[...], cah_ref[...])
        dbuf[pl.ds(0, tr), :] = dca
        dbuf[pl.ds(tr, HALO), :] = jnp.where(i < nt - 1, dca_h, 0.0)

        @pl.when(i == 0)
        def _():
            dlg_ref[...] = jnp.zeros_like(dlg_ref)
            dlb_ref[...] = jnp.zeros_like(dlb_ref)
            dcb_ref[...] = jnp.zeros_like(dcb_ref)
            dcw_ref[...] = jnp.zeros_like(dcw_ref)

        dlg_ref[...] += jnp.sum(dln * xh, axis=0, keepdims=True)
        dlb_ref[...] += jnp.sum(dln, axis=0, keepdims=True)
        dcb_ref[...] += jnp.sum(dca, axis=0, keepdims=True)

        sg = _sigmoid(ag_ref[...].astype(F32))
        sgbuf[...] = sg
        abuf[pl.ds(0, HALO), :] = jnp.where(i > 0, _glu(hv_ref[...], hg_ref[...]), 0.0)
        abuf[pl.ds(HALO, tr), :] = av_ref[...].astype(F32) * sg
        _shift_copies(abuf, ash)
        _shift_copies(dbuf, dsh)

        for r0, cols in _conv_chunks(tr, c):
            rows = pl.ds(r0, CONV_ROWS)
            d_a = _tap_sum(cw_ref, dsh, r0, cols, kw - 1, -1, kw)
            sgc = sgbuf[rows, cols]
            out_ref[rows, cols] = (d_a * sgc).astype(BF16)
            out_ref[rows, pl.ds(c + cols.start, cols.size)] = (
                d_a * abuf[pl.ds(HALO + r0, CONV_ROWS), cols] * (1.0 - sgc)).astype(BF16)
        for _, cols in _conv_chunks(CONV_ROWS, c):
            for k in range(kw):
                acc = jnp.zeros((CONV_ROWS, cols.size), F32)
                for r0 in range(0, tr, CONV_ROWS):
                    acc = acc + (dbuf[pl.ds(r0, CONV_ROWS), cols]
                                 * _shifted_rows(ash, HALO - (kw - 1) + k, r0, cols))
                dcw_ref[pl.ds(k, 1), cols] += jnp.sum(acc, axis=0, keepdims=True)

    vec = _const((1, c))
    return _pcall(
        body, name=name,
        out_shape=[jax.ShapeDtypeStruct(dproj.shape, BF16)] + [jax.ShapeDtypeStruct((1, c), F32)] * 3
        + [jax.ShapeDtypeStruct((kw, c), F32)],
        grid=(nt,),
        in_specs=[pl.BlockSpec(memory_space=pl.ANY),
                  _col(tr, c, 0), _next_halo(per, n_halo, c, 0),
                  _col(tr, c, 0), _next_halo(per, n_halo, c, 0),
                  _col(tr, c, 0), _col(tr, c, 1), _prev_halo(per, c, 0), _prev_halo(per, c, 1),
                  _const((kw, c)), vec, vec],
        out_specs=[pl.BlockSpec((tr, 2 * c), lambda i: (i, 0)), vec, vec, vec, _const((kw, c))],
        scratch_shapes=[pltpu.VMEM((HALO + tr + SUBLANES, c), F32), pltpu.VMEM((HALO + tr + SUBLANES, c), F32),
                        pltpu.VMEM((tr, c), F32),
                        pltpu.VMEM((SUBLANES, HALO + tr, c), F32), pltpu.VMEM((SUBLANES, HALO + tr, c), F32)],
        input_output_aliases={0: 0},
        compiler_params=_params(("arbitrary",), 6 * _nbytes((tr, c), F32),
                                (12 + 2 * SUBLANES) * _nbytes((tr + HALO, c), F32)),
    )(dproj, dact, dact, ca, ca, proj, proj, proj, proj, cw, lg.reshape(1, c), lb.reshape(1, c))


def _branch_b_fwd(proj, cw, name):
    t = proj.shape[0]
    kw, c = cw.shape
    tr, nt, per = _seq_tiles(t)

    def body(sb_ref, sc_ref, sx_ref, hc_ref, hx_ref, cw_ref, o_ref, pbuf):
        i = pl.program_id(0)
        hp = hc_ref[...].astype(F32) * hx_ref[...].astype(F32)
        pbuf[pl.ds(0, HALO), :] = jnp.where(i > 0, hp, 0.0)
        pbuf[pl.ds(HALO, tr), :] = sc_ref[...].astype(F32) * sx_ref[...].astype(F32)
        u = jnp.zeros((tr, c), F32)
        for k in range(kw):
            u = u + cw_ref[pl.ds(k, 1), :] * pbuf[pl.ds(HALO - (kw - 1) + k, tr), :]
        o_ref[...] = (sb_ref[...].astype(F32) * u).astype(BF16)

    return _pcall(
        body, name=name,
        out_shape=jax.ShapeDtypeStruct((t, c), BF16),
        grid=(nt,),
        in_specs=[_col(tr, c, 2), _col(tr, c, 3), _col(tr, c, 4),
                  _prev_halo(per, c, 3), _prev_halo(per, c, 4), _const((kw, c))],
        out_specs=_col(tr, c, 0),
        scratch_shapes=[pltpu.VMEM((HALO + tr, c), F32)],
        compiler_params=_params(("arbitrary",), 4 * _nbytes((tr, c), F32), 6 * _nbytes((tr + HALO, c), F32)),
    )(proj, proj, proj, proj, proj, cw)


def _branch_b_bwd(dproj, dbin, proj, cw, name):
    t = proj.shape[0]
    kw, c = cw.shape
    tr, nt, per = _seq_tiles(t)
    n_halo = t // HALO

    def body(dproj_in, db_ref, dbh_ref, sb_ref, sbh_ref, sc_ref, sx_ref, hc_ref, hx_ref, cw_ref,
             out_ref, dcw_ref, pbuf, dubuf, res):
        del dproj_in
        i = pl.program_id(0)
        j = pl.program_id(1)

        @pl.when(j == 0)
        def _():
            sb = sb_ref[...].astype(F32)
            sc = sc_ref[...].astype(F32)
            sx = sx_ref[...].astype(F32)
            dbin_v = db_ref[...].astype(F32)
            hp = hc_ref[...].astype(F32) * hx_ref[...].astype(F32)
            pbuf[pl.ds(0, HALO), :] = jnp.where(i > 0, hp, 0.0)
            pbuf[pl.ds(HALO, tr), :] = sc * sx
            du = dbin_v * sb
            du_h = dbh_ref[...].astype(F32) * sbh_ref[...].astype(F32)
            dubuf[pl.ds(0, tr), :] = du
            dubuf[pl.ds(tr, HALO), :] = jnp.where(i < nt - 1, du_h, 0.0)

            @pl.when(i == 0)
            def _():
                dcw_ref[...] = jnp.zeros_like(dcw_ref)

            u = jnp.zeros((tr, c), F32)
            dp = jnp.zeros((tr, c), F32)
            for k in range(kw):
                shifted = pbuf[pl.ds(HALO - (kw - 1) + k, tr), :]
                u = u + cw_ref[pl.ds(k, 1), :] * shifted
                dp = dp + cw_ref[pl.ds(k, 1), :] * dubuf[pl.ds(kw - 1 - k, tr), :]
                dcw_ref[pl.ds(k, 1), :] += jnp.sum(du * shifted, axis=0, keepdims=True)
            res[0] = (dbin_v * u).astype(BF16)
            res[1] = (dp * sx).astype(BF16)
            res[2] = (dp * sc).astype(BF16)

        out_ref[...] = res[j]

    def colj(cb):
        return pl.BlockSpec((tr, c), lambda i, j: (i, cb))

    def prevj(cb):
        return pl.BlockSpec((HALO, c), lambda i, j: (jnp.maximum(i * per - 1, 0), cb))

    def nextj(cb):
        return pl.BlockSpec((HALO, c), lambda i, j: (jnp.minimum((i + 1) * per, n_halo - 1), cb))

    return _pcall(
        body, name=name,
        out_shape=[jax.ShapeDtypeStruct(dproj.shape, BF16), jax.ShapeDtypeStruct((kw, c), F32)],
        grid=(nt, 3),
        in_specs=[pl.BlockSpec(memory_space=pl.ANY),
                  colj(0), nextj(0), colj(2), nextj(2), colj(3), colj(4), prevj(3), prevj(4),
                  pl.BlockSpec((kw, c), lambda i, j: (0, 0))],
        out_specs=[pl.BlockSpec((tr, c), lambda i, j: (i, 2 + j)),
                   pl.BlockSpec((kw, c), lambda i, j: (0, 0))],
        scratch_shapes=[pltpu.VMEM((HALO + tr, c), F32), pltpu.VMEM((HALO + tr, c), F32),
                        pltpu.VMEM((3, tr, c), BF16)],
        input_output_aliases={0: 0},
        compiler_params=_params(("arbitrary", "arbitrary"), 6 * _nbytes((tr, c), F32),
                                10 * _nbytes((tr + HALO, c), F32)),
    )(dproj, dbin, dbin, proj, proj, proj, proj, proj, proj, cw)


def _softmax_rows(s):
    e = jnp.exp(s - jnp.max(s, axis=-1, keepdims=True))
    return e / jnp.sum(e, axis=-1, keepdims=True)


def _attn_fwd(proj, kv, name):
    t = proj.shape[0]
    m, c2 = kv.shape
    c = c2 // 2
    hd = c // N_HEADS
    ta = _tile(t, ATTN_TILE, SUBLANE_BF16)
    scale = hd ** -0.5

    def body(q_ref, kv_ref, o_ref):
        for h in range(N_HEADS):
            qh = q_ref[:, pl.ds(h * hd, hd)]
            kh = kv_ref[:, pl.ds(h * hd, hd)]
            vh = kv_ref[:, pl.ds(c + h * hd, hd)]
            s = lax.dot_general(qh, kh, (((1,), (1,)), ((), ())), preferred_element_type=F32) * scale
            p = _softmax_rows(s)
            o_ref[:, pl.ds(h * hd, hd)] = jnp.dot(p.astype(BF16), vh,
                                                  preferred_element_type=F32).astype(BF16)

    return _pcall(
        body, name=name,
        out_shape=jax.ShapeDtypeStruct((t, c), BF16),
        grid=(t // ta,),
        in_specs=[pl.BlockSpec((ta, c), lambda i: (i, 5)), _const((m, c2))],
        out_specs=pl.BlockSpec((ta, c), lambda i: (i, 0)),
        compiler_params=_params(("parallel",), 2 * _nbytes((ta, c), BF16) + _nbytes((m, c2), BF16),
                                8 * _nbytes((ta, m), F32)),
    )(proj, kv)


def _attn_bwd(dproj, d_o, proj, kv, name):
    t = proj.shape[0]
    m, c2 = kv.shape
    c = c2 // 2
    hd = c // N_HEADS
    ta = _tile(t, ATTN_TILE, SUBLANE_BF16)
    scale = hd ** -0.5

    def body(dproj_in, do_ref, q_ref, kv_ref, dq_ref, dkv_ref):
        del dproj_in

        @pl.when(pl.program_id(0) == 0)
        def _():
            dkv_ref[...] = jnp.zeros_like(dkv_ref)

        for h in range(N_HEADS):
            qh = q_ref[:, pl.ds(h * hd, hd)]
            kh = kv_ref[:, pl.ds(h * hd, hd)]
            vh = kv_ref[:, pl.ds(c + h * hd, hd)]
            doh = do_ref[:, pl.ds(h * hd, hd)]
            s = lax.dot_general(qh, kh, (((1,), (1,)), ((), ())), preferred_element_type=F32) * scale
            p = _softmax_rows(s)
            dp = lax.dot_general(doh, vh, (((1,), (1,)), ((), ())), preferred_element_type=F32)
            ds = (p * (dp - jnp.sum(dp * p, axis=-1, keepdims=True))).astype(BF16)
            dq_ref[:, pl.ds(h * hd, hd)] = (jnp.dot(ds, kh, preferred_element_type=F32) * scale).astype(BF16)
            dkv_ref[:, pl.ds(h * hd, hd)] += lax.dot_general(
                ds, qh, (((0,), (0,)), ((), ())), preferred_element_type=F32) * scale
            dkv_ref[:, pl.ds(c + h * hd, hd)] += lax.dot_general(
                p.astype(BF16), doh, (((0,), (0,)), ((), ())), preferred_element_type=F32)

    return _pcall(
        body, name=name,
        out_shape=[jax.ShapeDtypeStruct(dproj.shape, BF16), jax.ShapeDtypeStruct((m, c2), F32)],
        grid=(t // ta,),
        in_specs=[pl.BlockSpec(memory_space=pl.ANY),
                  pl.BlockSpec((ta, c), lambda i: (i, 0)), pl.BlockSpec((ta, c), lambda i: (i, 5)),
                  _const((m, c2))],
        out_specs=[pl.BlockSpec((ta, c), lambda i: (i, 5)), _const((m, c2))],
        input_output_aliases={0: 0},
        compiler_params=_params(("arbitrary",), 3 * _nbytes((ta, c), BF16) + 2 * _nbytes((m, c2), F32),
                                10 * _nbytes((ta, m), F32)),
    )(dproj, d_o, proj, kv)


def _merge_fwd(proj, ya, yb, yx, name):
    t, d = ya.shape
    tr = _tile(t, ROW_TILE, SUBLANE_BF16)

    def body(g_ref, ya_ref, yb_ref, yx_ref, o_ref):
        acc = jnp.zeros((tr, d), F32)
        for b, y_ref in enumerate((ya_ref, yb_ref, yx_ref)):
            acc = acc + _sigmoid(g_ref[:, pl.ds(b * d, d)].astype(F32)) * y_ref[...].astype(F32)
        o_ref[...] = acc.astype(BF16)

    row = pl.BlockSpec((tr, d), lambda i: (i, 0))
    return _pcall(
        body, name=name,
        out_shape=jax.ShapeDtypeStruct((t, d), BF16),
        grid=(t // tr,),
        in_specs=[pl.BlockSpec((tr, 3 * d), lambda i: (i, 1)), row, row, row],
        out_specs=row,
        compiler_params=_params(("parallel",), 7 * _nbytes((tr, d), BF16), 6 * _nbytes((tr, d), F32)),
    )(proj, ya, yb, yx)


def _merge_bwd(dmerged, proj, ya, yb, yx, name):
    t, d = ya.shape
    tr = _tile(t, ROW_TILE, SUBLANE_BF16)

    def body(dm_ref, g_ref, ya_ref, yb_ref, yx_ref, dg_ref, dya_ref, dyb_ref, dyx_ref):
        dm = dm_ref[...].astype(F32)
        for b, (y_ref, dy_ref) in enumerate(((ya_ref, dya_ref), (yb_ref, dyb_ref), (yx_ref, dyx_ref))):
            sg = _sigmoid(g_ref[:, pl.ds(b * d, d)].astype(F32))
            dg_ref[:, pl.ds(b * d, d)] = (dm * y_ref[...].astype(F32) * sg * (1.0 - sg)).astype(BF16)
            dy_ref[...] = (dm * sg).astype(BF16)

    row = pl.BlockSpec((tr, d), lambda i: (i, 0))
    gates = pl.BlockSpec((tr, 3 * d), lambda i: (i, 1))
    return _pcall(
        body, name=name,
        out_shape=[jax.ShapeDtypeStruct(proj.shape, BF16)] + [jax.ShapeDtypeStruct((t, d), BF16)] * 3,
        grid=(t // tr,),
        in_specs=[row, gates, row, row, row],
        out_specs=[gates, row, row, row],
        compiler_params=_params(("parallel",), 14 * _nbytes((tr, d), BF16), 8 * _nbytes((tr, d), F32)),
    )(dmerged, proj, ya, yb, yx)


def _as3(a):
    return a.reshape((-1,) + a.shape[-2:])


def _ew_call(fn, ins, n_out, out_dtypes, name, batch=None, prev=()):
    b, r, c = ins[0].shape
    lo, hi = batch if batch is not None else (0, b)
    tr = _ew_rows(r, c, len(ins) + n_out)
    n_in = len(ins) + len(prev)

    def body(*refs):
        outs = fn(*[ref[...] for ref in refs[:len(ins)]])
        for ref, val in zip(refs[n_in:], outs):
            ref[...] = val.astype(ref.dtype)

    spec = pl.BlockSpec((None, tr, c), lambda i, j: (i + lo, j, 0))
    return _pcall(
        body, name=name,
        out_shape=[jax.ShapeDtypeStruct((b, r, c), dt) for dt in out_dtypes],
        grid=(hi - lo, r // tr),
        in_specs=[spec] * len(ins) + [pl.BlockSpec(memory_space=pl.ANY)] * len(prev), out_specs=[spec] * n_out,
        input_output_aliases={len(ins) + k: k for k in range(len(prev))},
        compiler_params=_params(("parallel", "parallel"), (len(ins) + n_out) * _nbytes((tr, c), F32),
                                6 * _nbytes((tr, c), F32)),
    )(*ins, *prev)


def _my_chip():
    return 2 * lax.axis_index("x") + lax.axis_index("y")


def _my_core():
    return lax.axis_index("c")


def _cast_to_slot(w, l, name, after=()):
    _, r, cs = w.shape
    tr = _ew_rows(r, cs, 2)

    def body(w_ref, *rest):
        rest[-1][...] = w_ref[...].astype(BF16)

    return _pcall(
        body, name=name,
        out_shape=jax.ShapeDtypeStruct((N_CHIPS, r, cs), BF16),
        grid=(r // tr,),
        in_specs=[pl.BlockSpec((None, tr, cs), lambda i: (l, i, 0))]
        + [pl.BlockSpec(memory_space=pl.ANY)] * len(after),
        out_specs=pl.BlockSpec((None, tr, cs), lambda i: (_my_chip(), i, 0)),
        compiler_params=_params(("parallel",), 2 * _nbytes((tr, cs), F32)),
    )(w, *after)


def _add_half(part, theirs, name):
    _, rh, cs = theirs.shape
    tr = _ew_rows(rh, cs, 3)
    nrb = rh // tr

    def body(a_ref, b_ref, o_ref):
        o_ref[...] = (a_ref[...].astype(F32) + b_ref[...].astype(F32)).astype(BF16)

    half = pl.BlockSpec((None, tr, cs), lambda j, i: (j, i, 0))
    return _pcall(
        body, name=name,
        out_shape=jax.ShapeDtypeStruct((N_CHIPS, rh, cs), BF16),
        grid=(N_CHIPS, nrb),
        in_specs=[pl.BlockSpec((None, tr, cs), lambda j, i: (j, _my_core() * nrb + i, 0)), half],
        out_specs=half,
        compiler_params=_params(("parallel", "parallel"), 3 * _nbytes((tr, cs), F32)),
    )(part, theirs)


def _sum_chips(chip_sum, q, total, l, nl, name):
    _, rh, cs = q.shape
    tr = _ew_rows(rh, cs, 5)
    nrb = rh // tr

    def body(own_ref, q1_ref, q2_ref, q3_ref, *rest):
        acc = own_ref[...].astype(F32)
        for ref in (q1_ref, q2_ref, q3_ref):
            acc = acc + ref[...].astype(F32)
        rest[-1][...] = acc

    def slot(k):
        return pl.BlockSpec((None, tr, cs), lambda i: ((_my_chip() + k) % N_CHIPS, i, 0))

    in_specs = [slot(0), slot(1), slot(2), slot(3)]
    operands = [chip_sum, q, q, q]
    aliases = {}
    if total is not None:
        in_specs.append(HBM_SPEC)
        operands.append(total)
        aliases = {4: 0}
    return _pcall(
        body, name=name,
        out_shape=jax.ShapeDtypeStruct((nl, 2 * rh, cs), F32),
        grid=(nrb,),
        in_specs=in_specs,
        out_specs=pl.BlockSpec((None, tr, cs), lambda i: (l, _my_core() * nrb + i, 0)),
        input_output_aliases=aliases,
        compiler_params=_params(("parallel",), 5 * _nbytes((tr, cs), F32)),
    )(*operands)


def _adamw(w, g, m, v, name, batch=None, prev=()):
    shape = w.shape
    c1 = 1.0 - ADAM_B1 ** ADAM_STEP
    c2 = 1.0 - ADAM_B2 ** ADAM_STEP

    def fn(wv, gv, mv, vv):
        mn = ADAM_B1 * mv + (1.0 - ADAM_B1) * gv
        vn = ADAM_B2 * vv + (1.0 - ADAM_B2) * (gv * gv)
        delta = -ADAM_LR * ((mn / c1) / (jnp.sqrt(vn / c2) + ADAM_EPS) + ADAM_WD * wv)
        return delta, mn, vn

    outs = _ew_call(fn, [_as3(w), _as3(g), _as3(m), _as3(v)], 3, [F32] * 3, name,
                    batch=batch, prev=[_as3(p) for p in prev])
    return [o.reshape(shape) for o in outs]


def _sum_leading(q, name):
    b, nj, r, c = q.shape
    tr = _ew_rows(r, c, nj + 1)

    def body(q_ref, o_ref):
        acc = q_ref[0].astype(F32)
        for j in range(1, nj):
            acc = acc + q_ref[j].astype(F32)
        o_ref[...] = acc

    return _pcall(
        body, name=name,
        out_shape=jax.ShapeDtypeStruct((b, r, c), F32),
        grid=(b, r // tr),
        in_specs=[pl.BlockSpec((None, nj, tr, c), lambda i, j: (i, 0, j, 0))],
        out_specs=pl.BlockSpec((None, tr, c), lambda i, j: (i, j, 0)),
        compiler_params=_params(("parallel", "parallel"), (nj + 1) * _nbytes((tr, c), F32),
                                2 * _nbytes((tr, c), F32)),
    )(q)


HBM_SPEC = pl.BlockSpec(memory_space=pl.ANY)


def _place():
    x, y, c = lax.axis_index("x"), lax.axis_index("y"), lax.axis_index("c")
    peers = [(1 - x, y), (x, 1 - y), (1 - x, 1 - y)]
    return x, y, c, 2 * x + y, peers


def _remote(src, dst, send_sem, recv_sem, dev):
    return pltpu.make_async_remote_copy(src_ref=src, dst_ref=dst, send_sem=send_sem, recv_sem=recv_sem,
                                        device_id=dev, device_id_type=MESH)


IN_HBM = pl.BlockSpec(memory_space=pltpu.HBM)
IN_SEM = pl.BlockSpec(memory_space=pltpu.SEMAPHORE)
DATAFLOW = pltpu.SideEffectType.DATAFLOW_SIDE_EFFECTING
TOKEN = jax.ShapeDtypeStruct((8, LANE), F32)


def _in_hbm(arrays):
    return [pltpu.with_memory_space_constraint(a, pltpu.HBM) for a in arrays]


def _half_rows(ref, c):
    rh = ref.shape[1] // 2
    return pl.ds(c * rh, rh)


def _split_start(body_copies, passed, landing, name, per_array=3, after=()):
    n_pass, n_land = len(passed), len(landing)
    n_arr = n_pass + n_land
    n = per_array * n_pass
    n_in = n_arr + len(after)

    def body(*refs):
        ins = refs[:n_arr]
        send, recv = refs[n_in], refs[n_in + 1]
        token = refs[-1]
        for src, dst, _, s_sem, r_sem, dev in body_copies(ins[:n_pass], ins[n_pass:], send, recv):
            _remote(src, dst, s_sem, r_sem, dev).start()
        token[...] = jnp.zeros_like(token)

    arrays = list(passed) + list(landing)
    return pl.pallas_call(
        body, name=name,
        out_shape=(pltpu.SemaphoreType.DMA((n,)), pltpu.SemaphoreType.DMA((n,)),
                   *[pltpu.HBM(a.shape, a.dtype) for a in arrays], TOKEN),
        in_specs=[IN_HBM] * n_arr + [HBM_SPEC] * len(after),
        out_specs=(IN_SEM, IN_SEM, *[IN_HBM] * n_arr, pl.BlockSpec(memory_space=pltpu.VMEM)),
        input_output_aliases={i: 2 + i for i in range(n_arr)},
        compiler_params=pltpu.CompilerParams(has_side_effects=DATAFLOW),
    )(*_in_hbm(arrays), *after)


def _split_wait(body_copies, send, recv, passed, landing, after, name):
    n_pass, n_land = len(passed), len(landing)
    n_arr = n_pass + n_land

    def body(*refs):
        ins = refs[:n_arr]
        send_ref, recv_ref = refs[n_arr], refs[n_arr + 1]
        for src, _, landed, s_sem, r_sem, dev in body_copies(ins[:n_pass], ins[n_pass:], send_ref, recv_ref):
            cp = _remote(src, landed, s_sem, r_sem, dev)
            cp.wait_send()
            cp.wait_recv()

    arrays = list(passed) + list(landing)
    return pl.pallas_call(
        body, name=name,
        out_shape=tuple(pltpu.HBM(a.shape, a.dtype) for a in arrays),
        in_specs=[IN_HBM] * n_arr + [IN_SEM, IN_SEM] + [HBM_SPEC] * len(after),
        out_specs=tuple([IN_HBM] * n_arr),
        input_output_aliases={i: i for i in range(n_arr)},
        compiler_params=pltpu.CompilerParams(has_side_effects=DATAFLOW),
    )(*arrays, send, recv, *after)


def _gather_copies(slots, _, send, recv):
    _, _, c, me, peers = _place()
    copies = []
    for a, ref in enumerate(slots):
        rows = _half_rows(ref, c)
        own = ref.at[me, rows, :]
        for j, (px, py) in enumerate(peers):
            k = 3 * a + j
            copies.append((own, own, ref.at[2 * px + py, rows, :], send.at[k], recv.at[k], (px, py, c)))
    return copies


def _exchange_copies(chip_sums, landing, send, recv):
    _, _, c, me, peers = _place()
    copies = []
    for a, (s_ref, q_ref) in enumerate(zip(chip_sums, landing)):
        for j, (px, py) in enumerate(peers):
            k = 3 * a + j
            copies.append((s_ref.at[2 * px + py], q_ref.at[me], q_ref.at[2 * px + py],
                           send.at[k], recv.at[k], (px, py, c)))
    return copies


def _gather_start(slots, name):
    out = _split_start(_gather_copies, slots, [], name)
    return out[0], out[1], list(out[2:-1]), out[-1]


def _gather_wait(send, recv, slots, after, name):
    return list(_split_wait(_gather_copies, send, recv, slots, [], after, name))


def _forward_copies(slots, _, send, recv):
    x, y, c, _, peers = _place()
    copies = []
    for a, ref in enumerate(slots):
        for j, (px, py) in enumerate(peers):
            k = 3 * a + j
            mine = ref.at[2 * px + py, _half_rows(ref, c), :]
            theirs = ref.at[2 * px + py, _half_rows(ref, 1 - c), :]
            copies.append((mine, mine, theirs, send.at[k], recv.at[k], (x, y, 1 - c)))
    return copies


def _forward_start(slots, name):
    out = _split_start(_forward_copies, slots, [], name)
    return out[0], out[1], list(out[2:-1]), out[-1]


def _forward_wait(send, recv, slots, after, name):
    return list(_split_wait(_forward_copies, send, recv, slots, [], after, name))


def _gather_small(v, name):
    def body(v_ref, o_ref, send, recv, lsem):
        x, y, c, me, peers = _place()
        local = pltpu.make_async_copy(v_ref, o_ref.at[me], lsem)
        local.start()
        sends = [_remote(v_ref, o_ref.at[me], send.at[j], recv.at[j], (px, py, c))
                 for j, (px, py) in enumerate(peers)]
        for cp in sends:
            cp.start()
        for j, (px, py) in enumerate(peers):
            _remote(v_ref, o_ref.at[2 * px + py], send.at[j], recv.at[j], (px, py, c)).wait_recv()
        for cp in sends:
            cp.wait_send()
        local.wait()

    return _pcall(
        body, name=name,
        out_shape=jax.ShapeDtypeStruct((N_CHIPS,) + v.shape, v.dtype),
        in_specs=[HBM_SPEC], out_specs=HBM_SPEC,
        scratch_shapes=[pltpu.SemaphoreType.DMA((3,)), pltpu.SemaphoreType.DMA((3,)), pltpu.SemaphoreType.DMA],
    )(v)


def _gather_all(v, name):
    def body(v_ref, o_ref, send, recv, lsem):
        x, y, c = lax.axis_index("x"), lax.axis_index("y"), lax.axis_index("c")
        me = 4 * x + 2 * y + c
        local = pltpu.make_async_copy(v_ref, o_ref.at[me], lsem)
        local.start()
        flips = [(fx, fy, fc) for fx in (0, 1) for fy in (0, 1) for fc in (0, 1)][1:]
        peers = [(x ^ fx, y ^ fy, c ^ fc) for fx, fy, fc in flips]
        sends = [_remote(v_ref, o_ref.at[me], send.at[k], recv.at[k], dev) for k, dev in enumerate(peers)]
        for cp in sends:
            cp.start()
        for k, (px, py, pc) in enumerate(peers):
            _remote(v_ref, o_ref.at[4 * px + 2 * py + pc], send.at[k], recv.at[k], (px, py, pc)).wait_recv()
        for cp in sends:
            cp.wait_send()
        local.wait()

    return _pcall(
        body, name=name,
        out_shape=jax.ShapeDtypeStruct((8,) + v.shape, v.dtype),
        in_specs=[HBM_SPEC], out_specs=HBM_SPEC,
        scratch_shapes=[pltpu.SemaphoreType.DMA((7,)), pltpu.SemaphoreType.DMA((7,)), pltpu.SemaphoreType.DMA],
    )(v)


def _send_copies(parts, landing, send, recv):
    x, y, c, _, _ = _place()
    return [(p.at[:, _half_rows(p, 1 - c), :], t, t, send.at[a], recv.at[a], (x, y, 1 - c))
            for a, (p, t) in enumerate(zip(parts, landing))]


def _send_start(parts, name, after=()):
    na = len(parts)
    landing = [lax.empty((N_CHIPS, p.shape[1] // 2, p.shape[2]), p.dtype) for p in parts]
    out = _split_start(_send_copies, parts, landing, name, per_array=1, after=after)
    return out[0], out[1], list(out[2:2 + na]), list(out[2 + na:2 + 2 * na]), out[-1]


def _join_copies_of(layer):
    def copies(reds, _, send, recv):
        x, y, c, _, _ = _place()
        out = []
        for a, ref in enumerate(reds):
            rh = ref.shape[1] // 2
            mine = ref.at[layer, pl.ds(c * rh, rh), :]
            other = ref.at[layer, pl.ds((1 - c) * rh, rh), :]
            out.append((mine, mine, other, send.at[a], recv.at[a], (x, y, 1 - c)))
        return out

    return copies


def _join_start(reds, layer, name):
    out = _split_start(_join_copies_of(layer), reds, [], name, per_array=1)
    return out[0], out[1], list(out[2:-1]), out[-1]


def _join_wait(send, recv, reds, layer, after, name):
    return list(_split_wait(_join_copies_of(layer), send, recv, reds, [], after, name))


def _send_wait(send, recv, parts, landing, after, name):
    na = len(parts)
    out = _split_wait(_send_copies, send, recv, parts, landing, after, name)
    return list(out[:na]), list(out[na:])


def _exchange_start(chip_sums, name):
    na = len(chip_sums)
    landing = [lax.empty(s.shape, s.dtype) for s in chip_sums]
    out = _split_start(_exchange_copies, chip_sums, landing, name)
    return out[0], out[1], list(out[2:2 + na]), list(out[2 + na:2 + 2 * na]), out[-1]


def _exchange_wait(send, recv, chip_sums, landing, after, name):
    na = len(chip_sums)
    out = _split_wait(_exchange_copies, send, recv, chip_sums, landing, after, name)
    return list(out[:na]), list(out[na:])


def _join_halves(reds, lo, hi, name, after=()):
    na = len(reds)
    n_in = na + len(after)
    layers = pl.ds(lo, hi - lo)

    def body(*refs):
        outs = refs[n_in:n_in + na]
        send, recv = refs[n_in + na:]
        x, y, c, _, _ = _place()
        sends = []
        for a, ref in enumerate(outs):
            rh = ref.shape[1] // 2
            mine = ref.at[layers, pl.ds(c * rh, rh), :]
            sends.append(_remote(mine, mine, send.at[a], recv.at[a], (x, y, 1 - c)))
            sends[-1].start()
        for a, ref in enumerate(outs):
            rh = ref.shape[1] // 2
            other = ref.at[layers, pl.ds((1 - c) * rh, rh), :]
            _remote(other, other, send.at[a], recv.at[a], (x, y, 1 - c)).wait_recv()
        for cp in sends:
            cp.wait_send()

    return _pcall(
        body, name=name,
        out_shape=[jax.ShapeDtypeStruct(r.shape, r.dtype) for r in reds],
        in_specs=[HBM_SPEC] * n_in, out_specs=[HBM_SPEC] * na,
        input_output_aliases={a: a for a in range(na)},
        scratch_shapes=[pltpu.SemaphoreType.DMA((na,)), pltpu.SemaphoreType.DMA((na,))],
    )(*reds, *after)


COL_SHARDED = ("w_in", "w_a_out", "w_b_out", "w_x_out", "w_up")
ROW_SHARDED = ("w_kv", "w_o", "w_down")
BIG = COL_SHARDED + ROW_SHARDED
SMALL_REPLICATED = ("g_mix_pre", "conv_a_b", "ln_a_g", "ln_a_b", "g_mem", "g_mix_post", "g_mlp_pre", "g_mlp_post")
SMALL_SHARDED = ("conv_a_w", "conv_b_w")
WEIGHT_ORDER = ("g_mix_pre", "w_in", "conv_a_w", "conv_a_b", "ln_a_g", "ln_a_b", "w_a_out", "conv_b_w", "w_b_out",
                "g_mem", "w_kv", "w_x_out", "w_o", "g_mix_post", "g_mlp_pre", "w_up", "w_down", "g_mlp_post")


def _pack_rows(arrays, width):
    rows = []
    for a in arrays:
        r = a.reshape(-1, width)
        rows.append(jnp.pad(r, ((0, (-r.shape[0]) % 8), (0, 0))))
    return jnp.concatenate(rows, axis=0)


def _unpack_rows(packed, like, width):
    out, at = [], 0
    for a in like:
        n = a.size // width
        out.append(packed[at:at + n].reshape(a.shape))
        at += n + (-n) % 8
    return out


def _step(w, m, v, x, mem, target):
    nl = w["w_in"].shape[0]
    d = x.shape[1]
    c = w["conv_a_b"].shape[1]
    ka = w["conv_a_w"].shape[1]

    conv_pack = jnp.concatenate([w["conv_a_w"], w["conv_b_w"]], axis=1)
    conv_rows = conv_pack.shape[1]
    conv_pack = jnp.pad(conv_pack, ((0, 0), (0, (-conv_rows) % 8), (0, 0)))
    conv_all = _gather_small(conv_pack, "gather_conv")
    slots = {name: [_cast_to_slot(w[name], 0, "cast_" + name, after=[conv_all])] for name in BIG}
    conv_all = jnp.moveaxis(conv_all, 0, 2).reshape(nl, conv_pack.shape[1], c)
    cw_a, cw_b = conv_all[:, :ka], conv_all[:, ka:conv_rows]

    def start_gather(l):
        return _gather_start([slots[name][l] for name in BIG], "gather_start_%d" % l)

    def start_forward(l, pending, after):
        send, recv, thru, _ = pending
        landed = _gather_wait(send, recv, thru, after, "gather_wait_%d" % l)
        return _forward_start(landed, "gather_fwd_start_%d" % l)

    def finish_forward(l, passing, after):
        send, recv, thru, _ = passing
        arrays = _forward_wait(send, recv, thru, after, "gather_fwd_wait_%d" % l)
        return {name: (a.reshape(1, -1, a.shape[-1]) if name in ROW_SHARDED else a)
                for name, a in zip(BIG, arrays)}

    pending = start_gather(0)
    for name in BIG:
        slots[name] += [_cast_to_slot(w[name], l, "cast_" + name, after=[pending[3]]) for l in range(1, nl)]
    (h,) = _norm_fwd(x, "norm_first", g_next=w["g_mix_pre"][0])
    passing = start_forward(0, pending, [h] + [s for name in BIG for s in slots[name][1:]])
    full = [finish_forward(0, passing, [passing[3]])]

    saved = []
    xl = x
    dy = loss_cols = None
    for l in range(nl):
        s = {"x": xl, "h": h}
        fw = full[l]
        token = None
        if l + 1 < nl:
            pending = start_gather(l + 1)
            token = pending[3]
        proj = _mm_nn(h, fw["w_in"], BF16, "mm_in", after=token)
        ca, act_a = _branch_a_fwd(proj, cw_a[l], w["conv_a_b"][l], w["ln_a_g"][l], w["ln_a_b"][l], "branch_a_fwd")
        b_in = _branch_b_fwd(proj, cw_b[l], "branch_b_fwd")
        (mem_n,) = _norm_fwd(mem, "norm_mem", g_next=w["g_mem"][l])
        kv = _mm_nn(mem_n, fw["w_kv"], BF16, "mm_kv")
        att = _attn_fwd(proj, kv, "attn_fwd")
        ya = _mm_nn(act_a, fw["w_a_out"], BF16, "mm_a_out")
        yb = _mm_nn(b_in, fw["w_b_out"], BF16, "mm_b_out")
        yx = _mm_nn(att, fw["w_x_out"], BF16, "mm_x_out")
        merged = _merge_fwd(proj, ya, yb, yx, "merge_fwd")
        z = _mm_nn(merged, fw["w_o"], BF16, "mm_o")
        x1, h2 = _norm_fwd(xl, "norm_mid", z=z, g_post=w["g_mix_post"][l], g_next=w["g_mlp_pre"][l])
        token = None
        if l + 1 < nl:
            passing = start_forward(l + 1, pending, [h2])
            token = passing[3]
        up = _mm_nn(h2, fw["w_up"], BF16, "mm_up", after=token)
        f = _mm_nn(up, fw["w_down"], BF16, "mm_down", a_act="relu2")
        s.update(proj=proj, ca=ca, act_a=act_a, b_in=b_in, mem_n=mem_n, kv=kv, att=att, ya=ya, yb=yb, yx=yx,
                 merged=merged, z=z, x1=x1, h2=h2, up=up, f=f)
        saved.append(s)
        if l + 1 < nl:
            full.append(finish_forward(l + 1, passing, [f]))
        if l + 1 < nl:
            xl, h = _norm_fwd(x1, "norm_mid", z=f, g_post=w["g_mlp_post"][l], g_next=w["g_mix_pre"][l + 1])
        else:
            dy, loss_cols = _norm_fwd(x1, "norm_loss", z=f, g_post=w["g_mlp_post"][l], target=target)

    part = {name: [None] * nl for name in BIG}
    small = {name: [None] * nl for name in SMALL_REPLICATED + SMALL_SHARDED}
    dxo = dy
    d_f, small["g_mlp_post"][nl - 1] = _norm_bwd("norm_bwd_top", dxo=dy,
                                                 post=(saved[-1]["f"], w["g_mlp_post"][nl - 1]))
    grad_x = None
    totals = {name: None for name in BIG}

    def start_send(l, after):
        return _send_start([part[name][l] for name in BIG], "rs_send_start_%d" % l, after=after)

    def start_exchange(l, sending, after):
        send, recv, parts, landing, _ = sending
        parts, theirs = _send_wait(send, recv, parts, landing, after, "rs_send_wait_%d" % l)
        chip_sums = [_add_half(p, t, "rs_add_" + name) for name, p, t in zip(BIG, parts, theirs)]
        return _exchange_start(chip_sums, "rs_xchg_start_%d" % l)

    joining = {}

    def finish_join(after):
        if joining:
            send, recv, reds, layer = joining.pop("pending")
            totals.update(zip(BIG, _join_wait(send, recv, reds, layer, after, "rs_join_wait_%d" % layer)))

    def finish_exchange(l, pending, after):
        send, recv, chip_sums, landing, _ = pending
        chip_sums, q = _exchange_wait(send, recv, chip_sums, landing, after, "rs_xchg_wait_%d" % l)
        finish_join(after)
        for name, own, got in zip(BIG, chip_sums, q):
            totals[name] = _sum_chips(own, got, totals[name], l, nl, "rs_sum_" + name)
        if l == 0:
            return None
        send, recv, reds, token = _join_start([totals[name] for name in BIG], l, "rs_join_start_%d" % l)
        joining["pending"] = (send, recv, reds, l)
        totals.update(zip(BIG, reds))
        return token

    sending = exchange = None
    for l in reversed(range(nl)):
        s = saved[l]
        fw = full[l]
        d_up = _mm_nt(d_f, fw["w_down"], BF16, "mm_down_dx", relu2_of=s["up"],
                      after=None if sending is None else sending[4])
        part["w_down"][l] = _mm_tn(s["up"], d_f, 1, "mm_down_dw", a_act="relu2").reshape(N_CHIPS, -1, d)
        d_h2 = _mm_nt(d_up, fw["w_up"], BF16, "mm_up_dx")
        part["w_up"][l] = _mm_tn(s["h2"], d_up, N_CHIPS, "mm_up_dw")
        dx1, d_z, small["g_mlp_pre"][l], small["g_mix_post"][l] = _norm_bwd(
            "norm_bwd_mid", dxo=dxo, pre=(d_h2, s["x1"], w["g_mlp_pre"][l]), post=(s["z"], w["g_mix_post"][l]))
        if sending is not None:
            exchange = start_exchange(l + 1, sending, [d_z])
        d_merged = _mm_nt(d_z, fw["w_o"], BF16, "mm_o_dx", after=None if exchange is None else exchange[4])
        part["w_o"][l] = _mm_tn(s["merged"], d_z, 1, "mm_o_dw").reshape(N_CHIPS, -1, d)
        dproj, d_ya, d_yb, d_yx = _merge_bwd(d_merged, s["proj"], s["ya"], s["yb"], s["yx"], "merge_bwd")
        d_act_a = _mm_nt(d_ya, fw["w_a_out"], BF16, "mm_a_out_dx")
        part["w_a_out"][l] = _mm_tn(s["act_a"], d_ya, N_CHIPS, "mm_a_out_dw")
        d_b_in = _mm_nt(d_yb, fw["w_b_out"], BF16, "mm_b_out_dx")
        part["w_b_out"][l] = _mm_tn(s["b_in"], d_yb, N_CHIPS, "mm_b_out_dw")
        d_att = _mm_nt(d_yx, fw["w_x_out"], BF16, "mm_x_out_dx")
        part["w_x_out"][l] = _mm_tn(s["att"], d_yx, N_CHIPS, "mm_x_out_dw")
        dproj, small["ln_a_g"][l], small["ln_a_b"][l], small["conv_a_b"][l], small["conv_a_w"][l] = _branch_a_bwd(
            dproj, d_act_a, s["ca"], s["proj"], cw_a[l], w["ln_a_g"][l], w["ln_a_b"][l], "branch_a_bwd")
        dproj, small["conv_b_w"][l] = _branch_b_bwd(dproj, d_b_in, s["proj"], cw_b[l], "branch_b_bwd")
        dproj, d_kv = _attn_bwd(dproj, d_att, s["proj"], s["kv"], "attn_bwd")
        part["w_kv"][l] = _mm_tn(s["mem_n"], d_kv, 1, "mm_kv_dw").reshape(N_CHIPS, -1, 2 * c)
        d_mem_n = _mm_nt(d_kv, fw["w_kv"], F32, "mm_kv_dx")
        (small["g_mem"][l],) = _norm_bwd("norm_bwd_mem", pre=(d_mem_n, mem, w["g_mem"][l]), want_dx=False)
        d_h = _mm_nt(dproj, fw["w_in"], BF16, "mm_in_dx")
        part["w_in"][l] = _mm_tn(s["h"], dproj, N_CHIPS, "mm_in_dw")
        join_token = finish_exchange(l + 1, exchange, [d_h]) if exchange is not None else None
        sending = start_send(l, [] if join_token is None else [join_token])
        if l > 0:
            dxo, d_f, small["g_mix_pre"][l], small["g_mlp_post"][l - 1] = _norm_bwd(
                "norm_bwd_mid", dxo=dx1, pre=(d_h, s["x"], w["g_mix_pre"][l]),
                post=(saved[l - 1]["f"], w["g_mlp_post"][l - 1]))
        else:
            grad_x, small["g_mix_pre"][0] = _norm_bwd("norm_bwd_last", dxo=dx1,
                                                      pre=(d_h, s["x"], w["g_mix_pre"][0]))
    exchange = start_exchange(0, sending, [grad_x])

    grads, delta, new_m, new_v = {}, {}, {}, {}
    small_names = SMALL_REPLICATED + SMALL_SHARDED
    stacked = [jnp.stack(small[n]) for n in small_names]
    packed = _pack_rows(stacked, c)
    total = _sum_leading(_gather_all(packed, "gather_small_grads")[None], "sum_small_grads")[0]
    reduced = dict(zip(small_names, _unpack_rows(total, stacked, c)))
    chip = 2 * lax.axis_index("x") + lax.axis_index("y")
    cs = c // N_CHIPS
    for name in SMALL_SHARDED:
        grads[name] = lax.dynamic_slice_in_dim(reduced[name], chip * cs, cs, axis=2)
    for name in SMALL_REPLICATED:
        grads[name] = reduced[name].reshape(w[name].shape)
    for group, width in ((SMALL_REPLICATED, c), (SMALL_SHARDED, cs)):
        packs = [_pack_rows([src[n] for n in group], width)[None] for src in (w, grads, m, v)]
        outs = _adamw(*packs, "adamw_small_%d" % width)
        for dst, out in zip((delta, new_m, new_v), outs):
            dst.update(zip(group, _unpack_rows(out[0], [w[n] for n in group], width)))

    state = {name: () for name in BIG}
    if nl > 1:
        finish_join([exchange[4]])
        for name in BIG:
            state[name] = _adamw(w[name], totals[name], m[name], v[name], "adamw_" + name, batch=(1, nl))
    finish_exchange(0, exchange, [state[name][0] for name in BIG] if nl > 1 else [grad_x])
    grads.update(zip(BIG, _join_halves([totals[name] for name in BIG], 0, 1, "rs_join_first")))
    for name in BIG:
        delta[name], new_m[name], new_v[name] = _adamw(w[name], grads[name], m[name], v[name], "adamw_" + name,
                                                       batch=(0, 1), prev=state[name])

    loss = lax.psum(0.5 * jnp.sum(loss_cols) / d, ("x", "y", "c"))
    return loss, grad_x, grads, delta, new_m, new_v


def kernel(x, mem, g_mix_pre, w_in, conv_a_w, conv_a_b, ln_a_g, ln_a_b, w_a_out, conv_b_w, w_b_out, g_mem, w_kv, w_x_out, w_o, g_mix_post, g_mlp_pre, w_up, w_down, g_mlp_post, loss_target, m_g_mix_pre, m_w_in, m_conv_a_w, m_conv_a_b, m_ln_a_g, m_ln_a_b, m_w_a_out, m_conv_b_w, m_w_b_out, m_g_mem, m_w_kv, m_w_x_out, m_w_o, m_g_mix_post, m_g_mlp_pre, m_w_up, m_w_down, m_g_mlp_post, v_g_mix_pre, v_w_in, v_conv_a_w, v_conv_a_b, v_ln_a_g, v_ln_a_b, v_w_a_out, v_conv_b_w, v_w_b_out, v_g_mem, v_w_kv, v_w_x_out, v_w_o, v_g_mix_post, v_g_mlp_pre, v_w_up, v_w_down, v_g_mlp_post):
    w = dict(g_mix_pre=g_mix_pre, w_in=w_in, conv_a_w=conv_a_w, conv_a_b=conv_a_b, ln_a_g=ln_a_g, ln_a_b=ln_a_b,
             w_a_out=w_a_out, conv_b_w=conv_b_w, w_b_out=w_b_out, g_mem=g_mem, w_kv=w_kv, w_x_out=w_x_out, w_o=w_o,
             g_mix_post=g_mix_post, g_mlp_pre=g_mlp_pre, w_up=w_up, w_down=w_down, g_mlp_post=g_mlp_post)
    m = dict(g_mix_pre=m_g_mix_pre, w_in=m_w_in, conv_a_w=m_conv_a_w, conv_a_b=m_conv_a_b, ln_a_g=m_ln_a_g,
             ln_a_b=m_ln_a_b, w_a_out=m_w_a_out, conv_b_w=m_conv_b_w, w_b_out=m_w_b_out, g_mem=m_g_mem, w_kv=m_w_kv,
             w_x_out=m_w_x_out, w_o=m_w_o, g_mix_post=m_g_mix_post, g_mlp_pre=m_g_mlp_pre, w_up=m_w_up,
             w_down=m_w_down, g_mlp_post=m_g_mlp_post)
    v = dict(g_mix_pre=v_g_mix_pre, w_in=v_w_in, conv_a_w=v_conv_a_w, conv_a_b=v_conv_a_b, ln_a_g=v_ln_a_g,
             ln_a_b=v_ln_a_b, w_a_out=v_w_a_out, conv_b_w=v_conv_b_w, w_b_out=v_w_b_out, g_mem=v_g_mem, w_kv=v_w_kv,
             w_x_out=v_w_x_out, w_o=v_w_o, g_mix_post=v_g_mix_post, g_mlp_pre=v_g_mlp_pre, w_up=v_w_up,
             w_down=v_w_down, g_mlp_post=v_g_mlp_post)
    loss, grad_x, grads, delta, new_m, new_v = _step(w, m, v, x[0], mem[0], loss_target[0])
    out = [loss, grad_x[None]]
    for group in (grads, delta, new_m, new_v):
        out += [group[n] for n in WEIGHT_ORDER]
    return tuple(out)
```

```python
import functools

import jax
import jax.numpy as jnp
from jax import lax
from jax.experimental import pallas as pl
from jax.experimental.pallas import tpu as pltpu

F32 = jnp.float32
BF16 = jnp.bfloat16
MESH = pl.DeviceIdType.MESH

NORM_EPS = 1e-6
N_HEADS = 4
ADAM_LR = 0.001
ADAM_B1 = 0.9
ADAM_B2 = 0.999
ADAM_EPS = 1e-08
ADAM_WD = 0.01
ADAM_STEP = 10

N_CHIPS = 4
V7X_VMEM_BYTES = 64 * 1024 * 1024
VMEM_CAP = V7X_VMEM_BYTES - 8 * 1024 * 1024
LANE = 128
SUBLANES = 8
SUBLANE_BF16 = 16
HALO = 32
ROW_TILE = 256
CONV_ROWS = 32
CONV_LANES = 512
ATTN_TILE = 512
MM_TM = 1024
MM_TN = 1024
MM_TK = 2048
MM_TC = 3072
MM_WHOLE_TILE = 512
MM_WHOLE_VMEM = 34 * 1024 * 1024
EW_VMEM_BYTES = 24 * 1024 * 1024


def _tile(n, pref, align):
    if n <= pref:
        return n
    t = (pref // align) * align
    while t >= align:
        if n % t == 0:
            return t
        t -= align
    return n


def _ew_rows(r, c, n_arrays):
    return _tile(r, max(SUBLANE_BF16, EW_VMEM_BYTES // (2 * n_arrays * c * 4)), SUBLANE_BF16)


def _nbytes(shape, dtype):
    n = 1
    for s in shape:
        if s is not None:
            n *= s
    return n * jnp.dtype(dtype).itemsize


def _params(semantics, block_bytes, temp_bytes=0):
    need = 2 * block_bytes + temp_bytes + (4 << 20)
    return pltpu.CompilerParams(dimension_semantics=semantics,
                                vmem_limit_bytes=int(min(max(need, 16 << 20), VMEM_CAP)))


def _sigmoid(v):
    return 1.0 / (1.0 + jnp.exp(-v))


def _pcall(body, **kwargs):
    call = pl.pallas_call(body, **kwargs)
    return lambda *operands: call(*[pltpu.with_memory_space_constraint(o, pltpu.HBM) for o in operands])


def _whole_contraction_fits(k):
    return 4 * k * MM_WHOLE_TILE * 2 <= MM_WHOLE_VMEM


def _mm_nn(a, b3, out_dtype, name, a_act=None, after=None):
    m, k = a.shape
    s, k2, ns = b3.shape
    assert k == k2
    tm = _tile(m, MM_TM, SUBLANE_BF16)
    tn = _tile(ns, MM_TN, LANE)
    tk = _tile(k, MM_TK, LANE)
    b_resident = k > tk and _whole_contraction_fits(k)
    if b_resident:
        tm, tn, tk = _tile(m, MM_WHOLE_TILE, SUBLANE_BF16), _tile(ns, MM_WHOLE_TILE, LANE), k
    q = ns // tn
    nk = k // tk

    def ij(g0, g1):
        return (g1, g0) if b_resident else (g0, g1)

    n_in = 2 if after is None else 3

    def body(*refs):
        a_ref, b_ref, o_ref = refs[0], refs[1], refs[n_in]
        scratch = refs[n_in + 1:]
        av = a_ref[...]
        if a_act == "relu2":
            r = jnp.maximum(av.astype(BF16), 0.0)
            av = r * r
        p = jnp.dot(av.astype(BF16), b_ref[...].astype(BF16), preferred_element_type=F32)
        if nk == 1:
            o_ref[...] = p.astype(o_ref.dtype)
        else:
            acc, = scratch
            kk = pl.program_id(2)

            @pl.when(kk == 0)
            def _():
                acc[...] = p

            @pl.when(kk > 0)
            def _():
                acc[...] += p

            @pl.when(kk == nk - 1)
            def _():
                o_ref[...] = acc[...].astype(o_ref.dtype)

    blocks = (_nbytes((tm, tk), a.dtype) + _nbytes((tk, tn), b3.dtype) + _nbytes((tm, tn), out_dtype))
    return _pcall(
        body, name=name,
        out_shape=jax.ShapeDtypeStruct((m, s * ns), out_dtype),
        grid=(*ij(m // tm, s * q), nk),
        in_specs=[pl.BlockSpec((tm, tk), lambda g0, g1, c: (ij(g0, g1)[0], c)),
                  pl.BlockSpec((None, tk, tn), lambda g0, g1, c: (ij(g0, g1)[1] // q, c, ij(g0, g1)[1] % q))]
        + ([] if after is None else [pl.BlockSpec(memory_space=pl.ANY)]),
        out_specs=pl.BlockSpec((tm, tn), lambda g0, g1, c: ij(g0, g1)),
        scratch_shapes=[pltpu.VMEM((tm, tn), F32)] if nk > 1 else [],
        compiler_params=_params(("parallel", "parallel", "arbitrary"), blocks,
                                3 * _nbytes((tm, tn), F32) + _nbytes((tm, tk), F32)),
    )(*([a, b3] if after is None else [a, b3, after]))


def _mm_nt(a, b3, out_dtype, name, relu2_of=None, after=None):
    m, n = a.shape
    s, kd, ns = b3.shape
    assert n == s * ns
    tm = _tile(m, MM_TM, SUBLANE_BF16)
    tj = _tile(kd, MM_TN, LANE)
    tc = _tile(ns, MM_TC, LANE)
    q = ns // tc
    nc = s * q

    n_in = 2 + (relu2_of is not None) + (after is not None)

    def body(*refs):
        a_ref, b_ref, u_ref, o_ref = refs[0], refs[1], refs[2], refs[n_in]
        scratch = refs[n_in + 1:]

        def finish(p):
            if relu2_of is not None:
                p = p * (2.0 * jnp.maximum(u_ref[...].astype(F32), 0.0))
            o_ref[...] = p.astype(o_ref.dtype)

        p = lax.dot_general(a_ref[...].astype(BF16), b_ref[...].astype(BF16),
                            (((1,), (1,)), ((), ())), preferred_element_type=F32)
        if nc == 1:
            finish(p)
        else:
            acc, = scratch
            cc = pl.program_id(2)

            @pl.when(cc == 0)
            def _():
                acc[...] = p

            @pl.when(cc > 0)
            def _():
                acc[...] += p

            @pl.when(cc == nc - 1)
            def _():
                finish(acc[...])

    in_specs = [pl.BlockSpec((tm, tc), lambda i, j, c: (i, c)),
                pl.BlockSpec((None, tj, tc), lambda i, j, c: (c // q, j, c % q))]
    operands = [a, b3]
    blocks = _nbytes((tm, tc), a.dtype) + _nbytes((tj, tc), b3.dtype) + _nbytes((tm, tj), out_dtype)
    if relu2_of is not None:
        in_specs.append(pl.BlockSpec((tm, tj), lambda i, j, c: (i, j)))
        operands.append(relu2_of)
        blocks += _nbytes((tm, tj), relu2_of.dtype)
    if after is not None:
        in_specs.append(pl.BlockSpec(memory_space=pl.ANY))
        operands.append(after)
    return _pcall(
        body, name=name,
        out_shape=jax.ShapeDtypeStruct((m, kd), out_dtype),
        grid=(m // tm, kd // tj, nc),
        in_specs=in_specs,
        out_specs=pl.BlockSpec((tm, tj), lambda i, j, c: (i, j)),
        scratch_shapes=[pltpu.VMEM((tm, tj), F32)] if nc > 1 else [],
        compiler_params=_params(("parallel", "parallel", "arbitrary"), blocks,
                                3 * _nbytes((tm, tj), F32)),
    )(*operands)


def _mm_tn(a, g, out_shards, name, a_act=None):
    m, ka = a.shape
    m2, n = g.shape
    assert m == m2
    ns = n // out_shards
    ta = _tile(ka, MM_TM, LANE)
    tn = _tile(ns, MM_TN, LANE)
    tm = _tile(m, MM_TK, SUBLANE_BF16)
    if m > tm and _whole_contraction_fits(m):
        ta, tn, tm = _tile(ka, MM_WHOLE_TILE, LANE), _tile(ns, MM_WHOLE_TILE, LANE), m
    q = ns // tn
    nm = m // tm

    def body(a_ref, g_ref, o_ref, *scratch):
        av = a_ref[...]
        if a_act == "relu2":
            r = jnp.maximum(av.astype(BF16), 0.0)
            av = r * r
        p = lax.dot_general(av.astype(BF16), g_ref[...].astype(BF16),
                            (((0,), (0,)), ((), ())), preferred_element_type=F32)
        if nm == 1:
            o_ref[...] = p.astype(o_ref.dtype)
        else:
            acc, = scratch
            cc = pl.program_id(2)

            @pl.when(cc == 0)
            def _():
                acc[...] = p

            @pl.when(cc > 0)
            def _():
                acc[...] += p

            @pl.when(cc == nm - 1)
            def _():
                o_ref[...] = acc[...].astype(o_ref.dtype)

    blocks = _nbytes((tm, ta), a.dtype) + _nbytes((tm, tn), g.dtype) + _nbytes((ta, tn), BF16)
    return _pcall(
        body, name=name,
        out_shape=jax.ShapeDtypeStruct((out_shards, ka, ns), BF16),
        grid=(ka // ta, out_shards * q, nm),
        in_specs=[pl.BlockSpec((tm, ta), lambda i, j, c: (c, i)),
                  pl.BlockSpec((tm, tn), lambda i, j, c: (c, j))],
        out_specs=pl.BlockSpec((None, ta, tn), lambda i, j, c: (j // q, i, j % q)),
        scratch_shapes=[pltpu.VMEM((ta, tn), F32)] if nm > 1 else [],
        compiler_params=_params(("parallel", "parallel", "arbitrary"), blocks,
                                3 * _nbytes((ta, tn), F32) + _nbytes((tm, ta), F32)),
    )(a, g)


def _rms_scale(v):
    return lax.rsqrt(jnp.mean(v * v, axis=-1, keepdims=True) + NORM_EPS)


def _norm_fwd(x, name, *, z=None, g_post=None, g_next=None, target=None):
    t, d = x.shape
    tr = _tile(t, ROW_TILE, SUBLANE_BF16)
    has_res, has_next, has_loss = z is not None, g_next is not None, target is not None

    def body(*refs):
        it = iter(refs)
        x_ref = next(it)
        z_ref, gp_ref = (next(it), next(it)) if has_res else (None, None)
        gn_ref = next(it) if has_next else None
        t_ref = next(it) if has_loss else None
        xv = x_ref[...]
        if has_res:
            zv = z_ref[...].astype(F32)
            xv = xv + zv * _rms_scale(zv) * gp_ref[...]
            if not has_loss:
                next(it)[...] = xv
        if has_next:
            next(it)[...] = (xv * _rms_scale(xv) * gn_ref[...]).astype(BF16)
        if has_loss:
            e = xv - t_ref[...]
            next(it)[...] = e * (1.0 / d)
            ls_ref = next(it)

            @pl.when(pl.program_id(0) == 0)
            def _():
                ls_ref[...] = jnp.zeros_like(ls_ref)

            ls_ref[...] += jnp.sum(e * e, axis=0, keepdims=True)

    row = pl.BlockSpec((tr, d), lambda i: (i, 0))
    vec = pl.BlockSpec((1, d), lambda i: (0, 0))
    operands, in_specs, out_shape, out_specs = [x], [row], [], []
    if has_res:
        operands += [z, g_post.reshape(1, d)]
        in_specs += [row, vec]
        if not has_loss:
            out_shape.append(jax.ShapeDtypeStruct((t, d), F32))
            out_specs.append(row)
    if has_next:
        operands.append(g_next.reshape(1, d))
        in_specs.append(vec)
        out_shape.append(jax.ShapeDtypeStruct((t, d), BF16))
        out_specs.append(row)
    if has_loss:
        operands.append(target)
        in_specs.append(row)
        out_shape += [jax.ShapeDtypeStruct((t, d), F32), jax.ShapeDtypeStruct((1, d), F32)]
        out_specs += [row, vec]
    return _pcall(
        body, name=name, out_shape=out_shape, grid=(t // tr,),
        in_specs=in_specs, out_specs=out_specs,
        compiler_params=_params(("arbitrary",), 5 * _nbytes((tr, d), F32), 4 * _nbytes((tr, d), F32)),
    )(*operands)


def _norm_bwd(name, *, dxo=None, pre=None, post=None, want_dx=True):
    ref_arr = dxo if dxo is not None else pre[1]
    t, d = ref_arr.shape
    tr = _tile(t, ROW_TILE, SUBLANE_BF16)
    has_dxo, has_pre, has_post = dxo is not None, pre is not None, post is not None

    def body(*refs):
        it = iter(refs)
        dxo_ref = next(it) if has_dxo else None
        dh_ref, xin_ref, gpre_ref = (next(it), next(it), next(it)) if has_pre else (None,) * 3
        z_ref, gpost_ref = (next(it), next(it)) if has_post else (None, None)
        dx_ref = next(it) if (has_pre and want_dx) else None
        dz_ref = next(it) if has_post else None
        dgpre_ref = next(it) if has_pre else None
        dgpost_ref = next(it) if has_post else None
        first = pl.program_id(0) == 0

        dx = dxo_ref[...] if has_dxo else None
        if has_pre:
            xin = xin_ref[...]
            dh = dh_ref[...].astype(F32)
            r = _rms_scale(xin)
            gy = dh * gpre_ref[...]
            dloc = r * gy - xin * (r * r * r) * jnp.mean(gy * xin, axis=-1, keepdims=True)
            dx = dloc if dx is None else dx + dloc
            if want_dx:
                dx_ref[...] = dx

            @pl.when(first)
            def _():
                dgpre_ref[...] = jnp.zeros_like(dgpre_ref)

            dgpre_ref[...] += jnp.sum(dh * xin * r, axis=0, keepdims=True)
        if has_post:
            zv = z_ref[...].astype(F32)
            r = _rms_scale(zv)
            gy = dx * gpost_ref[...]
            dz = r * gy - zv * (r * r * r) * jnp.mean(gy * zv, axis=-1, keepdims=True)
            dz_ref[...] = dz.astype(BF16)

            @pl.when(first)
            def _():
                dgpost_ref[...] = jnp.zeros_like(dgpost_ref)

            dgpost_ref[...] += jnp.sum(dx * zv * r, axis=0, keepdims=True)

    row = pl.BlockSpec((tr, d), lambda i: (i, 0))
    vec = pl.BlockSpec((1, d), lambda i: (0, 0))
    operands, in_specs, out_shape, out_specs = [], [], [], []
    if has_dxo:
        operands.append(dxo)
        in_specs.append(row)
    if has_pre:
        operands += [pre[0], pre[1], pre[2].reshape(1, d)]
        in_specs += [row, row, vec]
    if has_post:
        operands += [post[0], post[1].reshape(1, d)]
        in_specs += [row, vec]
    if has_pre and want_dx:
        out_shape.append(jax.ShapeDtypeStruct((t, d), F32))
        out_specs.append(row)
    if has_post:
        out_shape.append(jax.ShapeDtypeStruct((t, d), BF16))
        out_specs.append(row)
    if has_pre:
        out_shape.append(jax.ShapeDtypeStruct((1, d), F32))
        out_specs.append(vec)
    if has_post:
        out_shape.append(jax.ShapeDtypeStruct((1, d), F32))
        out_specs.append(vec)
    return _pcall(
        body, name=name, out_shape=out_shape, grid=(t // tr,),
        in_specs=in_specs, out_specs=out_specs,
        compiler_params=_params(("arbitrary",), 6 * _nbytes((tr, d), F32), 6 * _nbytes((tr, d), F32)),
    )(*operands)


def _seq_tiles(t):
    tr = _tile(t, ROW_TILE, HALO)
    assert tr % HALO == 0 and t % tr == 0
    return tr, t // tr, tr // HALO


def _col(tr, width, cb):
    return pl.BlockSpec((tr, width), lambda i: (i, cb))


def _prev_halo(per, width, cb):
    return pl.BlockSpec((HALO, width), lambda i: (jnp.maximum(i * per - 1, 0), cb))


def _next_halo(per, n_halo, width, cb):
    return pl.BlockSpec((HALO, width), lambda i: (jnp.minimum((i + 1) * per, n_halo - 1), cb))


def _const(shape):
    return pl.BlockSpec(shape, lambda i: (0,) * len(shape))


def _glu(val, gate):
    return val.astype(F32) * _sigmoid(gate.astype(F32))


def _conv_chunks(tr, c):
    lanes = min(CONV_LANES, c)
    return [(r0, pl.ds(c0, lanes)) for r0 in range(0, tr, CONV_ROWS) for c0 in range(0, c, lanes)]


def _shift_copies(buf, shifted):
    rows = shifted.shape[1]
    buf[pl.ds(rows, SUBLANES), :] = jnp.zeros((SUBLANES, buf.shape[1]), F32)
    for s in range(SUBLANES):
        shifted[s] = buf[pl.ds(s, rows), :]


def _shifted_rows(shifted, offset, r0, cols):
    return shifted[offset % SUBLANES, pl.ds(offset - offset % SUBLANES + r0, CONV_ROWS), cols]


def _tap_sum(w_ref, shifted, r0, cols, first, step, kw):
    acc = jnp.zeros((CONV_ROWS, cols.size), F32)
    for k in range(kw):
        acc = acc + w_ref[pl.ds(k, 1), cols] * _shifted_rows(shifted, first + step * k, r0, cols)
    return acc


def _layer_norm_parts(ca):
    mu = jnp.mean(ca, axis=-1, keepdims=True)
    xc = ca - mu
    rs = lax.rsqrt(jnp.mean(xc * xc, axis=-1, keepdims=True) + NORM_EPS)
    return xc * rs, rs


def _branch_a_fwd(proj, cw, cb, lg, lb, name):
    t = proj.shape[0]
    kw, c = cw.shape
    tr, nt, per = _seq_tiles(t)

    def body(av_ref, ag_ref, hv_ref, hg_ref, cw_ref, cb_ref, lg_ref, lb_ref, ca_ref, act_ref, abuf, ash):
        i = pl.program_id(0)
        abuf[pl.ds(0, HALO), :] = jnp.where(i > 0, _glu(hv_ref[...], hg_ref[...]), 0.0)
        abuf[pl.ds(HALO, tr), :] = _glu(av_ref[...], ag_ref[...])
        _shift_copies(abuf, ash)
        for r0, cols in _conv_chunks(tr, c):
            ca_ref[pl.ds(r0, CONV_ROWS), cols] = (
                _tap_sum(cw_ref, ash, r0, cols, HALO - (kw - 1), 1, kw) + cb_ref[:, cols])
        xh, _ = _layer_norm_parts(ca_ref[...])
        ln = xh * lg_ref[...] + lb_ref[...]
        act_ref[...] = (ln * _sigmoid(ln)).astype(BF16)

    return _pcall(
        body, name=name,
        out_shape=[jax.ShapeDtypeStruct((t, c), F32), jax.ShapeDtypeStruct((t, c), BF16)],
        grid=(nt,),
        in_specs=[_col(tr, c, 0), _col(tr, c, 1), _prev_halo(per, c, 0), _prev_halo(per, c, 1),
                  _const((kw, c)), _const((1, c)), _const((1, c)), _const((1, c))],
        out_specs=[_col(tr, c, 0), _col(tr, c, 0)],
        scratch_shapes=[pltpu.VMEM((HALO + tr + SUBLANES, c), F32), pltpu.VMEM((SUBLANES, HALO + tr, c), F32)],
        compiler_params=_params(("arbitrary",), 4 * _nbytes((tr, c), F32),
                                (8 + SUBLANES) * _nbytes((tr + HALO, c), F32)),
    )(proj, proj, proj, proj, cw, cb.reshape(1, c), lg.reshape(1, c), lb.reshape(1, c))


def _branch_a_bwd(dproj, dact, ca, proj, cw, lg, lb, name):
    t = proj.shape[0]
    kw, c = cw.shape
    tr, nt, per = _seq_tiles(t)
    n_halo = t // HALO

    def body(dproj_in, da_ref, dah_ref, ca_ref, cah_ref, av_ref, ag_ref, hv_ref, hg_ref,
             cw_ref, lg_ref, lb_ref, out_ref, dlg_ref, dlb_ref, dcb_ref, dcw_ref, abuf, dbuf, sgbuf, ash, dsh):
        del dproj_in
        i = pl.program_id(0)
        lgv, lbv = lg_ref[...], lb_ref[...]

        def conv_grad(dact_v, ca_v):
            xh, rs = _layer_norm_parts(ca_v)
            ln = xh * lgv + lbv
            sg = _sigmoid(ln)
            dln = dact_v.astype(F32) * (sg * (1.0 + ln * (1.0 - sg)))
            dxh = dln * lgv
            dca = rs * (dxh - jnp.mean(dxh, axis=-1, keepdims=True)
                        - xh * jnp.mean(dxh * xh, axis=-1, keepdims=True))
            return dca, dln, xh

        dca, dln, xh = conv_grad(da_ref[...], ca_ref[...])
        dca_h, _, _ = conv_grad(dah_ref[...], cah_ref[...])
        dbuf[pl.ds(0, tr), :] = dca
        dbuf[pl.ds(tr, HALO), :] = jnp.where(i < nt - 1, dca_h, 0.0)

        @pl.when(i == 0)
        def _():
            dlg_ref[...] = jnp.zeros_like(dlg_ref)
            dlb_ref[...] = jnp.zeros_like(dlb_ref)
            dcb_ref[...] = jnp.zeros_like(dcb_ref)
            dcw_ref[...] = jnp.zeros_like(dcw_ref)

        dlg_ref[...] += jnp.sum(dln * xh, axis=0, keepdims=True)
        dlb_ref[...] += jnp.sum(dln, axis=0, keepdims=True)
        dcb_ref[...] += jnp.sum(dca, axis=0, keepdims=True)

        sg = _sigmoid(ag_ref[...].astype(F32))
        sgbuf[...] = sg
        abuf[pl.ds(0, HALO), :] = jnp.where(i > 0, _glu(hv_ref[...], hg_ref[...]), 0.0)
        abuf[pl.ds(HALO, tr), :] = av_ref[...].astype(F32) * sg
        _shift_copies(abuf, ash)
        _shift_copies(dbuf, dsh)

        for r0, cols in _conv_chunks(tr, c):
            rows = pl.ds(r0, CONV_ROWS)
            d_a = _tap_sum(cw_ref, dsh, r0, cols, kw - 1, -1, kw)
            sgc = sgbuf[rows, cols]
            out_ref[rows, cols] = (d_a * sgc).astype(BF16)
            out_ref[rows, pl.ds(c + cols.start, cols.size)] = (
                d_a * abuf[pl.ds(HALO + r0, CONV_ROWS), cols] * (1.0 - sgc)).astype(BF16)
        for _, cols in _conv_chunks(CONV_ROWS, c):
            for k in range(kw):
                acc = jnp.zeros((CONV_ROWS, cols.size), F32)
                for r0 in range(0, tr, CONV_ROWS):
                    acc = acc + (dbuf[pl.ds(r0, CONV_ROWS), cols]
                                 * _shifted_rows(ash, HALO - (kw - 1) + k, r0, cols))
                dcw_ref[pl.ds(k, 1), cols] += jnp.sum(acc, axis=0, keepdims=True)

    vec = _const((1, c))
    return _pcall(
        body, name=name,
        out_shape=[jax.ShapeDtypeStruct(dproj.shape, BF16)] + [jax.ShapeDtypeStruct((1, c), F32)] * 3
        + [jax.ShapeDtypeStruct((kw, c), F32)],
        grid=(nt,),
        in_specs=[pl.BlockSpec(memory_space=pl.ANY),
                  _col(tr, c, 0), _next_halo(per, n_halo, c, 0),
                  _col(tr, c, 0), _next_halo(per, n_halo, c, 0),
                  _col(tr, c, 0), _col(tr, c, 1), _prev_halo(per, c, 0), _prev_halo(per, c, 1),
                  _const((kw, c)), vec, vec],
        out_specs=[pl.BlockSpec((tr, 2 * c), lambda i: (i, 0)), vec, vec, vec, _const((kw, c))],
        scratch_shapes=[pltpu.VMEM((HALO + tr + SUBLANES, c), F32), pltpu.VMEM((HALO + tr + SUBLANES, c), F32),
                        pltpu.VMEM((tr, c), F32),
                        pltpu.VMEM((SUBLANES, HALO + tr, c), F32), pltpu.VMEM((SUBLANES, HALO + tr, c), F32)],
        input_output_aliases={0: 0},
        compiler_params=_params(("arbitrary",), 6 * _nbytes((tr, c), F32),
                                (12 + 2 * SUBLANES) * _nbytes((tr + HALO, c), F32)),
    )(dproj, dact, dact, ca, ca, proj, proj, proj, proj, cw, lg.reshape(1, c), lb.reshape(1, c))


def _branch_b_fwd(proj, cw, name):
    t = proj.shape[0]
    kw, c = cw.shape
    tr, nt, per = _seq_tiles(t)

    def body(sb_ref, sc_ref, sx_ref, hc_ref, hx_ref, cw_ref, o_ref, pbuf):
        i = pl.program_id(0)
        hp = hc_ref[...].astype(F32) * hx_ref[...].astype(F32)
        pbuf[pl.ds(0, HALO), :] = jnp.where(i > 0, hp, 0.0)
        pbuf[pl.ds(HALO, tr), :] = sc_ref[...].astype(F32) * sx_ref[...].astype(F32)
        u = jnp.zeros((tr, c), F32)
        for k in range(kw):
            u = u + cw_ref[pl.ds(k, 1), :] * pbuf[pl.ds(HALO - (kw - 1) + k, tr), :]
        o_ref[...] = (sb_ref[...].astype(F32) * u).astype(BF16)

    return _pcall(
        body, name=name,
        out_shape=jax.ShapeDtypeStruct((t, c), BF16),
        grid=(nt,),
        in_specs=[_col(tr, c, 2), _col(tr, c, 3), _col(tr, c, 4),
                  _prev_halo(per, c, 3), _prev_halo(per, c, 4), _const((kw, c))],
        out_specs=_col(tr, c, 0),
        scratch_shapes=[pltpu.VMEM((HALO + tr, c), F32)],
        compiler_params=_params(("arbitrary",), 4 * _nbytes((tr, c), F32), 6 * _nbytes((tr + HALO, c), F32)),
    )(proj, proj, proj, proj, proj, cw)


def _branch_b_bwd(dproj, dbin, proj, cw, name):
    t = proj.shape[0]
    kw, c = cw.shape
    tr, nt, per = _seq_tiles(t)
    n_halo = t // HALO

    def body(dproj_in, db_ref, dbh_ref, sb_ref, sbh_ref, sc_ref, sx_ref, hc_ref, hx_ref, cw_ref,
             out_ref, dcw_ref, pbuf, dubuf, res):
        del dproj_in
        i = pl.program_id(0)
        j = pl.program_id(1)

        @pl.when(j == 0)
        def _():
            sb = sb_ref[...].astype(F32)
            sc = sc_ref[...].astype(F32)
            sx = sx_ref[...].astype(F32)
            dbin_v = db_ref[...].astype(F32)
            hp = hc_ref[...].astype(F32) * hx_ref[...].astype(F32)
            pbuf[pl.ds(0, HALO), :] = jnp.where(i > 0, hp, 0.0)
            pbuf[pl.ds(HALO, tr), :] = sc * sx
            du = dbin_v * sb
            du_h = dbh_ref[...].astype(F32) * sbh_ref[...].astype(F32)
            dubuf[pl.ds(0, tr), :] = du
            dubuf[pl.ds(tr, HALO), :] = jnp.where(i < nt - 1, du_h, 0.0)

            @pl.when(i == 0)
            def _():
                dcw_ref[...] = jnp.zeros_like(dcw_ref)

            u = jnp.zeros((tr, c), F32)
            dp = jnp.zeros((tr, c), F32)
            for k in range(kw):
                shifted = pbuf[pl.ds(HALO - (kw - 1) + k, tr), :]
                u = u + cw_ref[pl.ds(k, 1), :] * shifted
                dp = dp + cw_ref[pl.ds(k, 1), :] * dubuf[pl.ds(kw - 1 - k, tr), :]
                dcw_ref[pl.ds(k, 1), :] += jnp.sum(du * shifted, axis=0, keepdims=True)
            res[0] = (dbin_v * u).astype(BF16)
            res[1] = (dp * sx).astype(BF16)
            res[2] = (dp * sc).astype(BF16)

        out_ref[...] = res[j]

    def colj(cb):
        return pl.BlockSpec((tr, c), lambda i, j: (i, cb))

    def prevj(cb):
        return pl.BlockSpec((HALO, c), lambda i, j: (jnp.maximum(i * per - 1, 0), cb))

    def nextj(cb):
        return pl.BlockSpec((HALO, c), lambda i, j: (jnp.minimum((i + 1) * per, n_halo - 1), cb))

    return _pcall(
        body, name=name,
        out_shape=[jax.ShapeDtypeStruct(dproj.shape, BF16), jax.ShapeDtypeStruct((kw, c), F32)],
        grid=(nt, 3),
        in_specs=[pl.BlockSpec(memory_space=pl.ANY),
                  colj(0), nextj(0), colj(2), nextj(2), colj(3), colj(4), prevj(3), prevj(4),
                  pl.BlockSpec((kw, c), lambda i, j: (0, 0))],
        out_specs=[pl.BlockSpec((tr, c), lambda i, j: (i, 2 + j)),
                   pl.BlockSpec((kw, c), lambda i, j: (0, 0))],
        scratch_shapes=[pltpu.VMEM((HALO + tr, c), F32), pltpu.VMEM((HALO + tr, c), F32),
                        pltpu.VMEM((3, tr, c), BF16)],
        input_output_aliases={0: 0},
        compiler_params=_params(("arbitrary", "arbitrary"), 6 * _nbytes((tr, c), F32),
                                10 * _nbytes((tr + HALO, c), F32)),
    )(dproj, dbin, dbin, proj, proj, proj, proj, proj, proj, cw)


def _softmax_rows(s):
    e = jnp.exp(s - jnp.max(s, axis=-1, keepdims=True))
    return e / jnp.sum(e, axis=-1, keepdims=True)


def _attn_fwd(proj, kv, name):
    t = proj.shape[0]
    m, c2 = kv.shape
    c = c2 // 2
    hd = c // N_HEADS
    ta = _tile(t, ATTN_TILE, SUBLANE_BF16)
    scale = hd ** -0.5

    def body(q_ref, kv_ref, o_ref):
        for h in range(N_HEADS):
            qh = q_ref[:, pl.ds(h * hd, hd)]
            kh = kv_ref[:, pl.ds(h * hd, hd)]
            vh = kv_ref[:, pl.ds(c + h * hd, hd)]
            s = lax.dot_general(qh, kh, (((1,), (1,)), ((), ())), preferred_element_type=F32) * scale
            p = _softmax_rows(s)
            o_ref[:, pl.ds(h * hd, hd)] = jnp.dot(p.astype(BF16), vh,
                                                  preferred_element_type=F32).astype(BF16)

    return _pcall(
        body, name=name,
        out_shape=jax.ShapeDtypeStruct((t, c), BF16),
        grid=(t // ta,),
        in_specs=[pl.BlockSpec((ta, c), lambda i: (i, 5)), _const((m, c2))],
        out_specs=pl.BlockSpec((ta, c), lambda i: (i, 0)),
        compiler_params=_params(("parallel",), 2 * _nbytes((ta, c), BF16) + _nbytes((m, c2), BF16),
                                8 * _nbytes((ta, m), F32)),
    )(proj, kv)


def _attn_bwd(dproj, d_o, proj, kv, name):
    t = proj.shape[0]
    m, c2 = kv.shape
    c = c2 // 2
    hd = c // N_HEADS
    ta = _tile(t, ATTN_TILE, SUBLANE_BF16)
    scale = hd ** -0.5

    def body(dproj_in, do_ref, q_ref, kv_ref, dq_ref, dkv_ref):
        del dproj_in

        @pl.when(pl.program_id(0) == 0)
        def _():
            dkv_ref[...] = jnp.zeros_like(dkv_ref)

        for h in range(N_HEADS):
            qh = q_ref[:, pl.ds(h * hd, hd)]
            kh = kv_ref[:, pl.ds(h * hd, hd)]
            vh = kv_ref[:, pl.ds(c + h * hd, hd)]
            doh = do_ref[:, pl.ds(h * hd, hd)]
            s = lax.dot_general(qh, kh, (((1,), (1,)), ((), ())), preferred_element_type=F32) * scale
            p = _softmax_rows(s)
            dp = lax.dot_general(doh, vh, (((1,), (1,)), ((), ())), preferred_element_type=F32)
            ds = (p * (dp - jnp.sum(dp * p, axis=-1, keepdims=True))).astype(BF16)
            dq_ref[:, pl.ds(h * hd, hd)] = (jnp.dot(ds, kh, preferred_element_type=F32) * scale).astype(BF16)
            dkv_ref[:, pl.ds(h * hd, hd)] += lax.dot_general(
                ds, qh, (((0,), (0,)), ((), ())), preferred_element_type=F32) * scale
            dkv_ref[:, pl.ds(c + h * hd, hd)] += lax.dot_general(
                p.astype(BF16), doh, (((0,), (0,)), ((), ())), preferred_element_type=F32)

    return _pcall(
        body, name=name,
        out_shape=[jax.ShapeDtypeStruct(dproj.shape, BF16), jax.ShapeDtypeStruct((m, c2), F32)],
        grid=(t // ta,),
        in_specs=[pl.BlockSpec(memory_space=pl.ANY),
                  pl.BlockSpec((ta, c), lambda i: (i, 0)), pl.BlockSpec((ta, c), lambda i: (i, 5)),
                  _const((m, c2))],
        out_specs=[pl.BlockSpec((ta, c), lambda i: (i, 5)), _const((m, c2))],
        input_output_aliases={0: 0},
        compiler_params=_params(("arbitrary",), 3 * _nbytes((ta, c), BF16) + 2 * _nbytes((m, c2), F32),
                                10 * _nbytes((ta, m), F32)),
    )(dproj, d_o, proj, kv)


def _merge_fwd(proj, ya, yb, yx, name):
    t, d = ya.shape
    tr = _tile(t, ROW_TILE, SUBLANE_BF16)

    def body(g_ref, ya_ref, yb_ref, yx_ref, o_ref):
        acc = jnp.zeros((tr, d), F32)
        for b, y_ref in enumerate((ya_ref, yb_ref, yx_ref)):
            acc = acc + _sigmoid(g_ref[:, pl.ds(b * d, d)].astype(F32)) * y_ref[...].astype(F32)
        o_ref[...] = acc.astype(BF16)

    row = pl.BlockSpec((tr, d), lambda i: (i, 0))
    return _pcall(
        body, name=name,
        out_shape=jax.ShapeDtypeStruct((t, d), BF16),
        grid=(t // tr,),
        in_specs=[pl.BlockSpec((tr, 3 * d), lambda i: (i, 1)), row, row, row],
        out_specs=row,
        compiler_params=_params(("parallel",), 7 * _nbytes((tr, d), BF16), 6 * _nbytes((tr, d), F32)),
    )(proj, ya, yb, yx)


def _merge_bwd(dmerged, proj, ya, yb, yx, name):
    t, d = ya.shape
    tr = _tile(t, ROW_TILE, SUBLANE_BF16)

    def body(dm_ref, g_ref, ya_ref, yb_ref, yx_ref, dg_ref, dya_ref, dyb_ref, dyx_ref):
        dm = dm_ref[...].astype(F32)
        for b, (y_ref, dy_ref) in enumerate(((ya_ref, dya_ref), (yb_ref, dyb_ref), (yx_ref, dyx_ref))):
            sg = _sigmoid(g_ref[:, pl.ds(b * d, d)].astype(F32))
            dg_ref[:, pl.ds(b * d, d)] = (dm * y_ref[...].astype(F32) * sg * (1.0 - sg)).astype(BF16)
            dy_ref[...] = (dm * sg).astype(BF16)

    row = pl.BlockSpec((tr, d), lambda i: (i, 0))
    gates = pl.BlockSpec((tr, 3 * d), lambda i: (i, 1))
    return _pcall(
        body, name=name,
        out_shape=[jax.ShapeDtypeStruct(proj.shape, BF16)] + [jax.ShapeDtypeStruct((t, d), BF16)] * 3,
        grid=(t // tr,),
        in_specs=[row, gates, row, row, row],
        out_specs=[gates, row, row, row],
        compiler_params=_params(("parallel",), 14 * _nbytes((tr, d), BF16), 8 * _nbytes((tr, d), F32)),
    )(dmerged, proj, ya, yb, yx)


def _as3(a):
    return a.reshape((-1,) + a.shape[-2:])


def _ew_call(fn, ins, n_out, out_dtypes, name, batch=None, prev=()):
    b, r, c = ins[0].shape
    lo, hi = batch if batch is not None else (0, b)
    tr = _ew_rows(r, c, len(ins) + n_out)
    n_in = len(ins) + len(prev)

    def body(*refs):
        outs = fn(*[ref[...] for ref in refs[:len(ins)]])
        for ref, val in zip(refs[n_in:], outs):
            ref[...] = val.astype(ref.dtype)

    spec = pl.BlockSpec((None, tr, c), lambda i, j: (i + lo, j, 0))
    return _pcall(
        body, name=name,
        out_shape=[jax.ShapeDtypeStruct((b, r, c), dt) for dt in out_dtypes],
        grid=(hi - lo, r // tr),
        in_specs=[spec] * len(ins) + [pl.BlockSpec(memory_space=pl.ANY)] * len(prev), out_specs=[spec] * n_out,
        input_output_aliases={len(ins) + k: k for k in range(len(prev))},
        compiler_params=_params(("parallel", "parallel"), (len(ins) + n_out) * _nbytes((tr, c), F32),
                                6 * _nbytes((tr, c), F32)),
    )(*ins, *prev)


def _my_chip():
    return 2 * lax.axis_index("x") + lax.axis_index("y")


def _my_core():
    return lax.axis_index("c")


def _cast_to_slot(w, l, name, after=()):
    _, r, cs = w.shape
    tr = _ew_rows(r, cs, 2)

    def body(w_ref, *rest):
        rest[-1][...] = w_ref[...].astype(BF16)

    return _pcall(
        body, name=name,
        out_shape=jax.ShapeDtypeStruct((N_CHIPS, r, cs), BF16),
        grid=(r // tr,),
        in_specs=[pl.BlockSpec((None, tr, cs), lambda i: (l, i, 0))]
        + [pl.BlockSpec(memory_space=pl.ANY)] * len(after),
        out_specs=pl.BlockSpec((None, tr, cs), lambda i: (_my_chip(), i, 0)),
        compiler_params=_params(("parallel",), 2 * _nbytes((tr, cs), F32)),
    )(w, *after)


def _add_half(part, theirs, name):
    _, rh, cs = theirs.shape
    tr = _ew_rows(rh, cs, 3)
    nrb = rh // tr

    def body(a_ref, b_ref, o_ref):
        o_ref[...] = (a_ref[...].astype(F32) + b_ref[...].astype(F32)).astype(BF16)

    half = pl.BlockSpec((None, tr, cs), lambda j, i: (j, i, 0))
    return _pcall(
        body, name=name,
        out_shape=jax.ShapeDtypeStruct((N_CHIPS, rh, cs), BF16),
        grid=(N_CHIPS, nrb),
        in_specs=[pl.BlockSpec((None, tr, cs), lambda j, i: (j, _my_core() * nrb + i, 0)), half],
        out_specs=half,
        compiler_params=_params(("parallel", "parallel"), 3 * _nbytes((tr, cs), F32)),
    )(part, theirs)


def _sum_chips(chip_sum, q, total, l, nl, name):
    _, rh, cs = q.shape
    tr = _ew_rows(rh, cs, 5)
    nrb = rh // tr

    def body(own_ref, q1_ref, q2_ref, q3_ref, *rest):
        acc = own_ref[...].astype(F32)
        for ref in (q1_ref, q2_ref, q3_ref):
            acc = acc + ref[...].astype(F32)
        rest[-1][...] = acc

    def slot(k):
        return pl.BlockSpec((None, tr, cs), lambda i: ((_my_chip() + k) % N_CHIPS, i, 0))

    in_specs = [slot(0), slot(1), slot(2), slot(3)]
    operands = [chip_sum, q, q, q]
    aliases = {}
    if total is not None:
        in_specs.append(HBM_SPEC)
        operands.append(total)
        aliases = {4: 0}
    return _pcall(
        body, name=name,
        out_shape=jax.ShapeDtypeStruct((nl, 2 * rh, cs), F32),
        grid=(nrb,),
        in_specs=in_specs,
        out_specs=pl.BlockSpec((None, tr, cs), lambda i: (l, _my_core() * nrb + i, 0)),
        input_output_aliases=aliases,
        compiler_params=_params(("parallel",), 5 * _nbytes((tr, cs), F32)),
    )(*operands)


def _adamw(w, g, m, v, name, batch=None, prev=()):
    shape = w.shape
    c1 = 1.0 - ADAM_B1 ** ADAM_STEP
    c2 = 1.0 - ADAM_B2 ** ADAM_STEP

    def fn(wv, gv, mv, vv):
        mn = ADAM_B1 * mv + (1.0 - ADAM_B1) * gv
        vn = ADAM_B2 * vv + (1.0 - ADAM_B2) * (gv * gv)
        delta = -ADAM_LR * ((mn / c1) / (jnp.sqrt(vn / c2) + ADAM_EPS) + ADAM_WD * wv)
        return delta, mn, vn

    outs = _ew_call(fn, [_as3(w), _as3(g), _as3(m), _as3(v)], 3, [F32] * 3, name,
                    batch=batch, prev=[_as3(p) for p in prev])
    return [o.reshape(shape) for o in outs]


def _sum_leading(q, name):
    b, nj, r, c = q.shape
    tr = _ew_rows(r, c, nj + 1)

    def body(q_ref, o_ref):
        acc = q_ref[0].astype(F32)
        for j in range(1, nj):
            acc = acc + q_ref[j].astype(F32)
        o_ref[...] = acc

    return _pcall(
        body, name=name,
        out_shape=jax.ShapeDtypeStruct((b, r, c), F32),
        grid=(b, r // tr),
        in_specs=[pl.BlockSpec((None, nj, tr, c), lambda i, j: (i, 0, j, 0))],
        out_specs=pl.BlockSpec((None, tr, c), lambda i, j: (i, j, 0)),
        compiler_params=_params(("parallel", "parallel"), (nj + 1) * _nbytes((tr, c), F32),
                                2 * _nbytes((tr, c), F32)),
    )(q)


HBM_SPEC = pl.BlockSpec(memory_space=pl.ANY)


def _place():
    x, y, c = lax.axis_index("x"), lax.axis_index("y"), lax.axis_index("c")
    peers = [(1 - x, y), (x, 1 - y), (1 - x, 1 - y)]
    return x, y, c, 2 * x + y, peers


def _remote(src, dst, send_sem, recv_sem, dev):
    return pltpu.make_async_remote_copy(src_ref=src, dst_ref=dst, send_sem=send_sem, recv_sem=recv_sem,
                                        device_id=dev, device_id_type=MESH)


IN_HBM = pl.BlockSpec(memory_space=pltpu.HBM)
IN_SEM = pl.BlockSpec(memory_space=pltpu.SEMAPHORE)
DATAFLOW = pltpu.SideEffectType.DATAFLOW_SIDE_EFFECTING
TOKEN = jax.ShapeDtypeStruct((8, LANE), F32)


def _in_hbm(arrays):
    return [pltpu.with_memory_space_constraint(a, pltpu.HBM) for a in arrays]


def _half_rows(ref, c):
    rh = ref.shape[1] // 2
    return pl.ds(c * rh, rh)


def _split_start(body_copies, passed, landing, name, per_array=3, after=()):
    n_pass, n_land = len(passed), len(landing)
    n_arr = n_pass + n_land
    n = per_array * n_pass
    n_in = n_arr + len(after)

    def body(*refs):
        ins = refs[:n_arr]
        send, recv = refs[n_in], refs[n_in + 1]
        token = refs[-1]
        for src, dst, _, s_sem, r_sem, dev in body_copies(ins[:n_pass], ins[n_pass:], send, recv):
            _remote(src, dst, s_sem, r_sem, dev).start()
        token[...] = jnp.zeros_like(token)

    arrays = list(passed) + list(landing)
    return pl.pallas_call(
        body, name=name,
        out_shape=(pltpu.SemaphoreType.DMA((n,)), pltpu.SemaphoreType.DMA((n,)),
                   *[pltpu.HBM(a.shape, a.dtype) for a in arrays], TOKEN),
        in_specs=[IN_HBM] * n_arr + [HBM_SPEC] * len(after),
        out_specs=(IN_SEM, IN_SEM, *[IN_HBM] * n_arr, pl.BlockSpec(memory_space=pltpu.VMEM)),
        input_output_aliases={i: 2 + i for i in range(n_arr)},
        compiler_params=pltpu.CompilerParams(has_side_effects=DATAFLOW),
    )(*_in_hbm(arrays), *after)


def _split_wait(body_copies, send, recv, passed, landing, after, name):
    n_pass, n_land = len(passed), len(landing)
    n_arr = n_pass + n_land

    def body(*refs):
        ins = refs[:n_arr]
        send_ref, recv_ref = refs[n_arr], refs[n_arr + 1]
        for src, _, landed, s_sem, r_sem, dev in body_copies(ins[:n_pass], ins[n_pass:], send_ref, recv_ref):
            cp = _remote(src, landed, s_sem, r_sem, dev)
            cp.wait_send()
            cp.wait_recv()

    arrays = list(passed) + list(landing)
    return pl.pallas_call(
        body, name=name,
        out_shape=tuple(pltpu.HBM(a.shape, a.dtype) for a in arrays),
        in_specs=[IN_HBM] * n_arr + [IN_SEM, IN_SEM] + [HBM_SPEC] * len(after),
        out_specs=tuple([IN_HBM] * n_arr),
        input_output_aliases={i: i for i in range(n_arr)},
        compiler_params=pltpu.CompilerParams(has_side_effects=DATAFLOW),
    )(*arrays, send, recv, *after)


def _gather_copies(slots, _, send, recv):
    _, _, c, me, peers = _place()
    copies = []
    for a, ref in enumerate(slots):
        rows = _half_rows(ref, c)
        own = ref.at[me, rows, :]
        for j, (px, py) in enumerate(peers):
            k = 3 * a + j
            copies.append((own, own, ref.at[2 * px + py, rows, :], send.at[k], recv.at[k], (px, py, c)))
    return copies


def _exchange_copies(chip_sums, landing, send, recv):
    _, _, c, me, peers = _place()
    copies = []
    for a, (s_ref, q_ref) in enumerate(zip(chip_sums, landing)):
        for j, (px, py) in enumerate(peers):
            k = 3 * a + j
            copies.append((s_ref.at[2 * px + py], q_ref.at[me], q_ref.at[2 * px + py],
                           send.at[k], recv.at[k], (px, py, c)))
    return copies


def _gather_start(slots, name):
    out = _split_start(_gather_copies, slots, [], name)
    return out[0], out[1], list(out[2:-1]), out[-1]


def _gather_wait(send, recv, slots, after, name):
    return list(_split_wait(_gather_copies, send, recv, slots, [], after, name))


def _forward_copies(slots, _, send, recv):
    x, y, c, _, peers = _place()
    copies = []
    for a, ref in enumerate(slots):
        for j, (px, py) in enumerate(peers):
            k = 3 * a + j
            mine = ref.at[2 * px + py, _half_rows(ref, c), :]
            theirs = ref.at[2 * px + py, _half_rows(ref, 1 - c), :]
            copies.append((mine, mine, theirs, send.at[k], recv.at[k], (x, y, 1 - c)))
    return copies


def _forward_start(slots, name):
    out = _split_start(_forward_copies, slots, [], name)
    return out[0], out[1], list(out[2:-1]), out[-1]


def _forward_wait(send, recv, slots, after, name):
    return list(_split_wait(_forward_copies, send, recv, slots, [], after, name))


def _gather_small(v, name):
    def body(v_ref, o_ref, send, recv, lsem):
        x, y, c, me, peers = _place()
        local = pltpu.make_async_copy(v_ref, o_ref.at[me], lsem)
        local.start()
        sends = [_remote(v_ref, o_ref.at[me], send.at[j], recv.at[j], (px, py, c))
                 for j, (px, py) in enumerate(peers)]
        for cp in sends:
            cp.start()
        for j, (px, py) in enumerate(peers):
            _remote(v_ref, o_ref.at[2 * px + py], send.at[j], recv.at[j], (px, py, c)).wait_recv()
        for cp in sends:
            cp.wait_send()
        local.wait()

    return _pcall(
        body, name=name,
        out_shape=jax.ShapeDtypeStruct((N_CHIPS,) + v.shape, v.dtype),
        in_specs=[HBM_SPEC], out_specs=HBM_SPEC,
        scratch_shapes=[pltpu.SemaphoreType.DMA((3,)), pltpu.SemaphoreType.DMA((3,)), pltpu.SemaphoreType.DMA],
    )(v)


def _gather_all(v, name):
    def body(v_ref, o_ref, send, recv, lsem):
        x, y, c = lax.axis_index("x"), lax.axis_index("y"), lax.axis_index("c")
        me = 4 * x + 2 * y + c
        local = pltpu.make_async_copy(v_ref, o_ref.at[me], lsem)
        local.start()
        flips = [(fx, fy, fc) for fx in (0, 1) for fy in (0, 1) for fc in (0, 1)][1:]
        peers = [(x ^ fx, y ^ fy, c ^ fc) for fx, fy, fc in flips]
        sends = [_remote(v_ref, o_ref.at[me], send.at[k], recv.at[k], dev) for k, dev in enumerate(peers)]
        for cp in sends:
            cp.start()
        for k, (px, py, pc) in enumerate(peers):
            _remote(v_ref, o_ref.at[4 * px + 2 * py + pc], send.at[k], recv.at[k], (px, py, pc)).wait_recv()
        for cp in sends:
            cp.wait_send()
        local.wait()

    return _pcall(
        body, name=name,
        out_shape=jax.ShapeDtypeStruct((8,) + v.shape, v.dtype),
        in_specs=[HBM_SPEC], out_specs=HBM_SPEC,
        scratch_shapes=[pltpu.SemaphoreType.DMA((7,)), pltpu.SemaphoreType.DMA((7,)), pltpu.SemaphoreType.DMA],
    )(v)


def _send_copies(parts, landing, send, recv):
    x, y, c, _, _ = _place()
    return [(p.at[:, _half_rows(p, 1 - c), :], t, t, send.at[a], recv.at[a], (x, y, 1 - c))
            for a, (p, t) in enumerate(zip(parts, landing))]


def _send_start(parts, name, after=()):
    na = len(parts)
    landing = [lax.empty((N_CHIPS, p.shape[1] // 2, p.shape[2]), p.dtype) for p in parts]
    out = _split_start(_send_copies, parts, landing, name, per_array=1, after=after)
    return out[0], out[1], list(out[2:2 + na]), list(out[2 + na:2 + 2 * na]), out[-1]


def _join_copies_of(layer):
    def copies(reds, _, send, recv):
        x, y, c, _, _ = _place()
        out = []
        for a, ref in enumerate(reds):
            rh = ref.shape[1] // 2
            mine = ref.at[layer, pl.ds(c * rh, rh), :]
            other = ref.at[layer, pl.ds((1 - c) * rh, rh), :]
            out.append((mine, mine, other, send.at[a], recv.at[a], (x, y, 1 - c)))
        return out

    return copies


def _join_start(reds, layer, name):
    out = _split_start(_join_copies_of(layer), reds, [], name, per_array=1)
    return out[0], out[1], list(out[2:-1]), out[-1]


def _join_wait(send, recv, reds, layer, after, name):
    return list(_split_wait(_join_copies_of(layer), send, recv, reds, [], after, name))


def _send_wait(send, recv, parts, landing, after, name):
    na = len(parts)
    out = _split_wait(_send_copies, send, recv, parts, landing, after, name)
    return list(out[:na]), list(out[na:])


def _exchange_start(chip_sums, name):
    na = len(chip_sums)
    landing = [lax.empty(s.shape, s.dtype) for s in chip_sums]
    out = _split_start(_exchange_copies, chip_sums, landing, name)
    return out[0], out[1], list(out[2:2 + na]), list(out[2 + na:2 + 2 * na]), out[-1]


def _exchange_wait(send, recv, chip_sums, landing, after, name):
    na = len(chip_sums)
    out = _split_wait(_exchange_copies, send, recv, chip_sums, landing, after, name)
    return list(out[:na]), list(out[na:])


def _join_halves(reds, lo, hi, name, after=()):
    na = len(reds)
    n_in = na + len(after)
    layers = pl.ds(lo, hi - lo)

    def body(*refs):
        outs = refs[n_in:n_in + na]
        send, recv = refs[n_in + na:]
        x, y, c, _, _ = _place()
        sends = []
        for a, ref in enumerate(outs):
            rh = ref.shape[1] // 2
            mine = ref.at[layers, pl.ds(c * rh, rh), :]
            sends.append(_remote(mine, mine, send.at[a], recv.at[a], (x, y, 1 - c)))
            sends[-1].start()
        for a, ref in enumerate(outs):
            rh = ref.shape[1] // 2
            other = ref.at[layers, pl.ds((1 - c) * rh, rh), :]
            _remote(other, other, send.at[a], recv.at[a], (x, y, 1 - c)).wait_recv()
        for cp in sends:
            cp.wait_send()

    return _pcall(
        body, name=name,
        out_shape=[jax.ShapeDtypeStruct(r.shape, r.dtype) for r in reds],
        in_specs=[HBM_SPEC] * n_in, out_specs=[HBM_SPEC] * na,
        input_output_aliases={a: a for a in range(na)},
        scratch_shapes=[pltpu.SemaphoreType.DMA((na,)), pltpu.SemaphoreType.DMA((na,))],
    )(*reds, *after)


COL_SHARDED = ("w_in", "w_a_out", "w_b_out", "w_x_out", "w_up")
ROW_SHARDED = ("w_kv", "w_o", "w_down")
BIG = COL_SHARDED + ROW_SHARDED
SMALL_REPLICATED = ("g_mix_pre", "conv_a_b", "ln_a_g", "ln_a_b", "g_mem", "g_mix_post", "g_mlp_pre", "g_mlp_post")
SMALL_SHARDED = ("conv_a_w", "conv_b_w")
WEIGHT_ORDER = ("g_mix_pre", "w_in", "conv_a_w", "conv_a_b", "ln_a_g", "ln_a_b", "w_a_out", "conv_b_w", "w_b_out",
                "g_mem", "w_kv", "w_x_out", "w_o", "g_mix_post", "g_mlp_pre", "w_up", "w_down", "g_mlp_post")


def _pack_rows(arrays, width):
    rows = []
    for a in arrays:
        r = a.reshape(-1, width)
        rows.append(jnp.pad(r, ((0, (-r.shape[0]) % 8), (0, 0))))
    return jnp.concatenate(rows, axis=0)


def _unpack_rows(packed, like, width):
    out, at = [], 0
    for a in like:
        n = a.size // width
        out.append(packed[at:at + n].reshape(a.shape))
        at += n + (-n) % 8
    return out


def _step(w, m, v, x, mem, target):
    nl = w["w_in"].shape[0]
    d = x.shape[1]
    c = w["conv_a_b"].shape[1]
    ka = w["conv_a_w"].shape[1]

    conv_pack = jnp.concatenate([w["conv_a_w"], w["conv_b_w"]], axis=1)
    conv_rows = conv_pack.shape[1]
    conv_pack = jnp.pad(conv_pack, ((0, 0), (0, (-conv_rows) % 8), (0, 0)))
    conv_all = _gather_small(conv_pack, "gather_conv")
    slots = {name: [_cast_to_slot(w[name], 0, "cast_" + name, after=[conv_all])] for name in BIG}
    conv_all = jnp.moveaxis(conv_all, 0, 2).reshape(nl, conv_pack.shape[1], c)
    cw_a, cw_b = conv_all[:, :ka], conv_all[:, ka:conv_rows]

    def start_gather(l, names=BIG, tag=""):
        return _gather_start([slots[name][l] for name in names], "gather_start_%d%s" % (l, tag))

    def start_forward(l, pending, after, tag=""):
        send, recv, thru, _ = pending
        landed = _gather_wait(send, recv, thru, after, "gather_wait_%d%s" % (l, tag))
        return _forward_start(landed, "gather_fwd_start_%d%s" % (l, tag))

    def finish_forward(l, passing, after, names=BIG, tag=""):
        send, recv, thru, _ = passing
        arrays = _forward_wait(send, recv, thru, after, "gather_fwd_wait_%d%s" % (l, tag))
        return {name: (a.reshape(1, -1, a.shape[-1]) if name in ROW_SHARDED else a)
                for name, a in zip(names, arrays)}

    first, rest = BIG[:1], BIG[1:]
    pending = start_gather(0, first, "a")
    pending_rest = start_gather(0, rest, "b")
    for name in BIG:
        slots[name] += [_cast_to_slot(w[name], l, "cast_" + name, after=[pending_rest[3]]) for l in range(1, nl)]
    (h,) = _norm_fwd(x, "norm_first", g_next=w["g_mix_pre"][0])
    passing = start_forward(0, pending, [h] + [s for name in BIG for s in slots[name][1:]], "a")
    full = [finish_forward(0, passing, [passing[3]], first, "a")]

    saved = []
    xl = x
    dy = loss_cols = None
    for l in range(nl):
        s = {"x": xl, "h": h}
        fw = full[l]
        token = None
        if l + 1 < nl:
            pending = start_gather(l + 1)
            token = pending[3]
        proj = _mm_nn(h, fw["w_in"], BF16, "mm_in", after=token)
        if l == 0:
            passing = start_forward(0, pending_rest, [proj], "b")
            fw.update(finish_forward(0, passing, [passing[3]], rest, "b"))
        ca, act_a = _branch_a_fwd(proj, cw_a[l], w["conv_a_b"][l], w["ln_a_g"][l], w["ln_a_b"][l], "branch_a_fwd")
        b_in = _branch_b_fwd(proj, cw_b[l], "branch_b_fwd")
        (mem_n,) = _norm_fwd(mem, "norm_mem", g_next=w["g_mem"][l])
        kv = _mm_nn(mem_n, fw["w_kv"], BF16, "mm_kv")
        att = _attn_fwd(proj, kv, "attn_fwd")
        ya = _mm_nn(act_a, fw["w_a_out"], BF16, "mm_a_out")
        yb = _mm_nn(b_in, fw["w_b_out"], BF16, "mm_b_out")
        yx = _mm_nn(att, fw["w_x_out"], BF16, "mm_x_out")
        merged = _merge_fwd(proj, ya, yb, yx, "merge_fwd")
        z = _mm_nn(merged, fw["w_o"], BF16, "mm_o")
        x1, h2 = _norm_fwd(xl, "norm_mid", z=z, g_post=w["g_mix_post"][l], g_next=w["g_mlp_pre"][l])
        token = None
        if l + 1 < nl:
            passing = start_forward(l + 1, pending, [h2])
            token = passing[3]
        up = _mm_nn(h2, fw["w_up"], BF16, "mm_up", after=token)
        f = _mm_nn(up, fw["w_down"], BF16, "mm_down", a_act="relu2")
        s.update(proj=proj, ca=ca, act_a=act_a, b_in=b_in, mem_n=mem_n, kv=kv, att=att, ya=ya, yb=yb, yx=yx,
                 merged=merged, z=z, x1=x1, h2=h2, up=up, f=f)
        saved.append(s)
        if l + 1 < nl:
            full.append(finish_forward(l + 1, passing, [f]))
        if l + 1 < nl:
            xl, h = _norm_fwd(x1, "norm_mid", z=f, g_post=w["g_mlp_post"][l], g_next=w["g_mix_pre"][l + 1])
        else:
            dy, loss_cols = _norm_fwd(x1, "norm_loss", z=f, g_post=w["g_mlp_post"][l], target=target)

    part = {name: [None] * nl for name in BIG}
    small = {name: [None] * nl for name in SMALL_REPLICATED + SMALL_SHARDED}
    dxo = dy
    d_f, small["g_mlp_post"][nl - 1] = _norm_bwd("norm_bwd_top", dxo=dy,
                                                 post=(saved[-1]["f"], w["g_mlp_post"][nl - 1]))
    grad_x = None
    totals = {name: None for name in BIG}

    def start_send(l, after):
        return _send_start([part[name][l] for name in BIG], "rs_send_start_%d" % l, after=after)

    def start_exchange(l, sending, after):
        send, recv, parts, landing, _ = sending
        parts, theirs = _send_wait(send, recv, parts, landing, after, "rs_send_wait_%d" % l)
        chip_sums = [_add_half(p, t, "rs_add_" + name) for name, p, t in zip(BIG, parts, theirs)]
        return _exchange_start(chip_sums, "rs_xchg_start_%d" % l)

    joining = {}

    def finish_join(after):
        if joining:
            send, recv, reds, layer = joining.pop("pending")
            totals.update(zip(BIG, _join_wait(send, recv, reds, layer, after, "rs_join_wait_%d" % layer)))

    def finish_exchange(l, pending, after):
        send, recv, chip_sums, landing, _ = pending
        chip_sums, q = _exchange_wait(send, recv, chip_sums, landing, after, "rs_xchg_wait_%d" % l)
        finish_join(after)
        for name, own, got in zip(BIG, chip_sums, q):
            totals[name] = _sum_chips(own, got, totals[name], l, nl, "rs_sum_" + name)
        if l == 0:
            return None
        send, recv, reds, token = _join_start([totals[name] for name in BIG], l, "rs_join_start_%d" % l)
        joining["pending"] = (send, recv, reds, l)
        totals.update(zip(BIG, reds))
        return token

    sending = exchange = None
    for l in reversed(range(nl)):
        s = saved[l]
        fw = full[l]
        d_up = _mm_nt(d_f, fw["w_down"], BF16, "mm_down_dx", relu2_of=s["up"],
                      after=None if sending is None else sending[4])
        part["w_down"][l] = _mm_tn(s["up"], d_f, 1, "mm_down_dw", a_act="relu2").reshape(N_CHIPS, -1, d)
        d_h2 = _mm_nt(d_up, fw["w_up"], BF16, "mm_up_dx")
        part["w_up"][l] = _mm_tn(s["h2"], d_up, N_CHIPS, "mm_up_dw")
        dx1, d_z, small["g_mlp_pre"][l], small["g_mix_post"][l] = _norm_bwd(
            "norm_bwd_mid", dxo=dxo, pre=(d_h2, s["x1"], w["g_mlp_pre"][l]), post=(s["z"], w["g_mix_post"][l]))
        if sending is not None:
            exchange = start_exchange(l + 1, sending, [d_z])
        d_merged = _mm_nt(d_z, fw["w_o"], BF16, "mm_o_dx", after=None if exchange is None else exchange[4])
        part["w_o"][l] = _mm_tn(s["merged"], d_z, 1, "mm_o_dw").reshape(N_CHIPS, -1, d)
        dproj, d_ya, d_yb, d_yx = _merge_bwd(d_merged, s["proj"], s["ya"], s["yb"], s["yx"], "merge_bwd")
        d_act_a = _mm_nt(d_ya, fw["w_a_out"], BF16, "mm_a_out_dx")
        part["w_a_out"][l] = _mm_tn(s["act_a"], d_ya, N_CHIPS, "mm_a_out_dw")
        d_b_in = _mm_nt(d_yb, fw["w_b_out"], BF16, "mm_b_out_dx")
        part["w_b_out"][l] = _mm_tn(s["b_in"], d_yb, N_CHIPS, "mm_b_out_dw")
        d_att = _mm_nt(d_yx, fw["w_x_out"], BF16, "mm_x_out_dx")
        part["w_x_out"][l] = _mm_tn(s["att"], d_yx, N_CHIPS, "mm_x_out_dw")
        dproj, small["ln_a_g"][l], small["ln_a_b"][l], small["conv_a_b"][l], small["conv_a_w"][l] = _branch_a_bwd(
            dproj, d_act_a, s["ca"], s["proj"], cw_a[l], w["ln_a_g"][l], w["ln_a_b"][l], "branch_a_bwd")
        dproj, small["conv_b_w"][l] = _branch_b_bwd(dproj, d_b_in, s["proj"], cw_b[l], "branch_b_bwd")
        dproj, d_kv = _attn_bwd(dproj, d_att, s["proj"], s["kv"], "attn_bwd")
        part["w_kv"][l] = _mm_tn(s["mem_n"], d_kv, 1, "mm_kv_dw").reshape(N_CHIPS, -1, 2 * c)
        d_mem_n = _mm_nt(d_kv, fw["w_kv"], F32, "mm_kv_dx")
        (small["g_mem"][l],) = _norm_bwd("norm_bwd_mem", pre=(d_mem_n, mem, w["g_mem"][l]), want_dx=False)
        d_h = _mm_nt(dproj, fw["w_in"], BF16, "mm_in_dx")
        part["w_in"][l] = _mm_tn(s["h"], dproj, N_CHIPS, "mm_in_dw")
        join_token = finish_exchange(l + 1, exchange, [d_h]) if exchange is not None else None
        sending = start_send(l, [] if join_token is None else [join_token])
        if l > 0:
            dxo, d_f, small["g_mix_pre"][l], small["g_mlp_post"][l - 1] = _norm_bwd(
                "norm_bwd_mid", dxo=dx1, pre=(d_h, s["x"], w["g_mix_pre"][l]),
                post=(saved[l - 1]["f"], w["g_mlp_post"][l - 1]))
        else:
            grad_x, small["g_mix_pre"][0] = _norm_bwd("norm_bwd_last", dxo=dx1,
                                                      pre=(d_h, s["x"], w["g_mix_pre"][0]))
    exchange = start_exchange(0, sending, [grad_x])

    grads, delta, new_m, new_v = {}, {}, {}, {}
    small_names = SMALL_REPLICATED + SMALL_SHARDED
    stacked = [jnp.stack(small[n]) for n in small_names]
    packed = _pack_rows(stacked, c)
    total = _sum_leading(_gather_all(packed, "gather_small_grads")[None], "sum_small_grads")[0]
    reduced = dict(zip(small_names, _unpack_rows(total, stacked, c)))
    chip = 2 * lax.axis_index("x") + lax.axis_index("y")
    cs = c // N_CHIPS
    for name in SMALL_SHARDED:
        grads[name] = lax.dynamic_slice_in_dim(reduced[name], chip * cs, cs, axis=2)
    for name in SMALL_REPLICATED:
        grads[name] = reduced[name].reshape(w[name].shape)
    for group, width in ((SMALL_REPLICATED, c), (SMALL_SHARDED, cs)):
        packs = [_pack_rows([src[n] for n in group], width)[None] for src in (w, grads, m, v)]
        outs = _adamw(*packs, "adamw_small_%d" % width)
        for dst, out in zip((delta, new_m, new_v), outs):
            dst.update(zip(group, _unpack_rows(out[0], [w[n] for n in group], width)))

    state = {name: () for name in BIG}
    if nl > 1:
        finish_join([exchange[4]])
        for name in BIG:
            state[name] = _adamw(w[name], totals[name], m[name], v[name], "adamw_" + name, batch=(1, nl))
    finish_exchange(0, exchange, [state[name][0] for name in BIG] if nl > 1 else [grad_x])
    grads.update(zip(BIG, _join_halves([totals[name] for name in BIG], 0, 1, "rs_join_first")))
    for name in BIG:
        delta[name], new_m[name], new_v[name] = _adamw(w[name], grads[name], m[name], v[name], "adamw_" + name,
                                                       batch=(0, 1), prev=state[name])

    loss = lax.psum(0.5 * jnp.sum(loss_cols) / d, ("x", "y", "c"))
    return loss, grad_x, grads, delta, new_m, new_v


def kernel(x, mem, g_mix_pre, w_in, conv_a_w, conv_a_b, ln_a_g, ln_a_b, w_a_out, conv_b_w, w_b_out, g_mem, w_kv, w_x_out, w_o, g_mix_post, g_mlp_pre, w_up, w_down, g_mlp_post, loss_target, m_g_mix_pre, m_w_in, m_conv_a_w, m_conv_a_b, m_ln_a_g, m_ln_a_b, m_w_a_out, m_conv_b_w, m_w_b_out, m_g_mem, m_w_kv, m_w_x_out, m_w_o, m_g_mix_post, m_g_mlp_pre, m_w_up, m_w_down, m_g_mlp_post, v_g_mix_pre, v_w_in, v_conv_a_w, v_conv_a_b, v_ln_a_g, v_ln_a_b, v_w_a_out, v_conv_b_w, v_w_b_out, v_g_mem, v_w_kv, v_w_x_out, v_w_o, v_g_mix_post, v_g_mlp_pre, v_w_up, v_w_down, v_g_mlp_post):
    w = dict(g_mix_pre=g_mix_pre, w_in=w_in, conv_a_w=conv_a_w, conv_a_b=conv_a_b, ln_a_g=ln_a_g, ln_a_b=ln_a_b,
             w_a_out=w_a_out, conv_b_w=conv_b_w, w_b_out=w_b_out, g_mem=g_mem, w_kv=w_kv, w_x_out=w_x_out, w_o=w_o,
             g_mix_post=g_mix_post, g_mlp_pre=g_mlp_pre, w_up=w_up, w_down=w_down, g_mlp_post=g_mlp_post)
    m = dict(g_mix_pre=m_g_mix_pre, w_in=m_w_in, conv_a_w=m_conv_a_w, conv_a_b=m_conv_a_b, ln_a_g=m_ln_a_g,
             ln_a_b=m_ln_a_b, w_a_out=m_w_a_out, conv_b_w=m_conv_b_w, w_b_out=m_w_b_out, g_mem=m_g_mem, w_kv=m_w_kv,
             w_x_out=m_w_x_out, w_o=m_w_o, g_mix_post=m_g_mix_post, g_mlp_pre=m_g_mlp_pre, w_up=m_w_up,
             w_down=m_w_down, g_mlp_post=m_g_mlp_post)
    v = dict(g_mix_pre=v_g_mix_pre, w_in=v_w_in, conv_a_w=v_conv_a_w, conv_a_b=v_conv_a_b, ln_a_g=v_ln_a_g,
             ln_a_b=v_ln_a_b, w_a_out=v_w_a_out, conv_b_w=v_conv_b_w, w_b_out=v_w_b_out, g_mem=v_g_mem, w_kv=v_w_kv,
             w_x_out=v_w_x_out, w_o=v_w_o, g_mix_post=v_g_mix_post, g_mlp_pre=v_g_mlp_pre, w_up=v_w_up,
             w_down=v_w_down, g_mlp_post=v_g_mlp_post)
    loss, grad_x, grads, delta, new_m, new_v = _step(w, m, v, x[0], mem[0], loss_target[0])
    out = [loss, grad_x[None]]
    for group in (grads, delta, new_m, new_v):
        out += [group[n] for n in WEIGHT_ORDER]
    return tuple(out)
```

```python
import functools

import jax
import jax.numpy as jnp
from jax import lax
from jax.experimental import pallas as pl
from jax.experimental.pallas import tpu as pltpu

F32 = jnp.float32
BF16 = jnp.bfloat16
MESH = pl.DeviceIdType.MESH

NORM_EPS = 1e-6
N_HEADS = 4
ADAM_LR = 0.001
ADAM_B1 = 0.9
ADAM_B2 = 0.999
ADAM_EPS = 1e-08
ADAM_WD = 0.01
ADAM_STEP = 10

N_CHIPS = 4
V7X_VMEM_BYTES = 64 * 1024 * 1024
VMEM_CAP = V7X_VMEM_BYTES - 8 * 1024 * 1024
LANE = 128
SUBLANES = 8
SUBLANE_BF16 = 16
HALO = 32
ROW_TILE = 256
CONV_ROWS = 32
CONV_LANES = 512
ATTN_TILE = 512
MM_TM = 1024
MM_TN = 1024
MM_TK = 2048
MM_TC = 3072
MM_WHOLE_TILE = 512
MM_WHOLE_VMEM = 34 * 1024 * 1024
EW_VMEM_BYTES = 24 * 1024 * 1024


def _tile(n, pref, align):
    if n <= pref:
        return n
    t = (pref // align) * align
    while t >= align:
        if n % t == 0:
            return t
        t -= align
    return n


def _ew_rows(r, c, n_arrays):
    return _tile(r, max(SUBLANE_BF16, EW_VMEM_BYTES // (2 * n_arrays * c * 4)), SUBLANE_BF16)


def _nbytes(shape, dtype):
    n = 1
    for s in shape:
        if s is not None:
            n *= s
    return n * jnp.dtype(dtype).itemsize


def _params(semantics, block_bytes, temp_bytes=0):
    need = 2 * block_bytes + temp_bytes + (4 << 20)
    return pltpu.CompilerParams(dimension_semantics=semantics,
                                vmem_limit_bytes=int(min(max(need, 16 << 20), VMEM_CAP)))


def _sigmoid(v):
    return 1.0 / (1.0 + jnp.exp(-v))


def _pcall(body, **kwargs):
    call = pl.pallas_call(body, **kwargs)
    return lambda *operands: call(*[pltpu.with_memory_space_constraint(o, pltpu.HBM) for o in operands])


def _whole_contraction_fits(k):
    return 4 * k * MM_WHOLE_TILE * 2 <= MM_WHOLE_VMEM


def _mm_nn(a, b3, out_dtype, name, a_act=None, after=None):
    m, k = a.shape
    s, k2, ns = b3.shape
    assert k == k2
    tm = _tile(m, MM_TM, SUBLANE_BF16)
    tn = _tile(ns, MM_TN, LANE)
    tk = _tile(k, MM_TK, LANE)
    b_resident = k > tk and _whole_contraction_fits(k)
    if b_resident:
        tm, tn, tk = _tile(m, MM_WHOLE_TILE, SUBLANE_BF16), _tile(ns, MM_WHOLE_TILE, LANE), k
    q = ns // tn
    nk = k // tk

    def ij(g0, g1):
        return (g1, g0) if b_resident else (g0, g1)

    n_in = 2 if after is None else 3

    def body(*refs):
        a_ref, b_ref, o_ref = refs[0], refs[1], refs[n_in]
        scratch = refs[n_in + 1:]
        av = a_ref[...]
        if a_act == "relu2":
            r = jnp.maximum(av.astype(BF16), 0.0)
            av = r * r
        p = jnp.dot(av.astype(BF16), b_ref[...].astype(BF16), preferred_element_type=F32)
        if nk == 1:
            o_ref[...] = p.astype(o_ref.dtype)
        else:
            acc, = scratch
            kk = pl.program_id(2)

            @pl.when(kk == 0)
            def _():
                acc[...] = p

            @pl.when(kk > 0)
            def _():
                acc[...] += p

            @pl.when(kk == nk - 1)
            def _():
                o_ref[...] = acc[...].astype(o_ref.dtype)

    blocks = (_nbytes((tm, tk), a.dtype) + _nbytes((tk, tn), b3.dtype) + _nbytes((tm, tn), out_dtype))
    return _pcall(
        body, name=name,
        out_shape=jax.ShapeDtypeStruct((m, s * ns), out_dtype),
        grid=(*ij(m // tm, s * q), nk),
        in_specs=[pl.BlockSpec((tm, tk), lambda g0, g1, c: (ij(g0, g1)[0], c)),
                  pl.BlockSpec((None, tk, tn), lambda g0, g1, c: (ij(g0, g1)[1] // q, c, ij(g0, g1)[1] % q))]
        + ([] if after is None else [pl.BlockSpec(memory_space=pl.ANY)]),
        out_specs=pl.BlockSpec((tm, tn), lambda g0, g1, c: ij(g0, g1)),
        scratch_shapes=[pltpu.VMEM((tm, tn), F32)] if nk > 1 else [],
        compiler_params=_params(("parallel", "parallel", "arbitrary"), blocks,
                                3 * _nbytes((tm, tn), F32) + _nbytes((tm, tk), F32)),
    )(*([a, b3] if after is None else [a, b3, after]))


def _mm_nt(a, b3, out_dtype, name, relu2_of=None, after=None):
    m, n = a.shape
    s, kd, ns = b3.shape
    assert n == s * ns
    tm = _tile(m, MM_TM, SUBLANE_BF16)
    tj = _tile(kd, MM_TN, LANE)
    tc = _tile(ns, MM_TC, LANE)
    q = ns // tc
    nc = s * q

    n_in = 2 + (relu2_of is not None) + (after is not None)

    def body(*refs):
        a_ref, b_ref, u_ref, o_ref = refs[0], refs[1], refs[2], refs[n_in]
        scratch = refs[n_in + 1:]

        def finish(p):
            if relu2_of is not None:
                p = p * (2.0 * jnp.maximum(u_ref[...].astype(F32), 0.0))
            o_ref[...] = p.astype(o_ref.dtype)

        p = lax.dot_general(a_ref[...].astype(BF16), b_ref[...].astype(BF16),
                            (((1,), (1,)), ((), ())), preferred_element_type=F32)
        if nc == 1:
            finish(p)
        else:
            acc, = scratch
            cc = pl.program_id(2)

            @pl.when(cc == 0)
            def _():
                acc[...] = p

            @pl.when(cc > 0)
            def _():
                acc[...] += p

            @pl.when(cc == nc - 1)
            def _():
                finish(acc[...])

    in_specs = [pl.BlockSpec((tm, tc), lambda i, j, c: (i, c)),
                pl.BlockSpec((None, tj, tc), lambda i, j, c: (c // q, j, c % q))]
    operands = [a, b3]
    blocks = _nbytes((tm, tc), a.dtype) + _nbytes((tj, tc), b3.dtype) + _nbytes((tm, tj), out_dtype)
    if relu2_of is not None:
        in_specs.append(pl.BlockSpec((tm, tj), lambda i, j, c: (i, j)))
        operands.append(relu2_of)
        blocks += _nbytes((tm, tj), relu2_of.dtype)
    if after is not None:
        in_specs.append(pl.BlockSpec(memory_space=pl.ANY))
        operands.append(after)
    return _pcall(
        body, name=name,
        out_shape=jax.ShapeDtypeStruct((m, kd), out_dtype),
        grid=(m // tm, kd // tj, nc),
        in_specs=in_specs,
        out_specs=pl.BlockSpec((tm, tj), lambda i, j, c: (i, j)),
        scratch_shapes=[pltpu.VMEM((tm, tj), F32)] if nc > 1 else [],
        compiler_params=_params(("parallel", "parallel", "arbitrary"), blocks,
                                3 * _nbytes((tm, tj), F32)),
    )(*operands)


def _mm_tn(a, g, out_shards, name, a_act=None):
    m, ka = a.shape
    m2, n = g.shape
    assert m == m2
    ns = n // out_shards
    ta = _tile(ka, MM_TM, LANE)
    tn = _tile(ns, MM_TN, LANE)
    tm = _tile(m, MM_TK, SUBLANE_BF16)
    if m > tm and _whole_contraction_fits(m):
        ta, tn, tm = _tile(ka, MM_WHOLE_TILE, LANE), _tile(ns, MM_WHOLE_TILE, LANE), m
    q = ns // tn
    nm = m // tm

    def body(a_ref, g_ref, o_ref, *scratch):
        av = a_ref[...]
        if a_act == "relu2":
            r = jnp.maximum(av.astype(BF16), 0.0)
            av = r * r
        p = lax.dot_general(av.astype(BF16), g_ref[...].astype(BF16),
                            (((0,), (0,)), ((), ())), preferred_element_type=F32)
        if nm == 1:
            o_ref[...] = p.astype(o_ref.dtype)
        else:
            acc, = scratch
            cc = pl.program_id(2)

            @pl.when(cc == 0)
            def _():
                acc[...] = p

            @pl.when(cc > 0)
            def _():
                acc[...] += p

            @pl.when(cc == nm - 1)
            def _():
                o_ref[...] = acc[...].astype(o_ref.dtype)

    blocks = _nbytes((tm, ta), a.dtype) + _nbytes((tm, tn), g.dtype) + _nbytes((ta, tn), BF16)
    return _pcall(
        body, name=name,
        out_shape=jax.ShapeDtypeStruct((out_shards, ka, ns), BF16),
        grid=(ka // ta, out_shards * q, nm),
        in_specs=[pl.BlockSpec((tm, ta), lambda i, j, c: (c, i)),
                  pl.BlockSpec((tm, tn), lambda i, j, c: (c, j))],
        out_specs=pl.BlockSpec((None, ta, tn), lambda i, j, c: (j // q, i, j % q)),
        scratch_shapes=[pltpu.VMEM((ta, tn), F32)] if nm > 1 else [],
        compiler_params=_params(("parallel", "parallel", "arbitrary"), blocks,
                                3 * _nbytes((ta, tn), F32) + _nbytes((tm, ta), F32)),
    )(a, g)


def _rms_scale(v):
    return lax.rsqrt(jnp.mean(v * v, axis=-1, keepdims=True) + NORM_EPS)


def _norm_fwd(x, name, *, z=None, g_post=None, g_next=None, target=None):
    t, d = x.shape
    tr = _tile(t, ROW_TILE, SUBLANE_BF16)
    has_res, has_next, has_loss = z is not None, g_next is not None, target is not None

    def body(*refs):
        it = iter(refs)
        x_ref = next(it)
        z_ref, gp_ref = (next(it), next(it)) if has_res else (None, None)
        gn_ref = next(it) if has_next else None
        t_ref = next(it) if has_loss else None
        xv = x_ref[...]
        if has_res:
            zv = z_ref[...].astype(F32)
            xv = xv + zv * _rms_scale(zv) * gp_ref[...]
            if not has_loss:
                next(it)[...] = xv
        if has_next:
            next(it)[...] = (xv * _rms_scale(xv) * gn_ref[...]).astype(BF16)
        if has_loss:
            e = xv - t_ref[...]
            next(it)[...] = e * (1.0 / d)
            ls_ref = next(it)

            @pl.when(pl.program_id(0) == 0)
            def _():
                ls_ref[...] = jnp.zeros_like(ls_ref)

            ls_ref[...] += jnp.sum(e * e, axis=0, keepdims=True)

    row = pl.BlockSpec((tr, d), lambda i: (i, 0))
    vec = pl.BlockSpec((1, d), lambda i: (0, 0))
    operands, in_specs, out_shape, out_specs = [x], [row], [], []
    if has_res:
        operands += [z, g_post.reshape(1, d)]
        in_specs += [row, vec]
        if not has_loss:
            out_shape.append(jax.ShapeDtypeStruct((t, d), F32))
            out_specs.append(row)
    if has_next:
        operands.append(g_next.reshape(1, d))
        in_specs.append(vec)
        out_shape.append(jax.ShapeDtypeStruct((t, d), BF16))
        out_specs.append(row)
    if has_loss:
        operands.append(target)
        in_specs.append(row)
        out_shape += [jax.ShapeDtypeStruct((t, d), F32), jax.ShapeDtypeStruct((1, d), F32)]
        out_specs += [row, vec]
    return _pcall(
        body, name=name, out_shape=out_shape, grid=(t // tr,),
        in_specs=in_specs, out_specs=out_specs,
        compiler_params=_params(("arbitrary",), 5 * _nbytes((tr, d), F32), 4 * _nbytes((tr, d), F32)),
    )(*operands)


def _norm_bwd(name, *, dxo=None, pre=None, post=None, want_dx=True):
    ref_arr = dxo if dxo is not None else pre[1]
    t, d = ref_arr.shape
    tr = _tile(t, ROW_TILE, SUBLANE_BF16)
    has_dxo, has_pre, has_post = dxo is not None, pre is not None, post is not None

    def body(*refs):
        it = iter(refs)
        dxo_ref = next(it) if has_dxo else None
        dh_ref, xin_ref, gpre_ref = (next(it), next(it), next(it)) if has_pre else (None,) * 3
        z_ref, gpost_ref = (next(it), next(it)) if has_post else (None, None)
        dx_ref = next(it) if (has_pre and want_dx) else None
        dz_ref = next(it) if has_post else None
        dgpre_ref = next(it) if has_pre else None
        dgpost_ref = next(it) if has_post else None
        first = pl.program_id(0) == 0

        dx = dxo_ref[...] if has_dxo else None
        if has_pre:
            xin = xin_ref[...]
            dh = dh_ref[...].astype(F32)
            r = _rms_scale(xin)
            gy = dh * gpre_ref[...]
            dloc = r * gy - xin * (r * r * r) * jnp.mean(gy * xin, axis=-1, keepdims=True)
            dx = dloc if dx is None else dx + dloc
            if want_dx:
                dx_ref[...] = dx

            @pl.when(first)
            def _():
                dgpre_ref[...] = jnp.zeros_like(dgpre_ref)

            dgpre_ref[...] += jnp.sum(dh * xin * r, axis=0, keepdims=True)
        if has_post:
            zv = z_ref[...].astype(F32)
            r = _rms_scale(zv)
            gy = dx * gpost_ref[...]
            dz = r * gy - zv * (r * r * r) * jnp.mean(gy * zv, axis=-1, keepdims=True)
            dz_ref[...] = dz.astype(BF16)

            @pl.when(first)
            def _():
                dgpost_ref[...] = jnp.zeros_like(dgpost_ref)

            dgpost_ref[...] += jnp.sum(dx * zv * r, axis=0, keepdims=True)

    row = pl.BlockSpec((tr, d), lambda i: (i, 0))
    vec = pl.BlockSpec((1, d), lambda i: (0, 0))
    operands, in_specs, out_shape, out_specs = [], [], [], []
    if has_dxo:
        operands.append(dxo)
        in_specs.append(row)
    if has_pre:
        operands += [pre[0], pre[1], pre[2].reshape(1, d)]
        in_specs += [row, row, vec]
    if has_post:
        operands += [post[0], post[1].reshape(1, d)]
        in_specs += [row, vec]
    if has_pre and want_dx:
        out_shape.append(jax.ShapeDtypeStruct((t, d), F32))
        out_specs.append(row)
    if has_post:
        out_shape.append(jax.ShapeDtypeStruct((t, d), BF16))
        out_specs.append(row)
    if has_pre:
        out_shape.append(jax.ShapeDtypeStruct((1, d), F32))
        out_specs.append(vec)
    if has_post:
        out_shape.append(jax.ShapeDtypeStruct((1, d), F32))
        out_specs.append(vec)
    return _pcall(
        body, name=name, out_shape=out_shape, grid=(t // tr,),
        in_specs=in_specs, out_specs=out_specs,
        compiler_params=_params(("arbitrary",), 6 * _nbytes((tr, d), F32), 6 * _nbytes((tr, d), F32)),
    )(*operands)


def _seq_tiles(t):
    tr = _tile(t, ROW_TILE, HALO)
    assert tr % HALO == 0 and t % tr == 0
    return tr, t // tr, tr // HALO


def _col(tr, width, cb):
    return pl.BlockSpec((tr, width), lambda i: (i, cb))


def _prev_halo(per, width, cb):
    return pl.BlockSpec((HALO, width), lambda i: (jnp.maximum(i * per - 1, 0), cb))


def _next_halo(per, n_halo, width, cb):
    return pl.BlockSpec((HALO, width), lambda i: (jnp.minimum((i + 1) * per, n_halo - 1), cb))


def _const(shape):
    return pl.BlockSpec(shape, lambda i: (0,) * len(shape))


def _glu(val, gate):
    return val.astype(F32) * _sigmoid(gate.astype(F32))


def _conv_chunks(tr, c):
    lanes = min(CONV_LANES, c)
    return [(r0, pl.ds(c0, lanes)) for r0 in range(0, tr, CONV_ROWS) for c0 in range(0, c, lanes)]


def _shift_copies(buf, shifted):
    rows = shifted.shape[1]
    buf[pl.ds(rows, SUBLANES), :] = jnp.zeros((SUBLANES, buf.shape[1]), F32)
    for s in range(SUBLANES):
        shifted[s] = buf[pl.ds(s, rows), :]


def _shifted_rows(shifted, offset, r0, cols):
    return shifted[offset % SUBLANES, pl.ds(offset - offset % SUBLANES + r0, CONV_ROWS), cols]


def _tap_sum(w_ref, shifted, r0, cols, first, step, kw):
    acc = jnp.zeros((CONV_ROWS, cols.size), F32)
    for k in range(kw):
        acc = acc + w_ref[pl.ds(k, 1), cols] * _shifted_rows(shifted, first + step * k, r0, cols)
    return acc


def _layer_norm_parts(ca):
    mu = jnp.mean(ca, axis=-1, keepdims=True)
    xc = ca - mu
    rs = lax.rsqrt(jnp.mean(xc * xc, axis=-1, keepdims=True) + NORM_EPS)
    return xc * rs, rs


def _branch_a_fwd(proj, cw, cb, lg, lb, name):
    t = proj.shape[0]
    kw, c = cw.shape
    tr, nt, per = _seq_tiles(t)

    def body(av_ref, ag_ref, hv_ref, hg_ref, cw_ref, cb_ref, lg_ref, lb_ref, ca_ref, act_ref, abuf, ash):
        i = pl.program_id(0)
        abuf[pl.ds(0, HALO), :] = jnp.where(i > 0, _glu(hv_ref[...], hg_ref[...]), 0.0)
        abuf[pl.ds(HALO, tr), :] = _glu(av_ref[...], ag_ref[...])
        _shift_copies(abuf, ash)
        for r0, cols in _conv_chunks(tr, c):
            ca_ref[pl.ds(r0, CONV_ROWS), cols] = (
                _tap_sum(cw_ref, ash, r0, cols, HALO - (kw - 1), 1, kw) + cb_ref[:, cols])
        xh, _ = _layer_norm_parts(ca_ref[...])
        ln = xh * lg_ref[...] + lb_ref[...]
        act_ref[...] = (ln * _sigmoid(ln)).astype(BF16)

    return _pcall(
        body, name=name,
        out_shape=[jax.ShapeDtypeStruct((t, c), F32), jax.ShapeDtypeStruct((t, c), BF16)],
        grid=(nt,),
        in_specs=[_col(tr, c, 0), _col(tr, c, 1), _prev_halo(per, c, 0), _prev_halo(per, c, 1),
                  _const((kw, c)), _const((1, c)), _const((1, c)), _const((1, c))],
        out_specs=[_col(tr, c, 0), _col(tr, c, 0)],
        scratch_shapes=[pltpu.VMEM((HALO + tr + SUBLANES, c), F32), pltpu.VMEM((SUBLANES, HALO + tr, c), F32)],
        compiler_params=_params(("arbitrary",), 4 * _nbytes((tr, c), F32),
                                (8 + SUBLANES) * _nbytes((tr + HALO, c), F32)),
    )(proj, proj, proj, proj, cw, cb.reshape(1, c), lg.reshape(1, c), lb.reshape(1, c))


def _branch_a_bwd(dproj, dact, ca, proj, cw, lg, lb, name):
    t = proj.shape[0]
    kw, c = cw.shape
    tr, nt, per = _seq_tiles(t)
    n_halo = t // HALO

    def body(dproj_in, da_ref, dah_ref, ca_ref, cah_ref, av_ref, ag_ref, hv_ref, hg_ref,
             cw_ref, lg_ref, lb_ref, out_ref, dlg_ref, dlb_ref, dcb_ref, dcw_ref, abuf, dbuf, sgbuf, ash, dsh):
        del dproj_in
        i = pl.program_id(0)
        lgv, lbv = lg_ref[...], lb_ref[...]

        def conv_grad(dact_v, ca_v):
            xh, rs = _layer_norm_parts(ca_v)
            ln = xh * lgv + lbv
            sg = _sigmoid(ln)
            dln = dact_v.astype(F32) * (sg * (1.0 + ln * (1.0 - sg)))
            dxh = dln * lgv
            dca = rs * (dxh - jnp.mean(dxh, axis=-1, keepdims=True)
                        - xh * jnp.mean(dxh * xh, axis=-1, keepdims=True))
            return dca, dln, xh

        dca, dln, xh = conv_grad(da_ref[...], ca_ref[...])
        dca_h, _, _ = conv_grad(dah_ref[...], cah_ref[...])
        dbuf[pl.ds(0, tr), :] = dca
        dbuf[pl.ds(tr, HALO), :] = jnp.where(i < nt - 1, dca_h, 0.0)

        @pl.when(i == 0)
        def _():
            dlg_ref[...] = jnp.zeros_like(dlg_ref)
            dlb_ref[...] = jnp.zeros_like(dlb_ref)
            dcb_ref[...] = jnp.zeros_like(dcb_ref)
            dcw_ref[...] = jnp.zeros_like(dcw_ref)

        dlg_ref[...] += jnp.sum(dln * xh, axis=0, keepdims=True)
        dlb_ref[...] += jnp.sum(dln, axis=0, keepdims=True)
        dcb_ref[...] += jnp.sum(dca, axis=0, keepdims=True)

        sg = _sigmoid(ag_ref[...].astype(F32))
        sgbuf[...] = sg
        abuf[pl.ds(0, HALO), :] = jnp.where(i > 0, _glu(hv_ref[...], hg_ref[...]), 0.0)
        abuf[pl.ds(HALO, tr), :] = av_ref[...].astype(F32) * sg
        _shift_copies(abuf, ash)
        _shift_copies(dbuf, dsh)

        for r0, cols in _conv_chunks(tr, c):
            rows = pl.ds(r0, CONV_ROWS)
            d_a = _tap_sum(cw_ref, dsh, r0, cols, kw - 1, -1, kw)
            sgc = sgbuf[rows, cols]
            out_ref[rows, cols] = (d_a * sgc).astype(BF16)
            out_ref[rows, pl.ds(c + cols.start, cols.size)] = (
                d_a * abuf[pl.ds(HALO + r0, CONV_ROWS), cols] * (1.0 - sgc)).astype(BF16)
        for _, cols in _conv_chunks(CONV_ROWS, c):
            for k in range(kw):
                acc = jnp.zeros((CONV_ROWS, cols.size), F32)
                for r0 in range(0, tr, CONV_ROWS):
                    acc = acc + (dbuf[pl.ds(r0, CONV_ROWS), cols]
                                 * _shifted_rows(ash, HALO - (kw - 1) + k, r0, cols))
                dcw_ref[pl.ds(k, 1), cols] += jnp.sum(acc, axis=0, keepdims=True)

    vec = _const((1, c))
    return _pcall(
        body, name=name,
        out_shape=[jax.ShapeDtypeStruct(dproj.shape, BF16)] + [jax.ShapeDtypeStruct((1, c), F32)] * 3
        + [jax.ShapeDtypeStruct((kw, c), F32)],
        grid=(nt,),
        in_specs=[pl.BlockSpec(memory_space=pl.ANY),
                  _col(tr, c, 0), _next_halo(per, n_halo, c, 0),
                  _col(tr, c, 0), _next_halo(per, n_halo, c, 0),
                  _col(tr, c, 0), _col(tr, c, 1), _prev_halo(per, c, 0), _prev_halo(per, c, 1),
                  _const((kw, c)), vec, vec],
        out_specs=[pl.BlockSpec((tr, 2 * c), lambda i: (i, 0)), vec, vec, vec, _const((kw, c))],
        scratch_shapes=[pltpu.VMEM((HALO + tr + SUBLANES, c), F32), pltpu.VMEM((HALO + tr + SUBLANES, c), F32),
                        pltpu.VMEM((tr, c), F32),
                        pltpu.VMEM((SUBLANES, HALO + tr, c), F32), pltpu.VMEM((SUBLANES, HALO + tr, c), F32)],
        input_output_aliases={0: 0},
        compiler_params=_params(("arbitrary",), 6 * _nbytes((tr, c), F32),
                                (12 + 2 * SUBLANES) * _nbytes((tr + HALO, c), F32)),
    )(dproj, dact, dact, ca, ca, proj, proj, proj, proj, cw, lg.reshape(1, c), lb.reshape(1, c))


def _branch_b_fwd(proj, cw, name):
    t = proj.shape[0]
    kw, c = cw.shape
    tr, nt, per = _seq_tiles(t)

    def body(sb_ref, sc_ref, sx_ref, hc_ref, hx_ref, cw_ref, o_ref, pbuf):
        i = pl.program_id(0)
        hp = hc_ref[...].astype(F32) * hx_ref[...].astype(F32)
        pbuf[pl.ds(0, HALO), :] = jnp.where(i > 0, hp, 0.0)
        pbuf[pl.ds(HALO, tr), :] = sc_ref[...].astype(F32) * sx_ref[...].astype(F32)
        u = jnp.zeros((tr, c), F32)
        for k in range(kw):
            u = u + cw_ref[pl.ds(k, 1), :] * pbuf[pl.ds(HALO - (kw - 1) + k, tr), :]
        o_ref[...] = (sb_ref[...].astype(F32) * u).astype(BF16)

    return _pcall(
        body, name=name,
        out_shape=jax.ShapeDtypeStruct((t, c), BF16),
        grid=(nt,),
        in_specs=[_col(tr, c, 2), _col(tr, c, 3), _col(tr, c, 4),
                  _prev_halo(per, c, 3), _prev_halo(per, c, 4), _const((kw, c))],
        out_specs=_col(tr, c, 0),
        scratch_shapes=[pltpu.VMEM((HALO + tr, c), F32)],
        compiler_params=_params(("arbitrary",), 4 * _nbytes((tr, c), F32), 6 * _nbytes((tr + HALO, c), F32)),
    )(proj, proj, proj, proj, proj, cw)


def _branch_b_bwd(dproj, dbin, proj, cw, name):
    t = proj.shape[0]
    kw, c = cw.shape
    tr, nt, per = _seq_tiles(t)
    n_halo = t // HALO

    def body(dproj_in, db_ref, dbh_ref, sb_ref, sbh_ref, sc_ref, sx_ref, hc_ref, hx_ref, cw_ref,
             out_ref, dcw_ref, pbuf, dubuf, res):
        del dproj_in
        i = pl.program_id(0)
        j = pl.program_id(1)

        @pl.when(j == 0)
        def _():
            sb = sb_ref[...].astype(F32)
            sc = sc_ref[...].astype(F32)
            sx = sx_ref[...].astype(F32)
            dbin_v = db_ref[...].astype(F32)
            hp = hc_ref[...].astype(F32) * hx_ref[...].astype(F32)
            pbuf[pl.ds(0, HALO), :] = jnp.where(i > 0, hp, 0.0)
            pbuf[pl.ds(HALO, tr), :] = sc * sx
            du = dbin_v * sb
            du_h = dbh_ref[...].astype(F32) * sbh_ref[...].astype(F32)
            dubuf[pl.ds(0, tr), :] = du
            dubuf[pl.ds(tr, HALO), :] = jnp.where(i < nt - 1, du_h, 0.0)

            @pl.when(i == 0)
            def _():
                dcw_ref[...] = jnp.zeros_like(dcw_ref)

            u = jnp.zeros((tr, c), F32)
            dp = jnp.zeros((tr, c), F32)
            for k in range(kw):
                shifted = pbuf[pl.ds(HALO - (kw - 1) + k, tr), :]
                u = u + cw_ref[pl.ds(k, 1), :] * shifted
                dp = dp + cw_ref[pl.ds(k, 1), :] * dubuf[pl.ds(kw - 1 - k, tr), :]
                dcw_ref[pl.ds(k, 1), :] += jnp.sum(du * shifted, axis=0, keepdims=True)
            res[0] = (dbin_v * u).astype(BF16)
            res[1] = (dp * sx).astype(BF16)
            res[2] = (dp * sc).astype(BF16)

        out_ref[...] = res[j]

    def colj(cb):
        return pl.BlockSpec((tr, c), lambda i, j: (i, cb))

    def prevj(cb):
        return pl.BlockSpec((HALO, c), lambda i, j: (jnp.maximum(i * per - 1, 0), cb))

    def nextj(cb):
        return pl.BlockSpec((HALO, c), lambda i, j: (jnp.minimum((i + 1) * per, n_halo - 1), cb))

    return _pcall(
        body, name=name,
        out_shape=[jax.ShapeDtypeStruct(dproj.shape, BF16), jax.ShapeDtypeStruct((kw, c), F32)],
        grid=(nt, 3),
        in_specs=[pl.BlockSpec(memory_space=pl.ANY),
                  colj(0), nextj(0), colj(2), nextj(2), colj(3), colj(4), prevj(3), prevj(4),
                  pl.BlockSpec((kw, c), lambda i, j: (0, 0))],
        out_specs=[pl.BlockSpec((tr, c), lambda i, j: (i, 2 + j)),
                   pl.BlockSpec((kw, c), lambda i, j: (0, 0))],
        scratch_shapes=[pltpu.VMEM((HALO + tr, c), F32), pltpu.VMEM((HALO + tr, c), F32),
                        pltpu.VMEM((3, tr, c), BF16)],
        input_output_aliases={0: 0},
        compiler_params=_params(("arbitrary", "arbitrary"), 6 * _nbytes((tr, c), F32),
                                10 * _nbytes((tr + HALO, c), F32)),
    )(dproj, dbin, dbin, proj, proj, proj, proj, proj, proj, cw)


def _softmax_rows(s):
    e = jnp.exp(s - jnp.max(s, axis=-1, keepdims=True))
    return e / jnp.sum(e, axis=-1, keepdims=True)


def _attn_fwd(proj, kv, name):
    t = proj.shape[0]
    m, c2 = kv.shape
    c = c2 // 2
    hd = c // N_HEADS
    ta = _tile(t, ATTN_TILE, SUBLANE_BF16)
    scale = hd ** -0.5

    def body(q_ref, kv_ref, o_ref):
        for h in range(N_HEADS):
            qh = q_ref[:, pl.ds(h * hd, hd)]
            kh = kv_ref[:, pl.ds(h * hd, hd)]
            vh = kv_ref[:, pl.ds(c + h * hd, hd)]
            s = lax.dot_general(qh, kh, (((1,), (1,)), ((), ())), preferred_element_type=F32) * scale
            p = _softmax_rows(s)
            o_ref[:, pl.ds(h * hd, hd)] = jnp.dot(p.astype(BF16), vh,
                                                  preferred_element_type=F32).astype(BF16)

    return _pcall(
        body, name=name,
        out_shape=jax.ShapeDtypeStruct((t, c), BF16),
        grid=(t // ta,),
        in_specs=[pl.BlockSpec((ta, c), lambda i: (i, 5)), _const((m, c2))],
        out_specs=pl.BlockSpec((ta, c), lambda i: (i, 0)),
        compiler_params=_params(("parallel",), 2 * _nbytes((ta, c), BF16) + _nbytes((m, c2), BF16),
                                8 * _nbytes((ta, m), F32)),
    )(proj, kv)


def _attn_bwd(dproj, d_o, proj, kv, name):
    t = proj.shape[0]
    m, c2 = kv.shape
    c = c2 // 2
    hd = c // N_HEADS
    ta = _tile(t, ATTN_TILE, SUBLANE_BF16)
    scale = hd ** -0.5

    def body(dproj_in, do_ref, q_ref, kv_ref, dq_ref, dkv_ref):
        del dproj_in

        @pl.when(pl.program_id(0) == 0)
        def _():
            dkv_ref[...] = jnp.zeros_like(dkv_ref)

        for h in range(N_HEADS):
            qh = q_ref[:, pl.ds(h * hd, hd)]
            kh = kv_ref[:, pl.ds(h * hd, hd)]
            vh = kv_ref[:, pl.ds(c + h * hd, hd)]
            doh = do_ref[:, pl.ds(h * hd, hd)]
            s = lax.dot_general(qh, kh, (((1,), (1,)), ((), ())), preferred_element_type=F32) * scale
            p = _softmax_rows(s)
            dp = lax.dot_general(doh, vh, (((1,), (1,)), ((), ())), preferred_element_type=F32)
            ds = (p * (dp - jnp.sum(dp * p, axis=-1, keepdims=True))).astype(BF16)
            dq_ref[:, pl.ds(h * hd, hd)] = (jnp.dot(ds, kh, preferred_element_type=F32) * scale).astype(BF16)
            dkv_ref[:, pl.ds(h * hd, hd)] += lax.dot_general(
                ds, qh, (((0,), (0,)), ((), ())), preferred_element_type=F32) * scale
            dkv_ref[:, pl.ds(c + h * hd, hd)] += lax.dot_general(
                p.astype(BF16), doh, (((0,), (0,)), ((), ())), preferred_element_type=F32)

    return _pcall(
        body, name=name,
        out_shape=[jax.ShapeDtypeStruct(dproj.shape, BF16), jax.ShapeDtypeStruct((m, c2), F32)],
        grid=(t // ta,),
        in_specs=[pl.BlockSpec(memory_space=pl.ANY),
                  pl.BlockSpec((ta, c), lambda i: (i, 0)), pl.BlockSpec((ta, c), lambda i: (i, 5)),
                  _const((m, c2))],
        out_specs=[pl.BlockSpec((ta, c), lambda i: (i, 5)), _const((m, c2))],
        input_output_aliases={0: 0},
        compiler_params=_params(("arbitrary",), 3 * _nbytes((ta, c), BF16) + 2 * _nbytes((m, c2), F32),
                                10 * _nbytes((ta, m), F32)),
    )(dproj, d_o, proj, kv)


def _merge_fwd(proj, ya, yb, yx, name):
    t, d = ya.shape
    tr = _tile(t, ROW_TILE, SUBLANE_BF16)

    def body(g_ref, ya_ref, yb_ref, yx_ref, o_ref):
        acc = jnp.zeros((tr, d), F32)
        for b, y_ref in enumerate((ya_ref, yb_ref, yx_ref)):
            acc = acc + _sigmoid(g_ref[:, pl.ds(b * d, d)].astype(F32)) * y_ref[...].astype(F32)
        o_ref[...] = acc.astype(BF16)

    row = pl.BlockSpec((tr, d), lambda i: (i, 0))
    return _pcall(
        body, name=name,
        out_shape=jax.ShapeDtypeStruct((t, d), BF16),
        grid=(t // tr,),
        in_specs=[pl.BlockSpec((tr, 3 * d), lambda i: (i, 1)), row, row, row],
        out_specs=row,
        compiler_params=_params(("parallel",), 7 * _nbytes((tr, d), BF16), 6 * _nbytes((tr, d), F32)),
    )(proj, ya, yb, yx)


def _merge_bwd(dmerged, proj, ya, yb, yx, name):
    t, d = ya.shape
    tr = _tile(t, ROW_TILE, SUBLANE_BF16)

    def body(dm_ref, g_ref, ya_ref, yb_ref, yx_ref, dg_ref, dya_ref, dyb_ref, dyx_ref):
        dm = dm_ref[...].astype(F32)
        for b, (y_ref, dy_ref) in enumerate(((ya_ref, dya_ref), (yb_ref, dyb_ref), (yx_ref, dyx_ref))):
            sg = _sigmoid(g_ref[:, pl.ds(b * d, d)].astype(F32))
            dg_ref[:, pl.ds(b * d, d)] = (dm * y_ref[...].astype(F32) * sg * (1.0 - sg)).astype(BF16)
            dy_ref[...] = (dm * sg).astype(BF16)

    row = pl.BlockSpec((tr, d), lambda i: (i, 0))
    gates = pl.BlockSpec((tr, 3 * d), lambda i: (i, 1))
    return _pcall(
        body, name=name,
        out_shape=[jax.ShapeDtypeStruct(proj.shape, BF16)] + [jax.ShapeDtypeStruct((t, d), BF16)] * 3,
        grid=(t // tr,),
        in_specs=[row, gates, row, row, row],
        out_specs=[gates, row, row, row],
        compiler_params=_params(("parallel",), 14 * _nbytes((tr, d), BF16), 8 * _nbytes((tr, d), F32)),
    )(dmerged, proj, ya, yb, yx)


def _as3(a):
    return a.reshape((-1,) + a.shape[-2:])


def _ew_call(fn, ins, n_out, out_dtypes, name, batch=None, prev=()):
    b, r, c = ins[0].shape
    lo, hi = batch if batch is not None else (0, b)
    tr = _ew_rows(r, c, len(ins) + n_out)
    n_in = len(ins) + len(prev)

    def body(*refs):
        outs = fn(*[ref[...] for ref in refs[:len(ins)]])
        for ref, val in zip(refs[n_in:], outs):
            ref[...] = val.astype(ref.dtype)

    spec = pl.BlockSpec((None, tr, c), lambda i, j: (i + lo, j, 0))
    return _pcall(
        body, name=name,
        out_shape=[jax.ShapeDtypeStruct((b, r, c), dt) for dt in out_dtypes],
        grid=(hi - lo, r // tr),
        in_specs=[spec] * len(ins) + [pl.BlockSpec(memory_space=pl.ANY)] * len(prev), out_specs=[spec] * n_out,
        input_output_aliases={len(ins) + k: k for k in range(len(prev))},
        compiler_params=_params(("parallel", "parallel"), (len(ins) + n_out) * _nbytes((tr, c), F32),
                                6 * _nbytes((tr, c), F32)),
    )(*ins, *prev)


def _my_chip():
    return 2 * lax.axis_index("x") + lax.axis_index("y")


def _my_core():
    return lax.axis_index("c")


def _cast_to_slot(w, l, name, after=()):
    _, r, cs = w.shape
    tr = _ew_rows(r, cs, 2)

    def body(w_ref, *rest):
        rest[-1][...] = w_ref[...].astype(BF16)

    return _pcall(
        body, name=name,
        out_shape=jax.ShapeDtypeStruct((N_CHIPS, r, cs), BF16),
        grid=(r // tr,),
        in_specs=[pl.BlockSpec((None, tr, cs), lambda i: (l, i, 0))]
        + [pl.BlockSpec(memory_space=pl.ANY)] * len(after),
        out_specs=pl.BlockSpec((None, tr, cs), lambda i: (_my_chip(), i, 0)),
        compiler_params=_params(("parallel",), 2 * _nbytes((tr, cs), F32)),
    )(w, *after)


def _add_half(part, theirs, name):
    _, rh, cs = theirs.shape
    tr = _ew_rows(rh, cs, 3)
    nrb = rh // tr

    def body(a_ref, b_ref, o_ref):
        o_ref[...] = (a_ref[...].astype(F32) + b_ref[...].astype(F32)).astype(BF16)

    half = pl.BlockSpec((None, tr, cs), lambda j, i: (j, i, 0))
    return _pcall(
        body, name=name,
        out_shape=jax.ShapeDtypeStruct((N_CHIPS, rh, cs), BF16),
        grid=(N_CHIPS, nrb),
        in_specs=[pl.BlockSpec((None, tr, cs), lambda j, i: (j, _my_core() * nrb + i, 0)), half],
        out_specs=half,
        compiler_params=_params(("parallel", "parallel"), 3 * _nbytes((tr, cs), F32)),
    )(part, theirs)


def _sum_chips(chip_sum, q, total, l, nl, name):
    _, rh, cs = q.shape
    tr = _ew_rows(rh, cs, 5)
    nrb = rh // tr

    def body(own_ref, q1_ref, q2_ref, q3_ref, *rest):
        acc = own_ref[...].astype(F32)
        for ref in (q1_ref, q2_ref, q3_ref):
            acc = acc + ref[...].astype(F32)
        rest[-1][...] = acc

    def slot(k):
        return pl.BlockSpec((None, tr, cs), lambda i: ((_my_chip() + k) % N_CHIPS, i, 0))

    in_specs = [slot(0), slot(1), slot(2), slot(3)]
    operands = [chip_sum, q, q, q]
    aliases = {}
    if total is not None:
        in_specs.append(HBM_SPEC)
        operands.append(total)
        aliases = {4: 0}
    return _pcall(
        body, name=name,
        out_shape=jax.ShapeDtypeStruct((nl, 2 * rh, cs), F32),
        grid=(nrb,),
        in_specs=in_specs,
        out_specs=pl.BlockSpec((None, tr, cs), lambda i: (l, _my_core() * nrb + i, 0)),
        input_output_aliases=aliases,
        compiler_params=_params(("parallel",), 5 * _nbytes((tr, cs), F32)),
    )(*operands)


def _adamw(w, g, m, v, name, batch=None, prev=()):
    shape = w.shape
    c1 = 1.0 - ADAM_B1 ** ADAM_STEP
    c2 = 1.0 - ADAM_B2 ** ADAM_STEP

    def fn(wv, gv, mv, vv):
        mn = ADAM_B1 * mv + (1.0 - ADAM_B1) * gv
        vn = ADAM_B2 * vv + (1.0 - ADAM_B2) * (gv * gv)
        delta = -ADAM_LR * ((mn / c1) / (jnp.sqrt(vn / c2) + ADAM_EPS) + ADAM_WD * wv)
        return delta, mn, vn

    outs = _ew_call(fn, [_as3(w), _as3(g), _as3(m), _as3(v)], 3, [F32] * 3, name,
                    batch=batch, prev=[_as3(p) for p in prev])
    return [o.reshape(shape) for o in outs]


def _sum_leading(q, name):
    b, nj, r, c = q.shape
    tr = _ew_rows(r, c, nj + 1)

    def body(q_ref, o_ref):
        acc = q_ref[0].astype(F32)
        for j in range(1, nj):
            acc = acc + q_ref[j].astype(F32)
        o_ref[...] = acc

    return _pcall(
        body, name=name,
        out_shape=jax.ShapeDtypeStruct((b, r, c), F32),
        grid=(b, r // tr),
        in_specs=[pl.BlockSpec((None, nj, tr, c), lambda i, j: (i, 0, j, 0))],
        out_specs=pl.BlockSpec((None, tr, c), lambda i, j: (i, j, 0)),
        compiler_params=_params(("parallel", "parallel"), (nj + 1) * _nbytes((tr, c), F32),
                                2 * _nbytes((tr, c), F32)),
    )(q)


HBM_SPEC = pl.BlockSpec(memory_space=pl.ANY)


def _place():
    x, y, c = lax.axis_index("x"), lax.axis_index("y"), lax.axis_index("c")
    peers = [(1 - x, y), (x, 1 - y), (1 - x, 1 - y)]
    return x, y, c, 2 * x + y, peers


def _remote(src, dst, send_sem, recv_sem, dev):
    return pltpu.make_async_remote_copy(src_ref=src, dst_ref=dst, send_sem=send_sem, recv_sem=recv_sem,
                                        device_id=dev, device_id_type=MESH)


IN_HBM = pl.BlockSpec(memory_space=pltpu.HBM)
IN_SEM = pl.BlockSpec(memory_space=pltpu.SEMAPHORE)
DATAFLOW = pltpu.SideEffectType.DATAFLOW_SIDE_EFFECTING
TOKEN = jax.ShapeDtypeStruct((8, LANE), F32)


def _in_hbm(arrays):
    return [pltpu.with_memory_space_constraint(a, pltpu.HBM) for a in arrays]


def _half_rows(ref, c):
    rh = ref.shape[1] // 2
    return pl.ds(c * rh, rh)


def _split_start(body_copies, passed, landing, name, per_array=3, after=()):
    n_pass, n_land = len(passed), len(landing)
    n_arr = n_pass + n_land
    n = per_array * n_pass
    n_in = n_arr + len(after)

    def body(*refs):
        ins = refs[:n_arr]
        send, recv = refs[n_in], refs[n_in + 1]
        token = refs[-1]
        for src, dst, _, s_sem, r_sem, dev in body_copies(ins[:n_pass], ins[n_pass:], send, recv):
            _remote(src, dst, s_sem, r_sem, dev).start()
        token[...] = jnp.zeros_like(token)

    arrays = list(passed) + list(landing)
    return pl.pallas_call(
        body, name=name,
        out_shape=(pltpu.SemaphoreType.DMA((n,)), pltpu.SemaphoreType.DMA((n,)),
                   *[pltpu.HBM(a.shape, a.dtype) for a in arrays], TOKEN),
        in_specs=[IN_HBM] * n_arr + [HBM_SPEC] * len(after),
        out_specs=(IN_SEM, IN_SEM, *[IN_HBM] * n_arr, pl.BlockSpec(memory_space=pltpu.VMEM)),
        input_output_aliases={i: 2 + i for i in range(n_arr)},
        compiler_params=pltpu.CompilerParams(has_side_effects=DATAFLOW),
    )(*_in_hbm(arrays), *after)


def _split_wait(body_copies, send, recv, passed, landing, after, name):
    n_pass, n_land = len(passed), len(landing)
    n_arr = n_pass + n_land

    def body(*refs):
        ins = refs[:n_arr]
        send_ref, recv_ref = refs[n_arr], refs[n_arr + 1]
        for src, _, landed, s_sem, r_sem, dev in body_copies(ins[:n_pass], ins[n_pass:], send_ref, recv_ref):
            cp = _remote(src, landed, s_sem, r_sem, dev)
            cp.wait_send()
            cp.wait_recv()

    arrays = list(passed) + list(landing)
    return pl.pallas_call(
        body, name=name,
        out_shape=tuple(pltpu.HBM(a.shape, a.dtype) for a in arrays),
        in_specs=[IN_HBM] * n_arr + [IN_SEM, IN_SEM] + [HBM_SPEC] * len(after),
        out_specs=tuple([IN_HBM] * n_arr),
        input_output_aliases={i: i for i in range(n_arr)},
        compiler_params=pltpu.CompilerParams(has_side_effects=DATAFLOW),
    )(*arrays, send, recv, *after)


def _gather_copies(slots, _, send, recv):
    _, _, c, me, peers = _place()
    copies = []
    for a, ref in enumerate(slots):
        rows = _half_rows(ref, c)
        own = ref.at[me, rows, :]
        for j, (px, py) in enumerate(peers):
            k = 3 * a + j
            copies.append((own, own, ref.at[2 * px + py, rows, :], send.at[k], recv.at[k], (px, py, c)))
    return copies


def _exchange_copies(chip_sums, landing, send, recv):
    _, _, c, me, peers = _place()
    copies = []
    for a, (s_ref, q_ref) in enumerate(zip(chip_sums, landing)):
        for j, (px, py) in enumerate(peers):
            k = 3 * a + j
            copies.append((s_ref.at[2 * px + py], q_ref.at[me], q_ref.at[2 * px + py],
                           send.at[k], recv.at[k], (px, py, c)))
    return copies


def _gather_start(slots, name, after=()):
    out = _split_start(_gather_copies, slots, [], name, after=after)
    return out[0], out[1], list(out[2:-1]), out[-1]


def _gather_wait(send, recv, slots, after, name):
    return list(_split_wait(_gather_copies, send, recv, slots, [], after, name))


def _forward_copies(slots, _, send, recv):
    x, y, c, _, peers = _place()
    copies = []
    for a, ref in enumerate(slots):
        for j, (px, py) in enumerate(peers):
            k = 3 * a + j
            mine = ref.at[2 * px + py, _half_rows(ref, c), :]
            theirs = ref.at[2 * px + py, _half_rows(ref, 1 - c), :]
            copies.append((mine, mine, theirs, send.at[k], recv.at[k], (x, y, 1 - c)))
    return copies


def _forward_start(slots, name):
    out = _split_start(_forward_copies, slots, [], name)
    return out[0], out[1], list(out[2:-1]), out[-1]


def _forward_wait(send, recv, slots, after, name):
    return list(_split_wait(_forward_copies, send, recv, slots, [], after, name))


def _gather_small(v, name):
    def body(v_ref, o_ref, send, recv, lsem):
        x, y, c, me, peers = _place()
        local = pltpu.make_async_copy(v_ref, o_ref.at[me], lsem)
        local.start()
        sends = [_remote(v_ref, o_ref.at[me], send.at[j], recv.at[j], (px, py, c))
                 for j, (px, py) in enumerate(peers)]
        for cp in sends:
            cp.start()
        for j, (px, py) in enumerate(peers):
            _remote(v_ref, o_ref.at[2 * px + py], send.at[j], recv.at[j], (px, py, c)).wait_recv()
        for cp in sends:
            cp.wait_send()
        local.wait()

    return _pcall(
        body, name=name,
        out_shape=jax.ShapeDtypeStruct((N_CHIPS,) + v.shape, v.dtype),
        in_specs=[HBM_SPEC], out_specs=HBM_SPEC,
        scratch_shapes=[pltpu.SemaphoreType.DMA((3,)), pltpu.SemaphoreType.DMA((3,)), pltpu.SemaphoreType.DMA],
    )(v)


def _gather_all(v, name):
    def body(v_ref, o_ref, send, recv, lsem):
        x, y, c = lax.axis_index("x"), lax.axis_index("y"), lax.axis_index("c")
        me = 4 * x + 2 * y + c
        local = pltpu.make_async_copy(v_ref, o_ref.at[me], lsem)
        local.start()
        flips = [(fx, fy, fc) for fx in (0, 1) for fy in (0, 1) for fc in (0, 1)][1:]
        peers = [(x ^ fx, y ^ fy, c ^ fc) for fx, fy, fc in flips]
        sends = [_remote(v_ref, o_ref.at[me], send.at[k], recv.at[k], dev) for k, dev in enumerate(peers)]
        for cp in sends:
            cp.start()
        for k, (px, py, pc) in enumerate(peers):
            _remote(v_ref, o_ref.at[4 * px + 2 * py + pc], send.at[k], recv.at[k], (px, py, pc)).wait_recv()
        for cp in sends:
            cp.wait_send()
        local.wait()

    return _pcall(
        body, name=name,
        out_shape=jax.ShapeDtypeStruct((8,) + v.shape, v.dtype),
        in_specs=[HBM_SPEC], out_specs=HBM_SPEC,
        scratch_shapes=[pltpu.SemaphoreType.DMA((7,)), pltpu.SemaphoreType.DMA((7,)), pltpu.SemaphoreType.DMA],
    )(v)


def _send_copies(parts, landing, send, recv):
    x, y, c, _, _ = _place()
    return [(p.at[:, _half_rows(p, 1 - c), :], t, t, send.at[a], recv.at[a], (x, y, 1 - c))
            for a, (p, t) in enumerate(zip(parts, landing))]


def _send_start(parts, name, after=()):
    na = len(parts)
    landing = [lax.empty((N_CHIPS, p.shape[1] // 2, p.shape[2]), p.dtype) for p in parts]
    out = _split_start(_send_copies, parts, landing, name, per_array=1, after=after)
    return out[0], out[1], list(out[2:2 + na]), list(out[2 + na:2 + 2 * na]), out[-1]


def _join_copies_of(layer):
    def copies(reds, _, send, recv):
        x, y, c, _, _ = _place()
        out = []
        for a, ref in enumerate(reds):
            rh = ref.shape[1] // 2
            mine = ref.at[layer, pl.ds(c * rh, rh), :]
            other = ref.at[layer, pl.ds((1 - c) * rh, rh), :]
            out.append((mine, mine, other, send.at[a], recv.at[a], (x, y, 1 - c)))
        return out

    return copies


def _join_start(reds, layer, name):
    out = _split_start(_join_copies_of(layer), reds, [], name, per_array=1)
    return out[0], out[1], list(out[2:-1]), out[-1]


def _join_wait(send, recv, reds, layer, after, name):
    return list(_split_wait(_join_copies_of(layer), send, recv, reds, [], after, name))


def _send_wait(send, recv, parts, landing, after, name):
    na = len(parts)
    out = _split_wait(_send_copies, send, recv, parts, landing, after, name)
    return list(out[:na]), list(out[na:])


def _exchange_start(chip_sums, name):
    na = len(chip_sums)
    landing = [lax.empty(s.shape, s.dtype) for s in chip_sums]
    out = _split_start(_exchange_copies, chip_sums, landing, name)
    return out[0], out[1], list(out[2:2 + na]), list(out[2 + na:2 + 2 * na]), out[-1]


def _exchange_wait(send, recv, chip_sums, landing, after, name):
    na = len(chip_sums)
    out = _split_wait(_exchange_copies, send, recv, chip_sums, landing, after, name)
    return list(out[:na]), list(out[na:])


def _join_halves(reds, lo, hi, name, after=()):
    na = len(reds)
    n_in = na + len(after)
    layers = pl.ds(lo, hi - lo)

    def body(*refs):
        outs = refs[n_in:n_in + na]
        send, recv = refs[n_in + na:]
        x, y, c, _, _ = _place()
        sends = []
        for a, ref in enumerate(outs):
            rh = ref.shape[1] // 2
            mine = ref.at[layers, pl.ds(c * rh, rh), :]
            sends.append(_remote(mine, mine, send.at[a], recv.at[a], (x, y, 1 - c)))
            sends[-1].start()
        for a, ref in enumerate(outs):
            rh = ref.shape[1] // 2
            other = ref.at[layers, pl.ds((1 - c) * rh, rh), :]
            _remote(other, other, send.at[a], recv.at[a], (x, y, 1 - c)).wait_recv()
        for cp in sends:
            cp.wait_send()

    return _pcall(
        body, name=name,
        out_shape=[jax.ShapeDtypeStruct(r.shape, r.dtype) for r in reds],
        in_specs=[HBM_SPEC] * n_in, out_specs=[HBM_SPEC] * na,
        input_output_aliases={a: a for a in range(na)},
        scratch_shapes=[pltpu.SemaphoreType.DMA((na,)), pltpu.SemaphoreType.DMA((na,))],
    )(*reds, *after)


COL_SHARDED = ("w_in", "w_a_out", "w_b_out", "w_x_out", "w_up")
ROW_SHARDED = ("w_kv", "w_o", "w_down")
BIG = COL_SHARDED + ROW_SHARDED
SMALL_REPLICATED = ("g_mix_pre", "conv_a_b", "ln_a_g", "ln_a_b", "g_mem", "g_mix_post", "g_mlp_pre", "g_mlp_post")
SMALL_SHARDED = ("conv_a_w", "conv_b_w")
WEIGHT_ORDER = ("g_mix_pre", "w_in", "conv_a_w", "conv_a_b", "ln_a_g", "ln_a_b", "w_a_out", "conv_b_w", "w_b_out",
                "g_mem", "w_kv", "w_x_out", "w_o", "g_mix_post", "g_mlp_pre", "w_up", "w_down", "g_mlp_post")


def _pack_rows(arrays, width):
    rows = []
    for a in arrays:
        r = a.reshape(-1, width)
        rows.append(jnp.pad(r, ((0, (-r.shape[0]) % 8), (0, 0))))
    return jnp.concatenate(rows, axis=0)


def _unpack_rows(packed, like, width):
    out, at = [], 0
    for a in like:
        n = a.size // width
        out.append(packed[at:at + n].reshape(a.shape))
        at += n + (-n) % 8
    return out


def _step(w, m, v, x, mem, target):
    nl = w["w_in"].shape[0]
    d = x.shape[1]
    c = w["conv_a_b"].shape[1]
    ka = w["conv_a_w"].shape[1]

    conv_pack = jnp.concatenate([w["conv_a_w"], w["conv_b_w"]], axis=1)
    conv_rows = conv_pack.shape[1]
    conv_pack = jnp.pad(conv_pack, ((0, 0), (0, (-conv_rows) % 8), (0, 0)))
    conv_all = _gather_small(conv_pack, "gather_conv")
    slots = {name: [_cast_to_slot(w[name], 0, "cast_" + name, after=[conv_all])] for name in BIG}
    conv_all = jnp.moveaxis(conv_all, 0, 2).reshape(nl, conv_pack.shape[1], c)
    cw_a, cw_b = conv_all[:, :ka], conv_all[:, ka:conv_rows]

    def start_gather(l, names=BIG, tag="", after=()):
        return _gather_start([slots[name][l] for name in names], "gather_start_%d%s" % (l, tag), after=after)

    def start_forward(l, pending, after, tag=""):
        send, recv, thru, _ = pending
        landed = _gather_wait(send, recv, thru, after, "gather_wait_%d%s" % (l, tag))
        return _forward_start(landed, "gather_fwd_start_%d%s" % (l, tag))

    def finish_forward(l, passing, after, names=BIG, tag=""):
        send, recv, thru, _ = passing
        arrays = _forward_wait(send, recv, thru, after, "gather_fwd_wait_%d%s" % (l, tag))
        return {name: (a.reshape(1, -1, a.shape[-1]) if name in ROW_SHARDED else a)
                for name, a in zip(names, arrays)}

    first, rest = BIG[:1], BIG[1:]
    pending = start_gather(0, first, "a")
    for name in BIG:
        slots[name] += [_cast_to_slot(w[name], l, "cast_" + name, after=[pending[3]]) for l in range(1, nl)]
    (h,) = _norm_fwd(x, "norm_first", g_next=w["g_mix_pre"][0])
    passing = start_forward(0, pending, [h] + [s for name in BIG for s in slots[name][1:]], "a")
    full = [finish_forward(0, passing, [passing[3]], first, "a")]
    pending_rest = start_gather(0, rest, "b", after=[full[0]["w_in"]])

    saved = []
    xl = x
    dy = loss_cols = None
    for l in range(nl):
        s = {"x": xl, "h": h}
        fw = full[l]
        token = None
        if l + 1 < nl:
            pending = start_gather(l + 1, after=[pending_rest[3]] if l == 0 else ())
            token = pending[3]
        proj = _mm_nn(h, fw["w_in"], BF16, "mm_in", after=token)
        if l == 0:
            passing = start_forward(0, pending_rest, [proj], "b")
            fw.update(finish_forward(0, passing, [passing[3]], rest, "b"))
        ca, act_a = _branch_a_fwd(proj, cw_a[l], w["conv_a_b"][l], w["ln_a_g"][l], w["ln_a_b"][l], "branch_a_fwd")
        b_in = _branch_b_fwd(proj, cw_b[l], "branch_b_fwd")
        (mem_n,) = _norm_fwd(mem, "norm_mem", g_next=w["g_mem"][l])
        kv = _mm_nn(mem_n, fw["w_kv"], BF16, "mm_kv")
        att = _attn_fwd(proj, kv, "attn_fwd")
        ya = _mm_nn(act_a, fw["w_a_out"], BF16, "mm_a_out")
        yb = _mm_nn(b_in, fw["w_b_out"], BF16, "mm_b_out")
        yx = _mm_nn(att, fw["w_x_out"], BF16, "mm_x_out")
        merged = _merge_fwd(proj, ya, yb, yx, "merge_fwd")
        z = _mm_nn(merged, fw["w_o"], BF16, "mm_o")
        x1, h2 = _norm_fwd(xl, "norm_mid", z=z, g_post=w["g_mix_post"][l], g_next=w["g_mlp_pre"][l])
        token = None
        if l + 1 < nl:
            passing = start_forward(l + 1, pending, [h2])
            token = passing[3]
        up = _mm_nn(h2, fw["w_up"], BF16, "mm_up", after=token)
        f = _mm_nn(up, fw["w_down"], BF16, "mm_down", a_act="relu2")
        s.update(proj=proj, ca=ca, act_a=act_a, b_in=b_in, mem_n=mem_n, kv=kv, att=att, ya=ya, yb=yb, yx=yx,
                 merged=merged, z=z, x1=x1, h2=h2, up=up, f=f)
        saved.append(s)
        if l + 1 < nl:
            full.append(finish_forward(l + 1, passing, [f]))
        if l + 1 < nl:
            xl, h = _norm_fwd(x1, "norm_mid", z=f, g_post=w["g_mlp_post"][l], g_next=w["g_mix_pre"][l + 1])
        else:
            dy, loss_cols = _norm_fwd(x1, "norm_loss", z=f, g_post=w["g_mlp_post"][l], target=target)

    part = {name: [None] * nl for name in BIG}
    small = {name: [None] * nl for name in SMALL_REPLICATED + SMALL_SHARDED}
    dxo = dy
    d_f, small["g_mlp_post"][nl - 1] = _norm_bwd("norm_bwd_top", dxo=dy,
                                                 post=(saved[-1]["f"], w["g_mlp_post"][nl - 1]))
    grad_x = None
    totals = {name: None for name in BIG}

    def start_send(l, after):
        return _send_start([part[name][l] for name in BIG], "rs_send_start_%d" % l, after=after)

    def start_exchange(l, sending, after):
        send, recv, parts, landing, _ = sending
        parts, theirs = _send_wait(send, recv, parts, landing, after, "rs_send_wait_%d" % l)
        chip_sums = [_add_half(p, t, "rs_add_" + name) for name, p, t in zip(BIG, parts, theirs)]
        return _exchange_start(chip_sums, "rs_xchg_start_%d" % l)

    joining = {}

    def finish_join(after):
        if joining:
            send, recv, reds, layer = joining.pop("pending")
            totals.update(zip(BIG, _join_wait(send, recv, reds, layer, after, "rs_join_wait_%d" % layer)))

    def finish_exchange(l, pending, after):
        send, recv, chip_sums, landing, _ = pending
        chip_sums, q = _exchange_wait(send, recv, chip_sums, landing, after, "rs_xchg_wait_%d" % l)
        finish_join(after)
        for name, own, got in zip(BIG, chip_sums, q):
            totals[name] = _sum_chips(own, got, totals[name], l, nl, "rs_sum_" + name)
        if l == 0:
            return None
        send, recv, reds, token = _join_start([totals[name] for name in BIG], l, "rs_join_start_%d" % l)
        joining["pending"] = (send, recv, reds, l)
        totals.update(zip(BIG, reds))
        return token

    sending = exchange = None
    for l in reversed(range(nl)):
        s = saved[l]
        fw = full[l]
        d_up = _mm_nt(d_f, fw["w_down"], BF16, "mm_down_dx", relu2_of=s["up"],
                      after=None if sending is None else sending[4])
        part["w_down"][l] = _mm_tn(s["up"], d_f, 1, "mm_down_dw", a_act="relu2").reshape(N_CHIPS, -1, d)
        d_h2 = _mm_nt(d_up, fw["w_up"], BF16, "mm_up_dx")
        part["w_up"][l] = _mm_tn(s["h2"], d_up, N_CHIPS, "mm_up_dw")
        dx1, d_z, small["g_mlp_pre"][l], small["g_mix_post"][l] = _norm_bwd(
            "norm_bwd_mid", dxo=dxo, pre=(d_h2, s["x1"], w["g_mlp_pre"][l]), post=(s["z"], w["g_mix_post"][l]))
        if sending is not None:
            exchange = start_exchange(l + 1, sending, [d_z])
        d_merged = _mm_nt(d_z, fw["w_o"], BF16, "mm_o_dx", after=None if exchange is None else exchange[4])
        part["w_o"][l] = _mm_tn(s["merged"], d_z, 1, "mm_o_dw").reshape(N_CHIPS, -1, d)
        dproj, d_ya, d_yb, d_yx = _merge_bwd(d_merged, s["proj"], s["ya"], s["yb"], s["yx"], "merge_bwd")
        d_act_a = _mm_nt(d_ya, fw["w_a_out"], BF16, "mm_a_out_dx")
        part["w_a_out"][l] = _mm_tn(s["act_a"], d_ya, N_CHIPS, "mm_a_out_dw")
        d_b_in = _mm_nt(d_yb, fw["w_b_out"], BF16, "mm_b_out_dx")
        part["w_b_out"][l] = _mm_tn(s["b_in"], d_yb, N_CHIPS, "mm_b_out_dw")
        d_att = _mm_nt(d_yx, fw["w_x_out"], BF16, "mm_x_out_dx")
        part["w_x_out"][l] = _mm_tn(s["att"], d_yx, N_CHIPS, "mm_x_out_dw")
        dproj, small["ln_a_g"][l], small["ln_a_b"][l], small["conv_a_b"][l], small["conv_a_w"][l] = _branch_a_bwd(
            dproj, d_act_a, s["ca"], s["proj"], cw_a[l], w["ln_a_g"][l], w["ln_a_b"][l], "branch_a_bwd")
        dproj, small["conv_b_w"][l] = _branch_b_bwd(dproj, d_b_in, s["proj"], cw_b[l], "branch_b_bwd")
        dproj, d_kv = _attn_bwd(dproj, d_att, s["proj"], s["kv"], "attn_bwd")
        part["w_kv"][l] = _mm_tn(s["mem_n"], d_kv, 1, "mm_kv_dw").reshape(N_CHIPS, -1, 2 * c)
        d_mem_n = _mm_nt(d_kv, fw["w_kv"], F32, "mm_kv_dx")
        (small["g_mem"][l],) = _norm_bwd("norm_bwd_mem", pre=(d_mem_n, mem, w["g_mem"][l]), want_dx=False)
        d_h = _mm_nt(dproj, fw["w_in"], BF16, "mm_in_dx")
        part["w_in"][l] = _mm_tn(s["h"], dproj, N_CHIPS, "mm_in_dw")
        join_token = finish_exchange(l + 1, exchange, [d_h]) if exchange is not None else None
        sending = start_send(l, [] if join_token is None else [join_token])
        if l > 0:
            dxo, d_f, small["g_mix_pre"][l], small["g_mlp_post"][l - 1] = _norm_bwd(
                "norm_bwd_mid", dxo=dx1, pre=(d_h, s["x"], w["g_mix_pre"][l]),
                post=(saved[l - 1]["f"], w["g_mlp_post"][l - 1]))
        else:
            grad_x, small["g_mix_pre"][0] = _norm_bwd("norm_bwd_last", dxo=dx1,
                                                      pre=(d_h, s["x"], w["g_mix_pre"][0]))
    exchange = start_exchange(0, sending, [grad_x])

    grads, delta, new_m, new_v = {}, {}, {}, {}
    small_names = SMALL_REPLICATED + SMALL_SHARDED
    stacked = [jnp.stack(small[n]) for n in small_names]
    packed = _pack_rows(stacked, c)
    total = _sum_leading(_gather_all(packed, "gather_small_grads")[None], "sum_small_grads")[0]
    reduced = dict(zip(small_names, _unpack_rows(total, stacked, c)))
    chip = 2 * lax.axis_index("x") + lax.axis_index("y")
    cs = c // N_CHIPS
    for name in SMALL_SHARDED:
        grads[name] = lax.dynamic_slice_in_dim(reduced[name], chip * cs, cs, axis=2)
    for name in SMALL_REPLICATED:
        grads[name] = reduced[name].reshape(w[name].shape)
    for group, width in ((SMALL_REPLICATED, c), (SMALL_SHARDED, cs)):
        packs = [_pack_rows([src[n] for n in group], width)[None] for src in (w, grads, m, v)]
        outs = _adamw(*packs, "adamw_small_%d" % width)
        for dst, out in zip((delta, new_m, new_v), outs):
            dst.update(zip(group, _unpack_rows(out[0], [w[n] for n in group], width)))

    state = {name: () for name in BIG}
    if nl > 1:
        finish_join([exchange[4]])
        for name in BIG:
            state[name] = _adamw(w[name], totals[name], m[name], v[name], "adamw_" + name, batch=(1, nl))
    finish_exchange(0, exchange, [state[name][0] for name in BIG] if nl > 1 else [grad_x])
    grads.update(zip(BIG, _join_halves([totals[name] for name in BIG], 0, 1, "rs_join_first")))
    for name in BIG:
        delta[name], new_m[name], new_v[name] = _adamw(w[name], grads[name], m[name], v[name], "adamw_" + name,
                                                       batch=(0, 1), prev=state[name])

    loss = lax.psum(0.5 * jnp.sum(loss_cols) / d, ("x", "y", "c"))
    return loss, grad_x, grads, delta, new_m, new_v


def kernel(x, mem, g_mix_pre, w_in, conv_a_w, conv_a_b, ln_a_g, ln_a_b, w_a_out, conv_b_w, w_b_out, g_mem, w_kv, w_x_out, w_o, g_mix_post, g_mlp_pre, w_up, w_down, g_mlp_post, loss_target, m_g_mix_pre, m_w_in, m_conv_a_w, m_conv_a_b, m_ln_a_g, m_ln_a_b, m_w_a_out, m_conv_b_w, m_w_b_out, m_g_mem, m_w_kv, m_w_x_out, m_w_o, m_g_mix_post, m_g_mlp_pre, m_w_up, m_w_down, m_g_mlp_post, v_g_mix_pre, v_w_in, v_conv_a_w, v_conv_a_b, v_ln_a_g, v_ln_a_b, v_w_a_out, v_conv_b_w, v_w_b_out, v_g_mem, v_w_kv, v_w_x_out, v_w_o, v_g_mix_post, v_g_mlp_pre, v_w_up, v_w_down, v_g_mlp_post):
    w = dict(g_mix_pre=g_mix_pre, w_in=w_in, conv_a_w=conv_a_w, conv_a_b=conv_a_b, ln_a_g=ln_a_g, ln_a_b=ln_a_b,
             w_a_out=w_a_out, conv_b_w=conv_b_w, w_b_out=w_b_out, g_mem=g_mem, w_kv=w_kv, w_x_out=w_x_out, w_o=w_o,
             g_mix_post=g_mix_post, g_mlp_pre=g_mlp_pre, w_up=w_up, w_down=w_down, g_mlp_post=g_mlp_post)
    m = dict(g_mix_pre=m_g_mix_pre, w_in=m_w_in, conv_a_w=m_conv_a_w, conv_a_b=m_conv_a_b, ln_a_g=m_ln_a_g,
             ln_a_b=m_ln_a_b, w_a_out=m_w_a_out, conv_b_w=m_conv_b_w, w_b_out=m_w_b_out, g_mem=m_g_mem, w_kv=m_w_kv,
             w_x_out=m_w_x_out, w_o=m_w_o, g_mix_post=m_g_mix_post, g_mlp_pre=m_g_mlp_pre, w_up=m_w_up,
             w_down=m_w_down, g_mlp_post=m_g_mlp_post)
    v = dict(g_mix_pre=v_g_mix_pre, w_in=v_w_in, conv_a_w=v_conv_a_w, conv_a_b=v_conv_a_b, ln_a_g=v_ln_a_g,
             ln_a_b=v_ln_a_b, w_a_out=v_w_a_out, conv_b_w=v_conv_b_w, w_b_out=v_w_b_out, g_mem=v_g_mem, w_kv=v_w_kv,
             w_x_out=v_w_x_out, w_o=v_w_o, g_mix_post=v_g_mix_post, g_mlp_pre=v_g_mlp_pre, w_up=v_w_up,
             w_down=v_w_down, g_mlp_post=v_g_mlp_post)
    loss, grad_x, grads, delta, new_m, new_v = _step(w, m, v, x[0], mem[0], loss_target[0])
    out = [loss, grad_x[None]]
    for group in (grads, delta, new_m, new_v):
        out += [group[n] for n in WEIGHT_ORDER]
    return tuple(out)
```
